```python
import jax, jax.numpy as jnp
from jax import lax
import numpy as np

D_MODEL = 1024
BATCH = 1
SEQ = 16384
DEPTH = 2
DEC_BATCH = 8
DEC_SEQ = 64
PAST_LEN = 1024

CHUNK = 64
HEAD_DIM = 64
N_HEADS_A = 8
N_HEADS_B = 8
N_HEADS_C = 16
IDX_HEADS = 8
IDX_DIM = 64
A_PAST_CHUNKS = 8
A_PAST = A_PAST_CHUNKS * CHUNK
REL_CLIP = 128
TOPK_MAX = 256
Q_BLOCK = 128
D_FF = 4 * D_MODEL
D_PLE = 256
ROPE_THETA = 10000.0
RMS_EPS = 1e-6
N_EVEN = (DEPTH + 1) // 2
N_ODD = DEPTH // 2
WA = N_HEADS_A * HEAD_DIM
WB = N_HEADS_B * HEAD_DIM
WC = N_HEADS_C * HEAD_DIM
WI = IDX_HEADS * IDX_DIM
EVEN_SPLITS = (WA, WA, WA, WB, WB, WB, WI, IDX_DIM, IDX_HEADS)
P_EVEN = sum(EVEN_SPLITS)
ODD_SPLITS = (WC, WC, WC, N_HEADS_C)
P_ODD = sum(ODD_SPLITS)

kernel_name = 'hybrid_streaming_encoder_step'


def rms_norm(x, g):
    xf = x.astype(jnp.float32)
    y = xf * lax.rsqrt(jnp.mean(xf * xf, axis=-1, keepdims=True) + RMS_EPS)
    return (y * g.astype(jnp.float32)).astype(x.dtype)


def rope(x, pos):
    half = x.shape[-1] // 2
    inv = ROPE_THETA ** (-jnp.arange(half, dtype=jnp.float32) / half)
    ang = pos.astype(jnp.float32)[:, None] * inv[None, :]
    cos = jnp.cos(ang)[:, None, :]
    sin = jnp.sin(ang)[:, None, :]
    xf = x.astype(jnp.float32)
    x1, x2 = xf[..., :half], xf[..., half:]
    return jnp.concatenate([x1 * cos - x2 * sin, x2 * cos + x1 * sin], axis=-1).astype(x.dtype)


def _split(z, sizes):
    idx = [int(i) for i in np.cumsum(sizes)[:-1]]
    return jnp.split(z, idx, axis=-1)


def even_project(a, w_in, qn_a, kn_a, qn_b, kn_b, pos):
    B, S, _ = a.shape
    z = jnp.einsum('bsd,dp->bsp', a, w_in)
    qa, ka, va, qb, kb, vb, qi, ki, wi = _split(z, EVEN_SPLITS)
    hd = lambda t, h: t.reshape(B, S, h, HEAD_DIM)
    qa = rms_norm(hd(qa, N_HEADS_A), qn_a)
    ka = rms_norm(hd(ka, N_HEADS_A), kn_a)
    va = hd(va, N_HEADS_A)
    qb = rope(rms_norm(hd(qb, N_HEADS_B), qn_b), pos)
    kb = rope(rms_norm(hd(kb, N_HEADS_B), kn_b), pos)
    vb = hd(vb, N_HEADS_B)
    qi = rope(qi.reshape(B, S, IDX_HEADS, IDX_DIM), pos)
    ki = rope(ki[:, :, None, :], pos)[:, :, 0, :]
    return qa, ka, va, qb, kb, vb, qi, ki, wi


def odd_project(a, w_in, b_f, qn_c, kn_c):
    B, S, _ = a.shape
    z = jnp.einsum('bsd,dp->bsp', a, w_in)
    q, k, v, fl = _split(z, ODD_SPLITS)
    hd = lambda t: t.reshape(B, S, N_HEADS_C, HEAD_DIM)
    q = rms_norm(hd(q), qn_c)
    k = rms_norm(hd(k), kn_c)
    logf = jax.nn.log_sigmoid(fl.astype(jnp.float32) + b_f.astype(jnp.float32))
    return q, k, hd(v), logf


def band_attention(q, k, v, q_pos, k_pos, rel_bias):
    s = jnp.einsum('bqhd,bkhd->bhqk', q, k, preferred_element_type=jnp.float32) * (HEAD_DIM ** -0.5)
    rel = jnp.clip(q_pos[:, None] - k_pos[None, :], -REL_CLIP, REL_CLIP) + REL_CLIP
    s = s + rel_bias.astype(jnp.float32)[:, rel]
    qc = q_pos[:, None] // CHUNK
    kc = k_pos[None, :] // CHUNK
    ok = (k_pos[None, :] >= 0) & (kc <= qc) & (kc >= qc - A_PAST_CHUNKS)
    p = jax.nn.softmax(jnp.where(ok, s, -jnp.inf), axis=-1)
    return jnp.einsum('bhqk,bkhd->bqhd', p.astype(v.dtype), v)


def band_attention_prompt(q, k, v, rel_bias):
    B, S, H, D = q.shape
    pad = ((0, 0), (A_PAST, 0), (0, 0), (0, 0))
    kp, vp = jnp.pad(k, pad), jnp.pad(v, pad)

    def one_chunk(c):
        start = c * CHUNK
        qc = lax.dynamic_slice_in_dim(q, start, CHUNK, axis=1)
        kc = lax.dynamic_slice_in_dim(kp, start, A_PAST + CHUNK, axis=1)
        vc = lax.dynamic_slice_in_dim(vp, start, A_PAST + CHUNK, axis=1)
        q_pos = start + jnp.arange(CHUNK)
        k_pos = start - A_PAST + jnp.arange(A_PAST + CHUNK)
        return band_attention(qc, kc, vc, q_pos, k_pos, rel_bias)

    out = lax.map(one_chunk, jnp.arange(S // CHUNK))
    return out.transpose(1, 0, 2, 3, 4).reshape(B, S, H, D)


def dsa_attention(q, k, v, qi, ki, wi, q_pos, k_pos, topk):
    logits = jnp.einsum('bqhd,bkd->bqhk', qi, ki, preferred_element_type=jnp.float32) * (IDX_DIM ** -0.5)
    score = jnp.einsum('bqhk,bqh->bqk', jax.nn.relu(logits), wi.astype(jnp.float32) * (IDX_HEADS ** -0.5))
    ok = (k_pos[None, :] // CHUNK) <= (q_pos[:, None] // CHUNK)
    score = jnp.where(ok, score, -jnp.inf)
    _, sel = lax.top_k(score, topk)
    sel_ok = jnp.take_along_axis(jnp.broadcast_to(ok, score.shape), sel, axis=-1)
    gather = jax.vmap(lambda tb, sb: tb[sb])
    kg, vg = gather(k, sel), gather(v, sel)
    s = jnp.einsum('bqhd,bqkhd->bhqk', q, kg, preferred_element_type=jnp.float32) * (HEAD_DIM ** -0.5)
    s = jnp.where(sel_ok[:, None, :, :], s, -jnp.inf)
    p = jax.nn.softmax(s, axis=-1)
    return jnp.einsum('bhqk,bqkhd->bqhd', p.astype(v.dtype), vg)


def dsa_prompt(q, k, v, qi, ki, wi):
    B, S, H, D = q.shape
    topk = min(TOPK_MAX, S // 4)
    k_pos = jnp.arange(S)

    def one_block(blk):
        start = blk * Q_BLOCK
        sl = lambda t: lax.dynamic_slice_in_dim(t, start, Q_BLOCK, axis=1)
        return dsa_attention(sl(q), k, v, sl(qi), ki, sl(wi), start + jnp.arange(Q_BLOCK), k_pos, topk)

    out = lax.map(one_block, jnp.arange(S // Q_BLOCK))
    return out.transpose(1, 0, 2, 3, 4).reshape(B, S, H, D)


def fox_attention(q, k, v, Fq, Fk, q_pos, k_pos):
    s = jnp.einsum('bqhd,bkhd->bhqk', q, k, preferred_element_type=jnp.float32) * (HEAD_DIM ** -0.5)
    s = s + (jnp.swapaxes(Fq, 1, 2)[:, :, :, None] - jnp.swapaxes(Fk, 1, 2)[:, :, None, :])
    ok = k_pos[None, :] <= q_pos[:, None]
    p = jax.nn.softmax(jnp.where(ok, s, -jnp.inf), axis=-1)
    return jnp.einsum('bhqk,bkhd->bqhd', p.astype(v.dtype), v)


def fox_prompt(q, k, v, F):
    B, S, H, D = q.shape
    k_pos = jnp.arange(S)

    def one_block(blk):
        start = blk * Q_BLOCK
        sl = lambda t: lax.dynamic_slice_in_dim(t, start, Q_BLOCK, axis=1)
        return fox_attention(sl(q), k, v, sl(F), F, start + jnp.arange(Q_BLOCK), k_pos)

    out = lax.map(one_block, jnp.arange(S // Q_BLOCK))
    return out.transpose(1, 0, 2, 3, 4).reshape(B, S, H, D)


def channel_and_ple(h, p_i, g_mlp, w_up, w_down, g_ple, w_ple_gate, w_ple_proj):
    m = rms_norm(h, g_mlp)
    h = h + jnp.square(jax.nn.relu(m @ w_up)) @ w_down
    gate = jax.nn.sigmoid(rms_norm(h, g_ple) @ w_ple_gate)
    return h + gate * (p_i @ w_ple_proj)


def setup_inputs(seed: int = 0) -> dict:
    key = jax.random.key(seed)
    ks = iter(jax.random.split(key, 48))

    def nrm(shape, scale=1.0):
        return scale * jax.random.normal(next(ks), shape, jnp.float32)

    def gain(shape):
        return 1.0 + 0.05 * nrm(shape)

    a_win = min(A_PAST, PAST_LEN)
    return {
        'x_prompt': nrm((BATCH, SEQ, D_MODEL)),
        'x_sample': nrm((DEC_BATCH, DEC_SEQ, D_MODEL)),
        'cache_a_k': nrm((N_EVEN, DEC_BATCH, a_win, N_HEADS_A, HEAD_DIM)),
        'cache_a_v': nrm((N_EVEN, DEC_BATCH, a_win, N_HEADS_A, HEAD_DIM)),
        'cache_b_k': nrm((N_EVEN, DEC_BATCH, PAST_LEN, N_HEADS_B, HEAD_DIM)),
        'cache_b_v': nrm((N_EVEN, DEC_BATCH, PAST_LEN, N_HEADS_B, HEAD_DIM)),
        'cache_b_ik': nrm((N_EVEN, DEC_BATCH, PAST_LEN, IDX_DIM)),
        'cache_c_k': nrm((N_ODD, DEC_BATCH, PAST_LEN, N_HEADS_C, HEAD_DIM)),
        'cache_c_v': nrm((N_ODD, DEC_BATCH, PAST_LEN, N_HEADS_C, HEAD_DIM)),
        'cache_c_logf': jax.nn.log_sigmoid(2.0 + nrm((N_ODD, DEC_BATCH, PAST_LEN, N_HEADS_C))),
        'p_prompt': nrm((DEPTH, BATCH, SEQ, D_PLE)),
        'p_sample': nrm((DEPTH, DEC_BATCH, DEC_SEQ, D_PLE)),
        'g_mix': gain((DEPTH, D_MODEL)),
        'w_in_even': nrm((N_EVEN, D_MODEL, P_EVEN), D_MODEL ** -0.5),
        'qn_a': gain((N_EVEN, HEAD_DIM)),
        'kn_a': gain((N_EVEN, HEAD_DIM)),
        'rel_bias_a': nrm((N_EVEN, N_HEADS_A, 2 * REL_CLIP + 1), 0.1),
        'qn_b': gain((N_EVEN, HEAD_DIM)),
        'kn_b': gain((N_EVEN, HEAD_DIM)),
        'w_out_even': nrm((N_EVEN, WA + WB, D_MODEL), (WA + WB) ** -0.5),
        'w_in_odd': nrm((N_ODD, D_MODEL, P_ODD), D_MODEL ** -0.5),
        'b_f': 2.0 + 0.1 * nrm((N_ODD, N_HEADS_C)),
        'qn_c': gain((N_ODD, HEAD_DIM)),
        'kn_c': gain((N_ODD, HEAD_DIM)),
        'w_out_odd': nrm((N_ODD, WC, D_MODEL), WC ** -0.5),
        'g_mlp': gain((DEPTH, D_MODEL)),
        'w_up': nrm((DEPTH, D_MODEL, D_FF), D_MODEL ** -0.5),
        'w_down': nrm((DEPTH, D_FF, D_MODEL), D_FF ** -0.5),
        'g_ple': gain((DEPTH, D_MODEL)),
        'w_ple_gate': nrm((DEPTH, D_MODEL, D_MODEL), D_MODEL ** -0.5),
        'w_ple_proj': nrm((DEPTH, D_PLE, D_MODEL), D_PLE ** -0.5),
    }


def reference(x_prompt, x_sample, cache_a_k, cache_a_v, cache_b_k, cache_b_v, cache_b_ik,
              cache_c_k, cache_c_v, cache_c_logf, p_prompt, p_sample,
              g_mix, w_in_even, qn_a, kn_a, rel_bias_a, qn_b, kn_b, w_out_even,
              w_in_odd, b_f, qn_c, kn_c, w_out_odd, g_mlp, w_up, w_down,
              g_ple, w_ple_gate, w_ple_proj):
    B, S, _ = x_prompt.shape
    DB, n, _ = x_sample.shape
    P = cache_b_k.shape[2]
    W_A = cache_a_k.shape[2]
    W_AP = min(A_PAST, S)
    pos_p = jnp.arange(S)
    pos_s = P + jnp.arange(n)
    k_pos_s = jnp.arange(P + n)
    topk_s = min(TOPK_MAX, (P + n) // 4)

    a_k_p, a_v_p, b_k_p, b_v_p, b_ik_p = [], [], [], [], []
    a_k_s, a_v_s, b_k_s, b_v_s, b_ik_s = [], [], [], [], []
    c_k_p, c_v_p, c_lf_p, c_k_s, c_v_s, c_lf_s = [], [], [], [], [], []

    h_p, h_s = x_prompt, x_sample
    for i in range(DEPTH):
        a_p = rms_norm(h_p, g_mix[i])
        a_s = rms_norm(h_s, g_mix[i])
        if i % 2 == 0:
            e = i // 2
            qa, ka, va, qb, kb, vb, qi, ki, wi = even_project(a_p, w_in_even[e], qn_a[e], kn_a[e], qn_b[e], kn_b[e], pos_p)
            oa = band_attention_prompt(qa, ka, va, rel_bias_a[e])
            ob = dsa_prompt(qb, kb, vb, qi, ki, wi)
            mix_p = jnp.concatenate([oa.reshape(B, S, WA), ob.reshape(B, S, WB)], axis=-1) @ w_out_even[e]
            a_k_p.append(ka[:, S - W_AP:]); a_v_p.append(va[:, S - W_AP:])
            b_k_p.append(kb); b_v_p.append(vb); b_ik_p.append(ki)
            qa, ka, va, qb, kb, vb, qi, ki, wi = even_project(a_s, w_in_even[e], qn_a[e], kn_a[e], qn_b[e], kn_b[e], pos_s)
            kk_a = jnp.concatenate([cache_a_k[e], ka], axis=1)
            vv_a = jnp.concatenate([cache_a_v[e], va], axis=1)
            oa = band_attention(qa, kk_a, vv_a, pos_s, P - W_A + jnp.arange(W_A + n), rel_bias_a[e])
            kk_b = jnp.concatenate([cache_b_k[e], kb], axis=1)
            vv_b = jnp.concatenate([cache_b_v[e], vb], axis=1)
            ki_b = jnp.concatenate([cache_b_ik[e], ki], axis=1)
            ob = dsa_attention(qb, kk_b, vv_b, qi, ki_b, wi, pos_s, k_pos_s, topk_s)
            mix_s = jnp.concatenate([oa.reshape(DB, n, WA), ob.reshape(DB, n, WB)], axis=-1) @ w_out_even[e]
            a_k_s.append(kk_a[:, n:]); a_v_s.append(vv_a[:, n:])
            b_k_s.append(kb); b_v_s.append(vb); b_ik_s.append(ki)
        else:
            o = i // 2
            q, k, v, logf = odd_project(a_p, w_in_odd[o], b_f[o], qn_c[o], kn_c[o])
            F = jnp.cumsum(logf, axis=1)
            oc = fox_prompt(q, k, v, F)
            mix_p = oc.reshape(B, S, WC) @ w_out_odd[o]
            c_k_p.append(k); c_v_p.append(v); c_lf_p.append(logf)
            q, k, v, logf = odd_project(a_s, w_in_odd[o], b_f[o], qn_c[o], kn_c[o])
            lf_all = jnp.concatenate([cache_c_logf[o].astype(jnp.float32), logf], axis=1)
            F = jnp.cumsum(lf_all, axis=1)
            kk = jnp.concatenate([cache_c_k[o], k], axis=1)
            vv = jnp.concatenate([cache_c_v[o], v], axis=1)
            oc = fox_attention(q, kk, vv, F[:, P:], F, pos_s, k_pos_s)
            mix_s = oc.reshape(DB, n, WC) @ w_out_odd[o]
            c_k_s.append(k); c_v_s.append(v); c_lf_s.append(logf)
        h_p = channel_and_ple(h_p + mix_p, p_prompt[i], g_mlp[i], w_up[i], w_down[i], g_ple[i], w_ple_gate[i], w_ple_proj[i])
        h_s = channel_and_ple(h_s + mix_s, p_sample[i], g_mlp[i], w_up[i], w_down[i], g_ple[i], w_ple_gate[i], w_ple_proj[i])

    st = jnp.stack
    return (h_p, h_s,
            st(a_k_p), st(a_v_p), st(b_k_p), st(b_v_p), st(b_ik_p),
            st(c_k_p), st(c_v_p), st(c_lf_p),
            st(a_k_s), st(a_v_s), st(b_k_s), st(b_v_s), st(b_ik_s),
            st(c_k_s), st(c_v_s), st(c_lf_s))
```

```python
import functools

import numpy as np
import jax
import jax.numpy as jnp
from jax import lax
from jax.experimental import pallas as pl
from jax.experimental.pallas import tpu as pltpu

F32 = jnp.float32
I32 = jnp.int32
CDT = jnp.bfloat16

CHUNK = 64
HEAD_DIM = 64
IDX_HEADS = 8
A_PAST = 8 * CHUNK
REL_CLIP = 128
TOPK_MAX = 256
ROPE_THETA = 10000.0
RMS_EPS = 1e-6
LANES = 128
NEG = -1e30
INT_MIN = -(2 ** 31)
INT_MAX = 2 ** 31 - 1
VMEM_LIMIT = 56 * 1024 * 1024


def _mm(a, b):
    return jnp.dot(a, b, preferred_element_type=F32)


def _mm_nt(a, b):
    return lax.dot_general(a, b, (((1,), (1,)), ((), ())), preferred_element_type=F32)


def _pieces(x, n):
    if CDT == F32:
        return [x]
    out = []
    for _ in range(n - 1):
        p = x.astype(CDT)
        out.append(p)
        x = x - p.astype(F32)
    out.append(x.astype(CDT))
    return out


def _rms(x, g):
    return x * lax.rsqrt(jnp.mean(x * x, axis=-1, keepdims=True) + RMS_EPS) * g


def _head_rms(x, bd, g):
    x2 = x * x
    pcs = _pieces(x2, 2)
    cols = []
    for s in range(x.shape[1] // 256):
        sl = slice(s * 256, (s + 1) * 256)
        ms = _mm(pcs[0][:, sl], bd)
        for p in pcs[1:]:
            ms = ms + _mm(p[:, sl], bd)
        cols.append(ms)
    ms = cols[0] if len(cols) == 1 else jnp.concatenate(cols, axis=1)
    return x * lax.rsqrt(ms + RMS_EPS) * g


def _rope(x, cos, sin):
    w = x.shape[1]
    lane = lax.broadcasted_iota(I32, x.shape, 1)
    first = (lane & 63) < 32
    swapped = jnp.where(first, pltpu.roll(x, w - 32, 1), pltpu.roll(x, 32, 1))
    return x * cos + swapped * sin


def _tile_lanes(t, w):
    return t if w == t.shape[1] else jnp.concatenate([t] * (w // t.shape[1]), axis=1)


def _proj_even_kernel(h_ref, g_ref, w_ref, gn_ref, bd_ref, cos_ref, sin_ref,
                      ka_ref, va_ref, kb_ref, vb_ref, kw_ref,
                      qa_c, ka_c, va_c, qb_c, kb_c, vb_c, qi_c, ki2_c):
    a = _rms(h_ref[...], g_ref[...]).astype(CDT)
    z = _mm(a, w_ref[...])
    w = 512
    bd = bd_ref[...]
    gn = gn_ref[...]
    cos1, sin1 = cos_ref[...], sin_ref[...]
    cos, sin = _tile_lanes(cos1, w), _tile_lanes(sin1, w)
    scale = HEAD_DIM ** -0.5

    qa = _head_rms(z[:, 0:w], bd, gn[0:1])
    qa_c[...] = (qa * scale).astype(CDT)
    ka = _head_rms(z[:, w:2 * w], bd, gn[1:2])
    ka_ref[...] = ka
    ka_c[...] = ka.astype(CDT)
    va = z[:, 2 * w:3 * w]
    va_ref[...] = va
    va_c[...] = va.astype(CDT)
    qb = _rope(_head_rms(z[:, 3 * w:4 * w], bd, gn[2:3]), cos, sin)
    qb_c[...] = (qb * scale).astype(CDT)
    kb = _rope(_head_rms(z[:, 4 * w:5 * w], bd, gn[3:4]), cos, sin)
    kb_ref[...] = kb
    kb_c[...] = kb.astype(CDT)
    vb = z[:, 5 * w:6 * w]
    vb_ref[...] = vb
    vb_c[...] = vb.astype(CDT)
    qi = _rope(z[:, 6 * w:7 * w], cos, sin)
    qi_c[...] = (qi * scale).astype(CDT)
    kw = z[:, 7 * w:7 * w + LANES]
    kwr = _rope(kw, cos1, sin1)
    lane = lax.broadcasted_iota(I32, kw.shape, 1)
    kw_ref[...] = jnp.where(lane < 64, kwr, kw)
    ki2_c[...] = jnp.where(lane < 64, kwr, pltpu.roll(kwr, 64, 1)).astype(CDT)


def _proj_odd_kernel(h_ref, g_ref, w_ref, gn_ref, bf_ref, bd_ref,
                     k_ref, v_ref, lf_ref, q_c, k_c, v_c):
    a = _rms(h_ref[...], g_ref[...]).astype(CDT)
    z = _mm(a, w_ref[...])
    w = 1024
    bd = bd_ref[...]
    gn = gn_ref[...]
    q = _head_rms(z[:, 0:w], bd, gn[0:1])
    q_c[...] = (q * (HEAD_DIM ** -0.5)).astype(CDT)
    k = _head_rms(z[:, w:2 * w], bd, gn[1:2])
    k_ref[...] = k
    k_c[...] = k.astype(CDT)
    v = z[:, 2 * w:3 * w]
    v_ref[...] = v
    v_c[...] = v.astype(CDT)
    fl = z[:, 3 * w:3 * w + LANES] + bf_ref[...]
    lf_ref[...] = jnp.minimum(fl, 0.0) - jnp.log1p(jnp.exp(-jnp.abs(fl)))


def _const_spec(shape):
    nd = len(shape)
    return pl.BlockSpec(shape, lambda *_: (0,) * nd, pipeline_mode=pl.Buffered(1))


def _row_spec(tm, w):
    return pl.BlockSpec((tm, w), lambda i: (i, 0))


def _params(sem):
    return pltpu.CompilerParams(dimension_semantics=sem, vmem_limit_bytes=VMEM_LIMIT)


def _row_tile(rows, pref):
    return pref if rows % pref == 0 else rows


def _proj_even(h, g, w_all, gn, bd, cos, sin):
    rows, d = h.shape
    tm = _row_tile(rows, 256)
    f = lambda w, dt: jax.ShapeDtypeStruct((rows, w), dt)
    out_shape = [f(512, F32)] * 4 + [f(LANES, F32)] + [f(512, CDT)] * 7 + [f(LANES, CDT)]
    out_specs = [_row_spec(tm, 512)] * 4 + [_row_spec(tm, LANES)] + [_row_spec(tm, 512)] * 7 + [_row_spec(tm, LANES)]
    return pl.pallas_call(
        _proj_even_kernel,
        grid=(rows // tm,),
        in_specs=[_row_spec(tm, d), _const_spec(g.shape), _const_spec(w_all.shape), _const_spec(gn.shape),
                  _const_spec(bd.shape), _row_spec(tm, LANES), _row_spec(tm, LANES)],
        out_specs=out_specs,
        out_shape=out_shape,
        compiler_params=_params(("parallel",)),
        name="proj_even",
    )(h, g, w_all, gn, bd, cos, sin)


def _proj_odd(h, g, w_all, gn, bf, bd):
    rows, d = h.shape
    tm = _row_tile(rows, 256)
    f = lambda w, dt: jax.ShapeDtypeStruct((rows, w), dt)
    out_shape = [f(1024, F32)] * 2 + [f(LANES, F32)] + [f(1024, CDT)] * 3
    out_specs = [_row_spec(tm, 1024)] * 2 + [_row_spec(tm, LANES)] + [_row_spec(tm, 1024)] * 3
    return pl.pallas_call(
        _proj_odd_kernel,
        grid=(rows // tm,),
        in_specs=[_row_spec(tm, d), _const_spec(g.shape), _const_spec(w_all.shape), _const_spec(gn.shape),
                  _const_spec(bf.shape), _const_spec(bd.shape)],
        out_specs=out_specs,
        out_shape=out_shape,
        compiler_params=_params(("parallel",)),
        name="proj_odd",
    )(h, g, w_all, gn, bf, bd)


def _post_kernel(*refs, n_o, ff_chunk):
    h_ref = refs[0]
    o_refs = refs[1:1 + n_o]
    p_ref = refs[1 + n_o]
    wo_refs = refs[2 + n_o:2 + 2 * n_o]
    gm_ref, wu_ref, wd_ref, gp_ref, wg_ref, wp_ref, out_ref = refs[2 + 2 * n_o:]
    mix = None
    for o_ref, wo_ref in zip(o_refs, wo_refs):
        t = _mm(o_ref[...], wo_ref[...])
        mix = t if mix is None else mix + t
    h = h_ref[...] + mix
    m = _rms(h, gm_ref[...]).astype(CDT)
    mlp = None
    d_ff = wu_ref.shape[1]
    for c in range(d_ff // ff_chunk):
        sl = slice(c * ff_chunk, (c + 1) * ff_chunk)
        u = jnp.square(jnp.maximum(_mm(m, wu_ref[:, sl]), 0.0)).astype(CDT)
        t = _mm(u, wd_ref[sl, :])
        mlp = t if mlp is None else mlp + t
    acc = h + mlp
    gate_in = _rms(acc, gp_ref[...]).astype(CDT)
    gate = 1.0 / (1.0 + jnp.exp(-_mm(gate_in, wg_ref[...])))
    out_ref[...] = acc + gate * _mm(p_ref[...].astype(CDT), wp_ref[...])


def _post(h, os_, p, wos, gm, wu, wd, gp, wg, wp):
    rows, d = h.shape
    tm = _row_tile(rows, 512)
    n_o = len(os_)
    in_specs = ([_row_spec(tm, d)] + [_row_spec(tm, o.shape[1]) for o in os_] + [_row_spec(tm, p.shape[1])]
                + [_const_spec(w.shape) for w in wos]
                + [_const_spec(x.shape) for x in (gm, wu, wd, gp, wg, wp)])
    return pl.pallas_call(
        functools.partial(_post_kernel, n_o=n_o, ff_chunk=1024),
        grid=(rows // tm,),
        in_specs=in_specs,
        out_specs=_row_spec(tm, d),
        out_shape=jax.ShapeDtypeStruct((rows, d), F32),
        compiler_params=_params(("parallel",)),
        name="post",
    )(h, *os_, p, *wos, gm, wu, wd, gp, wg, wp)


def _pair_stack(qp, e_rows):
    lane = lax.broadcasted_iota(I32, qp.shape, 1)
    zero = jnp.zeros_like(qp)
    return jnp.concatenate([jnp.where(lane < 64, qp, zero), jnp.where(lane >= 64, qp, zero)], axis=0)


def _pair_merge(o2, tq):
    lane = lax.broadcasted_iota(I32, (tq, LANES), 1)
    return jnp.where(lane < 64, o2[:tq], o2[tq:])


def _last_tile(i, tq, tk, qoff, length, chunked):
    qend = qoff + i * tq + tq - 1
    kmax = (qend // CHUNK + 1) * CHUNK if chunked else qend + 1
    kmax = jnp.minimum(kmax, length)
    return (kmax - 1) // tk


def _online_softmax_step(s, v, m_ref, l_ref, acc_ref, g):
    m_old = m_ref[g]
    m_new = jnp.maximum(m_old, jnp.max(s, axis=-1, keepdims=True))
    alpha = jnp.exp(m_old - m_new)
    p = jnp.exp(s - m_new)
    l_ref[g] = alpha * l_ref[g] + jnp.sum(p, axis=-1, keepdims=True)
    m_ref[g] = m_new
    acc_ref[g] = alpha * acc_ref[g] + _mm(p.astype(CDT), v)


def _band_kernel(*refs, n_kv, tq, mask_neg):
    q_ref = refs[0]
    k_refs = refs[1:1 + n_kv]
    v_refs = refs[1 + n_kv:1 + 2 * n_kv]
    bias_ref = refs[1 + 2 * n_kv]
    o_ref = refs[2 + 2 * n_kv]
    i = pl.program_id(1)
    q = q_ref[0]
    cat = lambda rs: (rs[0][0].astype(CDT) if len(rs) == 1
                      else jnp.concatenate([r[0].astype(CDT) for r in rs], axis=0))
    k = cat(k_refs)
    v = cat(v_refs)
    tk = bias_ref.shape[2]
    if k.shape[0] < tk:
        zpad = jnp.zeros((tk - k.shape[0], k.shape[1]), CDT)
        k = jnp.concatenate([k, zpad], axis=0)
        v = jnp.concatenate([v, zpad], axis=0)
    if mask_neg:
        kpos = lax.broadcasted_iota(I32, (2 * tq, tk), 1) + (i * tq - A_PAST)
        neg = kpos < 0
    for g in range(q.shape[1] // LANES):
        sl = slice(g * LANES, (g + 1) * LANES)
        s = _mm_nt(_pair_stack(q[:, sl], tq), k[:, sl])
        s = s + jnp.concatenate([bias_ref[2 * g], bias_ref[2 * g + 1]], axis=0)
        if mask_neg:
            s = jnp.where(neg, NEG, s)
        m = jnp.max(s, axis=-1, keepdims=True)
        p = jnp.exp(s - m)
        l = jnp.sum(p, axis=-1, keepdims=True)
        o2 = _mm(p.astype(CDT), v[:, sl]) / l
        o_ref[0, :, sl] = _pair_merge(o2, tq).astype(o_ref.dtype)


def _band(q, k_parts, v_parts, k_specs, bias, tq, mask_neg):
    b, s, w = q.shape
    n_kv = len(k_parts)
    q_spec = pl.BlockSpec((1, tq, w), lambda bb, i: (bb, i, 0))
    return pl.pallas_call(
        functools.partial(_band_kernel, n_kv=n_kv, tq=tq, mask_neg=mask_neg),
        grid=(b, s // tq),
        in_specs=[q_spec] + k_specs + k_specs + [_const_spec(bias.shape)],
        out_specs=q_spec,
        out_shape=jax.ShapeDtypeStruct((b, s, w), CDT),
        compiler_params=_params(("parallel", "parallel")),
        name="band_attn",
    )(q, *k_parts, *v_parts, bias)


def _band_bias(rel_bias, q_pos, k_pos):
    rel = np.clip(q_pos[:, None] - k_pos[None, :], -REL_CLIP, REL_CLIP) + REL_CLIP
    qc = q_pos[:, None] // CHUNK
    kc = k_pos[None, :] // CHUNK
    ok = (kc <= qc) & (kc >= qc - A_PAST // CHUNK)
    bias = jnp.take(rel_bias.astype(F32), jnp.asarray(rel.reshape(-1)), axis=1).reshape(
        rel_bias.shape[0], *rel.shape)
    return jnp.where(jnp.asarray(ok)[None], bias, NEG)


def _dsa_index_kernel(qi_ref, ki2_ref, kw_ref, keys_ref, t_ref, c_ref, qst_ref, kscr_ref,
                      *, tq, tk, qoff, length, topk, idx_bits):
    i = pl.program_id(1)
    j = pl.program_id(2)
    last = _last_tile(i, tq, tk, qoff, length, True)
    q0 = qoff + i * tq

    @pl.when(j == 0)
    def _():
        q = qi_ref[0]
        for g in range(IDX_HEADS // 2):
            qst_ref[2 * g * tq:(2 * g + 2) * tq, :] = _pair_stack(q[:, g * LANES:(g + 1) * LANES], tq)

    @pl.when(j <= last)
    def _():
        lg = _mm_nt(qst_ref[...], ki2_ref[0])
        wgt = kw_ref[0][:, 64:64 + IDX_HEADS] * (IDX_HEADS ** -0.5)
        sc = jnp.zeros((tq, tk), F32)
        for h in range(IDX_HEADS):
            sc = sc + jnp.maximum(lg[h * tq:(h + 1) * tq], 0.0) * wgt[:, h:h + 1]
        bits = lax.bitcast_convert_type(sc, I32)
        key = bits ^ ((bits >> 31) & 0x7FFFFFFF)
        kpos = j * tk + lax.broadcasted_iota(I32, (tq, tk), 1)
        qpos = q0 + lax.broadcasted_iota(I32, (tq, tk), 0)
        ok = ((kpos >> 6) <= (qpos >> 6)) & (kpos < length)
        key = jnp.where(ok, key, INT_MIN)
        keys_ref[0] = key
        kscr_ref[j] = key

    @pl.when(j == last)
    def _():
        def count(pred):
            def body(t, acc):
                kpos = t * tk + lax.broadcasted_iota(I32, (tq, tk), 1)
                m = jnp.where(pred(kscr_ref[t], kpos), 1.0, 0.0)
                part = m[:, 0:LANES]
                for s in range(1, tk // LANES):
                    part = part + m[:, s * LANES:(s + 1) * LANES]
                return acc + part
            acc = lax.fori_loop(0, last + 1, body, jnp.zeros((tq, LANES), F32))
            return jnp.sum(acc, axis=-1, keepdims=True)

        kf = float(topk)

        def bit_body(it, carry):
            prefix, nge = carry
            cand = prefix + lax.shift_left(jnp.int32(1), 31 - it)
            cnt = count(lambda kt, _: kt >= cand)
            take = cnt >= kf
            return jnp.where(take, cand, prefix), jnp.where(take, cnt, nge)

        prefix, nge = lax.fori_loop(
            0, 32, bit_body, (jnp.full((tq, 1), INT_MIN, I32), jnp.zeros((tq, 1), F32)))
        found = prefix > INT_MIN
        thr = jnp.maximum(prefix, INT_MIN + 1)
        ngt = count(lambda kt, _: kt > thr)
        need = kf - ngt
        t_ref[0] = thr
        c_ref[0] = jnp.full((tq, 1), 2 ** 30, I32)
        straddle = jnp.max(jnp.where(found & (nge > kf), 1.0, 0.0))

        @pl.when(straddle > 0.0)
        def _():
            def tie_body(it, x):
                cx = x + lax.shift_left(jnp.int32(1), idx_bits - 1 - it)
                g = count(lambda kt, kpos: jnp.where(kt == thr, kpos, INT_MAX) < cx)
                return jnp.where(g < need, cx, x)
            c_ref[0] = lax.fori_loop(0, idx_bits, tie_body, jnp.zeros((tq, 1), I32))


def _dsa_index(qi, ki2, kw, *, tq, tk, qoff, length, topk):
    b, s, _ = qi.shape
    lp = ki2.shape[1]
    nq, nk = s // tq, lp // tk
    last = lambda i: _last_tile(i, tq, tk, qoff, length, True)
    idx_bits = max(1, int(np.ceil(np.log2(lp))))
    return pl.pallas_call(
        functools.partial(_dsa_index_kernel, tq=tq, tk=tk, qoff=qoff, length=length, topk=topk,
                          idx_bits=idx_bits),
        grid=(b, nq, nk),
        in_specs=[pl.BlockSpec((1, tq, 512), lambda bb, i, j: (bb, i, 0)),
                  pl.BlockSpec((1, tk, LANES), lambda bb, i, j: (bb, jnp.minimum(j, last(i)), 0)),
                  pl.BlockSpec((1, tq, LANES), lambda bb, i, j: (bb, i, 0))],
        out_specs=[pl.BlockSpec((1, tq, tk), lambda bb, i, j: (bb, i, jnp.minimum(j, last(i)))),
                   pl.BlockSpec((1, tq, 1), lambda bb, i, j: (bb, i, 0)),
                   pl.BlockSpec((1, tq, 1), lambda bb, i, j: (bb, i, 0))],
        out_shape=[jax.ShapeDtypeStruct((b, s, lp), I32),
                   jax.ShapeDtypeStruct((b, s, 1), I32),
                   jax.ShapeDtypeStruct((b, s, 1), I32)],
        scratch_shapes=[pltpu.VMEM((IDX_HEADS * tq, LANES), CDT),
                        pltpu.VMEM((nk, tq, tk), I32)],
        compiler_params=_params(("parallel", "parallel", "arbitrary")),
        name="dsa_index",
    )(qi, ki2, kw)


def _attn_init(q_ref, qst_ref, m_ref, l_ref, acc_ref, tq):
    q = q_ref[0]
    for g in range(q.shape[1] // LANES):
        qst_ref[g] = _pair_stack(q[:, g * LANES:(g + 1) * LANES], tq)
    m_ref[...] = jnp.full(m_ref.shape, NEG, F32)
    l_ref[...] = jnp.zeros(l_ref.shape, F32)
    acc_ref[...] = jnp.zeros(acc_ref.shape, F32)


def _attn_finish(o_ref, l_ref, acc_ref, tq):
    for g in range(acc_ref.shape[0]):
        o_ref[0, :, g * LANES:(g + 1) * LANES] = _pair_merge(acc_ref[g] / l_ref[g], tq).astype(o_ref.dtype)


def _dsa_attn_kernel(q_ref, k_ref, v_ref, keys_ref, t_ref, c_ref, o_ref, qst_ref, m_ref, l_ref, acc_ref,
                     *, tq, tk, qoff, length):
    i = pl.program_id(1)
    j = pl.program_id(2)
    last = _last_tile(i, tq, tk, qoff, length, True)

    @pl.when(j == 0)
    def _():
        _attn_init(q_ref, qst_ref, m_ref, l_ref, acc_ref, tq)

    @pl.when(j <= last)
    def _():
        kt = keys_ref[0]
        thr = t_ref[0]
        kpos = j * tk + lax.broadcasted_iota(I32, (tq, tk), 1)
        sel = (kt > thr) | (jnp.where(kt == thr, kpos, INT_MAX) <= c_ref[0])
        mb = jnp.where(sel, 0.0, NEG)
        mb2 = jnp.concatenate([mb, mb], axis=0)
        for g in range(qst_ref.shape[0]):
            sl = slice(g * LANES, (g + 1) * LANES)
            s = _mm_nt(qst_ref[g], k_ref[0, :, sl]) + mb2
            _online_softmax_step(s, v_ref[0, :, sl], m_ref, l_ref, acc_ref, g)

    @pl.when(j == last)
    def _():
        _attn_finish(o_ref, l_ref, acc_ref, tq)


def _dsa_attn(q, k, v, keys, thr, cut, *, tq, tk, qoff, length):
    b, s, w = q.shape
    lp = k.shape[1]
    last = lambda i: _last_tile(i, tq, tk, qoff, length, True)
    kmap = lambda bb, i, j: (bb, jnp.minimum(j, last(i)), 0)
    qmap = lambda bb, i, j: (bb, i, 0)
    npair = w // LANES
    return pl.pallas_call(
        functools.partial(_dsa_attn_kernel, tq=tq, tk=tk, qoff=qoff, length=length),
        grid=(b, s // tq, lp // tk),
        in_specs=[pl.BlockSpec((1, tq, w), qmap),
                  pl.BlockSpec((1, tk, w), kmap),
                  pl.BlockSpec((1, tk, w), kmap),
                  pl.BlockSpec((1, tq, tk), lambda bb, i, j: (bb, i, jnp.minimum(j, last(i)))),
                  pl.BlockSpec((1, tq, 1), qmap),
                  pl.BlockSpec((1, tq, 1), qmap)],
        out_specs=pl.BlockSpec((1, tq, w), qmap),
        out_shape=jax.ShapeDtypeStruct((b, s, w), CDT),
        scratch_shapes=[pltpu.VMEM((npair, 2 * tq, LANES), CDT),
                        pltpu.VMEM((npair, 2 * tq, 1), F32),
                        pltpu.VMEM((npair, 2 * tq, 1), F32),
                        pltpu.VMEM((npair, 2 * tq, LANES), F32)],
        compiler_params=_params(("parallel", "parallel", "arbitrary")),
        name="dsa_attn",
    )(q, k, v, keys, thr, cut)


def _cumsum_kernel(x_ref, tri_ref, o_ref, carry_ref):
    @pl.when(pl.program_id(1) == 0)
    def _():
        carry_ref[...] = jnp.zeros(carry_ref.shape, F32)
    tri = tri_ref[...]
    tot = None
    for p in _pieces(x_ref[0], 3):
        t = _mm(tri, p)
        tot = t if tot is None else tot + t
    acc = carry_ref[...] + tot
    o_ref[0] = acc
    carry_ref[...] = acc[-1:, :]


def _cumsum(x, tb):
    b, s, w = x.shape
    tri = jnp.tril(jnp.ones((tb, tb), F32)).astype(CDT)
    spec = pl.BlockSpec((1, tb, w), lambda bb, i: (bb, i, 0))
    return pl.pallas_call(
        _cumsum_kernel,
        grid=(b, s // tb),
        in_specs=[spec, _const_spec(tri.shape)],
        out_specs=spec,
        out_shape=jax.ShapeDtypeStruct(x.shape, F32),
        scratch_shapes=[pltpu.VMEM((1, w), F32)],
        compiler_params=_params(("parallel", "arbitrary")),
        name="logf_cumsum",
    )(x, tri)


def _fox_kernel(q_ref, k_ref, v_ref, ft_ref, f0_ref, o_ref, qst_ref, m_ref, l_ref, acc_ref,
                *, tq, tk, qoff, length):
    i = pl.program_id(1)
    j = pl.program_id(2)
    last = _last_tile(i, tq, tk, qoff, length, False)
    q0 = qoff + i * tq

    @pl.when(j == 0)
    def _():
        _attn_init(q_ref, qst_ref, m_ref, l_ref, acc_ref, tq)

    def step(masked):
        bias = f0_ref[0, 0] - ft_ref[0]
        if masked:
            kpos = j * tk + lax.broadcasted_iota(I32, (tq, tk), 1)
            qpos = q0 + lax.broadcasted_iota(I32, (tq, tk), 0)
            causal = kpos <= qpos
        for g in range(qst_ref.shape[0]):
            sl = slice(g * LANES, (g + 1) * LANES)
            s = _mm_nt(qst_ref[g], k_ref[0, :, sl])
            halves = []
            for e in range(2):
                se = s[e * tq:(e + 1) * tq] + bias[2 * g + e:2 * g + e + 1, :]
                halves.append(jnp.where(causal, se, NEG) if masked else se)
            _online_softmax_step(jnp.concatenate(halves, axis=0), v_ref[0, :, sl], m_ref, l_ref, acc_ref, g)

    diag = (j + 1) * tk - 1 > q0

    @pl.when((j <= last) & diag)
    def _():
        step(True)

    @pl.when((j <= last) & jnp.logical_not(diag))
    def _():
        step(False)

    @pl.when(j == last)
    def _():
        _attn_finish(o_ref, l_ref, acc_ref, tq)


def _fox(q, k, v, ft, f0, *, tq, tk, qoff, length):
    b, s, w = q.shape
    lp = k.shape[1]
    nh = ft.shape[1]
    last = lambda i: _last_tile(i, tq, tk, qoff, length, False)
    kmap = lambda bb, i, j: (bb, jnp.minimum(j, last(i)), 0)
    qmap = lambda bb, i, j: (bb, i, 0)
    npair = w // LANES
    return pl.pallas_call(
        functools.partial(_fox_kernel, tq=tq, tk=tk, qoff=qoff, length=length),
        grid=(b, s // tq, lp // tk),
        in_specs=[pl.BlockSpec((1, tq, w), qmap),
                  pl.BlockSpec((1, tk, w), kmap),
                  pl.BlockSpec((1, tk, w), kmap),
                  pl.BlockSpec((1, nh, tk), lambda bb, i, j: (bb, 0, jnp.minimum(j, last(i)))),
                  pl.BlockSpec((1, 1, nh, 1), lambda bb, i, j: (bb, i, 0, 0))],
        out_specs=pl.BlockSpec((1, tq, w), qmap),
        out_shape=jax.ShapeDtypeStruct((b, s, w), CDT),
        scratch_shapes=[pltpu.VMEM((npair, 2 * tq, LANES), CDT),
                        pltpu.VMEM((npair, 2 * tq, 1), F32),
                        pltpu.VMEM((npair, 2 * tq, 1), F32),
                        pltpu.VMEM((npair, 2 * tq, LANES), F32)],
        compiler_params=_params(("parallel", "parallel", "arbitrary")),
        name="fox_attn",
    )(q, k, v, ft, f0)


def _rope_tables(pos):
    half = HEAD_DIM // 2
    inv = ROPE_THETA ** (-jnp.arange(half, dtype=F32) / half)
    ang = pos.astype(F32)[:, None] * inv[None, :]
    cos, sin = jnp.cos(ang), jnp.sin(ang)
    return (jnp.tile(jnp.concatenate([cos, cos], axis=1), (1, 2)),
            jnp.tile(jnp.concatenate([-sin, sin], axis=1), (1, 2)))


def _pad_rows(x, n):
    return x if x.shape[1] == n else jnp.pad(x, ((0, 0), (0, n - x.shape[1]), (0, 0)))


def _tiles(s):
    tq = 256 if s % 256 == 0 else s
    tk = 512 if s % 512 == 0 else s
    return tq, tk


def kernel(x_prompt, x_sample, cache_a_k, cache_a_v, cache_b_k, cache_b_v, cache_b_ik, cache_c_k, cache_c_v,
           cache_c_logf, p_prompt, p_sample, g_mix, w_in_even, qn_a, kn_a, rel_bias_a, qn_b, kn_b, w_out_even,
           w_in_odd, b_f, qn_c, kn_c, w_out_odd, g_mlp, w_up, w_down, g_ple, w_ple_gate, w_ple_proj):
    B, S, D = x_prompt.shape
    DB, n, _ = x_sample.shape
    P = cache_b_k.shape[2]
    W_A = cache_a_k.shape[2]
    W_AP = min(A_PAST, S)
    depth = g_mix.shape[0]
    L = P + n
    Lp = -(-L // LANES) * LANES
    assert B == 1 and S % 128 == 0 and n % 16 == 0 and P % 16 == 0

    row = lambda v: v.astype(F32).reshape(1, -1)
    bd = jnp.asarray(np.kron(np.eye(256 // HEAD_DIM), np.full((HEAD_DIM, HEAD_DIM), 1.0 / HEAD_DIM)), CDT)
    cos_p, sin_p = _rope_tables(jnp.arange(S))
    cos_s, sin_s = (jnp.tile(t, (DB, 1)) for t in _rope_tables(P + jnp.arange(n)))
    tq, tk = _tiles(S)

    outs = {name: [] for name in ("a_k_p", "a_v_p", "b_k_p", "b_v_p", "b_ik_p", "c_k_p", "c_v_p", "c_lf_p",
                                  "a_k_s", "a_v_s", "b_k_s", "b_v_s", "b_ik_s", "c_k_s", "c_v_s", "c_lf_s")}
    h_p = x_prompt.reshape(B * S, D)
    h_s = x_sample.reshape(DB * n, D)
    for i in range(depth):
        g = row(g_mix[i])
        if i % 2 == 0:
            e = i // 2
            wsz = 512
            w = w_in_even[e]
            w_all = jnp.concatenate([w[:, :7 * wsz], jnp.pad(w[:, 7 * wsz:], ((0, 0), (0, LANES - 72)))],
                                    axis=1).astype(CDT)
            gn = jnp.stack([jnp.tile(v[e].astype(F32), wsz // HEAD_DIM) for v in (qn_a, kn_a, qn_b, kn_b)])
            (ka, va, kb, vb, kw, qa_c, ka_c, va_c, qb_c, kb_c, vb_c, qi_c, ki2_c) = _proj_even(
                h_p, g, w_all, gn, bd, cos_p, sin_p)
            r3 = lambda t: t.reshape(B, S, t.shape[-1])
            tqa = 128
            npc = (A_PAST + tqa) // tqa
            kpad = jnp.pad(r3(ka_c), ((0, 0), (A_PAST, 0), (0, 0)))
            vpad = jnp.pad(r3(va_c), ((0, 0), (A_PAST, 0), (0, 0)))
            k_specs = [pl.BlockSpec((1, tqa, wsz), functools.partial(lambda bb, ii, pp: (bb, ii + pp, 0), pp=pp))
                       for pp in range(npc)]
            bias_p = _band_bias(rel_bias_a[e], np.arange(tqa), np.arange(-A_PAST, tqa))
            oa = _band(r3(qa_c), [kpad] * npc, [vpad] * npc, k_specs, bias_p, tqa, True)
            topk_p = min(TOPK_MAX, S // 4)
            keys, thr, cut = _dsa_index(r3(qi_c), r3(ki2_c), r3(kw), tq=128, tk=tk, qoff=0, length=S, topk=topk_p)
            ob = _dsa_attn(r3(qb_c), r3(kb_c), r3(vb_c), keys, thr, cut, tq=tq, tk=tk, qoff=0, length=S)
            os_p = [oa.reshape(B * S, wsz), ob.reshape(B * S, wsz)]
            hd = lambda t: t.reshape(B, S, -1, HEAD_DIM)
            outs["a_k_p"].append(hd(ka)[:, S - W_AP:]); outs["a_v_p"].append(hd(va)[:, S - W_AP:])
            outs["b_k_p"].append(hd(kb)); outs["b_v_p"].append(hd(vb))
            outs["b_ik_p"].append(kw[:, :64].reshape(B, S, 64))
            (ka, va, kb, vb, kw, qa_c, ka_c, va_c, qb_c, kb_c, vb_c, qi_c, ki2_c) = _proj_even(
                h_s, g, w_all, gn, bd, cos_s, sin_s)
            r3 = lambda t: t.reshape(DB, n, t.shape[-1])
            ca_k = cache_a_k[e].reshape(DB, W_A, wsz)
            ca_v = cache_a_v[e].reshape(DB, W_A, wsz)
            full = lambda rows, wd: pl.BlockSpec((1, rows, wd), lambda bb, ii: (bb, 0, 0))
            bias_s = _band_bias(rel_bias_a[e], P + np.arange(n), P - W_A + np.arange(W_A + n))
            bias_s = jnp.pad(bias_s, ((0, 0), (0, 0), (0, -(W_A + n) % LANES)), constant_values=NEG)
            oa = _band(r3(qa_c), [ca_k, r3(ka)], [ca_v, r3(va)], [full(W_A, wsz), full(n, wsz)], bias_s, n, False)
            cat = lambda c, new: _pad_rows(jnp.concatenate([c.astype(CDT), new], axis=1), Lp)
            ik = cache_b_ik[e].astype(CDT)
            ki2_all = cat(jnp.concatenate([ik, ik], axis=-1), r3(ki2_c))
            kb_all = cat(cache_b_k[e].reshape(DB, P, wsz), r3(kb_c))
            vb_all = cat(cache_b_v[e].reshape(DB, P, wsz), r3(vb_c))
            topk_s = min(TOPK_MAX, L // 4)
            keys, thr, cut = _dsa_index(r3(qi_c), ki2_all, r3(kw), tq=n, tk=Lp, qoff=P, length=L, topk=topk_s)
            ob = _dsa_attn(r3(qb_c), kb_all, vb_all, keys, thr, cut, tq=n, tk=Lp, qoff=P, length=L)
            os_s = [oa.reshape(DB * n, wsz), ob.reshape(DB * n, wsz)]
            hd = lambda t: t.reshape(DB, n, -1, HEAD_DIM)
            outs["a_k_s"].append(jnp.concatenate([cache_a_k[e], hd(ka)], axis=1)[:, n:])
            outs["a_v_s"].append(jnp.concatenate([cache_a_v[e], hd(va)], axis=1)[:, n:])
            outs["b_k_s"].append(hd(kb)); outs["b_v_s"].append(hd(vb))
            outs["b_ik_s"].append(kw[:, :64].reshape(DB, n, 64))
            w_out = w_out_even[e].astype(CDT)
            wos = [w_out[:wsz], w_out[wsz:]]
        else:
            o = i // 2
            nh = b_f.shape[1]
            wsz = nh * HEAD_DIM
            w = w_in_odd[o]
            w_all = jnp.concatenate([w[:, :3 * wsz], jnp.pad(w[:, 3 * wsz:], ((0, 0), (0, LANES - nh)))],
                                    axis=1).astype(CDT)
            gn = jnp.stack([jnp.tile(v[o].astype(F32), wsz // HEAD_DIM) for v in (qn_c, kn_c)])
            bf = jnp.pad(b_f[o].astype(F32), (0, LANES - nh)).reshape(1, LANES)
            k, v, lf, q_c, k_c, v_c = _proj_odd(h_p, g, w_all, gn, bf, bd)
            r3 = lambda t: t.reshape(B, S, t.shape[-1])
            F = _cumsum(r3(lf), 512 if S % 512 == 0 else S)[..., :nh]
            ft = jnp.swapaxes(F, 1, 2)
            f0 = F[:, ::tq].reshape(B, S // tq, nh, 1)
            oc = _fox(r3(q_c), r3(k_c), r3(v_c), ft, f0, tq=tq, tk=tk, qoff=0, length=S)
            os_p = [oc.reshape(B * S, wsz)]
            hd = lambda t: t.reshape(B, S, nh, HEAD_DIM)
            outs["c_k_p"].append(hd(k)); outs["c_v_p"].append(hd(v)); outs["c_lf_p"].append(lf[:, :nh].reshape(B, S, nh))
            k, v, lf, q_c, k_c, v_c = _proj_odd(h_s, g, w_all, gn, bf, bd)
            r3 = lambda t: t.reshape(DB, n, t.shape[-1])
            lf_all = jnp.concatenate([jnp.pad(cache_c_logf[o].astype(F32), ((0, 0), (0, 0), (0, LANES - nh))), r3(lf)],
                                     axis=1)
            F = _cumsum(lf_all, L)[..., :nh]
            ft = jnp.swapaxes(_pad_rows(F, Lp), 1, 2)
            f0 = F[:, P:P + 1].reshape(DB, 1, nh, 1)
            cat = lambda c, new: _pad_rows(jnp.concatenate([c.astype(CDT), new], axis=1), Lp)
            k_all = cat(cache_c_k[o].reshape(DB, P, wsz), r3(k_c))
            v_all = cat(cache_c_v[o].reshape(DB, P, wsz), r3(v_c))
            oc = _fox(r3(q_c), k_all, v_all, ft, f0, tq=n, tk=Lp, qoff=P, length=L)
            os_s = [oc.reshape(DB * n, wsz)]
            hd = lambda t: t.reshape(DB, n, nh, HEAD_DIM)
            outs["c_k_s"].append(hd(k)); outs["c_v_s"].append(hd(v)); outs["c_lf_s"].append(lf[:, :nh].reshape(DB, n, nh))
            w_out = w_out_odd[o].astype(CDT)
            wos = [w_out]
        post_w = (row(g_mlp[i]), w_up[i].astype(CDT), w_down[i].astype(CDT), row(g_ple[i]),
                  w_ple_gate[i].astype(CDT), w_ple_proj[i].astype(CDT))
        h_p = _post(h_p, os_p, p_prompt[i].reshape(B * S, -1), wos, *post_w)
        h_s = _post(h_s, os_s, p_sample[i].reshape(DB * n, -1), wos, *post_w)

    st = jnp.stack
    names = ("a_k_p", "a_v_p", "b_k_p", "b_v_p", "b_ik_p", "c_k_p", "c_v_p", "c_lf_p",
             "a_k_s", "a_v_s", "b_k_s", "b_v_s", "b_ik_s", "c_k_s", "c_v_s", "c_lf_s")
    return (h_p.reshape(B, S, D), h_s.reshape(DB, n, D)) + tuple(st(outs[nm]) for nm in names)
```

```python
import functools

import numpy as np
import jax
import jax.numpy as jnp
from jax import lax
from jax.experimental import pallas as pl
from jax.experimental.pallas import tpu as pltpu

F32 = jnp.float32
I32 = jnp.int32
CDT = jnp.bfloat16

CHUNK = 64
HEAD_DIM = 64
IDX_HEADS = 8
A_PAST = 8 * CHUNK
REL_CLIP = 128
TOPK_MAX = 256
ROPE_THETA = 10000.0
RMS_EPS = 1e-6
LANES = 128
NEG = -1e30
INT_MIN = -(2 ** 31)
INT_MAX = 2 ** 31 - 1
LOG2E = 1.4426950408889634
VMEM_LIMIT = 56 * 1024 * 1024


def _mm(a, b):
    return jnp.dot(a, b, preferred_element_type=F32)


def _mm_nt(a, b):
    return lax.dot_general(a, b, (((1,), (1,)), ((), ())), preferred_element_type=F32)


def _pieces(x, n):
    if CDT == F32:
        return [x]
    out = []
    for _ in range(n - 1):
        p = x.astype(CDT)
        out.append(p)
        x = x - p.astype(F32)
    out.append(x.astype(CDT))
    return out


def _rms(x, g):
    return x * lax.rsqrt(jnp.mean(x * x, axis=-1, keepdims=True) + RMS_EPS) * g


def _head_rms(x, bd, g):
    x2 = x * x
    pcs = _pieces(x2, 2)
    cols = []
    for s in range(x.shape[1] // 256):
        sl = slice(s * 256, (s + 1) * 256)
        ms = _mm(pcs[0][:, sl], bd)
        for p in pcs[1:]:
            ms = ms + _mm(p[:, sl], bd)
        cols.append(ms)
    ms = cols[0] if len(cols) == 1 else jnp.concatenate(cols, axis=1)
    return x * lax.rsqrt(ms + RMS_EPS) * g


def _rope(x, cos, sin):
    w = x.shape[1]
    lane = lax.broadcasted_iota(I32, x.shape, 1)
    first = (lane & 63) < 32
    swapped = jnp.where(first, pltpu.roll(x, w - 32, 1), pltpu.roll(x, 32, 1))
    return x * cos + swapped * sin


def _tile_lanes(t, w):
    return t if w == t.shape[1] else jnp.concatenate([t] * (w // t.shape[1]), axis=1)


def _proj_even_kernel(h_ref, g_ref, w_ref, gn_ref, bd_ref, cos_ref, sin_ref,
                      ka_ref, va_ref, kb_ref, vb_ref, kw_ref,
                      qa_c, ka_c, va_c, qb_c, kb_c, vb_c, qi_c, ki2_c):
    a = _rms(h_ref[...], g_ref[...]).astype(CDT)
    z = _mm(a, w_ref[...])
    w = 512
    bd = bd_ref[...]
    gn = gn_ref[...]
    cos1, sin1 = cos_ref[...], sin_ref[...]
    cos, sin = _tile_lanes(cos1, w), _tile_lanes(sin1, w)
    scale = HEAD_DIM ** -0.5

    qa = _head_rms(z[:, 0:w], bd, gn[0:1])
    qa_c[...] = (qa * scale).astype(CDT)
    ka = _head_rms(z[:, w:2 * w], bd, gn[1:2])
    ka_ref[...] = ka
    ka_c[...] = ka.astype(CDT)
    va = z[:, 2 * w:3 * w]
    va_ref[...] = va
    va_c[...] = va.astype(CDT)
    qb = _rope(_head_rms(z[:, 3 * w:4 * w], bd, gn[2:3]), cos, sin)
    qb_c[...] = (qb * (scale * LOG2E)).astype(CDT)
    kb = _rope(_head_rms(z[:, 4 * w:5 * w], bd, gn[3:4]), cos, sin)
    kb_ref[...] = kb
    kb_c[...] = kb.astype(CDT)
    vb = z[:, 5 * w:6 * w]
    vb_ref[...] = vb
    vb_c[...] = vb.astype(CDT)
    qi = _rope(z[:, 6 * w:7 * w], cos, sin)
    qi_c[...] = (qi * scale).astype(CDT)
    kw = z[:, 7 * w:7 * w + LANES]
    kwr = _rope(kw, cos1, sin1)
    lane = lax.broadcasted_iota(I32, kw.shape, 1)
    kw_ref[...] = jnp.where(lane < 64, kwr, kw)
    ki2_c[...] = jnp.where(lane < 64, kwr, pltpu.roll(kwr, 64, 1)).astype(CDT)


def _proj_odd_kernel(h_ref, g_ref, w_ref, gn_ref, bf_ref, bd_ref,
                     k_ref, v_ref, lf_ref, q_c, k_c, v_c):
    a = _rms(h_ref[...], g_ref[...]).astype(CDT)
    z = _mm(a, w_ref[...])
    w = 1024
    bd = bd_ref[...]
    gn = gn_ref[...]
    q = _head_rms(z[:, 0:w], bd, gn[0:1])
    q_c[...] = (q * (HEAD_DIM ** -0.5 * LOG2E)).astype(CDT)
    k = _head_rms(z[:, w:2 * w], bd, gn[1:2])
    k_ref[...] = k
    k_c[...] = k.astype(CDT)
    v = z[:, 2 * w:3 * w]
    v_ref[...] = v
    v_c[...] = v.astype(CDT)
    fl = z[:, 3 * w:3 * w + LANES] + bf_ref[...]
    lf_ref[...] = jnp.minimum(fl, 0.0) - jnp.log1p(jnp.exp(-jnp.abs(fl)))


def _const_spec(shape):
    nd = len(shape)
    return pl.BlockSpec(shape, lambda *_: (0,) * nd, pipeline_mode=pl.Buffered(1))


def _row_spec(tm, w):
    return pl.BlockSpec((tm, w), lambda i: (i, 0))


def _params(sem):
    return pltpu.CompilerParams(dimension_semantics=sem, vmem_limit_bytes=VMEM_LIMIT)


def _row_tile(rows, pref):
    return pref if rows % pref == 0 else rows


def _proj_even(h, g, w_all, gn, bd, cos, sin):
    rows, d = h.shape
    tm = _row_tile(rows, 256)
    f = lambda w, dt: jax.ShapeDtypeStruct((rows, w), dt)
    out_shape = [f(512, F32)] * 4 + [f(LANES, F32)] + [f(512, CDT)] * 7 + [f(LANES, CDT)]
    out_specs = [_row_spec(tm, 512)] * 4 + [_row_spec(tm, LANES)] + [_row_spec(tm, 512)] * 7 + [_row_spec(tm, LANES)]
    return pl.pallas_call(
        _proj_even_kernel,
        grid=(rows // tm,),
        in_specs=[_row_spec(tm, d), _const_spec(g.shape), _const_spec(w_all.shape), _const_spec(gn.shape),
                  _const_spec(bd.shape), _row_spec(tm, LANES), _row_spec(tm, LANES)],
        out_specs=out_specs,
        out_shape=out_shape,
        compiler_params=_params(("parallel",)),
        name="proj_even",
    )(h, g, w_all, gn, bd, cos, sin)


def _proj_odd(h, g, w_all, gn, bf, bd):
    rows, d = h.shape
    tm = _row_tile(rows, 256)
    f = lambda w, dt: jax.ShapeDtypeStruct((rows, w), dt)
    out_shape = [f(1024, F32)] * 2 + [f(LANES, F32)] + [f(1024, CDT)] * 3
    out_specs = [_row_spec(tm, 1024)] * 2 + [_row_spec(tm, LANES)] + [_row_spec(tm, 1024)] * 3
    return pl.pallas_call(
        _proj_odd_kernel,
        grid=(rows // tm,),
        in_specs=[_row_spec(tm, d), _const_spec(g.shape), _const_spec(w_all.shape), _const_spec(gn.shape),
                  _const_spec(bf.shape), _const_spec(bd.shape)],
        out_specs=out_specs,
        out_shape=out_shape,
        compiler_params=_params(("parallel",)),
        name="proj_odd",
    )(h, g, w_all, gn, bf, bd)


def _post_kernel(*refs, n_o, ff_chunk):
    h_ref = refs[0]
    o_refs = refs[1:1 + n_o]
    p_ref = refs[1 + n_o]
    wo_refs = refs[2 + n_o:2 + 2 * n_o]
    gm_ref, wu_ref, wd_ref, gp_ref, wg_ref, wp_ref, out_ref = refs[2 + 2 * n_o:]
    mix = None
    for o_ref, wo_ref in zip(o_refs, wo_refs):
        t = _mm(o_ref[...], wo_ref[...])
        mix = t if mix is None else mix + t
    h = h_ref[...] + mix
    m = _rms(h, gm_ref[...]).astype(CDT)
    mlp = None
    d_ff = wu_ref.shape[1]
    for c in range(d_ff // ff_chunk):
        sl = slice(c * ff_chunk, (c + 1) * ff_chunk)
        u = jnp.square(jnp.maximum(_mm(m, wu_ref[:, sl]), 0.0)).astype(CDT)
        t = _mm(u, wd_ref[sl, :])
        mlp = t if mlp is None else mlp + t
    acc = h + mlp
    gate_in = _rms(acc, gp_ref[...]).astype(CDT)
    gate = 1.0 / (1.0 + jnp.exp(-_mm(gate_in, wg_ref[...])))
    out_ref[...] = acc + gate * _mm(p_ref[...].astype(CDT), wp_ref[...])


def _post(h, os_, p, wos, gm, wu, wd, gp, wg, wp):
    rows, d = h.shape
    tm = _row_tile(rows, 512)
    n_o = len(os_)
    in_specs = ([_row_spec(tm, d)] + [_row_spec(tm, o.shape[1]) for o in os_] + [_row_spec(tm, p.shape[1])]
                + [_const_spec(w.shape) for w in wos]
                + [_const_spec(x.shape) for x in (gm, wu, wd, gp, wg, wp)])
    return pl.pallas_call(
        functools.partial(_post_kernel, n_o=n_o, ff_chunk=1024),
        grid=(rows // tm,),
        in_specs=in_specs,
        out_specs=_row_spec(tm, d),
        out_shape=jax.ShapeDtypeStruct((rows, d), F32),
        compiler_params=_params(("parallel",)),
        name="post",
    )(h, *os_, p, *wos, gm, wu, wd, gp, wg, wp)


def _pair_stack(qp, tq):
    lane = lax.broadcasted_iota(I32, qp.shape, 1)
    zero = jnp.zeros_like(qp)
    return jnp.concatenate([jnp.where(lane < 64, qp, zero), jnp.where(lane >= 64, qp, zero)], axis=0)


def _pair_merge(o2, tq):
    lane = lax.broadcasted_iota(I32, (tq, LANES), 1)
    return jnp.where(lane < 64, o2[:tq], o2[tq:])


def _last_tile(i, tq, tk, qoff, length, chunked):
    qend = qoff + i * tq + tq - 1
    kmax = (qend // CHUNK + 1) * CHUNK if chunked else qend + 1
    kmax = jnp.minimum(kmax, length)
    return (kmax - 1) // tk


def _flash_init(q_ref, aug_ref, qst_ref, m_ref, l_ref, acc_ref, tq):
    q = q_ref[0]
    for g in range(qst_ref.shape[0]):
        st = _pair_stack(q[:, g * LANES:(g + 1) * LANES], tq)
        if aug_ref is not None:
            a = aug_ref[0, 0]
            ext = jnp.concatenate([jnp.broadcast_to(a[2 * g:2 * g + 1], (tq, LANES)),
                                   jnp.broadcast_to(a[2 * g + 1:2 * g + 2], (tq, LANES))], axis=0)
            st = jnp.concatenate([st, ext.astype(CDT)], axis=1)
        qst_ref[g] = st
    m_ref[...] = jnp.full(m_ref.shape, NEG, F32)
    l_ref[...] = jnp.zeros(l_ref.shape, F32)
    acc_ref[...] = jnp.zeros(acc_ref.shape, F32)


def _flash_step(st, vt_ref, m_ref, l_ref, acc_ref, g, tq):
    m_old = m_ref[g]
    m_new = jnp.maximum(m_old, jnp.max(st, axis=0, keepdims=True))
    alpha = jnp.exp2(m_old - m_new)
    p = jnp.exp2(st - m_new)
    l_ref[g] = alpha * l_ref[g] + jnp.sum(p, axis=0, keepdims=True)
    m_ref[g] = m_new
    pc = p.astype(CDT)
    for e in range(2):
        rows = slice(g * LANES + e * HEAD_DIM, g * LANES + (e + 1) * HEAD_DIM)
        cols = slice(e * tq, (e + 1) * tq)
        acc_ref[rows, :] = alpha[:, cols] * acc_ref[rows, :] + _mm(vt_ref[0, rows, :], pc[:, cols])


def _flash_finish(o_ref, l_ref, acc_ref, tq):
    for g in range(l_ref.shape[0]):
        l = l_ref[g]
        den = jnp.concatenate([jnp.broadcast_to(l[:, :tq], (HEAD_DIM, tq)),
                               jnp.broadcast_to(l[:, tq:], (HEAD_DIM, tq))], axis=0)
        o = acc_ref[g * LANES:(g + 1) * LANES, :] / den
        o_ref[0, :, g * LANES:(g + 1) * LANES] = o.T.astype(o_ref.dtype)


def _flash_scratch(npair, tq, kd, w):
    return [pltpu.VMEM((npair, 2 * tq, kd), CDT),
            pltpu.VMEM((npair, 1, 2 * tq), F32),
            pltpu.VMEM((npair, 1, 2 * tq), F32),
            pltpu.VMEM((w, tq), F32)]


def _band_kernel(*refs, n_kv, tq, mask_neg):
    q_ref = refs[0]
    k_refs = refs[1:1 + n_kv]
    v_refs = refs[1 + n_kv:1 + 2 * n_kv]
    bias_ref = refs[1 + 2 * n_kv]
    o_ref = refs[2 + 2 * n_kv]
    i = pl.program_id(1)
    q = q_ref[0]
    cat = lambda rs: (rs[0][0].astype(CDT) if len(rs) == 1
                      else jnp.concatenate([r[0].astype(CDT) for r in rs], axis=0))
    k = cat(k_refs)
    v = cat(v_refs)
    tk = bias_ref.shape[2]
    if k.shape[0] < tk:
        zpad = jnp.zeros((tk - k.shape[0], k.shape[1]), CDT)
        k = jnp.concatenate([k, zpad], axis=0)
        v = jnp.concatenate([v, zpad], axis=0)
    if mask_neg:
        kpos = lax.broadcasted_iota(I32, (2 * tq, tk), 1) + (i * tq - A_PAST)
        neg = kpos < 0
    for g in range(q.shape[1] // LANES):
        sl = slice(g * LANES, (g + 1) * LANES)
        s = _mm_nt(_pair_stack(q[:, sl], tq), k[:, sl])
        s = s + jnp.concatenate([bias_ref[2 * g], bias_ref[2 * g + 1]], axis=0)
        if mask_neg:
            s = jnp.where(neg, NEG, s)
        m = jnp.max(s, axis=-1, keepdims=True)
        p = jnp.exp(s - m)
        l = jnp.sum(p, axis=-1, keepdims=True)
        o2 = _mm(p.astype(CDT), v[:, sl]) / l
        o_ref[0, :, sl] = _pair_merge(o2, tq).astype(o_ref.dtype)


def _band(q, k_parts, v_parts, k_specs, bias, tq, mask_neg):
    b, s, w = q.shape
    n_kv = len(k_parts)
    q_spec = pl.BlockSpec((1, tq, w), lambda bb, i: (bb, i, 0))
    return pl.pallas_call(
        functools.partial(_band_kernel, n_kv=n_kv, tq=tq, mask_neg=mask_neg),
        grid=(b, s // tq),
        in_specs=[q_spec] + k_specs + k_specs + [_const_spec(bias.shape)],
        out_specs=q_spec,
        out_shape=jax.ShapeDtypeStruct((b, s, w), CDT),
        compiler_params=_params(("parallel", "parallel")),
        name="band_attn",
    )(q, *k_parts, *v_parts, bias)


def _band_bias(rel_bias, q_pos, k_pos):
    rel = np.clip(q_pos[:, None] - k_pos[None, :], -REL_CLIP, REL_CLIP) + REL_CLIP
    qc = q_pos[:, None] // CHUNK
    kc = k_pos[None, :] // CHUNK
    ok = (kc <= qc) & (kc >= qc - A_PAST // CHUNK)
    bias = jnp.take(rel_bias.astype(F32), jnp.asarray(rel.reshape(-1)), axis=1).reshape(
        rel_bias.shape[0], *rel.shape)
    return jnp.where(jnp.asarray(ok)[None], bias, NEG)


def _dsa_index_kernel(qi_ref, ki2_ref, wt_ref, keys_ref, t_ref, c_ref, qst_ref, kscr_ref,
                      *, tq, tk, qoff, length, topk, idx_bits):
    i = pl.program_id(1)
    j = pl.program_id(2)
    last = _last_tile(i, tq, tk, qoff, length, True)
    q0 = qoff + i * tq

    @pl.when(j == 0)
    def _():
        q = qi_ref[0]
        for g in range(IDX_HEADS // 2):
            qst_ref[2 * g * tq:(2 * g + 2) * tq, :] = _pair_stack(q[:, g * LANES:(g + 1) * LANES], tq)

    @pl.when(j <= last)
    def _():
        lg = _mm_nt(ki2_ref[0], qst_ref[...])
        wgt = wt_ref[0] * (IDX_HEADS ** -0.5)
        sc = jnp.zeros((tk, tq), F32)
        for h in range(IDX_HEADS):
            sc = sc + jnp.maximum(lg[:, h * tq:(h + 1) * tq], 0.0) * wgt[h:h + 1, :]
        bits = lax.bitcast_convert_type(sc, I32)
        key = bits ^ ((bits >> 31) & 0x7FFFFFFF)
        kpos = j * tk + lax.broadcasted_iota(I32, (tk, tq), 0)
        qpos = q0 + lax.broadcasted_iota(I32, (tk, tq), 1)
        ok = ((kpos >> 6) <= (qpos >> 6)) & (kpos < length)
        key = jnp.where(ok, key, INT_MIN)
        keys_ref[0] = key
        kscr_ref[j] = key

    @pl.when(j == last)
    def _():
        def count(pred):
            def body(t, acc):
                kpos = t * tk + lax.broadcasted_iota(I32, (tk, tq), 0)
                m = jnp.where(pred(kscr_ref[t], kpos), 1.0, 0.0)
                return acc + jnp.sum(m, axis=0, keepdims=True)
            return lax.fori_loop(0, last + 1, body, jnp.zeros((1, tq), F32))

        kf = float(topk)

        def bit_body(it, carry):
            prefix, nge = carry
            cand = prefix + lax.shift_left(jnp.int32(1), 31 - it)
            cnt = count(lambda kt, _: kt >= cand)
            take = cnt >= kf
            return jnp.where(take, cand, prefix), jnp.where(take, cnt, nge)

        prefix, nge = lax.fori_loop(
            0, 32, bit_body, (jnp.full((1, tq), INT_MIN, I32), jnp.zeros((1, tq), F32)))
        found = prefix > INT_MIN
        thr = jnp.maximum(prefix, INT_MIN + 1)
        ngt = count(lambda kt, _: kt > thr)
        need = kf - ngt
        t_ref[0] = thr
        c_ref[0] = jnp.full((1, tq), 2 ** 30, I32)
        straddle = jnp.max(jnp.where(found & (nge > kf), 1.0, 0.0))

        @pl.when(straddle > 0.0)
        def _():
            def tie_body(it, x):
                cx = x + lax.shift_left(jnp.int32(1), idx_bits - 1 - it)
                g = count(lambda kt, kpos: jnp.where(kt == thr, kpos, INT_MAX) < cx)
                return jnp.where(g < need, cx, x)
            c_ref[0] = lax.fori_loop(0, idx_bits, tie_body, jnp.zeros((1, tq), I32))


def _dsa_index(qi, ki2, wt, *, tq, tk, qoff, length, topk):
    b, s, _ = qi.shape
    lp = ki2.shape[1]
    nq, nk = s // tq, lp // tk
    last = lambda i: _last_tile(i, tq, tk, qoff, length, True)
    idx_bits = max(1, int(np.ceil(np.log2(lp))))
    qvec = pl.BlockSpec((1, 1, tq), lambda bb, i, j: (bb, 0, i))
    return pl.pallas_call(
        functools.partial(_dsa_index_kernel, tq=tq, tk=tk, qoff=qoff, length=length, topk=topk,
                          idx_bits=idx_bits),
        grid=(b, nq, nk),
        in_specs=[pl.BlockSpec((1, tq, 512), lambda bb, i, j: (bb, i, 0)),
                  pl.BlockSpec((1, tk, LANES), lambda bb, i, j: (bb, jnp.minimum(j, last(i)), 0)),
                  pl.BlockSpec((1, IDX_HEADS, tq), lambda bb, i, j: (bb, 0, i))],
        out_specs=[pl.BlockSpec((1, tk, tq), lambda bb, i, j: (bb, jnp.minimum(j, last(i)), i)), qvec, qvec],
        out_shape=[jax.ShapeDtypeStruct((b, lp, s), I32),
                   jax.ShapeDtypeStruct((b, 1, s), I32),
                   jax.ShapeDtypeStruct((b, 1, s), I32)],
        scratch_shapes=[pltpu.VMEM((IDX_HEADS * tq, LANES), CDT),
                        pltpu.VMEM((nk, tk, tq), I32)],
        compiler_params=_params(("parallel", "parallel", "arbitrary")),
        name="dsa_index",
    )(qi, ki2, wt)


def _dsa_attn_kernel(q_ref, k_ref, vt_ref, keys_ref, t_ref, c_ref, o_ref, qst_ref, m_ref, l_ref, acc_ref,
                     *, tq, tk, qoff, length):
    i = pl.program_id(1)
    j = pl.program_id(2)
    last = _last_tile(i, tq, tk, qoff, length, True)

    @pl.when(j == 0)
    def _():
        _flash_init(q_ref, None, qst_ref, m_ref, l_ref, acc_ref, tq)

    @pl.when(j <= last)
    def _():
        kt = keys_ref[0]
        thr = t_ref[0]
        kpos = j * tk + lax.broadcasted_iota(I32, (tk, tq), 0)
        sel = (kt > thr) | (jnp.where(kt == thr, kpos, INT_MAX) <= c_ref[0])
        mb = jnp.where(sel, 0.0, NEG)
        mb2 = jnp.concatenate([mb, mb], axis=1)
        for g in range(qst_ref.shape[0]):
            st = _mm_nt(k_ref[0, :, g * LANES:(g + 1) * LANES], qst_ref[g]) + mb2
            _flash_step(st, vt_ref, m_ref, l_ref, acc_ref, g, tq)

    @pl.when(j == last)
    def _():
        _flash_finish(o_ref, l_ref, acc_ref, tq)


def _dsa_attn(q, k, vt, keys, thr, cut, *, tq, tk, qoff, length):
    b, s, w = q.shape
    lp = k.shape[1]
    last = lambda i: _last_tile(i, tq, tk, qoff, length, True)
    jj = lambda i, j: jnp.minimum(j, last(i))
    qmap = lambda bb, i, j: (bb, i, 0)
    qvec = pl.BlockSpec((1, 1, tq), lambda bb, i, j: (bb, 0, i))
    npair = w // LANES
    return pl.pallas_call(
        functools.partial(_dsa_attn_kernel, tq=tq, tk=tk, qoff=qoff, length=length),
        grid=(b, s // tq, lp // tk),
        in_specs=[pl.BlockSpec((1, tq, w), qmap),
                  pl.BlockSpec((1, tk, w), lambda bb, i, j: (bb, jj(i, j), 0)),
                  pl.BlockSpec((1, w, tk), lambda bb, i, j: (bb, 0, jj(i, j))),
                  pl.BlockSpec((1, tk, tq), lambda bb, i, j: (bb, jj(i, j), i)),
                  qvec, qvec],
        out_specs=pl.BlockSpec((1, tq, w), qmap),
        out_shape=jax.ShapeDtypeStruct((b, s, w), CDT),
        scratch_shapes=_flash_scratch(npair, tq, LANES, w),
        compiler_params=_params(("parallel", "parallel", "arbitrary")),
        name="dsa_attn",
    )(q, k, vt, keys, thr, cut)


def _pieces3(x):
    pcs = _pieces(x, 3)
    return pcs + [jnp.zeros_like(pcs[0])] * (3 - len(pcs))


def _cumsum_kernel(x_ref, tri_ref, p1_ref, p2_ref, p3_ref, carry_ref, *, scale):
    @pl.when(pl.program_id(1) == 0)
    def _():
        carry_ref[...] = jnp.zeros(carry_ref.shape, F32)
    tri = tri_ref[...]
    tot = None
    for p in _pieces(x_ref[0], 3):
        t = _mm(tri, p)
        tot = t if tot is None else tot + t
    acc = carry_ref[...] + tot
    carry_ref[...] = acc[-1:, :]
    for ref, p in zip((p1_ref, p2_ref, p3_ref), _pieces3(acc * (-scale))):
        ref[0] = p


def _cumsum_neg_pieces(x, tb, scale):
    b, s, w = x.shape
    tri = jnp.tril(jnp.ones((tb, tb), F32)).astype(CDT)
    spec = pl.BlockSpec((1, tb, w), lambda bb, i: (bb, i, 0))
    return pl.pallas_call(
        functools.partial(_cumsum_kernel, scale=scale),
        grid=(b, s // tb),
        in_specs=[spec, _const_spec(tri.shape)],
        out_specs=[spec] * 3,
        out_shape=[jax.ShapeDtypeStruct(x.shape, CDT)] * 3,
        scratch_shapes=[pltpu.VMEM((1, w), F32)],
        compiler_params=_params(("parallel", "arbitrary")),
        name="logf_cumsum",
    )(x, tri)


def _fox_kernel(q_ref, k_ref, vt_ref, aug_ref, o_ref, qst_ref, m_ref, l_ref, acc_ref,
                *, tq, tk, qoff, length):
    i = pl.program_id(1)
    j = pl.program_id(2)
    last = _last_tile(i, tq, tk, qoff, length, False)
    q0 = qoff + i * tq
    kd = qst_ref.shape[2]

    @pl.when(j == 0)
    def _():
        _flash_init(q_ref, aug_ref, qst_ref, m_ref, l_ref, acc_ref, tq)

    def step(masked):
        if masked:
            kpos = j * tk + lax.broadcasted_iota(I32, (tk, 2 * tq), 0)
            lane = lax.broadcasted_iota(I32, (tk, 2 * tq), 1)
            causal = kpos <= q0 + jnp.where(lane >= tq, lane - tq, lane)
        for g in range(qst_ref.shape[0]):
            st = _mm_nt(k_ref[0, :, g * kd:(g + 1) * kd], qst_ref[g])
            if masked:
                st = jnp.where(causal, st, NEG)
            _flash_step(st, vt_ref, m_ref, l_ref, acc_ref, g, tq)

    diag = (j + 1) * tk - 1 > q0

    @pl.when((j <= last) & diag)
    def _():
        step(True)

    @pl.when((j <= last) & jnp.logical_not(diag))
    def _():
        step(False)

    @pl.when(j == last)
    def _():
        _flash_finish(o_ref, l_ref, acc_ref, tq)


def _fox(q, kaug, vt, qaug, *, tq, tk, qoff, length):
    b, s, w = q.shape
    lp = kaug.shape[1]
    npair = w // LANES
    kd = kaug.shape[2] // npair
    last = lambda i: _last_tile(i, tq, tk, qoff, length, False)
    jj = lambda i, j: jnp.minimum(j, last(i))
    qmap = lambda bb, i, j: (bb, i, 0)
    return pl.pallas_call(
        functools.partial(_fox_kernel, tq=tq, tk=tk, qoff=qoff, length=length),
        grid=(b, s // tq, lp // tk),
        in_specs=[pl.BlockSpec((1, tq, w), qmap),
                  pl.BlockSpec((1, tk, npair * kd), lambda bb, i, j: (bb, jj(i, j), 0)),
                  pl.BlockSpec((1, w, tk), lambda bb, i, j: (bb, 0, jj(i, j))),
                  pl.BlockSpec((1, 1, 2 * npair, LANES), lambda bb, i, j: (bb, i, 0, 0))],
        out_specs=pl.BlockSpec((1, tq, w), qmap),
        out_shape=jax.ShapeDtypeStruct((b, s, w), CDT),
        scratch_shapes=_flash_scratch(npair, tq, kd, w),
        compiler_params=_params(("parallel", "parallel", "arbitrary")),
        name="fox_attn",
    )(q, kaug, vt, qaug)


def _fox_kaug(k_c, nf):
    b, l, _ = k_c.shape
    npair = nf[0].shape[-1] // 2
    pe = jnp.stack([p[..., 0::2] for p in nf], axis=-1)
    po = jnp.stack([p[..., 1::2] for p in nf], axis=-1)
    aug = jnp.concatenate([pe, jnp.ones((b, l, npair, 3), CDT), po,
                           jnp.zeros((b, l, npair, LANES - 9), CDT)], axis=-1)
    return jnp.concatenate([k_c.reshape(b, l, npair, LANES), aug], axis=-1).reshape(b, l, npair * 2 * LANES)


def _fox_qaug(nf0):
    nh = nf0[0].shape[-1]
    f3 = jnp.stack([-p.astype(F32) for p in nf0], axis=-1)
    one, zero = jnp.ones_like(f3), jnp.zeros_like(f3)
    even = jnp.concatenate([one, f3, zero], axis=-1)
    odd = jnp.concatenate([zero, f3, one], axis=-1)
    a = jnp.where((jnp.arange(nh) % 2 == 0)[None, None, :, None], even, odd)
    return jnp.pad(a, ((0, 0), (0, 0), (0, 0), (0, LANES - 9)))


def _rope_tables(pos):
    half = HEAD_DIM // 2
    inv = ROPE_THETA ** (-jnp.arange(half, dtype=F32) / half)
    ang = pos.astype(F32)[:, None] * inv[None, :]
    cos, sin = jnp.cos(ang), jnp.sin(ang)
    return (jnp.tile(jnp.concatenate([cos, cos], axis=1), (1, 2)),
            jnp.tile(jnp.concatenate([-sin, sin], axis=1), (1, 2)))


def _pad_rows(x, n):
    return x if x.shape[1] == n else jnp.pad(x, ((0, 0), (0, n - x.shape[1]), (0, 0)))


def _tiles(s):
    tq = 256 if s % 256 == 0 else s
    tk = 512 if s % 512 == 0 else s
    return tq, tk


def kernel(x_prompt, x_sample, cache_a_k, cache_a_v, cache_b_k, cache_b_v, cache_b_ik, cache_c_k, cache_c_v,
           cache_c_logf, p_prompt, p_sample, g_mix, w_in_even, qn_a, kn_a, rel_bias_a, qn_b, kn_b, w_out_even,
           w_in_odd, b_f, qn_c, kn_c, w_out_odd, g_mlp, w_up, w_down, g_ple, w_ple_gate, w_ple_proj):
    B, S, D = x_prompt.shape
    DB, n, _ = x_sample.shape
    P = cache_b_k.shape[2]
    W_A = cache_a_k.shape[2]
    W_AP = min(A_PAST, S)
    depth = g_mix.shape[0]
    L = P + n
    Lp = -(-L // LANES) * LANES
    nqp = -(-n // LANES) * LANES
    assert B == 1 and S % 128 == 0 and n % 16 == 0 and P % 16 == 0

    row = lambda v: v.astype(F32).reshape(1, -1)
    bd = jnp.asarray(np.kron(np.eye(256 // HEAD_DIM), np.full((HEAD_DIM, HEAD_DIM), 1.0 / HEAD_DIM)), CDT)
    cos_p, sin_p = _rope_tables(jnp.arange(S))
    cos_s, sin_s = (jnp.tile(t, (DB, 1)) for t in _rope_tables(P + jnp.arange(n)))
    tq, tk = _tiles(S)
    tr = lambda t: jnp.swapaxes(t, 1, 2)
    catp = lambda c, new: _pad_rows(jnp.concatenate([c.astype(CDT), new], axis=1), Lp)

    outs = {name: [] for name in ("a_k_p", "a_v_p", "b_k_p", "b_v_p", "b_ik_p", "c_k_p", "c_v_p", "c_lf_p",
                                  "a_k_s", "a_v_s", "b_k_s", "b_v_s", "b_ik_s", "c_k_s", "c_v_s", "c_lf_s")}
    h_p = x_prompt.reshape(B * S, D)
    h_s = x_sample.reshape(DB * n, D)
    for i in range(depth):
        g = row(g_mix[i])
        if i % 2 == 0:
            e = i // 2
            wsz = 512
            w = w_in_even[e]
            w_all = jnp.concatenate([w[:, :7 * wsz], jnp.pad(w[:, 7 * wsz:], ((0, 0), (0, LANES - 72)))],
                                    axis=1).astype(CDT)
            gn = jnp.stack([jnp.tile(v[e].astype(F32), wsz // HEAD_DIM) for v in (qn_a, kn_a, qn_b, kn_b)])
            (ka, va, kb, vb, kw, qa_c, ka_c, va_c, qb_c, kb_c, vb_c, qi_c, ki2_c) = _proj_even(
                h_p, g, w_all, gn, bd, cos_p, sin_p)
            r3 = lambda t: t.reshape(B, S, t.shape[-1])
            tqa = 128
            npc = (A_PAST + tqa) // tqa
            kpad = jnp.pad(r3(ka_c), ((0, 0), (A_PAST, 0), (0, 0)))
            vpad = jnp.pad(r3(va_c), ((0, 0), (A_PAST, 0), (0, 0)))
            k_specs = [pl.BlockSpec((1, tqa, wsz), functools.partial(lambda bb, ii, pp: (bb, ii + pp, 0), pp=pp))
                       for pp in range(npc)]
            bias_p = _band_bias(rel_bias_a[e], np.arange(tqa), np.arange(-A_PAST, tqa))
            oa = _band(r3(qa_c), [kpad] * npc, [vpad] * npc, k_specs, bias_p, tqa, True)
            topk_p = min(TOPK_MAX, S // 4)
            wt = tr(r3(kw)[..., 64:64 + IDX_HEADS])
            keys, thr, cut = _dsa_index(r3(qi_c), r3(ki2_c), wt, tq=tq, tk=tk, qoff=0, length=S, topk=topk_p)
            ob = _dsa_attn(r3(qb_c), r3(kb_c), tr(r3(vb_c)), keys, thr, cut, tq=tq, tk=tk, qoff=0, length=S)
            os_p = [oa.reshape(B * S, wsz), ob.reshape(B * S, wsz)]
            hd = lambda t: t.reshape(B, S, -1, HEAD_DIM)
            outs["a_k_p"].append(hd(ka)[:, S - W_AP:]); outs["a_v_p"].append(hd(va)[:, S - W_AP:])
            outs["b_k_p"].append(hd(kb)); outs["b_v_p"].append(hd(vb))
            outs["b_ik_p"].append(kw[:, :64].reshape(B, S, 64))
            (ka, va, kb, vb, kw, qa_c, ka_c, va_c, qb_c, kb_c, vb_c, qi_c, ki2_c) = _proj_even(
                h_s, g, w_all, gn, bd, cos_s, sin_s)
            r3 = lambda t: t.reshape(DB, n, t.shape[-1])
            ca_k = cache_a_k[e].reshape(DB, W_A, wsz)
            ca_v = cache_a_v[e].reshape(DB, W_A, wsz)
            full = lambda rows, wd: pl.BlockSpec((1, rows, wd), lambda bb, ii: (bb, 0, 0))
            bias_s = _band_bias(rel_bias_a[e], P + np.arange(n), P - W_A + np.arange(W_A + n))
            bias_s = jnp.pad(bias_s, ((0, 0), (0, 0), (0, -(W_A + n) % LANES)), constant_values=NEG)
            oa = _band(r3(qa_c), [ca_k, r3(ka)], [ca_v, r3(va)], [full(W_A, wsz), full(n, wsz)], bias_s, n, False)
            ik = cache_b_ik[e].astype(CDT)
            ki2_all = catp(jnp.concatenate([ik, ik], axis=-1), r3(ki2_c))
            kb_all = catp(cache_b_k[e].reshape(DB, P, wsz), r3(kb_c))
            vb_all = catp(cache_b_v[e].reshape(DB, P, wsz), r3(vb_c))
            topk_s = min(TOPK_MAX, L // 4)
            wt = _pad_rows(r3(kw)[..., 64:64 + IDX_HEADS], nqp)
            keys, thr, cut = _dsa_index(_pad_rows(r3(qi_c), nqp), ki2_all, tr(wt),
                                        tq=nqp, tk=Lp, qoff=P, length=L, topk=topk_s)
            ob = _dsa_attn(_pad_rows(r3(qb_c), nqp), kb_all, tr(vb_all), keys, thr, cut,
                           tq=nqp, tk=Lp, qoff=P, length=L)[:, :n]
            os_s = [oa.reshape(DB * n, wsz), ob.reshape(DB * n, wsz)]
            hd = lambda t: t.reshape(DB, n, -1, HEAD_DIM)
            outs["a_k_s"].append(jnp.concatenate([cache_a_k[e], hd(ka)], axis=1)[:, n:])
            outs["a_v_s"].append(jnp.concatenate([cache_a_v[e], hd(va)], axis=1)[:, n:])
            outs["b_k_s"].append(hd(kb)); outs["b_v_s"].append(hd(vb))
            outs["b_ik_s"].append(kw[:, :64].reshape(DB, n, 64))
            w_out = w_out_even[e].astype(CDT)
            wos = [w_out[:wsz], w_out[wsz:]]
        else:
            o = i // 2
            nh = b_f.shape[1]
            wsz = nh * HEAD_DIM
            w = w_in_odd[o]
            w_all = jnp.concatenate([w[:, :3 * wsz], jnp.pad(w[:, 3 * wsz:], ((0, 0), (0, LANES - nh)))],
                                    axis=1).astype(CDT)
            gn = jnp.stack([jnp.tile(v[o].astype(F32), wsz // HEAD_DIM) for v in (qn_c, kn_c)])
            bf = jnp.pad(b_f[o].astype(F32), (0, LANES - nh)).reshape(1, LANES)
            k, v, lf, q_c, k_c, v_c = _proj_odd(h_p, g, w_all, gn, bf, bd)
            r3 = lambda t: t.reshape(B, S, t.shape[-1])
            nf = [p[..., :nh] for p in _cumsum_neg_pieces(r3(lf), 512 if S % 512 == 0 else S, LOG2E)]
            oc = _fox(r3(q_c), _fox_kaug(r3(k_c), nf), tr(r3(v_c)), _fox_qaug([p[:, ::tq] for p in nf]),
                      tq=tq, tk=tk, qoff=0, length=S)
            os_p = [oc.reshape(B * S, wsz)]
            hd = lambda t: t.reshape(B, S, nh, HEAD_DIM)
            outs["c_k_p"].append(hd(k)); outs["c_v_p"].append(hd(v)); outs["c_lf_p"].append(lf[:, :nh].reshape(B, S, nh))
            k, v, lf, q_c, k_c, v_c = _proj_odd(h_s, g, w_all, gn, bf, bd)
            r3 = lambda t: t.reshape(DB, n, t.shape[-1])
            lf_all = jnp.concatenate([jnp.pad(cache_c_logf[o].astype(F32), ((0, 0), (0, 0), (0, LANES - nh))), r3(lf)],
                                     axis=1)
            nf = [p[..., :nh] for p in _cumsum_neg_pieces(lf_all, L, LOG2E)]
            k_all = catp(cache_c_k[o].reshape(DB, P, wsz), r3(k_c))
            v_all = catp(cache_c_v[o].reshape(DB, P, wsz), r3(v_c))
            oc = _fox(_pad_rows(r3(q_c), nqp), _fox_kaug(k_all, [_pad_rows(p, Lp) for p in nf]), tr(v_all),
                      _fox_qaug([p[:, P:P + 1] for p in nf]), tq=nqp, tk=Lp, qoff=P, length=L)[:, :n]
            os_s = [oc.reshape(DB * n, wsz)]
            hd = lambda t: t.reshape(DB, n, nh, HEAD_DIM)
            outs["c_k_s"].append(hd(k)); outs["c_v_s"].append(hd(v)); outs["c_lf_s"].append(lf[:, :nh].reshape(DB, n, nh))
            w_out = w_out_odd[o].astype(CDT)
            wos = [w_out]
        post_w = (row(g_mlp[i]), w_up[i].astype(CDT), w_down[i].astype(CDT), row(g_ple[i]),
                  w_ple_gate[i].astype(CDT), w_ple_proj[i].astype(CDT))
        h_p = _post(h_p, os_p, p_prompt[i].reshape(B * S, -1), wos, *post_w)
        h_s = _post(h_s, os_s, p_sample[i].reshape(DB * n, -1), wos, *post_w)

    st = jnp.stack
    names = ("a_k_p", "a_v_p", "b_k_p", "b_v_p", "b_ik_p", "c_k_p", "c_v_p", "c_lf_p",
             "a_k_s", "a_v_s", "b_k_s", "b_v_s", "b_ik_s", "c_k_s", "c_v_s", "c_lf_s")
    return (h_p.reshape(B, S, D), h_s.reshape(DB, n, D)) + tuple(st(outs[nm]) for nm in names)
```

```python
import functools

import numpy as np
import jax
import jax.numpy as jnp
from jax import lax
from jax.experimental import pallas as pl
from jax.experimental.pallas import tpu as pltpu

F32 = jnp.float32
I32 = jnp.int32
CDT = jnp.bfloat16

CHUNK = 64
HEAD_DIM = 64
IDX_HEADS = 8
A_PAST = 8 * CHUNK
REL_CLIP = 128
TOPK_MAX = 256
ROPE_THETA = 10000.0
RMS_EPS = 1e-6
LANES = 128
NEG = -1e30
INT_MIN = -(2 ** 31)
INT_MAX = 2 ** 31 - 1
LOG2E = 1.4426950408889634
VMEM_LIMIT = 56 * 1024 * 1024


def _mm(a, b):
    return jnp.dot(a, b, preferred_element_type=F32)


def _mm_nt(a, b):
    return lax.dot_general(a, b, (((1,), (1,)), ((), ())), preferred_element_type=F32)


def _pieces(x, n):
    if CDT == F32:
        return [x]
    out = []
    for _ in range(n - 1):
        p = x.astype(CDT)
        out.append(p)
        x = x - p.astype(F32)
    out.append(x.astype(CDT))
    return out


def _rms(x, g):
    return x * lax.rsqrt(jnp.mean(x * x, axis=-1, keepdims=True) + RMS_EPS) * g


def _head_rms(x, bd, g):
    x2 = x * x
    pcs = _pieces(x2, 2)
    cols = []
    for s in range(x.shape[1] // 256):
        sl = slice(s * 256, (s + 1) * 256)
        ms = _mm(pcs[0][:, sl], bd)
        for p in pcs[1:]:
            ms = ms + _mm(p[:, sl], bd)
        cols.append(ms)
    ms = cols[0] if len(cols) == 1 else jnp.concatenate(cols, axis=1)
    return x * lax.rsqrt(ms + RMS_EPS) * g


def _rope(x, cos, sin):
    w = x.shape[1]
    lane = lax.broadcasted_iota(I32, x.shape, 1)
    first = (lane & 63) < 32
    swapped = jnp.where(first, pltpu.roll(x, w - 32, 1), pltpu.roll(x, 32, 1))
    return x * cos + swapped * sin


def _tile_lanes(t, w):
    return t if w == t.shape[1] else jnp.concatenate([t] * (w // t.shape[1]), axis=1)


def _proj_even_kernel(h_ref, g_ref, w_ref, gn_ref, bd_ref, cos_ref, sin_ref,
                      ka_ref, va_ref, kb_ref, vb_ref, kw_ref,
                      qa_c, ka_c, va_c, qb_c, kb_c, vb_c, qi_c, ki2_c):
    a = _rms(h_ref[...], g_ref[...]).astype(CDT)
    z = _mm(a, w_ref[...])
    w = 512
    bd = bd_ref[...]
    gn = gn_ref[...]
    cos1, sin1 = cos_ref[...], sin_ref[...]
    cos, sin = _tile_lanes(cos1, w), _tile_lanes(sin1, w)
    scale = HEAD_DIM ** -0.5

    qa = _head_rms(z[:, 0:w], bd, gn[0:1])
    qa_c[...] = (qa * scale).astype(CDT)
    ka = _head_rms(z[:, w:2 * w], bd, gn[1:2])
    ka_ref[...] = ka
    ka_c[...] = ka.astype(CDT)
    va = z[:, 2 * w:3 * w]
    va_ref[...] = va
    va_c[...] = va.astype(CDT)
    qb = _rope(_head_rms(z[:, 3 * w:4 * w], bd, gn[2:3]), cos, sin)
    qb_c[...] = (qb * (scale * LOG2E)).astype(CDT)
    kb = _rope(_head_rms(z[:, 4 * w:5 * w], bd, gn[3:4]), cos, sin)
    kb_ref[...] = kb
    kb_c[...] = kb.astype(CDT)
    vb = z[:, 5 * w:6 * w]
    vb_ref[...] = vb
    vb_c[...] = vb.astype(CDT)
    qi = _rope(z[:, 6 * w:7 * w], cos, sin)
    qi_c[...] = (qi * scale).astype(CDT)
    kw = z[:, 7 * w:7 * w + LANES]
    kwr = _rope(kw, cos1, sin1)
    lane = lax.broadcasted_iota(I32, kw.shape, 1)
    kw_ref[...] = jnp.where(lane < 64, kwr, kw)
    ki2_c[...] = jnp.where(lane < 64, kwr, pltpu.roll(kwr, 64, 1)).astype(CDT)


def _proj_odd_kernel(h_ref, g_ref, w_ref, gn_ref, bf_ref, bd_ref,
                     k_ref, v_ref, lf_ref, q_c, k_c, v_c):
    a = _rms(h_ref[...], g_ref[...]).astype(CDT)
    z = _mm(a, w_ref[...])
    w = 1024
    bd = bd_ref[...]
    gn = gn_ref[...]
    q = _head_rms(z[:, 0:w], bd, gn[0:1])
    q_c[...] = (q * (HEAD_DIM ** -0.5 * LOG2E)).astype(CDT)
    k = _head_rms(z[:, w:2 * w], bd, gn[1:2])
    k_ref[...] = k
    k_c[...] = k.astype(CDT)
    v = z[:, 2 * w:3 * w]
    v_ref[...] = v
    v_c[...] = v.astype(CDT)
    fl = z[:, 3 * w:3 * w + LANES] + bf_ref[...]
    lf_ref[...] = jnp.minimum(fl, 0.0) - jnp.log1p(jnp.exp(-jnp.abs(fl)))


def _const_spec(shape):
    nd = len(shape)
    return pl.BlockSpec(shape, lambda *_: (0,) * nd, pipeline_mode=pl.Buffered(1))


def _row_spec(tm, w):
    return pl.BlockSpec((tm, w), lambda i: (i, 0))


def _params(sem):
    return pltpu.CompilerParams(dimension_semantics=sem, vmem_limit_bytes=VMEM_LIMIT)


def _row_tile(rows, pref):
    return pref if rows % pref == 0 else rows


def _proj_even(h, g, w_all, gn, bd, cos, sin):
    rows, d = h.shape
    tm = _row_tile(rows, 256)
    f = lambda w, dt: jax.ShapeDtypeStruct((rows, w), dt)
    out_shape = [f(512, F32)] * 4 + [f(LANES, F32)] + [f(512, CDT)] * 7 + [f(LANES, CDT)]
    out_specs = [_row_spec(tm, 512)] * 4 + [_row_spec(tm, LANES)] + [_row_spec(tm, 512)] * 7 + [_row_spec(tm, LANES)]
    return pl.pallas_call(
        _proj_even_kernel,
        grid=(rows // tm,),
        in_specs=[_row_spec(tm, d), _const_spec(g.shape), _const_spec(w_all.shape), _const_spec(gn.shape),
                  _const_spec(bd.shape), _row_spec(tm, LANES), _row_spec(tm, LANES)],
        out_specs=out_specs,
        out_shape=out_shape,
        compiler_params=_params(("parallel",)),
        name="proj_even",
    )(h, g, w_all, gn, bd, cos, sin)


def _proj_odd(h, g, w_all, gn, bf, bd):
    rows, d = h.shape
    tm = _row_tile(rows, 256)
    f = lambda w, dt: jax.ShapeDtypeStruct((rows, w), dt)
    out_shape = [f(1024, F32)] * 2 + [f(LANES, F32)] + [f(1024, CDT)] * 3
    out_specs = [_row_spec(tm, 1024)] * 2 + [_row_spec(tm, LANES)] + [_row_spec(tm, 1024)] * 3
    return pl.pallas_call(
        _proj_odd_kernel,
        grid=(rows // tm,),
        in_specs=[_row_spec(tm, d), _const_spec(g.shape), _const_spec(w_all.shape), _const_spec(gn.shape),
                  _const_spec(bf.shape), _const_spec(bd.shape)],
        out_specs=out_specs,
        out_shape=out_shape,
        compiler_params=_params(("parallel",)),
        name="proj_odd",
    )(h, g, w_all, gn, bf, bd)


def _post_kernel(*refs, n_o, ff_chunk):
    h_ref = refs[0]
    o_refs = refs[1:1 + n_o]
    p_ref = refs[1 + n_o]
    wo_refs = refs[2 + n_o:2 + 2 * n_o]
    gm_ref, wu_ref, wd_ref, gp_ref, wg_ref, wp_ref, out_ref = refs[2 + 2 * n_o:]
    mix = None
    for o_ref, wo_ref in zip(o_refs, wo_refs):
        t = _mm(o_ref[...], wo_ref[...])
        mix = t if mix is None else mix + t
    h = h_ref[...] + mix
    m = _rms(h, gm_ref[...]).astype(CDT)
    mlp = None
    d_ff = wu_ref.shape[1]
    for c in range(d_ff // ff_chunk):
        sl = slice(c * ff_chunk, (c + 1) * ff_chunk)
        u = jnp.square(jnp.maximum(_mm(m, wu_ref[:, sl]), 0.0)).astype(CDT)
        t = _mm(u, wd_ref[sl, :])
        mlp = t if mlp is None else mlp + t
    acc = h + mlp
    gate_in = _rms(acc, gp_ref[...]).astype(CDT)
    gate = 1.0 / (1.0 + jnp.exp(-_mm(gate_in, wg_ref[...])))
    out_ref[...] = acc + gate * _mm(p_ref[...].astype(CDT), wp_ref[...])


def _post(h, os_, p, wos, gm, wu, wd, gp, wg, wp):
    rows, d = h.shape
    tm = _row_tile(rows, 512)
    n_o = len(os_)
    in_specs = ([_row_spec(tm, d)] + [_row_spec(tm, o.shape[1]) for o in os_] + [_row_spec(tm, p.shape[1])]
                + [_const_spec(w.shape) for w in wos]
                + [_const_spec(x.shape) for x in (gm, wu, wd, gp, wg, wp)])
    return pl.pallas_call(
        functools.partial(_post_kernel, n_o=n_o, ff_chunk=1024),
        grid=(rows // tm,),
        in_specs=in_specs,
        out_specs=_row_spec(tm, d),
        out_shape=jax.ShapeDtypeStruct((rows, d), F32),
        compiler_params=_params(("parallel",)),
        name="post",
    )(h, *os_, p, *wos, gm, wu, wd, gp, wg, wp)


def _pair_stack(qp, tq):
    lane = lax.broadcasted_iota(I32, qp.shape, 1)
    zero = jnp.zeros_like(qp)
    return jnp.concatenate([jnp.where(lane < 64, qp, zero), jnp.where(lane >= 64, qp, zero)], axis=0)


def _pair_merge(o2, tq):
    lane = lax.broadcasted_iota(I32, (tq, LANES), 1)
    return jnp.where(lane < 64, o2[:tq], o2[tq:])


def _last_tile(i, tq, tk, qoff, length, chunked):
    qend = qoff + i * tq + tq - 1
    kmax = (qend // CHUNK + 1) * CHUNK if chunked else qend + 1
    kmax = jnp.minimum(kmax, length)
    return (kmax - 1) // tk


def _flash_init(q_ref, aug_ref, qst_ref, m_ref, l_ref, acc_ref, tq):
    q = q_ref[0]
    for g in range(qst_ref.shape[0]):
        st = _pair_stack(q[:, g * LANES:(g + 1) * LANES], tq)
        if aug_ref is not None:
            a = aug_ref[0, 0]
            ext = jnp.concatenate([jnp.broadcast_to(a[2 * g:2 * g + 1], (tq, LANES)),
                                   jnp.broadcast_to(a[2 * g + 1:2 * g + 2], (tq, LANES))], axis=0)
            st = jnp.concatenate([st, ext.astype(CDT)], axis=1)
        qst_ref[g] = st
    m_ref[...] = jnp.full(m_ref.shape, NEG, F32)
    l_ref[...] = jnp.zeros(l_ref.shape, F32)
    acc_ref[...] = jnp.zeros(acc_ref.shape, F32)


def _flash_pairs(score_fn, s_ref, vt_ref, m_ref, l_ref, acc_ref, tq):
    npair = m_ref.shape[0]
    s_ref[0] = score_fn(0)
    for g in range(npair):
        if g + 1 < npair:
            s_ref[(g + 1) % 2] = score_fn(g + 1)
        _flash_step(s_ref[g % 2], vt_ref, m_ref, l_ref, acc_ref, g, tq)


def _flash_step(st, vt_ref, m_ref, l_ref, acc_ref, g, tq):
    m_old = m_ref[g]
    m_new = jnp.maximum(m_old, jnp.max(st, axis=0, keepdims=True))
    alpha = jnp.exp2(m_old - m_new)
    p = jnp.exp2(st - m_new)
    l_ref[g] = alpha * l_ref[g] + jnp.sum(p, axis=0, keepdims=True)
    m_ref[g] = m_new
    pc = p.astype(CDT)
    for e in range(2):
        rows = slice(g * LANES + e * HEAD_DIM, g * LANES + (e + 1) * HEAD_DIM)
        cols = slice(e * tq, (e + 1) * tq)
        acc_ref[rows, :] = alpha[:, cols] * acc_ref[rows, :] + _mm(vt_ref[0, rows, :], pc[:, cols])


def _flash_finish(o_ref, l_ref, acc_ref, tq):
    for g in range(l_ref.shape[0]):
        l = l_ref[g]
        den = jnp.concatenate([jnp.broadcast_to(l[:, :tq], (HEAD_DIM, tq)),
                               jnp.broadcast_to(l[:, tq:], (HEAD_DIM, tq))], axis=0)
        o = acc_ref[g * LANES:(g + 1) * LANES, :] / den
        o_ref[0, :, g * LANES:(g + 1) * LANES] = o.T.astype(o_ref.dtype)


def _flash_scratch(npair, tq, tk, kd, w):
    return [pltpu.VMEM((npair, 2 * tq, kd), CDT),
            pltpu.VMEM((npair, 1, 2 * tq), F32),
            pltpu.VMEM((npair, 1, 2 * tq), F32),
            pltpu.VMEM((w, tq), F32),
            pltpu.VMEM((2, tk, 2 * tq), F32)]


def _band_kernel(*refs, n_kv, tq, mask_neg):
    q_ref = refs[0]
    k_refs = refs[1:1 + n_kv]
    v_refs = refs[1 + n_kv:1 + 2 * n_kv]
    bias_ref = refs[1 + 2 * n_kv]
    o_ref = refs[2 + 2 * n_kv]
    i = pl.program_id(1)
    q = q_ref[0]
    cat = lambda rs: (rs[0][0].astype(CDT) if len(rs) == 1
                      else jnp.concatenate([r[0].astype(CDT) for r in rs], axis=0))
    k = cat(k_refs)
    v = cat(v_refs)
    tk = bias_ref.shape[2]
    if k.shape[0] < tk:
        zpad = jnp.zeros((tk - k.shape[0], k.shape[1]), CDT)
        k = jnp.concatenate([k, zpad], axis=0)
        v = jnp.concatenate([v, zpad], axis=0)
    if mask_neg:
        kpos = lax.broadcasted_iota(I32, (2 * tq, tk), 1) + (i * tq - A_PAST)
        neg = kpos < 0
    for g in range(q.shape[1] // LANES):
        sl = slice(g * LANES, (g + 1) * LANES)
        s = _mm_nt(_pair_stack(q[:, sl], tq), k[:, sl])
        s = s + jnp.concatenate([bias_ref[2 * g], bias_ref[2 * g + 1]], axis=0)
        if mask_neg:
            s = jnp.where(neg, NEG, s)
        m = jnp.max(s, axis=-1, keepdims=True)
        p = jnp.exp(s - m)
        l = jnp.sum(p, axis=-1, keepdims=True)
        o2 = _mm(p.astype(CDT), v[:, sl]) / l
        o_ref[0, :, sl] = _pair_merge(o2, tq).astype(o_ref.dtype)


def _band(q, k_parts, v_parts, k_specs, bias, tq, mask_neg):
    b, s, w = q.shape
    n_kv = len(k_parts)
    q_spec = pl.BlockSpec((1, tq, w), lambda bb, i: (bb, i, 0))
    return pl.pallas_call(
        functools.partial(_band_kernel, n_kv=n_kv, tq=tq, mask_neg=mask_neg),
        grid=(b, s // tq),
        in_specs=[q_spec] + k_specs + k_specs + [_const_spec(bias.shape)],
        out_specs=q_spec,
        out_shape=jax.ShapeDtypeStruct((b, s, w), CDT),
        compiler_params=_params(("parallel", "parallel")),
        name="band_attn",
    )(q, *k_parts, *v_parts, bias)


def _band_bias(rel_bias, q_pos, k_pos):
    rel = np.clip(q_pos[:, None] - k_pos[None, :], -REL_CLIP, REL_CLIP) + REL_CLIP
    qc = q_pos[:, None] // CHUNK
    kc = k_pos[None, :] // CHUNK
    ok = (kc <= qc) & (kc >= qc - A_PAST // CHUNK)
    bias = jnp.take(rel_bias.astype(F32), jnp.asarray(rel.reshape(-1)), axis=1).reshape(
        rel_bias.shape[0], *rel.shape)
    return jnp.where(jnp.asarray(ok)[None], bias, NEG)


def _dsa_index_kernel(qi_ref, ki2_ref, wt_ref, keys_ref, t_ref, c_ref, qst_ref, kscr_ref,
                      *, tq, tk, nk, unroll, qoff, length, topk, idx_bits):
    i = pl.program_id(1)
    j = pl.program_id(2)
    last = _last_tile(i, tq, tk, qoff, length, True)
    q0 = qoff + i * tq

    @pl.when(j == 0)
    def _():
        q = qi_ref[0]
        for g in range(IDX_HEADS // 2):
            qst_ref[2 * g * tq:(2 * g + 2) * tq, :] = _pair_stack(q[:, g * LANES:(g + 1) * LANES], tq)

    @pl.when(j <= last)
    def _():
        lg = _mm_nt(ki2_ref[0], qst_ref[...])
        wgt = wt_ref[0] * (IDX_HEADS ** -0.5)
        sc = jnp.zeros((tk, tq), F32)
        for h in range(IDX_HEADS):
            sc = sc + jnp.maximum(lg[:, h * tq:(h + 1) * tq], 0.0) * wgt[h:h + 1, :]
        bits = lax.bitcast_convert_type(sc, I32)
        key = bits ^ ((bits >> 31) & 0x7FFFFFFF)
        kpos = j * tk + lax.broadcasted_iota(I32, (tk, tq), 0)
        qpos = q0 + lax.broadcasted_iota(I32, (tk, tq), 1)
        ok = ((kpos >> 6) <= (qpos >> 6)) & (kpos < length)
        key = jnp.where(ok, key, INT_MIN)
        keys_ref[0] = key
        kscr_ref[j] = key
        if unroll > 1:
            @pl.when(j + 1 < nk)
            def _():
                kscr_ref[j + 1] = jnp.full((tk, tq), INT_MIN, I32)

    @pl.when(j == last)
    def _():
        def count(pred):
            def body(t, acc):
                for u in range(unroll):
                    tt = unroll * t + u
                    kpos = tt * tk + lax.broadcasted_iota(I32, (tk, tq), 0)
                    m = jnp.where(pred(kscr_ref[tt], kpos), 1.0, 0.0)
                    acc = acc + jnp.sum(m, axis=0, keepdims=True)
                return acc
            return lax.fori_loop(0, (last + unroll) // unroll, body, jnp.zeros((1, tq), F32))

        kf = float(topk)
        qpos = q0 + lax.broadcasted_iota(I32, (1, tq), 1)
        n_adm = jnp.minimum(((qpos >> 6) + 1) << 6, length).astype(F32)
        take_all = n_adm < kf

        def bit_cond(carry):
            return (carry[0] < 32) & (carry[3] == 0)

        def bit_body(carry):
            it, prefix, nge, _ = carry
            cand = prefix + lax.shift_left(jnp.int32(1), 31 - it)
            cnt = count(lambda kt, _: kt >= cand)
            take = cnt >= kf
            prefix = jnp.where(take, cand, prefix)
            nge = jnp.where(take, cnt, nge)
            settled = jnp.min(jnp.where(take_all | (nge == kf), 1.0, 0.0))
            return it + 1, prefix, nge, (settled > 0.0).astype(I32)

        _, prefix, nge, _ = lax.while_loop(
            bit_cond, bit_body,
            (jnp.int32(0), jnp.full((1, tq), INT_MIN, I32), jnp.zeros((1, tq), F32), jnp.int32(0)))
        found = prefix > INT_MIN
        thr = jnp.maximum(prefix, INT_MIN + 1)
        t_ref[0] = thr
        c_ref[0] = jnp.full((1, tq), 2 ** 30, I32)
        straddle = jnp.max(jnp.where(found & (nge > kf), 1.0, 0.0))

        @pl.when(straddle > 0.0)
        def _():
            need = kf - count(lambda kt, _: kt > thr)

            def tie_body(it, x):
                cx = x + lax.shift_left(jnp.int32(1), idx_bits - 1 - it)
                g = count(lambda kt, kpos: jnp.where(kt == thr, kpos, INT_MAX) < cx)
                return jnp.where(g < need, cx, x)
            c_ref[0] = lax.fori_loop(0, idx_bits, tie_body, jnp.zeros((1, tq), I32))


def _dsa_index(qi, ki2, wt, *, tq, tk, qoff, length, topk):
    b, s, _ = qi.shape
    lp = ki2.shape[1]
    nq, nk = s // tq, lp // tk
    last = lambda i: _last_tile(i, tq, tk, qoff, length, True)
    idx_bits = max(1, int(np.ceil(np.log2(lp))))
    qvec = pl.BlockSpec((1, 1, tq), lambda bb, i, j: (bb, 0, i))
    return pl.pallas_call(
        functools.partial(_dsa_index_kernel, tq=tq, tk=tk, nk=nk, unroll=2 if nk % 2 == 0 else 1,
                          qoff=qoff, length=length, topk=topk, idx_bits=idx_bits),
        grid=(b, nq, nk),
        in_specs=[pl.BlockSpec((1, tq, 512), lambda bb, i, j: (bb, i, 0)),
                  pl.BlockSpec((1, tk, LANES), lambda bb, i, j: (bb, jnp.minimum(j, last(i)), 0)),
                  pl.BlockSpec((1, IDX_HEADS, tq), lambda bb, i, j: (bb, 0, i))],
        out_specs=[pl.BlockSpec((1, tk, tq), lambda bb, i, j: (bb, jnp.minimum(j, last(i)), i)), qvec, qvec],
        out_shape=[jax.ShapeDtypeStruct((b, lp, s), I32),
                   jax.ShapeDtypeStruct((b, 1, s), I32),
                   jax.ShapeDtypeStruct((b, 1, s), I32)],
        scratch_shapes=[pltpu.VMEM((IDX_HEADS * tq, LANES), CDT),
                        pltpu.VMEM((nk, tk, tq), I32)],
        compiler_params=_params(("parallel", "parallel", "arbitrary")),
        name="dsa_index",
    )(qi, ki2, wt)


def _dsa_attn_kernel(q_ref, k_ref, vt_ref, keys_ref, t_ref, c_ref, o_ref, qst_ref, m_ref, l_ref, acc_ref, s_ref,
                     *, tq, tk, qoff, length):
    i = pl.program_id(1)
    j = pl.program_id(2)
    last = _last_tile(i, tq, tk, qoff, length, True)

    @pl.when(j == 0)
    def _():
        _flash_init(q_ref, None, qst_ref, m_ref, l_ref, acc_ref, tq)

    @pl.when(j <= last)
    def _():
        kt = keys_ref[0]
        thr = t_ref[0]
        kpos = j * tk + lax.broadcasted_iota(I32, (tk, tq), 0)
        sel = (kt > thr) | (jnp.where(kt == thr, kpos, INT_MAX) <= c_ref[0])
        mb = jnp.where(sel, 0.0, NEG)
        mb2 = jnp.concatenate([mb, mb], axis=1)
        score = lambda g: _mm_nt(k_ref[0, :, g * LANES:(g + 1) * LANES], qst_ref[g]) + mb2
        _flash_pairs(score, s_ref, vt_ref, m_ref, l_ref, acc_ref, tq)

    @pl.when(j == last)
    def _():
        _flash_finish(o_ref, l_ref, acc_ref, tq)


def _dsa_attn(q, k, vt, keys, thr, cut, *, tq, tk, qoff, length):
    b, s, w = q.shape
    lp = k.shape[1]
    last = lambda i: _last_tile(i, tq, tk, qoff, length, True)
    jj = lambda i, j: jnp.minimum(j, last(i))
    qmap = lambda bb, i, j: (bb, i, 0)
    qvec = pl.BlockSpec((1, 1, tq), lambda bb, i, j: (bb, 0, i))
    npair = w // LANES
    return pl.pallas_call(
        functools.partial(_dsa_attn_kernel, tq=tq, tk=tk, qoff=qoff, length=length),
        grid=(b, s // tq, lp // tk),
        in_specs=[pl.BlockSpec((1, tq, w), qmap),
                  pl.BlockSpec((1, tk, w), lambda bb, i, j: (bb, jj(i, j), 0)),
                  pl.BlockSpec((1, w, tk), lambda bb, i, j: (bb, 0, jj(i, j))),
                  pl.BlockSpec((1, tk, tq), lambda bb, i, j: (bb, jj(i, j), i)),
                  qvec, qvec],
        out_specs=pl.BlockSpec((1, tq, w), qmap),
        out_shape=jax.ShapeDtypeStruct((b, s, w), CDT),
        scratch_shapes=_flash_scratch(npair, tq, tk, LANES, w),
        compiler_params=_params(("parallel", "parallel", "arbitrary")),
        name="dsa_attn",
    )(q, k, vt, keys, thr, cut)


def _pieces3(x):
    pcs = _pieces(x, 3)
    return pcs + [jnp.zeros_like(pcs[0])] * (3 - len(pcs))


def _cumsum_kernel(x_ref, tri_ref, p1_ref, p2_ref, p3_ref, carry_ref, *, scale):
    @pl.when(pl.program_id(1) == 0)
    def _():
        carry_ref[...] = jnp.zeros(carry_ref.shape, F32)
    tri = tri_ref[...]
    tot = None
    for p in _pieces(x_ref[0], 3):
        t = _mm(tri, p)
        tot = t if tot is None else tot + t
    acc = carry_ref[...] + tot
    carry_ref[...] = acc[-1:, :]
    for ref, p in zip((p1_ref, p2_ref, p3_ref), _pieces3(acc * (-scale))):
        ref[0] = p


def _cumsum_neg_pieces(x, tb, scale):
    b, s, w = x.shape
    tri = jnp.tril(jnp.ones((tb, tb), F32)).astype(CDT)
    spec = pl.BlockSpec((1, tb, w), lambda bb, i: (bb, i, 0))
    return pl.pallas_call(
        functools.partial(_cumsum_kernel, scale=scale),
        grid=(b, s // tb),
        in_specs=[spec, _const_spec(tri.shape)],
        out_specs=[spec] * 3,
        out_shape=[jax.ShapeDtypeStruct(x.shape, CDT)] * 3,
        scratch_shapes=[pltpu.VMEM((1, w), F32)],
        compiler_params=_params(("parallel", "arbitrary")),
        name="logf_cumsum",
    )(x, tri)


def _fox_kernel(q_ref, k_ref, vt_ref, aug_ref, o_ref, qst_ref, m_ref, l_ref, acc_ref, s_ref,
                *, tq, tk, qoff, length):
    i = pl.program_id(1)
    j = pl.program_id(2)
    last = _last_tile(i, tq, tk, qoff, length, False)
    q0 = qoff + i * tq
    kd = qst_ref.shape[2]

    @pl.when(j == 0)
    def _():
        _flash_init(q_ref, aug_ref, qst_ref, m_ref, l_ref, acc_ref, tq)

    def step(masked):
        if masked:
            kpos = j * tk + lax.broadcasted_iota(I32, (tk, 2 * tq), 0)
            lane = lax.broadcasted_iota(I32, (tk, 2 * tq), 1)
            causal = kpos <= q0 + jnp.where(lane >= tq, lane - tq, lane)
        def score(g):
            st = _mm_nt(k_ref[0, :, g * kd:(g + 1) * kd], qst_ref[g])
            return jnp.where(causal, st, NEG) if masked else st
        _flash_pairs(score, s_ref, vt_ref, m_ref, l_ref, acc_ref, tq)

    diag = (j + 1) * tk - 1 > q0

    @pl.when((j <= last) & diag)
    def _():
        step(True)

    @pl.when((j <= last) & jnp.logical_not(diag))
    def _():
        step(False)

    @pl.when(j == last)
    def _():
        _flash_finish(o_ref, l_ref, acc_ref, tq)


def _fox(q, kaug, vt, qaug, *, tq, tk, qoff, length):
    b, s, w = q.shape
    lp = kaug.shape[1]
    npair = w // LANES
    kd = kaug.shape[2] // npair
    last = lambda i: _last_tile(i, tq, tk, qoff, length, False)
    jj = lambda i, j: jnp.minimum(j, last(i))
    qmap = lambda bb, i, j: (bb, i, 0)
    return pl.pallas_call(
        functools.partial(_fox_kernel, tq=tq, tk=tk, qoff=qoff, length=length),
        grid=(b, s // tq, lp // tk),
        in_specs=[pl.BlockSpec((1, tq, w), qmap),
                  pl.BlockSpec((1, tk, npair * kd), lambda bb, i, j: (bb, jj(i, j), 0)),
                  pl.BlockSpec((1, w, tk), lambda bb, i, j: (bb, 0, jj(i, j))),
                  pl.BlockSpec((1, 1, 2 * npair, LANES), lambda bb, i, j: (bb, i, 0, 0))],
        out_specs=pl.BlockSpec((1, tq, w), qmap),
        out_shape=jax.ShapeDtypeStruct((b, s, w), CDT),
        scratch_shapes=_flash_scratch(npair, tq, tk, kd, w),
        compiler_params=_params(("parallel", "parallel", "arbitrary")),
        name="fox_attn",
    )(q, kaug, vt, qaug)


def _fox_kaug(k_c, nf):
    b, l, _ = k_c.shape
    npair = nf[0].shape[-1] // 2
    pe = jnp.stack([p[..., 0::2] for p in nf], axis=-1)
    po = jnp.stack([p[..., 1::2] for p in nf], axis=-1)
    aug = jnp.concatenate([pe, jnp.ones((b, l, npair, 3), CDT), po,
                           jnp.zeros((b, l, npair, LANES - 9), CDT)], axis=-1)
    return jnp.concatenate([k_c.reshape(b, l, npair, LANES), aug], axis=-1).reshape(b, l, npair * 2 * LANES)


def _fox_qaug(nf0):
    nh = nf0[0].shape[-1]
    f3 = jnp.stack([-p.astype(F32) for p in nf0], axis=-1)
    one, zero = jnp.ones_like(f3), jnp.zeros_like(f3)
    even = jnp.concatenate([one, f3, zero], axis=-1)
    odd = jnp.concatenate([zero, f3, one], axis=-1)
    a = jnp.where((jnp.arange(nh) % 2 == 0)[None, None, :, None], even, odd)
    return jnp.pad(a, ((0, 0), (0, 0), (0, 0), (0, LANES - 9)))


def _rope_tables(pos):
    half = HEAD_DIM // 2
    inv = ROPE_THETA ** (-jnp.arange(half, dtype=F32) / half)
    ang = pos.astype(F32)[:, None] * inv[None, :]
    cos, sin = jnp.cos(ang), jnp.sin(ang)
    return (jnp.tile(jnp.concatenate([cos, cos], axis=1), (1, 2)),
            jnp.tile(jnp.concatenate([-sin, sin], axis=1), (1, 2)))


def _pad_rows(x, n):
    return x if x.shape[1] == n else jnp.pad(x, ((0, 0), (0, n - x.shape[1]), (0, 0)))


def _tiles(s):
    tq = 256 if s % 256 == 0 else s
    tk = 512 if s % 512 == 0 else s
    return tq, tk


def kernel(x_prompt, x_sample, cache_a_k, cache_a_v, cache_b_k, cache_b_v, cache_b_ik, cache_c_k, cache_c_v,
           cache_c_logf, p_prompt, p_sample, g_mix, w_in_even, qn_a, kn_a, rel_bias_a, qn_b, kn_b, w_out_even,
           w_in_odd, b_f, qn_c, kn_c, w_out_odd, g_mlp, w_up, w_down, g_ple, w_ple_gate, w_ple_proj):
    B, S, D = x_prompt.shape
    DB, n, _ = x_sample.shape
    P = cache_b_k.shape[2]
    W_A = cache_a_k.shape[2]
    W_AP = min(A_PAST, S)
    depth = g_mix.shape[0]
    L = P + n
    Lp = -(-L // LANES) * LANES
    nqp = -(-n // LANES) * LANES
    assert B == 1 and S % 128 == 0 and n % 16 == 0 and P % 16 == 0

    row = lambda v: v.astype(F32).reshape(1, -1)
    bd = jnp.asarray(np.kron(np.eye(256 // HEAD_DIM), np.full((HEAD_DIM, HEAD_DIM), 1.0 / HEAD_DIM)), CDT)
    cos_p, sin_p = _rope_tables(jnp.arange(S))
    cos_s, sin_s = (jnp.tile(t, (DB, 1)) for t in _rope_tables(P + jnp.arange(n)))
    tq, tk = _tiles(S)
    tr = lambda t: jnp.swapaxes(t, 1, 2)
    catp = lambda c, new: _pad_rows(jnp.concatenate([c.astype(CDT), new], axis=1), Lp)

    outs = {name: [] for name in ("a_k_p", "a_v_p", "b_k_p", "b_v_p", "b_ik_p", "c_k_p", "c_v_p", "c_lf_p",
                                  "a_k_s", "a_v_s", "b_k_s", "b_v_s", "b_ik_s", "c_k_s", "c_v_s", "c_lf_s")}
    h_p = x_prompt.reshape(B * S, D)
    h_s = x_sample.reshape(DB * n, D)
    for i in range(depth):
        g = row(g_mix[i])
        if i % 2 == 0:
            e = i // 2
            wsz = 512
            w = w_in_even[e]
            w_all = jnp.concatenate([w[:, :7 * wsz], jnp.pad(w[:, 7 * wsz:], ((0, 0), (0, LANES - 72)))],
                                    axis=1).astype(CDT)
            gn = jnp.stack([jnp.tile(v[e].astype(F32), wsz // HEAD_DIM) for v in (qn_a, kn_a, qn_b, kn_b)])
            (ka, va, kb, vb, kw, qa_c, ka_c, va_c, qb_c, kb_c, vb_c, qi_c, ki2_c) = _proj_even(
                h_p, g, w_all, gn, bd, cos_p, sin_p)
            r3 = lambda t: t.reshape(B, S, t.shape[-1])
            tqa = 128
            npc = (A_PAST + tqa) // tqa
            kpad = jnp.pad(r3(ka_c), ((0, 0), (A_PAST, 0), (0, 0)))
            vpad = jnp.pad(r3(va_c), ((0, 0), (A_PAST, 0), (0, 0)))
            k_specs = [pl.BlockSpec((1, tqa, wsz), functools.partial(lambda bb, ii, pp: (bb, ii + pp, 0), pp=pp))
                       for pp in range(npc)]
            bias_p = _band_bias(rel_bias_a[e], np.arange(tqa), np.arange(-A_PAST, tqa))
            oa = _band(r3(qa_c), [kpad] * npc, [vpad] * npc, k_specs, bias_p, tqa, True)
            topk_p = min(TOPK_MAX, S // 4)
            wt = tr(r3(kw)[..., 64:64 + IDX_HEADS])
            keys, thr, cut = _dsa_index(r3(qi_c), r3(ki2_c), wt, tq=tq, tk=tk, qoff=0, length=S, topk=topk_p)
            ob = _dsa_attn(r3(qb_c), r3(kb_c), tr(r3(vb_c)), keys, thr, cut, tq=tq, tk=tk, qoff=0, length=S)
            os_p = [oa.reshape(B * S, wsz), ob.reshape(B * S, wsz)]
            hd = lambda t: t.reshape(B, S, -1, HEAD_DIM)
            outs["a_k_p"].append(hd(ka)[:, S - W_AP:]); outs["a_v_p"].append(hd(va)[:, S - W_AP:])
            outs["b_k_p"].append(hd(kb)); outs["b_v_p"].append(hd(vb))
            outs["b_ik_p"].append(kw[:, :64].reshape(B, S, 64))
            (ka, va, kb, vb, kw, qa_c, ka_c, va_c, qb_c, kb_c, vb_c, qi_c, ki2_c) = _proj_even(
                h_s, g, w_all, gn, bd, cos_s, sin_s)
            r3 = lambda t: t.reshape(DB, n, t.shape[-1])
            ca_k = cache_a_k[e].reshape(DB, W_A, wsz)
            ca_v = cache_a_v[e].reshape(DB, W_A, wsz)
            full = lambda rows, wd: pl.BlockSpec((1, rows, wd), lambda bb, ii: (bb, 0, 0))
            bias_s = _band_bias(rel_bias_a[e], P + np.arange(n), P - W_A + np.arange(W_A + n))
            bias_s = jnp.pad(bias_s, ((0, 0), (0, 0), (0, -(W_A + n) % LANES)), constant_values=NEG)
            oa = _band(r3(qa_c), [ca_k, r3(ka)], [ca_v, r3(va)], [full(W_A, wsz), full(n, wsz)], bias_s, n, False)
            ik = cache_b_ik[e].astype(CDT)
            ki2_all = catp(jnp.concatenate([ik, ik], axis=-1), r3(ki2_c))
            kb_all = catp(cache_b_k[e].reshape(DB, P, wsz), r3(kb_c))
            vb_all = catp(cache_b_v[e].reshape(DB, P, wsz), r3(vb_c))
            topk_s = min(TOPK_MAX, L // 4)
            wt = _pad_rows(r3(kw)[..., 64:64 + IDX_HEADS], nqp)
            keys, thr, cut = _dsa_index(_pad_rows(r3(qi_c), nqp), ki2_all, tr(wt),
                                        tq=nqp, tk=Lp, qoff=P, length=L, topk=topk_s)
            ob = _dsa_attn(_pad_rows(r3(qb_c), nqp), kb_all, tr(vb_all), keys, thr, cut,
                           tq=nqp, tk=Lp, qoff=P, length=L)[:, :n]
            os_s = [oa.reshape(DB * n, wsz), ob.reshape(DB * n, wsz)]
            hd = lambda t: t.reshape(DB, n, -1, HEAD_DIM)
            outs["a_k_s"].append(jnp.concatenate([cache_a_k[e], hd(ka)], axis=1)[:, n:])
            outs["a_v_s"].append(jnp.concatenate([cache_a_v[e], hd(va)], axis=1)[:, n:])
            outs["b_k_s"].append(hd(kb)); outs["b_v_s"].append(hd(vb))
            outs["b_ik_s"].append(kw[:, :64].reshape(DB, n, 64))
            w_out = w_out_even[e].astype(CDT)
            wos = [w_out[:wsz], w_out[wsz:]]
        else:
            o = i // 2
            nh = b_f.shape[1]
            wsz = nh * HEAD_DIM
            w = w_in_odd[o]
            w_all = jnp.concatenate([w[:, :3 * wsz], jnp.pad(w[:, 3 * wsz:], ((0, 0), (0, LANES - nh)))],
                                    axis=1).astype(CDT)
            gn = jnp.stack([jnp.tile(v[o].astype(F32), wsz // HEAD_DIM) for v in (qn_c, kn_c)])
            bf = jnp.pad(b_f[o].astype(F32), (0, LANES - nh)).reshape(1, LANES)
            k, v, lf, q_c, k_c, v_c = _proj_odd(h_p, g, w_all, gn, bf, bd)
            r3 = lambda t: t.reshape(B, S, t.shape[-1])
            nf = [p[..., :nh] for p in _cumsum_neg_pieces(r3(lf), 512 if S % 512 == 0 else S, LOG2E)]
            oc = _fox(r3(q_c), _fox_kaug(r3(k_c), nf), tr(r3(v_c)), _fox_qaug([p[:, ::tq] for p in nf]),
                      tq=tq, tk=tk, qoff=0, length=S)
            os_p = [oc.reshape(B * S, wsz)]
            hd = lambda t: t.reshape(B, S, nh, HEAD_DIM)
            outs["c_k_p"].append(hd(k)); outs["c_v_p"].append(hd(v)); outs["c_lf_p"].append(lf[:, :nh].reshape(B, S, nh))
            k, v, lf, q_c, k_c, v_c = _proj_odd(h_s, g, w_all, gn, bf, bd)
            r3 = lambda t: t.reshape(DB, n, t.shape[-1])
            lf_all = jnp.concatenate([jnp.pad(cache_c_logf[o].astype(F32), ((0, 0), (0, 0), (0, LANES - nh))), r3(lf)],
                                     axis=1)
            nf = [p[..., :nh] for p in _cumsum_neg_pieces(lf_all, L, LOG2E)]
            k_all = catp(cache_c_k[o].reshape(DB, P, wsz), r3(k_c))
            v_all = catp(cache_c_v[o].reshape(DB, P, wsz), r3(v_c))
            oc = _fox(_pad_rows(r3(q_c), nqp), _fox_kaug(k_all, [_pad_rows(p, Lp) for p in nf]), tr(v_all),
                      _fox_qaug([p[:, P:P + 1] for p in nf]), tq=nqp, tk=Lp, qoff=P, length=L)[:, :n]
            os_s = [oc.reshape(DB * n, wsz)]
            hd = lambda t: t.reshape(DB, n, nh, HEAD_DIM)
            outs["c_k_s"].append(hd(k)); outs["c_v_s"].append(hd(v)); outs["c_lf_s"].append(lf[:, :nh].reshape(DB, n, nh))
            w_out = w_out_odd[o].astype(CDT)
            wos = [w_out]
        post_w = (row(g_mlp[i]), w_up[i].astype(CDT), w_down[i].astype(CDT), row(g_ple[i]),
                  w_ple_gate[i].astype(CDT), w_ple_proj[i].astype(CDT))
        h_p = _post(h_p, os_p, p_prompt[i].reshape(B * S, -1), wos, *post_w)
        h_s = _post(h_s, os_s, p_sample[i].reshape(DB * n, -1), wos, *post_w)

    st = jnp.stack
    names = ("a_k_p", "a_v_p", "b_k_p", "b_v_p", "b_ik_p", "c_k_p", "c_v_p", "c_lf_p",
             "a_k_s", "a_v_s", "b_k_s", "b_v_s", "b_ik_s", "c_k_s", "c_v_s", "c_lf_s")
    return (h_p.reshape(B, S, D), h_s.reshape(DB, n, D)) + tuple(st(outs[nm]) for nm in names)
```

```python
import functools

import numpy as np
import jax
import jax.numpy as jnp
from jax import lax
from jax.experimental import pallas as pl
from jax.experimental.pallas import tpu as pltpu

F32 = jnp.float32
I32 = jnp.int32
CDT = jnp.bfloat16

CHUNK = 64
HEAD_DIM = 64
IDX_HEADS = 8
A_PAST = 8 * CHUNK
REL_CLIP = 128
TOPK_MAX = 256
ROPE_THETA = 10000.0
RMS_EPS = 1e-6
LANES = 128
NEG = -1e30
INT_MIN = -(2 ** 31)
INT_MAX = 2 ** 31 - 1
LOG2E = 1.4426950408889634
VMEM_LIMIT = 56 * 1024 * 1024


def _mm(a, b):
    return jnp.dot(a, b, preferred_element_type=F32)


def _mm_nt(a, b):
    return lax.dot_general(a, b, (((1,), (1,)), ((), ())), preferred_element_type=F32)


def _pieces(x, n):
    if CDT == F32:
        return [x]
    out = []
    for _ in range(n - 1):
        p = x.astype(CDT)
        out.append(p)
        x = x - p.astype(F32)
    out.append(x.astype(CDT))
    return out


def _rms(x, g):
    return x * lax.rsqrt(jnp.mean(x * x, axis=-1, keepdims=True) + RMS_EPS) * g


def _head_rms(x, bd, g):
    x2 = x * x
    pcs = _pieces(x2, 2)
    cols = []
    for s in range(x.shape[1] // 256):
        sl = slice(s * 256, (s + 1) * 256)
        ms = _mm(pcs[0][:, sl], bd)
        for p in pcs[1:]:
            ms = ms + _mm(p[:, sl], bd)
        cols.append(ms)
    ms = cols[0] if len(cols) == 1 else jnp.concatenate(cols, axis=1)
    return x * lax.rsqrt(ms + RMS_EPS) * g


def _rope(x, cos, sin):
    w = x.shape[1]
    lane = lax.broadcasted_iota(I32, x.shape, 1)
    first = (lane & 63) < 32
    swapped = jnp.where(first, pltpu.roll(x, w - 32, 1), pltpu.roll(x, 32, 1))
    return x * cos + swapped * sin


def _tile_lanes(t, w):
    return t if w == t.shape[1] else jnp.concatenate([t] * (w // t.shape[1]), axis=1)


def _proj_even_kernel(h_ref, g_ref, w_ref, gn_ref, bd_ref, cos_ref, sin_ref,
                      ka_ref, va_ref, kb_ref, vb_ref, kw_ref,
                      qa_c, ka_c, va_c, qb_c, kb_c, vb_c, qi_c, ki2_c):
    a = _rms(h_ref[...], g_ref[...]).astype(CDT)
    z = _mm(a, w_ref[...])
    w = 512
    bd = bd_ref[...]
    gn = gn_ref[...]
    cos1, sin1 = cos_ref[...], sin_ref[...]
    cos, sin = _tile_lanes(cos1, w), _tile_lanes(sin1, w)
    scale = HEAD_DIM ** -0.5

    qa = _head_rms(z[:, 0:w], bd, gn[0:1])
    qa_c[...] = (qa * scale).astype(CDT)
    ka = _head_rms(z[:, w:2 * w], bd, gn[1:2])
    ka_ref[...] = ka
    ka_c[...] = ka.astype(CDT)
    va = z[:, 2 * w:3 * w]
    va_ref[...] = va
    va_c[...] = va.astype(CDT)
    qb = _rope(_head_rms(z[:, 3 * w:4 * w], bd, gn[2:3]), cos, sin)
    qb_c[...] = (qb * (scale * LOG2E)).astype(CDT)
    kb = _rope(_head_rms(z[:, 4 * w:5 * w], bd, gn[3:4]), cos, sin)
    kb_ref[...] = kb
    kb_c[...] = kb.astype(CDT)
    vb = z[:, 5 * w:6 * w]
    vb_ref[...] = vb
    vb_c[...] = vb.astype(CDT)
    qi = _rope(z[:, 6 * w:7 * w], cos, sin)
    qi_c[...] = (qi * scale).astype(CDT)
    kw = z[:, 7 * w:7 * w + LANES]
    kwr = _rope(kw, cos1, sin1)
    lane = lax.broadcasted_iota(I32, kw.shape, 1)
    kw_ref[...] = jnp.where(lane < 64, kwr, kw)
    ki2_c[...] = jnp.where(lane < 64, kwr, pltpu.roll(kwr, 64, 1)).astype(CDT)


def _proj_odd_kernel(h_ref, g_ref, w_ref, gn_ref, bf_ref, bd_ref,
                     k_ref, v_ref, lf_ref, q_c, k_c, v_c):
    a = _rms(h_ref[...], g_ref[...]).astype(CDT)
    z = _mm(a, w_ref[...])
    w = 1024
    bd = bd_ref[...]
    gn = gn_ref[...]
    q = _head_rms(z[:, 0:w], bd, gn[0:1])
    q_c[...] = (q * (HEAD_DIM ** -0.5 * LOG2E)).astype(CDT)
    k = _head_rms(z[:, w:2 * w], bd, gn[1:2])
    k_ref[...] = k
    k_c[...] = k.astype(CDT)
    v = z[:, 2 * w:3 * w]
    v_ref[...] = v
    v_c[...] = v.astype(CDT)
    fl = z[:, 3 * w:3 * w + LANES] + bf_ref[...]
    lf_ref[...] = jnp.minimum(fl, 0.0) - jnp.log1p(jnp.exp(-jnp.abs(fl)))


def _const_spec(shape):
    nd = len(shape)
    return pl.BlockSpec(shape, lambda *_: (0,) * nd, pipeline_mode=pl.Buffered(1))


def _row_spec(tm, w):
    return pl.BlockSpec((tm, w), lambda i: (i, 0))


def _params(sem):
    return pltpu.CompilerParams(dimension_semantics=sem, vmem_limit_bytes=VMEM_LIMIT)


def _row_tile(rows, pref):
    return pref if rows % pref == 0 else rows


def _proj_even(h, g, w_all, gn, bd, cos, sin):
    rows, d = h.shape
    tm = _row_tile(rows, 256)
    f = lambda w, dt: jax.ShapeDtypeStruct((rows, w), dt)
    out_shape = [f(512, F32)] * 4 + [f(LANES, F32)] + [f(512, CDT)] * 7 + [f(LANES, CDT)]
    out_specs = [_row_spec(tm, 512)] * 4 + [_row_spec(tm, LANES)] + [_row_spec(tm, 512)] * 7 + [_row_spec(tm, LANES)]
    return pl.pallas_call(
        _proj_even_kernel,
        grid=(rows // tm,),
        in_specs=[_row_spec(tm, d), _const_spec(g.shape), _const_spec(w_all.shape), _const_spec(gn.shape),
                  _const_spec(bd.shape), _row_spec(tm, LANES), _row_spec(tm, LANES)],
        out_specs=out_specs,
        out_shape=out_shape,
        compiler_params=_params(("parallel",)),
        name="proj_even",
    )(h, g, w_all, gn, bd, cos, sin)


def _proj_odd(h, g, w_all, gn, bf, bd):
    rows, d = h.shape
    tm = _row_tile(rows, 256)
    f = lambda w, dt: jax.ShapeDtypeStruct((rows, w), dt)
    out_shape = [f(1024, F32)] * 2 + [f(LANES, F32)] + [f(1024, CDT)] * 3
    out_specs = [_row_spec(tm, 1024)] * 2 + [_row_spec(tm, LANES)] + [_row_spec(tm, 1024)] * 3
    return pl.pallas_call(
        _proj_odd_kernel,
        grid=(rows // tm,),
        in_specs=[_row_spec(tm, d), _const_spec(g.shape), _const_spec(w_all.shape), _const_spec(gn.shape),
                  _const_spec(bf.shape), _const_spec(bd.shape)],
        out_specs=out_specs,
        out_shape=out_shape,
        compiler_params=_params(("parallel",)),
        name="proj_odd",
    )(h, g, w_all, gn, bf, bd)


def _post_kernel(*refs, n_o, ff_chunk):
    h_ref = refs[0]
    o_refs = refs[1:1 + n_o]
    p_ref = refs[1 + n_o]
    wo_refs = refs[2 + n_o:2 + 2 * n_o]
    gm_ref, wu_ref, wd_ref, gp_ref, wg_ref, wp_ref, out_ref = refs[2 + 2 * n_o:]
    mix = None
    for o_ref, wo_ref in zip(o_refs, wo_refs):
        t = _mm(o_ref[...], wo_ref[...])
        mix = t if mix is None else mix + t
    h = h_ref[...] + mix
    m = _rms(h, gm_ref[...]).astype(CDT)
    mlp = None
    d_ff = wu_ref.shape[1]
    for c in range(d_ff // ff_chunk):
        sl = slice(c * ff_chunk, (c + 1) * ff_chunk)
        u = jnp.square(jnp.maximum(_mm(m, wu_ref[:, sl]), 0.0)).astype(CDT)
        t = _mm(u, wd_ref[sl, :])
        mlp = t if mlp is None else mlp + t
    acc = h + mlp
    gate_in = _rms(acc, gp_ref[...]).astype(CDT)
    gate = 1.0 / (1.0 + jnp.exp(-_mm(gate_in, wg_ref[...])))
    out_ref[...] = acc + gate * _mm(p_ref[...].astype(CDT), wp_ref[...])


def _post(h, os_, p, wos, gm, wu, wd, gp, wg, wp):
    rows, d = h.shape
    tm = _row_tile(rows, 512)
    n_o = len(os_)
    in_specs = ([_row_spec(tm, d)] + [_row_spec(tm, o.shape[1]) for o in os_] + [_row_spec(tm, p.shape[1])]
                + [_const_spec(w.shape) for w in wos]
                + [_const_spec(x.shape) for x in (gm, wu, wd, gp, wg, wp)])
    return pl.pallas_call(
        functools.partial(_post_kernel, n_o=n_o, ff_chunk=1024),
        grid=(rows // tm,),
        in_specs=in_specs,
        out_specs=_row_spec(tm, d),
        out_shape=jax.ShapeDtypeStruct((rows, d), F32),
        compiler_params=_params(("parallel",)),
        name="post",
    )(h, *os_, p, *wos, gm, wu, wd, gp, wg, wp)


def _pair_stack(qp, tq):
    lane = lax.broadcasted_iota(I32, qp.shape, 1)
    zero = jnp.zeros_like(qp)
    return jnp.concatenate([jnp.where(lane < 64, qp, zero), jnp.where(lane >= 64, qp, zero)], axis=0)


def _pair_merge(o2, tq):
    lane = lax.broadcasted_iota(I32, (tq, LANES), 1)
    return jnp.where(lane < 64, o2[:tq], o2[tq:])


def _last_tile(i, tq, tk, qoff, length, chunked):
    qend = qoff + i * tq + tq - 1
    kmax = (qend // CHUNK + 1) * CHUNK if chunked else qend + 1
    kmax = jnp.minimum(kmax, length)
    return (kmax - 1) // tk


def _tri_steps(nq, tq, tk, qoff, length, chunked):
    ii, jj = [], []
    for i in range(nq):
        qend = qoff + i * tq + tq - 1
        kmax = min((qend // CHUNK + 1) * CHUNK if chunked else qend + 1, length)
        for j in range((kmax - 1) // tk + 1):
            ii.append(i)
            jj.append(j)
    return jnp.asarray(ii, I32), jnp.asarray(jj, I32)


def _tri_call(kernel_fn, steps, batch, in_specs, out_specs, out_shape, scratch, name):
    ii, jj = steps
    grid_spec = pltpu.PrefetchScalarGridSpec(
        num_scalar_prefetch=2, grid=(batch, ii.shape[0]),
        in_specs=in_specs, out_specs=out_specs, scratch_shapes=scratch)
    call = pl.pallas_call(kernel_fn, grid_spec=grid_spec, out_shape=out_shape,
                          compiler_params=_params(("parallel", "arbitrary")), name=name)
    return functools.partial(call, ii, jj)


def _flash_init(q_ref, aug_ref, qst_ref, m_ref, l_ref, acc_ref, tq):
    q = q_ref[0]
    for g in range(qst_ref.shape[0]):
        st = _pair_stack(q[:, g * LANES:(g + 1) * LANES], tq)
        if aug_ref is not None:
            a = aug_ref[0, 0]
            ext = jnp.concatenate([jnp.broadcast_to(a[2 * g:2 * g + 1], (tq, LANES)),
                                   jnp.broadcast_to(a[2 * g + 1:2 * g + 2], (tq, LANES))], axis=0)
            st = jnp.concatenate([st, ext.astype(CDT)], axis=1)
        qst_ref[g] = st
    m_ref[...] = jnp.full(m_ref.shape, NEG, F32)
    l_ref[...] = jnp.zeros(l_ref.shape, F32)
    acc_ref[...] = jnp.zeros(acc_ref.shape, F32)


def _flash_pairs(score_fn, s_ref, vt_ref, m_ref, l_ref, acc_ref, tq):
    npair = m_ref.shape[0]
    s_ref[0] = score_fn(0)
    for g in range(npair):
        if g + 1 < npair:
            s_ref[(g + 1) % 2] = score_fn(g + 1)
        _flash_step(s_ref[g % 2], vt_ref, m_ref, l_ref, acc_ref, g, tq)


def _flash_step(st, vt_ref, m_ref, l_ref, acc_ref, g, tq):
    m_old = m_ref[g]
    m_new = jnp.maximum(m_old, jnp.max(st, axis=0, keepdims=True))
    alpha = jnp.exp2(m_old - m_new)
    p = jnp.exp2(st - m_new)
    l_ref[g] = alpha * l_ref[g] + jnp.sum(p, axis=0, keepdims=True)
    m_ref[g] = m_new
    pc = p.astype(CDT)
    for e in range(2):
        rows = slice(g * LANES + e * HEAD_DIM, g * LANES + (e + 1) * HEAD_DIM)
        cols = slice(e * tq, (e + 1) * tq)
        acc_ref[rows, :] = alpha[:, cols] * acc_ref[rows, :] + _mm(vt_ref[0, rows, :], pc[:, cols])


def _flash_finish(o_ref, l_ref, acc_ref, tq):
    for g in range(l_ref.shape[0]):
        l = l_ref[g]
        den = jnp.concatenate([jnp.broadcast_to(l[:, :tq], (HEAD_DIM, tq)),
                               jnp.broadcast_to(l[:, tq:], (HEAD_DIM, tq))], axis=0)
        o = acc_ref[g * LANES:(g + 1) * LANES, :] / den
        o_ref[0, :, g * LANES:(g + 1) * LANES] = o.T.astype(o_ref.dtype)


def _flash_scratch(npair, tq, tk, kd, w):
    return [pltpu.VMEM((npair, 2 * tq, kd), CDT),
            pltpu.VMEM((npair, 1, 2 * tq), F32),
            pltpu.VMEM((npair, 1, 2 * tq), F32),
            pltpu.VMEM((w, tq), F32),
            pltpu.VMEM((2, tk, 2 * tq), F32)]


def _band_kernel(*refs, n_kv, tq, mask_neg):
    q_ref = refs[0]
    k_refs = refs[1:1 + n_kv]
    v_refs = refs[1 + n_kv:1 + 2 * n_kv]
    bias_ref = refs[1 + 2 * n_kv]
    o_ref = refs[2 + 2 * n_kv]
    i = pl.program_id(1)
    q = q_ref[0]
    cat = lambda rs: (rs[0][0].astype(CDT) if len(rs) == 1
                      else jnp.concatenate([r[0].astype(CDT) for r in rs], axis=0))
    k = cat(k_refs)
    v = cat(v_refs)
    tk = bias_ref.shape[2]
    if k.shape[0] < tk:
        zpad = jnp.zeros((tk - k.shape[0], k.shape[1]), CDT)
        k = jnp.concatenate([k, zpad], axis=0)
        v = jnp.concatenate([v, zpad], axis=0)
    if mask_neg:
        kpos = lax.broadcasted_iota(I32, (2 * tq, tk), 1) + (i * tq - A_PAST)
        neg = kpos < 0
    for g in range(q.shape[1] // LANES):
        sl = slice(g * LANES, (g + 1) * LANES)
        s = _mm_nt(_pair_stack(q[:, sl], tq), k[:, sl])
        s = s + jnp.concatenate([bias_ref[2 * g], bias_ref[2 * g + 1]], axis=0)
        if mask_neg:
            s = jnp.where(neg, NEG, s)
        m = jnp.max(s, axis=-1, keepdims=True)
        p = jnp.exp(s - m)
        l = jnp.sum(p, axis=-1, keepdims=True)
        o2 = _mm(p.astype(CDT), v[:, sl]) / l
        o_ref[0, :, sl] = _pair_merge(o2, tq).astype(o_ref.dtype)


def _band(q, k_parts, v_parts, k_specs, bias, tq, mask_neg):
    b, s, w = q.shape
    n_kv = len(k_parts)
    q_spec = pl.BlockSpec((1, tq, w), lambda bb, i: (bb, i, 0))
    return pl.pallas_call(
        functools.partial(_band_kernel, n_kv=n_kv, tq=tq, mask_neg=mask_neg),
        grid=(b, s // tq),
        in_specs=[q_spec] + k_specs + k_specs + [_const_spec(bias.shape)],
        out_specs=q_spec,
        out_shape=jax.ShapeDtypeStruct((b, s, w), CDT),
        compiler_params=_params(("parallel", "parallel")),
        name="band_attn",
    )(q, *k_parts, *v_parts, bias)


def _band_bias(rel_bias, q_pos, k_pos):
    nq, nk = len(q_pos), len(k_pos)
    qc = q_pos[:, None] // CHUNK
    kc = k_pos[None, :] // CHUNK
    ok = (kc <= qc) & (kc >= qc - A_PAST // CHUNK)
    n = nq + nk - 1
    d0 = int(q_pos[0] - k_pos[0])
    m = np.arange(n + 1)
    rel = np.where(m < nk, d0 - m, d0 - m + n + 1)
    table = jnp.take(rel_bias.astype(F32), jnp.asarray(np.clip(rel, -REL_CLIP, REL_CLIP) + REL_CLIP), axis=1)
    bias = jnp.tile(table, (1, nq))[:, :nq * n].reshape(-1, nq, n)[:, :, :nk]
    return jnp.where(jnp.asarray(ok)[None], bias, NEG)


def _dsa_index_kernel(ii_ref, jj_ref, qi_ref, ki2_ref, wt_ref, keys_ref, t_ref, c_ref, qst_ref, kscr_ref,
                      *, tq, tk, nk, unroll, qoff, length, topk, idx_bits):
    i = ii_ref[pl.program_id(1)]
    j = jj_ref[pl.program_id(1)]
    last = _last_tile(i, tq, tk, qoff, length, True)
    q0 = qoff + i * tq

    @pl.when(j == 0)
    def _():
        q = qi_ref[0]
        for g in range(IDX_HEADS // 2):
            qst_ref[2 * g * tq:(2 * g + 2) * tq, :] = _pair_stack(q[:, g * LANES:(g + 1) * LANES], tq)

    @pl.when(j <= last)
    def _():
        lg = _mm_nt(ki2_ref[0], qst_ref[...])
        wgt = wt_ref[0] * (IDX_HEADS ** -0.5)
        sc = jnp.zeros((tk, tq), F32)
        for h in range(IDX_HEADS):
            sc = sc + jnp.maximum(lg[:, h * tq:(h + 1) * tq], 0.0) * wgt[h:h + 1, :]
        bits = lax.bitcast_convert_type(sc, I32)
        key = bits ^ ((bits >> 31) & 0x7FFFFFFF)
        kpos = j * tk + lax.broadcasted_iota(I32, (tk, tq), 0)
        qpos = q0 + lax.broadcasted_iota(I32, (tk, tq), 1)
        ok = ((kpos >> 6) <= (qpos >> 6)) & (kpos < length)
        key = jnp.where(ok, key, INT_MIN)
        keys_ref[0] = key
        kscr_ref[j] = key
        if unroll > 1:
            @pl.when(j + 1 < nk)
            def _():
                kscr_ref[j + 1] = jnp.full((tk, tq), INT_MIN, I32)

    @pl.when(j == last)
    def _():
        def count(pred):
            def body(t, acc):
                for u in range(unroll):
                    tt = unroll * t + u
                    kpos = tt * tk + lax.broadcasted_iota(I32, (tk, tq), 0)
                    m = jnp.where(pred(kscr_ref[tt], kpos), 1.0, 0.0)
                    acc = acc + jnp.sum(m, axis=0, keepdims=True)
                return acc
            return lax.fori_loop(0, (last + unroll) // unroll, body, jnp.zeros((1, tq), F32))

        kf = float(topk)
        qpos = q0 + lax.broadcasted_iota(I32, (1, tq), 1)
        n_adm = jnp.minimum(((qpos >> 6) + 1) << 6, length).astype(F32)
        take_all = n_adm < kf

        def bit_cond(carry):
            return (carry[0] < 32) & (carry[3] == 0)

        def bit_body(carry):
            it, prefix, nge, _ = carry
            cand = prefix + lax.shift_left(jnp.int32(1), 31 - it)
            cnt = count(lambda kt, _: kt >= cand)
            take = cnt >= kf
            prefix = jnp.where(take, cand, prefix)
            nge = jnp.where(take, cnt, nge)
            settled = jnp.min(jnp.where(take_all | (nge == kf), 1.0, 0.0))
            return it + 1, prefix, nge, (settled > 0.0).astype(I32)

        _, prefix, nge, _ = lax.while_loop(
            bit_cond, bit_body,
            (jnp.int32(0), jnp.full((1, tq), INT_MIN, I32), jnp.zeros((1, tq), F32), jnp.int32(0)))
        found = prefix > INT_MIN
        thr = jnp.maximum(prefix, INT_MIN + 1)
        t_ref[0] = thr
        c_ref[0] = jnp.full((1, tq), 2 ** 30, I32)
        straddle = jnp.max(jnp.where(found & (nge > kf), 1.0, 0.0))

        @pl.when(straddle > 0.0)
        def _():
            need = kf - count(lambda kt, _: kt > thr)

            def tie_body(it, x):
                cx = x + lax.shift_left(jnp.int32(1), idx_bits - 1 - it)
                g = count(lambda kt, kpos: jnp.where(kt == thr, kpos, INT_MAX) < cx)
                return jnp.where(g < need, cx, x)
            c_ref[0] = lax.fori_loop(0, idx_bits, tie_body, jnp.zeros((1, tq), I32))


def _dsa_index(qi, ki2, wt, *, tq, tk, qoff, length, topk):
    b, s, _ = qi.shape
    lp = ki2.shape[1]
    nq, nk = s // tq, lp // tk
    idx_bits = max(1, int(np.ceil(np.log2(lp))))
    qvec = pl.BlockSpec((1, 1, tq), lambda bb, t, ii, jj: (bb, 0, ii[t]))
    return _tri_call(
        functools.partial(_dsa_index_kernel, tq=tq, tk=tk, nk=nk, unroll=2 if nk % 2 == 0 else 1,
                          qoff=qoff, length=length, topk=topk, idx_bits=idx_bits),
        _tri_steps(nq, tq, tk, qoff, length, True), b,
        in_specs=[pl.BlockSpec((1, tq, 512), lambda bb, t, ii, jj: (bb, ii[t], 0)),
                  pl.BlockSpec((1, tk, LANES), lambda bb, t, ii, jj: (bb, jj[t], 0)),
                  pl.BlockSpec((1, IDX_HEADS, tq), lambda bb, t, ii, jj: (bb, 0, ii[t]))],
        out_specs=[pl.BlockSpec((1, tk, tq), lambda bb, t, ii, jj: (bb, jj[t], ii[t])), qvec, qvec],
        out_shape=[jax.ShapeDtypeStruct((b, lp, s), I32),
                   jax.ShapeDtypeStruct((b, 1, s), I32),
                   jax.ShapeDtypeStruct((b, 1, s), I32)],
        scratch=[pltpu.VMEM((IDX_HEADS * tq, LANES), CDT),
                 pltpu.VMEM((nk, tk, tq), I32)],
        name="dsa_index",
    )(qi, ki2, wt)


def _dsa_attn_kernel(ii_ref, jj_ref, q_ref, k_ref, vt_ref, keys_ref, t_ref, c_ref, o_ref,
                     qst_ref, m_ref, l_ref, acc_ref, s_ref, *, tq, tk, qoff, length):
    i = ii_ref[pl.program_id(1)]
    j = jj_ref[pl.program_id(1)]
    last = _last_tile(i, tq, tk, qoff, length, True)

    @pl.when(j == 0)
    def _():
        _flash_init(q_ref, None, qst_ref, m_ref, l_ref, acc_ref, tq)

    @pl.when(j <= last)
    def _():
        kt = keys_ref[0]
        thr = t_ref[0]
        kpos = j * tk + lax.broadcasted_iota(I32, (tk, tq), 0)
        sel = (kt > thr) | (jnp.where(kt == thr, kpos, INT_MAX) <= c_ref[0])
        mb = jnp.where(sel, 0.0, NEG)
        mb2 = jnp.concatenate([mb, mb], axis=1)
        score = lambda g: _mm_nt(k_ref[0, :, g * LANES:(g + 1) * LANES], qst_ref[g]) + mb2
        _flash_pairs(score, s_ref, vt_ref, m_ref, l_ref, acc_ref, tq)

    @pl.when(j == last)
    def _():
        _flash_finish(o_ref, l_ref, acc_ref, tq)


def _dsa_attn(q, k, vt, keys, thr, cut, *, tq, tk, qoff, length):
    b, s, w = q.shape
    qmap = lambda bb, t, ii, jj: (bb, ii[t], 0)
    qvec = pl.BlockSpec((1, 1, tq), lambda bb, t, ii, jj: (bb, 0, ii[t]))
    npair = w // LANES
    return _tri_call(
        functools.partial(_dsa_attn_kernel, tq=tq, tk=tk, qoff=qoff, length=length),
        _tri_steps(s // tq, tq, tk, qoff, length, True), b,
        in_specs=[pl.BlockSpec((1, tq, w), qmap),
                  pl.BlockSpec((1, tk, w), lambda bb, t, ii, jj: (bb, jj[t], 0)),
                  pl.BlockSpec((1, w, tk), lambda bb, t, ii, jj: (bb, 0, jj[t])),
                  pl.BlockSpec((1, tk, tq), lambda bb, t, ii, jj: (bb, jj[t], ii[t])),
                  qvec, qvec],
        out_specs=pl.BlockSpec((1, tq, w), qmap),
        out_shape=jax.ShapeDtypeStruct((b, s, w), CDT),
        scratch=_flash_scratch(npair, tq, tk, LANES, w),
        name="dsa_attn",
    )(q, k, vt, keys, thr, cut)


def _pieces3(x):
    pcs = _pieces(x, 3)
    return pcs + [jnp.zeros_like(pcs[0])] * (3 - len(pcs))


FOX_EXT = 9


def _cumsum_kernel(x_ref, tri_ref, place_ref, ones_ref, kx_ref, carry_ref, *, scale):
    @pl.when(pl.program_id(1) == 0)
    def _():
        carry_ref[...] = jnp.zeros(carry_ref.shape, F32)
    tri = tri_ref[...]
    tot = None
    for p in _pieces(x_ref[0], 3):
        t = _mm(tri, p)
        tot = t if tot is None else tot + t
    acc = carry_ref[...] + tot
    carry_ref[...] = acc[-1:, :]
    ext = None
    for c, p in enumerate(_pieces3(acc * (-scale))):
        t = _mm(p, place_ref[c])
        ext = t if ext is None else ext + t
    kx_ref[0] = (ext + ones_ref[...]).astype(kx_ref.dtype)


def _fox_key_lanes(x, tb, scale, nh):
    b, s, w = x.shape
    npair = nh // 2
    place = np.zeros((3, w, npair * LANES), np.float32)
    ones = np.zeros((1, npair * LANES), np.float32)
    for h in range(nh):
        for c in range(3):
            place[c, h, (h // 2) * LANES + (h % 2) * 6 + c] = 1.0
    for g in range(npair):
        ones[0, g * LANES + 3:g * LANES + 6] = 1.0
    tri = jnp.tril(jnp.ones((tb, tb), F32)).astype(CDT)
    place = jnp.asarray(place, CDT)
    spec = lambda wd: pl.BlockSpec((1, tb, wd), lambda bb, i: (bb, i, 0))
    return pl.pallas_call(
        functools.partial(_cumsum_kernel, scale=scale),
        grid=(b, s // tb),
        in_specs=[spec(w), _const_spec(tri.shape), _const_spec(place.shape), _const_spec(ones.shape)],
        out_specs=spec(npair * LANES),
        out_shape=jax.ShapeDtypeStruct((b, s, npair * LANES), CDT),
        scratch_shapes=[pltpu.VMEM((1, w), F32)],
        compiler_params=_params(("parallel", "arbitrary")),
        name="logf_cumsum",
    )(x, tri, place, jnp.asarray(ones))


def _fox_kernel(ii_ref, jj_ref, q_ref, k_ref, kx_ref, vt_ref, aug_ref, o_ref,
                qst_ref, m_ref, l_ref, acc_ref, s_ref, *, tq, tk, qoff, length):
    i = ii_ref[pl.program_id(1)]
    j = jj_ref[pl.program_id(1)]
    last = _last_tile(i, tq, tk, qoff, length, False)
    q0 = qoff + i * tq

    @pl.when(j == 0)
    def _():
        _flash_init(q_ref, aug_ref, qst_ref, m_ref, l_ref, acc_ref, tq)

    def step(masked):
        if masked:
            kpos = j * tk + lax.broadcasted_iota(I32, (tk, 2 * tq), 0)
            lane = lax.broadcasted_iota(I32, (tk, 2 * tq), 1)
            causal = kpos <= q0 + jnp.where(lane >= tq, lane - tq, lane)
        def score(g):
            sl = slice(g * LANES, (g + 1) * LANES)
            st = _mm_nt(jnp.concatenate([k_ref[0, :, sl], kx_ref[0, :, sl]], axis=1), qst_ref[g])
            return jnp.where(causal, st, NEG) if masked else st
        _flash_pairs(score, s_ref, vt_ref, m_ref, l_ref, acc_ref, tq)

    diag = (j + 1) * tk - 1 > q0

    @pl.when((j <= last) & diag)
    def _():
        step(True)

    @pl.when((j <= last) & jnp.logical_not(diag))
    def _():
        step(False)

    @pl.when(j == last)
    def _():
        _flash_finish(o_ref, l_ref, acc_ref, tq)


def _fox(q, k, kx, vt, qaug, *, tq, tk, qoff, length):
    b, s, w = q.shape
    npair = w // LANES
    qmap = lambda bb, t, ii, jj: (bb, ii[t], 0)
    kmap = lambda bb, t, ii, jj: (bb, jj[t], 0)
    return _tri_call(
        functools.partial(_fox_kernel, tq=tq, tk=tk, qoff=qoff, length=length),
        _tri_steps(s // tq, tq, tk, qoff, length, False), b,
        in_specs=[pl.BlockSpec((1, tq, w), qmap),
                  pl.BlockSpec((1, tk, w), kmap),
                  pl.BlockSpec((1, tk, npair * LANES), kmap),
                  pl.BlockSpec((1, w, tk), lambda bb, t, ii, jj: (bb, 0, jj[t])),
                  pl.BlockSpec((1, 1, 2 * npair, LANES), lambda bb, t, ii, jj: (bb, ii[t], 0, 0))],
        out_specs=pl.BlockSpec((1, tq, w), qmap),
        out_shape=jax.ShapeDtypeStruct((b, s, w), CDT),
        scratch=_flash_scratch(npair, tq, tk, 2 * LANES, w),
        name="fox_attn",
    )(q, k, kx, vt, qaug)


def _fox_qaug(kx0):
    b, nq, _ = kx0.shape
    a = kx0.reshape(b, nq, -1, LANES).astype(F32)
    one, zero = jnp.ones_like(a[..., 0:3]), jnp.zeros_like(a[..., 0:3])
    even = jnp.concatenate([one, -a[..., 0:3], zero], axis=-1)
    odd = jnp.concatenate([zero, -a[..., 6:9], one], axis=-1)
    heads = jnp.stack([even, odd], axis=3).reshape(b, nq, -1, FOX_EXT)
    return jnp.pad(heads, ((0, 0), (0, 0), (0, 0), (0, LANES - FOX_EXT)))


def _rope_tables(pos):
    half = HEAD_DIM // 2
    inv = ROPE_THETA ** (-jnp.arange(half, dtype=F32) / half)
    ang = pos.astype(F32)[:, None] * inv[None, :]
    cos, sin = jnp.cos(ang), jnp.sin(ang)
    return (jnp.tile(jnp.concatenate([cos, cos], axis=1), (1, 2)),
            jnp.tile(jnp.concatenate([-sin, sin], axis=1), (1, 2)))


def _pad_rows(x, n):
    return x if x.shape[1] == n else jnp.pad(x, ((0, 0), (0, n - x.shape[1]), (0, 0)))


def _tiles(s):
    tk = 512 if s % 512 == 0 else s
    tq = 512 if s % 512 == 0 else s
    tqi = 256 if s % 256 == 0 else s
    return tq, tqi, tk


def kernel(x_prompt, x_sample, cache_a_k, cache_a_v, cache_b_k, cache_b_v, cache_b_ik, cache_c_k, cache_c_v,
           cache_c_logf, p_prompt, p_sample, g_mix, w_in_even, qn_a, kn_a, rel_bias_a, qn_b, kn_b, w_out_even,
           w_in_odd, b_f, qn_c, kn_c, w_out_odd, g_mlp, w_up, w_down, g_ple, w_ple_gate, w_ple_proj):
    B, S, D = x_prompt.shape
    DB, n, _ = x_sample.shape
    P = cache_b_k.shape[2]
    W_A = cache_a_k.shape[2]
    W_AP = min(A_PAST, S)
    depth = g_mix.shape[0]
    L = P + n
    Lp = -(-L // LANES) * LANES
    nqp = -(-n // LANES) * LANES
    assert B == 1 and S % 128 == 0 and n % 16 == 0 and P % 16 == 0

    row = lambda v: v.astype(F32).reshape(1, -1)
    bd = jnp.asarray(np.kron(np.eye(256 // HEAD_DIM), np.full((HEAD_DIM, HEAD_DIM), 1.0 / HEAD_DIM)), CDT)
    cos_p, sin_p = _rope_tables(jnp.arange(S))
    cos_s, sin_s = (jnp.tile(t, (DB, 1)) for t in _rope_tables(P + jnp.arange(n)))
    tq, tqi, tk = _tiles(S)
    tr = lambda t: jnp.swapaxes(t, 1, 2)
    catp = lambda c, new: _pad_rows(jnp.concatenate([c.astype(CDT), new], axis=1), Lp)

    outs = {name: [] for name in ("a_k_p", "a_v_p", "b_k_p", "b_v_p", "b_ik_p", "c_k_p", "c_v_p", "c_lf_p",
                                  "a_k_s", "a_v_s", "b_k_s", "b_v_s", "b_ik_s", "c_k_s", "c_v_s", "c_lf_s")}
    h_p = x_prompt.reshape(B * S, D)
    h_s = x_sample.reshape(DB * n, D)
    for i in range(depth):
        g = row(g_mix[i])
        if i % 2 == 0:
            e = i // 2
            wsz = 512
            w = w_in_even[e]
            w_all = jnp.concatenate([w[:, :7 * wsz], jnp.pad(w[:, 7 * wsz:], ((0, 0), (0, LANES - 72)))],
                                    axis=1).astype(CDT)
            gn = jnp.stack([jnp.tile(v[e].astype(F32), wsz // HEAD_DIM) for v in (qn_a, kn_a, qn_b, kn_b)])
            (ka, va, kb, vb, kw, qa_c, ka_c, va_c, qb_c, kb_c, vb_c, qi_c, ki2_c) = _proj_even(
                h_p, g, w_all, gn, bd, cos_p, sin_p)
            r3 = lambda t: t.reshape(B, S, t.shape[-1])
            tqa = 128
            npc = (A_PAST + tqa) // tqa
            kpad = jnp.pad(r3(ka_c), ((0, 0), (A_PAST, 0), (0, 0)))
            vpad = jnp.pad(r3(va_c), ((0, 0), (A_PAST, 0), (0, 0)))
            k_specs = [pl.BlockSpec((1, tqa, wsz), functools.partial(lambda bb, ii, pp: (bb, ii + pp, 0), pp=pp))
                       for pp in range(npc)]
            bias_p = _band_bias(rel_bias_a[e], np.arange(tqa), np.arange(-A_PAST, tqa))
            oa = _band(r3(qa_c), [kpad] * npc, [vpad] * npc, k_specs, bias_p, tqa, True)
            topk_p = min(TOPK_MAX, S // 4)
            wt = tr(r3(kw)[..., 64:64 + IDX_HEADS])
            keys, thr, cut = _dsa_index(r3(qi_c), r3(ki2_c), wt, tq=tqi, tk=tk, qoff=0, length=S, topk=topk_p)
            ob = _dsa_attn(r3(qb_c), r3(kb_c), tr(r3(vb_c)), keys, thr, cut, tq=tq, tk=tk, qoff=0, length=S)
            os_p = [oa.reshape(B * S, wsz), ob.reshape(B * S, wsz)]
            hd = lambda t: t.reshape(B, S, -1, HEAD_DIM)
            outs["a_k_p"].append(hd(ka)[:, S - W_AP:]); outs["a_v_p"].append(hd(va)[:, S - W_AP:])
            outs["b_k_p"].append(hd(kb)); outs["b_v_p"].append(hd(vb))
            outs["b_ik_p"].append(kw[:, :64].reshape(B, S, 64))
            (ka, va, kb, vb, kw, qa_c, ka_c, va_c, qb_c, kb_c, vb_c, qi_c, ki2_c) = _proj_even(
                h_s, g, w_all, gn, bd, cos_s, sin_s)
            r3 = lambda t: t.reshape(DB, n, t.shape[-1])
            ca_k = cache_a_k[e].reshape(DB, W_A, wsz)
            ca_v = cache_a_v[e].reshape(DB, W_A, wsz)
            full = lambda rows, wd: pl.BlockSpec((1, rows, wd), lambda bb, ii: (bb, 0, 0))
            bias_s = _band_bias(rel_bias_a[e], P + np.arange(n), P - W_A + np.arange(W_A + n))
            bias_s = jnp.pad(bias_s, ((0, 0), (0, 0), (0, -(W_A + n) % LANES)), constant_values=NEG)
            oa = _band(r3(qa_c), [ca_k, r3(ka)], [ca_v, r3(va)], [full(W_A, wsz), full(n, wsz)], bias_s, n, False)
            ik = cache_b_ik[e].astype(CDT)
            ki2_all = catp(jnp.concatenate([ik, ik], axis=-1), r3(ki2_c))
            kb_all = catp(cache_b_k[e].reshape(DB, P, wsz), r3(kb_c))
            vb_all = catp(cache_b_v[e].reshape(DB, P, wsz), r3(vb_c))
            topk_s = min(TOPK_MAX, L // 4)
            wt = _pad_rows(r3(kw)[..., 64:64 + IDX_HEADS], nqp)
            keys, thr, cut = _dsa_index(_pad_rows(r3(qi_c), nqp), ki2_all, tr(wt),
                                        tq=nqp, tk=Lp, qoff=P, length=L, topk=topk_s)
            ob = _dsa_attn(_pad_rows(r3(qb_c), nqp), kb_all, tr(vb_all), keys, thr, cut,
                           tq=nqp, tk=Lp, qoff=P, length=L)[:, :n]
            os_s = [oa.reshape(DB * n, wsz), ob.reshape(DB * n, wsz)]
            hd = lambda t: t.reshape(DB, n, -1, HEAD_DIM)
            outs["a_k_s"].append(jnp.concatenate([cache_a_k[e], hd(ka)], axis=1)[:, n:])
            outs["a_v_s"].append(jnp.concatenate([cache_a_v[e], hd(va)], axis=1)[:, n:])
            outs["b_k_s"].append(hd(kb)); outs["b_v_s"].append(hd(vb))
            outs["b_ik_s"].append(kw[:, :64].reshape(DB, n, 64))
            w_out = w_out_even[e].astype(CDT)
            wos = [w_out[:wsz], w_out[wsz:]]
        else:
            o = i // 2
            nh = b_f.shape[1]
            wsz = nh * HEAD_DIM
            w = w_in_odd[o]
            w_all = jnp.concatenate([w[:, :3 * wsz], jnp.pad(w[:, 3 * wsz:], ((0, 0), (0, LANES - nh)))],
                                    axis=1).astype(CDT)
            gn = jnp.stack([jnp.tile(v[o].astype(F32), wsz // HEAD_DIM) for v in (qn_c, kn_c)])
            bf = jnp.pad(b_f[o].astype(F32), (0, LANES - nh)).reshape(1, LANES)
            k, v, lf, q_c, k_c, v_c = _proj_odd(h_p, g, w_all, gn, bf, bd)
            r3 = lambda t: t.reshape(B, S, t.shape[-1])
            kx = _fox_key_lanes(r3(lf), 512 if S % 512 == 0 else S, LOG2E, nh)
            oc = _fox(r3(q_c), r3(k_c), kx, tr(r3(v_c)), _fox_qaug(kx[:, ::tq]), tq=tq, tk=tk, qoff=0, length=S)
            os_p = [oc.reshape(B * S, wsz)]
            hd = lambda t: t.reshape(B, S, nh, HEAD_DIM)
            outs["c_k_p"].append(hd(k)); outs["c_v_p"].append(hd(v)); outs["c_lf_p"].append(lf[:, :nh].reshape(B, S, nh))
            k, v, lf, q_c, k_c, v_c = _proj_odd(h_s, g, w_all, gn, bf, bd)
            r3 = lambda t: t.reshape(DB, n, t.shape[-1])
            lf_all = jnp.concatenate([jnp.pad(cache_c_logf[o].astype(F32), ((0, 0), (0, 0), (0, LANES - nh))), r3(lf)],
                                     axis=1)
            kx = _fox_key_lanes(lf_all, L, LOG2E, nh)
            k_all = catp(cache_c_k[o].reshape(DB, P, wsz), r3(k_c))
            v_all = catp(cache_c_v[o].reshape(DB, P, wsz), r3(v_c))
            oc = _fox(_pad_rows(r3(q_c), nqp), k_all, _pad_rows(kx, Lp), tr(v_all),
                      _fox_qaug(kx[:, P:P + 1]), tq=nqp, tk=Lp, qoff=P, length=L)[:, :n]
            os_s = [oc.reshape(DB * n, wsz)]
            hd = lambda t: t.reshape(DB, n, nh, HEAD_DIM)
            outs["c_k_s"].append(hd(k)); outs["c_v_s"].append(hd(v)); outs["c_lf_s"].append(lf[:, :nh].reshape(DB, n, nh))
            w_out = w_out_odd[o].astype(CDT)
            wos = [w_out]
        post_w = (row(g_mlp[i]), w_up[i].astype(CDT), w_down[i].astype(CDT), row(g_ple[i]),
                  w_ple_gate[i].astype(CDT), w_ple_proj[i].astype(CDT))
        h_p = _post(h_p, os_p, p_prompt[i].reshape(B * S, -1), wos, *post_w)
        h_s = _post(h_s, os_s, p_sample[i].reshape(DB * n, -1), wos, *post_w)

    st = jnp.stack
    names = ("a_k_p", "a_v_p", "b_k_p", "b_v_p", "b_ik_p", "c_k_p", "c_v_p", "c_lf_p",
             "a_k_s", "a_v_s", "b_k_s", "b_v_s", "b_ik_s", "c_k_s", "c_v_s", "c_lf_s")
    return (h_p.reshape(B, S, D), h_s.reshape(DB, n, D)) + tuple(st(outs[nm]) for nm in names)
```

```python
import functools

import numpy as np
import jax
import jax.numpy as jnp
from jax import lax
from jax.experimental import pallas as pl
from jax.experimental.pallas import tpu as pltpu

F32 = jnp.float32
I32 = jnp.int32
CDT = jnp.bfloat16

CHUNK = 64
HEAD_DIM = 64
IDX_HEADS = 8
A_PAST = 8 * CHUNK
REL_CLIP = 128
TOPK_MAX = 256
ROPE_THETA = 10000.0
RMS_EPS = 1e-6
LANES = 128
NEG = -1e30
INT_MIN = -(2 ** 31)
INT_MAX = 2 ** 31 - 1
LOG2E = 1.4426950408889634
VMEM_LIMIT = 56 * 1024 * 1024


def _mm(a, b):
    return jnp.dot(a, b, preferred_element_type=F32)


def _mm_nt(a, b):
    return lax.dot_general(a, b, (((1,), (1,)), ((), ())), preferred_element_type=F32)


def _pieces(x, n):
    if CDT == F32:
        return [x]
    out = []
    for _ in range(n - 1):
        p = x.astype(CDT)
        out.append(p)
        x = x - p.astype(F32)
    out.append(x.astype(CDT))
    return out


def _rms(x, g):
    return x * lax.rsqrt(jnp.mean(x * x, axis=-1, keepdims=True) + RMS_EPS) * g


def _head_rms(x, bd, g):
    x2 = x * x
    pcs = _pieces(x2, 2)
    cols = []
    for s in range(x.shape[1] // 256):
        sl = slice(s * 256, (s + 1) * 256)
        ms = _mm(pcs[0][:, sl], bd)
        for p in pcs[1:]:
            ms = ms + _mm(p[:, sl], bd)
        cols.append(ms)
    ms = cols[0] if len(cols) == 1 else jnp.concatenate(cols, axis=1)
    return x * lax.rsqrt(ms + RMS_EPS) * g


def _rope(x, cos, sin):
    w = x.shape[1]
    lane = lax.broadcasted_iota(I32, x.shape, 1)
    first = (lane & 63) < 32
    swapped = jnp.where(first, pltpu.roll(x, w - 32, 1), pltpu.roll(x, 32, 1))
    return x * cos + swapped * sin


def _tile_lanes(t, w):
    return t if w == t.shape[1] else jnp.concatenate([t] * (w // t.shape[1]), axis=1)


def _proj_even_kernel(h_ref, g_ref, w_ref, gn_ref, bd_ref, cos_ref, sin_ref,
                      ka_ref, va_ref, kb_ref, vb_ref, kw_ref,
                      qa_c, ka_c, va_c, qb_c, kb_c, vb_c, qi_c, ki2_c):
    a = _rms(h_ref[...], g_ref[...]).astype(CDT)
    z = _mm(a, w_ref[...])
    w = 512
    bd = bd_ref[...]
    gn = gn_ref[...]
    cos1, sin1 = cos_ref[...], sin_ref[...]
    cos, sin = _tile_lanes(cos1, w), _tile_lanes(sin1, w)
    scale = HEAD_DIM ** -0.5

    qa = _head_rms(z[:, 0:w], bd, gn[0:1])
    qa_c[...] = (qa * scale).astype(CDT)
    ka = _head_rms(z[:, w:2 * w], bd, gn[1:2])
    ka_ref[...] = ka
    ka_c[...] = ka.astype(CDT)
    va = z[:, 2 * w:3 * w]
    va_ref[...] = va
    va_c[...] = va.astype(CDT)
    qb = _rope(_head_rms(z[:, 3 * w:4 * w], bd, gn[2:3]), cos, sin)
    qb_c[...] = (qb * (scale * LOG2E)).astype(CDT)
    kb = _rope(_head_rms(z[:, 4 * w:5 * w], bd, gn[3:4]), cos, sin)
    kb_ref[...] = kb
    kb_c[...] = kb.astype(CDT)
    vb = z[:, 5 * w:6 * w]
    vb_ref[...] = vb
    vb_c[...] = vb.astype(CDT)
    qi = _rope(z[:, 6 * w:7 * w], cos, sin)
    qi_c[...] = (qi * scale).astype(CDT)
    kw = z[:, 7 * w:7 * w + LANES]
    kwr = _rope(kw, cos1, sin1)
    lane = lax.broadcasted_iota(I32, kw.shape, 1)
    kw_ref[...] = jnp.where(lane < 64, kwr, kw)
    ki2_c[...] = jnp.where(lane < 64, kwr, pltpu.roll(kwr, 64, 1)).astype(CDT)


def _proj_odd_kernel(h_ref, g_ref, w_ref, gn_ref, bf_ref, bd_ref,
                     k_ref, v_ref, lf_ref, q_c, k_c, v_c):
    a = _rms(h_ref[...], g_ref[...]).astype(CDT)
    z = _mm(a, w_ref[...])
    w = 1024
    bd = bd_ref[...]
    gn = gn_ref[...]
    q = _head_rms(z[:, 0:w], bd, gn[0:1])
    q_c[...] = (q * (HEAD_DIM ** -0.5 * LOG2E)).astype(CDT)
    k = _head_rms(z[:, w:2 * w], bd, gn[1:2])
    k_ref[...] = k
    k_c[...] = k.astype(CDT)
    v = z[:, 2 * w:3 * w]
    v_ref[...] = v
    v_c[...] = v.astype(CDT)
    fl = z[:, 3 * w:3 * w + LANES] + bf_ref[...]
    lf_ref[...] = jnp.minimum(fl, 0.0) - jnp.log1p(jnp.exp(-jnp.abs(fl)))


def _const_spec(shape):
    nd = len(shape)
    return pl.BlockSpec(shape, lambda *_: (0,) * nd, pipeline_mode=pl.Buffered(1))


def _row_spec(tm, w):
    return pl.BlockSpec((tm, w), lambda i: (i, 0))


def _params(sem):
    return pltpu.CompilerParams(dimension_semantics=sem, vmem_limit_bytes=VMEM_LIMIT)


def _row_tile(rows, pref):
    return pref if rows % pref == 0 else rows


def _proj_even(h, g, w_all, gn, bd, cos, sin):
    rows, d = h.shape
    tm = _row_tile(rows, 256)
    f = lambda w, dt: jax.ShapeDtypeStruct((rows, w), dt)
    out_shape = [f(512, F32)] * 4 + [f(LANES, F32)] + [f(512, CDT)] * 7 + [f(LANES, CDT)]
    out_specs = [_row_spec(tm, 512)] * 4 + [_row_spec(tm, LANES)] + [_row_spec(tm, 512)] * 7 + [_row_spec(tm, LANES)]
    return pl.pallas_call(
        _proj_even_kernel,
        grid=(rows // tm,),
        in_specs=[_row_spec(tm, d), _const_spec(g.shape), _const_spec(w_all.shape), _const_spec(gn.shape),
                  _const_spec(bd.shape), _row_spec(tm, LANES), _row_spec(tm, LANES)],
        out_specs=out_specs,
        out_shape=out_shape,
        compiler_params=_params(("parallel",)),
        name="proj_even",
    )(h, g, w_all, gn, bd, cos, sin)


def _proj_odd(h, g, w_all, gn, bf, bd):
    rows, d = h.shape
    tm = _row_tile(rows, 256)
    f = lambda w, dt: jax.ShapeDtypeStruct((rows, w), dt)
    out_shape = [f(1024, F32)] * 2 + [f(LANES, F32)] + [f(1024, CDT)] * 3
    out_specs = [_row_spec(tm, 1024)] * 2 + [_row_spec(tm, LANES)] + [_row_spec(tm, 1024)] * 3
    return pl.pallas_call(
        _proj_odd_kernel,
        grid=(rows // tm,),
        in_specs=[_row_spec(tm, d), _const_spec(g.shape), _const_spec(w_all.shape), _const_spec(gn.shape),
                  _const_spec(bf.shape), _const_spec(bd.shape)],
        out_specs=out_specs,
        out_shape=out_shape,
        compiler_params=_params(("parallel",)),
        name="proj_odd",
    )(h, g, w_all, gn, bf, bd)


def _post_kernel(*refs, n_o, ff_chunk):
    h_ref = refs[0]
    o_refs = refs[1:1 + n_o]
    p_ref = refs[1 + n_o]
    wo_refs = refs[2 + n_o:2 + 2 * n_o]
    gm_ref, wu_ref, wd_ref, gp_ref, wg_ref, wp_ref, out_ref = refs[2 + 2 * n_o:]
    mix = None
    for o_ref, wo_ref in zip(o_refs, wo_refs):
        t = _mm(o_ref[...], wo_ref[...])
        mix = t if mix is None else mix + t
    h = h_ref[...] + mix
    m = _rms(h, gm_ref[...]).astype(CDT)
    mlp = None
    d_ff = wu_ref.shape[1]
    for c in range(d_ff // ff_chunk):
        sl = slice(c * ff_chunk, (c + 1) * ff_chunk)
        u = jnp.square(jnp.maximum(_mm(m, wu_ref[:, sl]), 0.0)).astype(CDT)
        t = _mm(u, wd_ref[sl, :])
        mlp = t if mlp is None else mlp + t
    acc = h + mlp
    gate_in = _rms(acc, gp_ref[...]).astype(CDT)
    gate = 1.0 / (1.0 + jnp.exp(-_mm(gate_in, wg_ref[...])))
    out_ref[...] = acc + gate * _mm(p_ref[...].astype(CDT), wp_ref[...])


def _post(h, os_, p, wos, gm, wu, wd, gp, wg, wp):
    rows, d = h.shape
    tm = _row_tile(rows, 512)
    n_o = len(os_)
    in_specs = ([_row_spec(tm, d)] + [_row_spec(tm, o.shape[1]) for o in os_] + [_row_spec(tm, p.shape[1])]
                + [_const_spec(w.shape) for w in wos]
                + [_const_spec(x.shape) for x in (gm, wu, wd, gp, wg, wp)])
    return pl.pallas_call(
        functools.partial(_post_kernel, n_o=n_o, ff_chunk=1024),
        grid=(rows // tm,),
        in_specs=in_specs,
        out_specs=_row_spec(tm, d),
        out_shape=jax.ShapeDtypeStruct((rows, d), F32),
        compiler_params=_params(("parallel",)),
        name="post",
    )(h, *os_, p, *wos, gm, wu, wd, gp, wg, wp)


def _pair_stack(qp, tq):
    lane = lax.broadcasted_iota(I32, qp.shape, 1)
    zero = jnp.zeros_like(qp)
    return jnp.concatenate([jnp.where(lane < 64, qp, zero), jnp.where(lane >= 64, qp, zero)], axis=0)


def _pair_merge(o2, tq):
    lane = lax.broadcasted_iota(I32, (tq, LANES), 1)
    return jnp.where(lane < 64, o2[:tq], o2[tq:])


def _last_tile(i, tq, tk, qoff, length, chunked):
    qend = qoff + i * tq + tq - 1
    kmax = (qend // CHUNK + 1) * CHUNK if chunked else qend + 1
    kmax = jnp.minimum(kmax, length)
    return (kmax - 1) // tk


def _tri_steps(nq, tq, tk, qoff, length, chunked):
    ii, jj = [], []
    for i in range(nq):
        qend = qoff + i * tq + tq - 1
        kmax = min((qend // CHUNK + 1) * CHUNK if chunked else qend + 1, length)
        for j in range((kmax - 1) // tk + 1):
            ii.append(i)
            jj.append(j)
    return jnp.asarray(ii, I32), jnp.asarray(jj, I32)


def _tri_call(kernel_fn, steps, batch, in_specs, out_specs, out_shape, scratch, name):
    ii, jj = steps
    grid_spec = pltpu.PrefetchScalarGridSpec(
        num_scalar_prefetch=2, grid=(batch, ii.shape[0]),
        in_specs=in_specs, out_specs=out_specs, scratch_shapes=scratch)
    call = pl.pallas_call(kernel_fn, grid_spec=grid_spec, out_shape=out_shape,
                          compiler_params=_params(("parallel", "arbitrary")), name=name)
    return functools.partial(call, ii, jj)


def _flash_init(q_ref, aug_ref, qst_ref, m_ref, l_ref, acc_ref, tq):
    q = q_ref[0]
    for g in range(qst_ref.shape[0]):
        st = _pair_stack(q[:, g * LANES:(g + 1) * LANES], tq)
        if aug_ref is not None:
            a = aug_ref[0, 0]
            ext = jnp.concatenate([jnp.broadcast_to(a[2 * g:2 * g + 1], (tq, LANES)),
                                   jnp.broadcast_to(a[2 * g + 1:2 * g + 2], (tq, LANES))], axis=0)
            st = jnp.concatenate([st, ext.astype(CDT)], axis=1)
        qst_ref[g] = st
    m_ref[...] = jnp.full(m_ref.shape, NEG, F32)
    l_ref[...] = jnp.zeros(l_ref.shape, F32)
    acc_ref[...] = jnp.zeros(acc_ref.shape, F32)


def _flash_pairs(score_fn, s_ref, vt_ref, m_ref, l_ref, acc_ref, tq, mxu_sum):
    npair = m_ref.shape[0]
    s_ref[0] = score_fn(0)
    for g in range(npair):
        if g + 1 < npair:
            s_ref[(g + 1) % 2] = score_fn(g + 1)
        _flash_step(s_ref[g % 2], vt_ref, m_ref, l_ref, acc_ref, g, tq, mxu_sum)


def _flash_step(st, vt_ref, m_ref, l_ref, acc_ref, g, tq, mxu_sum):
    m_old = m_ref[g]
    m_new = jnp.maximum(m_old, jnp.max(st, axis=0, keepdims=True))
    alpha = jnp.exp2(m_old - m_new)
    m_ref[g] = m_new
    if mxu_sum:
        pc = jnp.exp2((st - m_new).astype(CDT))
        ones = jnp.ones((16, st.shape[0]), CDT)
    else:
        p = jnp.exp2(st - m_new)
        pc = p.astype(CDT)
    sums = []
    for e in range(2):
        rows = slice(g * LANES + e * HEAD_DIM, g * LANES + (e + 1) * HEAD_DIM)
        cols = slice(e * tq, (e + 1) * tq)
        vt = vt_ref[0, rows, :]
        pv = _mm(jnp.concatenate([vt, ones], axis=0) if mxu_sum else vt, pc[:, cols])
        acc_ref[rows, :] = alpha[:, cols] * acc_ref[rows, :] + pv[:HEAD_DIM]
        if mxu_sum:
            sums.append(pv[HEAD_DIM:HEAD_DIM + 1])
    psum = jnp.concatenate(sums, axis=1) if mxu_sum else jnp.sum(p, axis=0, keepdims=True)
    l_ref[g] = alpha * l_ref[g] + psum


def _flash_finish(o_ref, l_ref, acc_ref, tq):
    for g in range(l_ref.shape[0]):
        l = l_ref[g]
        den = jnp.concatenate([jnp.broadcast_to(l[:, :tq], (HEAD_DIM, tq)),
                               jnp.broadcast_to(l[:, tq:], (HEAD_DIM, tq))], axis=0)
        o = acc_ref[g * LANES:(g + 1) * LANES, :] / den
        o_ref[0, :, g * LANES:(g + 1) * LANES] = o.T.astype(o_ref.dtype)


def _flash_scratch(npair, tq, tk, kd, w):
    return [pltpu.VMEM((npair, 2 * tq, kd), CDT),
            pltpu.VMEM((npair, 1, 2 * tq), F32),
            pltpu.VMEM((npair, 1, 2 * tq), F32),
            pltpu.VMEM((w, tq), F32),
            pltpu.VMEM((2, tk, 2 * tq), F32)]


def _band_kernel(*refs, n_kv, tq, mask_neg):
    q_ref = refs[0]
    k_refs = refs[1:1 + n_kv]
    v_refs = refs[1 + n_kv:1 + 2 * n_kv]
    bias_ref = refs[1 + 2 * n_kv]
    o_ref = refs[2 + 2 * n_kv]
    i = pl.program_id(1)
    q = q_ref[0]
    cat = lambda rs: (rs[0][0].astype(CDT) if len(rs) == 1
                      else jnp.concatenate([r[0].astype(CDT) for r in rs], axis=0))
    k = cat(k_refs)
    v = cat(v_refs)
    tk = bias_ref.shape[2]
    if k.shape[0] < tk:
        zpad = jnp.zeros((tk - k.shape[0], k.shape[1]), CDT)
        k = jnp.concatenate([k, zpad], axis=0)
        v = jnp.concatenate([v, zpad], axis=0)
    if mask_neg:
        kpos = lax.broadcasted_iota(I32, (2 * tq, tk), 1) + (i * tq - A_PAST)
        neg = kpos < 0
    for g in range(q.shape[1] // LANES):
        sl = slice(g * LANES, (g + 1) * LANES)
        s = _mm_nt(_pair_stack(q[:, sl], tq), k[:, sl])
        s = s + jnp.concatenate([bias_ref[2 * g], bias_ref[2 * g + 1]], axis=0)
        if mask_neg:
            s = jnp.where(neg, NEG, s)
        m = jnp.max(s, axis=-1, keepdims=True)
        p = jnp.exp(s - m)
        l = jnp.sum(p, axis=-1, keepdims=True)
        o2 = _mm(p.astype(CDT), v[:, sl]) / l
        o_ref[0, :, sl] = _pair_merge(o2, tq).astype(o_ref.dtype)


def _band(q, k_parts, v_parts, k_specs, bias, tq, mask_neg):
    b, s, w = q.shape
    n_kv = len(k_parts)
    q_spec = pl.BlockSpec((1, tq, w), lambda bb, i: (bb, i, 0))
    return pl.pallas_call(
        functools.partial(_band_kernel, n_kv=n_kv, tq=tq, mask_neg=mask_neg),
        grid=(b, s // tq),
        in_specs=[q_spec] + k_specs + k_specs + [_const_spec(bias.shape)],
        out_specs=q_spec,
        out_shape=jax.ShapeDtypeStruct((b, s, w), CDT),
        compiler_params=_params(("parallel", "parallel")),
        name="band_attn",
    )(q, *k_parts, *v_parts, bias)


def _band_bias(rel_bias, q_pos, k_pos):
    nq, nk = len(q_pos), len(k_pos)
    qc = q_pos[:, None] // CHUNK
    kc = k_pos[None, :] // CHUNK
    ok = (kc <= qc) & (kc >= qc - A_PAST // CHUNK)
    n = nq + nk - 1
    d0 = int(q_pos[0] - k_pos[0])
    m = np.arange(n + 1)
    rel = np.where(m < nk, d0 - m, d0 - m + n + 1)
    table = jnp.take(rel_bias.astype(F32), jnp.asarray(np.clip(rel, -REL_CLIP, REL_CLIP) + REL_CLIP), axis=1)
    bias = jnp.tile(table, (1, nq))[:, :nq * n].reshape(-1, nq, n)[:, :, :nk]
    return jnp.where(jnp.asarray(ok)[None], bias, NEG)


def _dsa_index_kernel(ii_ref, jj_ref, qi_ref, ki2_ref, wt_ref, keys_ref, t_ref, c_ref, qst_ref, kscr_ref,
                      *, tq, tk, nk, unroll, qoff, length, topk, idx_bits, nreal):
    i = ii_ref[pl.program_id(1)]
    j = jj_ref[pl.program_id(1)]
    last = _last_tile(i, tq, tk, qoff, length, True)
    q0 = qoff + i * tq

    @pl.when(j == 0)
    def _():
        q = qi_ref[0]
        for g in range(IDX_HEADS // 2):
            qst_ref[2 * g * tq:(2 * g + 2) * tq, :] = _pair_stack(q[:, g * LANES:(g + 1) * LANES], tq)

    @pl.when(j <= last)
    def _():
        lg = _mm_nt(ki2_ref[0], qst_ref[...])
        wgt = wt_ref[0] * (IDX_HEADS ** -0.5)
        sc = jnp.zeros((tk, tq), F32)
        for h in range(IDX_HEADS):
            sc = sc + jnp.maximum(lg[:, h * tq:(h + 1) * tq], 0.0) * wgt[h:h + 1, :]
        bits = lax.bitcast_convert_type(sc, I32)
        key = bits ^ ((bits >> 31) & 0x7FFFFFFF)
        kpos = j * tk + lax.broadcasted_iota(I32, (tk, tq), 0)
        qpos = q0 + lax.broadcasted_iota(I32, (tk, tq), 1)
        ok = ((kpos >> 6) <= (qpos >> 6)) & (kpos < length)
        key = jnp.where(ok, key, INT_MIN)
        keys_ref[0] = key
        kscr_ref[j] = key
        if unroll > 1:
            @pl.when(j + 1 < nk)
            def _():
                kscr_ref[j + 1] = jnp.full((tk, tq), INT_MIN, I32)

    @pl.when(j == last)
    def _():
        def count(pred):
            def body(t, acc):
                for u in range(unroll):
                    tt = unroll * t + u
                    kpos = tt * tk + lax.broadcasted_iota(I32, (tk, tq), 0)
                    m = jnp.where(pred(kscr_ref[tt], kpos), 1.0, 0.0)
                    acc = acc + jnp.sum(m, axis=0, keepdims=True)
                return acc
            return lax.fori_loop(0, (last + unroll) // unroll, body, jnp.zeros((1, tq), F32))

        kf = float(topk)
        qpos = q0 + lax.broadcasted_iota(I32, (1, tq), 1)
        n_adm = jnp.minimum(((qpos >> 6) + 1) << 6, length).astype(F32)
        take_all = n_adm < kf

        def bit_cond(carry):
            return (carry[0] < 32) & (carry[3] == 0)

        def bit_body(carry):
            it, prefix, nge, _ = carry
            cand = prefix + lax.shift_left(jnp.int32(1), 31 - it)
            cnt = count(lambda kt, _: kt >= cand)
            take = cnt >= kf
            prefix = jnp.where(take, cand, prefix)
            nge = jnp.where(take, cnt, nge)
            settled = jnp.min(jnp.where(take_all | (nge == kf), 1.0, 0.0))
            return it + 1, prefix, nge, (settled > 0.0).astype(I32)

        _, prefix, nge, _ = lax.while_loop(
            bit_cond, bit_body,
            (jnp.int32(0), jnp.full((1, tq), INT_MIN, I32), jnp.zeros((1, tq), F32), jnp.int32(0)))
        found = prefix > INT_MIN
        thr = jnp.maximum(prefix, INT_MIN + 1)
        t_ref[0] = thr
        c_ref[0] = jnp.full((1, tq), 2 ** 30, I32)
        real = qpos - qoff < nreal
        straddle = jnp.max(jnp.where(found & (nge > kf) & real, 1.0, 0.0))

        @pl.when(straddle > 0.0)
        def _():
            need = kf - count(lambda kt, _: kt > thr)

            def tie_body(it, x):
                cx = x + lax.shift_left(jnp.int32(1), idx_bits - 1 - it)
                g = count(lambda kt, kpos: jnp.where(kt == thr, kpos, INT_MAX) < cx)
                return jnp.where(g < need, cx, x)
            c_ref[0] = lax.fori_loop(0, idx_bits, tie_body, jnp.zeros((1, tq), I32))


def _dsa_index(qi, ki2, wt, *, tq, tk, qoff, length, topk, nreal):
    b, s, _ = qi.shape
    lp = ki2.shape[1]
    nq, nk = s // tq, lp // tk
    idx_bits = max(1, int(np.ceil(np.log2(lp))))
    qvec = pl.BlockSpec((1, 1, tq), lambda bb, t, ii, jj: (bb, 0, ii[t]))
    return _tri_call(
        functools.partial(_dsa_index_kernel, tq=tq, tk=tk, nk=nk, unroll=2 if nk % 2 == 0 else 1,
                          qoff=qoff, length=length, topk=topk, idx_bits=idx_bits, nreal=nreal),
        _tri_steps(nq, tq, tk, qoff, length, True), b,
        in_specs=[pl.BlockSpec((1, tq, 512), lambda bb, t, ii, jj: (bb, ii[t], 0)),
                  pl.BlockSpec((1, tk, LANES), lambda bb, t, ii, jj: (bb, jj[t], 0)),
                  pl.BlockSpec((1, IDX_HEADS, tq), lambda bb, t, ii, jj: (bb, 0, ii[t]))],
        out_specs=[pl.BlockSpec((1, tk, tq), lambda bb, t, ii, jj: (bb, jj[t], ii[t])), qvec, qvec],
        out_shape=[jax.ShapeDtypeStruct((b, lp, s), I32),
                   jax.ShapeDtypeStruct((b, 1, s), I32),
                   jax.ShapeDtypeStruct((b, 1, s), I32)],
        scratch=[pltpu.VMEM((IDX_HEADS * tq, LANES), CDT),
                 pltpu.VMEM((nk, tk, tq), I32)],
        name="dsa_index",
    )(qi, ki2, wt)


def _dsa_attn_kernel(ii_ref, jj_ref, q_ref, k_ref, vt_ref, keys_ref, t_ref, c_ref, o_ref,
                     qst_ref, m_ref, l_ref, acc_ref, s_ref, *, tq, tk, qoff, length):
    i = ii_ref[pl.program_id(1)]
    j = jj_ref[pl.program_id(1)]
    last = _last_tile(i, tq, tk, qoff, length, True)

    @pl.when(j == 0)
    def _():
        _flash_init(q_ref, None, qst_ref, m_ref, l_ref, acc_ref, tq)

    @pl.when(j <= last)
    def _():
        kt = keys_ref[0]
        thr = t_ref[0]
        kpos = j * tk + lax.broadcasted_iota(I32, (tk, tq), 0)
        sel = (kt > thr) | (jnp.where(kt == thr, kpos, INT_MAX) <= c_ref[0])
        mb = jnp.where(sel, 0.0, NEG)
        mb2 = jnp.concatenate([mb, mb], axis=1)
        score = lambda g: _mm_nt(k_ref[0, :, g * LANES:(g + 1) * LANES], qst_ref[g]) + mb2
        _flash_pairs(score, s_ref, vt_ref, m_ref, l_ref, acc_ref, tq, True)

    @pl.when(j == last)
    def _():
        _flash_finish(o_ref, l_ref, acc_ref, tq)


def _dsa_attn(q, k, vt, keys, thr, cut, *, tq, tk, qoff, length):
    b, s, w = q.shape
    qmap = lambda bb, t, ii, jj: (bb, ii[t], 0)
    qvec = pl.BlockSpec((1, 1, tq), lambda bb, t, ii, jj: (bb, 0, ii[t]))
    npair = w // LANES
    return _tri_call(
        functools.partial(_dsa_attn_kernel, tq=tq, tk=tk, qoff=qoff, length=length),
        _tri_steps(s // tq, tq, tk, qoff, length, True), b,
        in_specs=[pl.BlockSpec((1, tq, w), qmap),
                  pl.BlockSpec((1, tk, w), lambda bb, t, ii, jj: (bb, jj[t], 0)),
                  pl.BlockSpec((1, w, tk), lambda bb, t, ii, jj: (bb, 0, jj[t])),
                  pl.BlockSpec((1, tk, tq), lambda bb, t, ii, jj: (bb, jj[t], ii[t])),
                  qvec, qvec],
        out_specs=pl.BlockSpec((1, tq, w), qmap),
        out_shape=jax.ShapeDtypeStruct((b, s, w), CDT),
        scratch=_flash_scratch(npair, tq, tk, LANES, w),
        name="dsa_attn",
    )(q, k, vt, keys, thr, cut)


def _pieces3(x):
    pcs = _pieces(x, 3)
    return pcs + [jnp.zeros_like(pcs[0])] * (3 - len(pcs))


FOX_EXT = 9


def _cumsum_kernel(x_ref, tri_ref, place_ref, ones_ref, kx_ref, carry_ref, *, scale):
    @pl.when(pl.program_id(1) == 0)
    def _():
        carry_ref[...] = jnp.zeros(carry_ref.shape, F32)
    tri = tri_ref[...]
    tot = None
    for p in _pieces(x_ref[0], 3):
        t = _mm(tri, p)
        tot = t if tot is None else tot + t
    acc = carry_ref[...] + tot
    carry_ref[...] = acc[-1:, :]
    ext = None
    for c, p in enumerate(_pieces3(acc * (-scale))):
        t = _mm(p, place_ref[c])
        ext = t if ext is None else ext + t
    kx_ref[0] = (ext + ones_ref[...]).astype(kx_ref.dtype)


def _fox_key_lanes(x, tb, scale, nh):
    b, s, w = x.shape
    npair = nh // 2
    place = np.zeros((3, w, npair * LANES), np.float32)
    ones = np.zeros((1, npair * LANES), np.float32)
    for h in range(nh):
        for c in range(3):
            place[c, h, (h // 2) * LANES + (h % 2) * 6 + c] = 1.0
    for g in range(npair):
        ones[0, g * LANES + 3:g * LANES + 6] = 1.0
    tri = jnp.tril(jnp.ones((tb, tb), F32)).astype(CDT)
    place = jnp.asarray(place, CDT)
    spec = lambda wd: pl.BlockSpec((1, tb, wd), lambda bb, i: (bb, i, 0))
    return pl.pallas_call(
        functools.partial(_cumsum_kernel, scale=scale),
        grid=(b, s // tb),
        in_specs=[spec(w), _const_spec(tri.shape), _const_spec(place.shape), _const_spec(ones.shape)],
        out_specs=spec(npair * LANES),
        out_shape=jax.ShapeDtypeStruct((b, s, npair * LANES), CDT),
        scratch_shapes=[pltpu.VMEM((1, w), F32)],
        compiler_params=_params(("parallel", "arbitrary")),
        name="logf_cumsum",
    )(x, tri, place, jnp.asarray(ones))


def _fox_kernel(ii_ref, jj_ref, q_ref, k_ref, kx_ref, vt_ref, aug_ref, o_ref,
                qst_ref, m_ref, l_ref, acc_ref, s_ref, *, tq, tk, qoff, length):
    i = ii_ref[pl.program_id(1)]
    j = jj_ref[pl.program_id(1)]
    last = _last_tile(i, tq, tk, qoff, length, False)
    q0 = qoff + i * tq

    @pl.when(j == 0)
    def _():
        _flash_init(q_ref, aug_ref, qst_ref, m_ref, l_ref, acc_ref, tq)

    def step(masked):
        if masked:
            kpos = j * tk + lax.broadcasted_iota(I32, (tk, 2 * tq), 0)
            lane = lax.broadcasted_iota(I32, (tk, 2 * tq), 1)
            causal = kpos <= q0 + jnp.where(lane >= tq, lane - tq, lane)
        def score(g):
            sl = slice(g * LANES, (g + 1) * LANES)
            st = _mm_nt(jnp.concatenate([k_ref[0, :, sl], kx_ref[0, :, sl]], axis=1), qst_ref[g])
            return jnp.where(causal, st, NEG) if masked else st
        _flash_pairs(score, s_ref, vt_ref, m_ref, l_ref, acc_ref, tq, False)

    diag = (j + 1) * tk - 1 > q0

    @pl.when((j <= last) & diag)
    def _():
        step(True)

    @pl.when((j <= last) & jnp.logical_not(diag))
    def _():
        step(False)

    @pl.when(j == last)
    def _():
        _flash_finish(o_ref, l_ref, acc_ref, tq)


def _fox(q, k, kx, vt, qaug, *, tq, tk, qoff, length):
    b, s, w = q.shape
    npair = w // LANES
    qmap = lambda bb, t, ii, jj: (bb, ii[t], 0)
    kmap = lambda bb, t, ii, jj: (bb, jj[t], 0)
    return _tri_call(
        functools.partial(_fox_kernel, tq=tq, tk=tk, qoff=qoff, length=length),
        _tri_steps(s // tq, tq, tk, qoff, length, False), b,
        in_specs=[pl.BlockSpec((1, tq, w), qmap),
                  pl.BlockSpec((1, tk, w), kmap),
                  pl.BlockSpec((1, tk, npair * LANES), kmap),
                  pl.BlockSpec((1, w, tk), lambda bb, t, ii, jj: (bb, 0, jj[t])),
                  pl.BlockSpec((1, 1, 2 * npair, LANES), lambda bb, t, ii, jj: (bb, ii[t], 0, 0))],
        out_specs=pl.BlockSpec((1, tq, w), qmap),
        out_shape=jax.ShapeDtypeStruct((b, s, w), CDT),
        scratch=_flash_scratch(npair, tq, tk, 2 * LANES, w),
        name="fox_attn",
    )(q, k, kx, vt, qaug)


def _fox_qaug(kx0):
    b, nq, _ = kx0.shape
    a = kx0.reshape(b, nq, -1, LANES).astype(F32)
    one, zero = jnp.ones_like(a[..., 0:3]), jnp.zeros_like(a[..., 0:3])
    even = jnp.concatenate([one, -a[..., 0:3], zero], axis=-1)
    odd = jnp.concatenate([zero, -a[..., 6:9], one], axis=-1)
    heads = jnp.stack([even, odd], axis=3).reshape(b, nq, -1, FOX_EXT)
    return jnp.pad(heads, ((0, 0), (0, 0), (0, 0), (0, LANES - FOX_EXT)))


def _rope_tables(pos):
    half = HEAD_DIM // 2
    inv = ROPE_THETA ** (-jnp.arange(half, dtype=F32) / half)
    ang = pos.astype(F32)[:, None] * inv[None, :]
    cos, sin = jnp.cos(ang), jnp.sin(ang)
    return (jnp.tile(jnp.concatenate([cos, cos], axis=1), (1, 2)),
            jnp.tile(jnp.concatenate([-sin, sin], axis=1), (1, 2)))


def _pad_rows(x, n):
    return x if x.shape[1] == n else jnp.pad(x, ((0, 0), (0, n - x.shape[1]), (0, 0)))


def _tiles(s):
    tk = 512 if s % 512 == 0 else s
    tq = 512 if s % 512 == 0 else s
    tqi = 256 if s % 256 == 0 else s
    return tq, tqi, tk


def kernel(x_prompt, x_sample, cache_a_k, cache_a_v, cache_b_k, cache_b_v, cache_b_ik, cache_c_k, cache_c_v,
           cache_c_logf, p_prompt, p_sample, g_mix, w_in_even, qn_a, kn_a, rel_bias_a, qn_b, kn_b, w_out_even,
           w_in_odd, b_f, qn_c, kn_c, w_out_odd, g_mlp, w_up, w_down, g_ple, w_ple_gate, w_ple_proj):
    B, S, D = x_prompt.shape
    DB, n, _ = x_sample.shape
    P = cache_b_k.shape[2]
    W_A = cache_a_k.shape[2]
    W_AP = min(A_PAST, S)
    depth = g_mix.shape[0]
    L = P + n
    Lp = -(-L // LANES) * LANES
    nqp = -(-n // LANES) * LANES
    assert B == 1 and S % 128 == 0 and n % 16 == 0 and P % 16 == 0

    row = lambda v: v.astype(F32).reshape(1, -1)
    bd = jnp.asarray(np.kron(np.eye(256 // HEAD_DIM), np.full((HEAD_DIM, HEAD_DIM), 1.0 / HEAD_DIM)), CDT)
    cos_p, sin_p = _rope_tables(jnp.arange(S))
    cos_s, sin_s = (jnp.tile(t, (DB, 1)) for t in _rope_tables(P + jnp.arange(n)))
    tq, tqi, tk = _tiles(S)
    tr = lambda t: jnp.swapaxes(t, 1, 2)
    catp = lambda c, new: _pad_rows(jnp.concatenate([c.astype(CDT), new], axis=1), Lp)

    outs = {name: [] for name in ("a_k_p", "a_v_p", "b_k_p", "b_v_p", "b_ik_p", "c_k_p", "c_v_p", "c_lf_p",
                                  "a_k_s", "a_v_s", "b_k_s", "b_v_s", "b_ik_s", "c_k_s", "c_v_s", "c_lf_s")}
    h_p = x_prompt.reshape(B * S, D)
    h_s = x_sample.reshape(DB * n, D)
    for i in range(depth):
        g = row(g_mix[i])
        if i % 2 == 0:
            e = i // 2
            wsz = 512
            w = w_in_even[e]
            w_all = jnp.concatenate([w[:, :7 * wsz], jnp.pad(w[:, 7 * wsz:], ((0, 0), (0, LANES - 72)))],
                                    axis=1).astype(CDT)
            gn = jnp.stack([jnp.tile(v[e].astype(F32), wsz // HEAD_DIM) for v in (qn_a, kn_a, qn_b, kn_b)])
            (ka, va, kb, vb, kw, qa_c, ka_c, va_c, qb_c, kb_c, vb_c, qi_c, ki2_c) = _proj_even(
                h_p, g, w_all, gn, bd, cos_p, sin_p)
            r3 = lambda t: t.reshape(B, S, t.shape[-1])
            tqa = 256 if S % 256 == 0 else 128
            npc = (A_PAST + tqa) // tqa
            kpad = jnp.pad(r3(ka_c), ((0, 0), (A_PAST, 0), (0, 0)))
            vpad = jnp.pad(r3(va_c), ((0, 0), (A_PAST, 0), (0, 0)))
            k_specs = [pl.BlockSpec((1, tqa, wsz), functools.partial(lambda bb, ii, pp: (bb, ii + pp, 0), pp=pp))
                       for pp in range(npc)]
            bias_p = _band_bias(rel_bias_a[e], np.arange(tqa), np.arange(-A_PAST, tqa))
            oa = _band(r3(qa_c), [kpad] * npc, [vpad] * npc, k_specs, bias_p, tqa, True)
            topk_p = min(TOPK_MAX, S // 4)
            wt = tr(r3(kw)[..., 64:64 + IDX_HEADS])
            keys, thr, cut = _dsa_index(r3(qi_c), r3(ki2_c), wt, tq=tqi, tk=tk, qoff=0, length=S, topk=topk_p,
                                        nreal=S)
            ob = _dsa_attn(r3(qb_c), r3(kb_c), tr(r3(vb_c)), keys, thr, cut, tq=tq, tk=tk, qoff=0, length=S)
            os_p = [oa.reshape(B * S, wsz), ob.reshape(B * S, wsz)]
            hd = lambda t: t.reshape(B, S, -1, HEAD_DIM)
            outs["a_k_p"].append(hd(ka)[:, S - W_AP:]); outs["a_v_p"].append(hd(va)[:, S - W_AP:])
            outs["b_k_p"].append(hd(kb)); outs["b_v_p"].append(hd(vb))
            outs["b_ik_p"].append(kw[:, :64].reshape(B, S, 64))
            (ka, va, kb, vb, kw, qa_c, ka_c, va_c, qb_c, kb_c, vb_c, qi_c, ki2_c) = _proj_even(
                h_s, g, w_all, gn, bd, cos_s, sin_s)
            r3 = lambda t: t.reshape(DB, n, t.shape[-1])
            ca_k = cache_a_k[e].reshape(DB, W_A, wsz)
            ca_v = cache_a_v[e].reshape(DB, W_A, wsz)
            full = lambda rows, wd: pl.BlockSpec((1, rows, wd), lambda bb, ii: (bb, 0, 0))
            bias_s = _band_bias(rel_bias_a[e], P + np.arange(n), P - W_A + np.arange(W_A + n))
            bias_s = jnp.pad(bias_s, ((0, 0), (0, 0), (0, -(W_A + n) % LANES)), constant_values=NEG)
            oa = _band(r3(qa_c), [ca_k, r3(ka)], [ca_v, r3(va)], [full(W_A, wsz), full(n, wsz)], bias_s, n, False)
            ik = cache_b_ik[e].astype(CDT)
            ki2_all = catp(jnp.concatenate([ik, ik], axis=-1), r3(ki2_c))
            kb_all = catp(cache_b_k[e].reshape(DB, P, wsz), r3(kb_c))
            vb_all = catp(cache_b_v[e].reshape(DB, P, wsz), r3(vb_c))
            topk_s = min(TOPK_MAX, L // 4)
            wt = _pad_rows(r3(kw)[..., 64:64 + IDX_HEADS], nqp)
            keys, thr, cut = _dsa_index(_pad_rows(r3(qi_c), nqp), ki2_all, tr(wt),
                                        tq=nqp, tk=Lp, qoff=P, length=L, topk=topk_s, nreal=n)
            ob = _dsa_attn(_pad_rows(r3(qb_c), nqp), kb_all, tr(vb_all), keys, thr, cut,
                           tq=nqp, tk=Lp, qoff=P, length=L)[:, :n]
            os_s = [oa.reshape(DB * n, wsz), ob.reshape(DB * n, wsz)]
            hd = lambda t: t.reshape(DB, n, -1, HEAD_DIM)
            outs["a_k_s"].append(jnp.concatenate([cache_a_k[e], hd(ka)], axis=1)[:, n:])
            outs["a_v_s"].append(jnp.concatenate([cache_a_v[e], hd(va)], axis=1)[:, n:])
            outs["b_k_s"].append(hd(kb)); outs["b_v_s"].append(hd(vb))
            outs["b_ik_s"].append(kw[:, :64].reshape(DB, n, 64))
            w_out = w_out_even[e].astype(CDT)
            wos = [w_out[:wsz], w_out[wsz:]]
        else:
            o = i // 2
            nh = b_f.shape[1]
            wsz = nh * HEAD_DIM
            w = w_in_odd[o]
            w_all = jnp.concatenate([w[:, :3 * wsz], jnp.pad(w[:, 3 * wsz:], ((0, 0), (0, LANES - nh)))],
                                    axis=1).astype(CDT)
            gn = jnp.stack([jnp.tile(v[o].astype(F32), wsz // HEAD_DIM) for v in (qn_c, kn_c)])
            bf = jnp.pad(b_f[o].astype(F32), (0, LANES - nh)).reshape(1, LANES)
            k, v, lf, q_c, k_c, v_c = _proj_odd(h_p, g, w_all, gn, bf, bd)
            r3 = lambda t: t.reshape(B, S, t.shape[-1])
            kx = _fox_key_lanes(r3(lf), 512 if S % 512 == 0 else S, LOG2E, nh)
            oc = _fox(r3(q_c), r3(k_c), kx, tr(r3(v_c)), _fox_qaug(kx[:, ::tq]), tq=tq, tk=tk, qoff=0, length=S)
            os_p = [oc.reshape(B * S, wsz)]
            hd = lambda t: t.reshape(B, S, nh, HEAD_DIM)
            outs["c_k_p"].append(hd(k)); outs["c_v_p"].append(hd(v)); outs["c_lf_p"].append(lf[:, :nh].reshape(B, S, nh))
            k, v, lf, q_c, k_c, v_c = _proj_odd(h_s, g, w_all, gn, bf, bd)
            r3 = lambda t: t.reshape(DB, n, t.shape[-1])
            lf_all = jnp.concatenate([jnp.pad(cache_c_logf[o].astype(F32), ((0, 0), (0, 0), (0, LANES - nh))), r3(lf)],
                                     axis=1)
            kx = _fox_key_lanes(lf_all, L, LOG2E, nh)
            k_all = catp(cache_c_k[o].reshape(DB, P, wsz), r3(k_c))
            v_all = catp(cache_c_v[o].reshape(DB, P, wsz), r3(v_c))
            oc = _fox(_pad_rows(r3(q_c), nqp), k_all, _pad_rows(kx, Lp), tr(v_all),
                      _fox_qaug(kx[:, P:P + 1]), tq=nqp, tk=Lp, qoff=P, length=L)[:, :n]
            os_s = [oc.reshape(DB * n, wsz)]
            hd = lambda t: t.reshape(DB, n, nh, HEAD_DIM)
            outs["c_k_s"].append(hd(k)); outs["c_v_s"].append(hd(v)); outs["c_lf_s"].append(lf[:, :nh].reshape(DB, n, nh))
            w_out = w_out_odd[o].astype(CDT)
            wos = [w_out]
        post_w = (row(g_mlp[i]), w_up[i].astype(CDT), w_down[i].astype(CDT), row(g_ple[i]),
                  w_ple_gate[i].astype(CDT), w_ple_proj[i].astype(CDT))
        h_p = _post(h_p, os_p, p_prompt[i].reshape(B * S, -1), wos, *post_w)
        h_s = _post(h_s, os_s, p_sample[i].reshape(DB * n, -1), wos, *post_w)

    st = jnp.stack
    names = ("a_k_p", "a_v_p", "b_k_p", "b_v_p", "b_ik_p", "c_k_p", "c_v_p", "c_lf_p",
             "a_k_s", "a_v_s", "b_k_s", "b_v_s", "b_ik_s", "c_k_s", "c_v_s", "c_lf_s")
    return (h_p.reshape(B, S, D), h_s.reshape(DB, n, D)) + tuple(st(outs[nm]) for nm in names)
```

```python
import functools

import numpy as np
import jax
import jax.numpy as jnp
from jax import lax
from jax.experimental import pallas as pl
from jax.experimental.pallas import tpu as pltpu

F32 = jnp.float32
I32 = jnp.int32
CDT = jnp.bfloat16

CHUNK = 64
HEAD_DIM = 64
IDX_HEADS = 8
A_PAST = 8 * CHUNK
REL_CLIP = 128
TOPK_MAX = 256
ROPE_THETA = 10000.0
RMS_EPS = 1e-6
LANES = 128
NEG = -1e30
INT_MIN = -(2 ** 31)
INT_MAX = 2 ** 31 - 1
LOG2E = 1.4426950408889634
VMEM_LIMIT = 56 * 1024 * 1024


def _mm(a, b):
    return jnp.dot(a, b, preferred_element_type=F32)


def _mm_nt(a, b):
    return lax.dot_general(a, b, (((1,), (1,)), ((), ())), preferred_element_type=F32)


def _pieces(x, n):
    if CDT == F32:
        return [x]
    out = []
    for _ in range(n - 1):
        p = x.astype(CDT)
        out.append(p)
        x = x - p.astype(F32)
    out.append(x.astype(CDT))
    return out


def _rms(x, g):
    return x * lax.rsqrt(jnp.mean(x * x, axis=-1, keepdims=True) + RMS_EPS) * g


def _head_rms(x, bd, g):
    x2 = x * x
    pcs = _pieces(x2, 2)
    cols = []
    for s in range(x.shape[1] // 256):
        sl = slice(s * 256, (s + 1) * 256)
        ms = _mm(pcs[0][:, sl], bd)
        for p in pcs[1:]:
            ms = ms + _mm(p[:, sl], bd)
        cols.append(ms)
    ms = cols[0] if len(cols) == 1 else jnp.concatenate(cols, axis=1)
    return x * lax.rsqrt(ms + RMS_EPS) * g


def _rope(x, cos, sin):
    w = x.shape[1]
    lane = lax.broadcasted_iota(I32, x.shape, 1)
    first = (lane & 63) < 32
    swapped = jnp.where(first, pltpu.roll(x, w - 32, 1), pltpu.roll(x, 32, 1))
    return x * cos + swapped * sin


def _tile_lanes(t, w):
    return t if w == t.shape[1] else jnp.concatenate([t] * (w // t.shape[1]), axis=1)


def _proj_even_kernel(h_ref, g_ref, w_ref, gn_ref, bd_ref, cos_ref, sin_ref,
                      ka_ref, va_ref, kb_ref, vb_ref, kw_ref,
                      qa_c, ka_c, va_c, qb_c, kb_c, vb_c, qi_c, ki2_c):
    a = _rms(h_ref[...], g_ref[...]).astype(CDT)
    z = _mm(a, w_ref[...])
    w = 512
    bd = bd_ref[...]
    gn = gn_ref[...]
    cos1, sin1 = cos_ref[...], sin_ref[...]
    cos, sin = _tile_lanes(cos1, w), _tile_lanes(sin1, w)
    scale = HEAD_DIM ** -0.5

    qa = _head_rms(z[:, 0:w], bd, gn[0:1])
    qa_c[...] = (qa * scale).astype(CDT)
    ka = _head_rms(z[:, w:2 * w], bd, gn[1:2])
    ka_ref[...] = ka
    ka_c[...] = ka.astype(CDT)
    va = z[:, 2 * w:3 * w]
    va_ref[...] = va
    va_c[...] = va.astype(CDT)
    qb = _rope(_head_rms(z[:, 3 * w:4 * w], bd, gn[2:3]), cos, sin)
    qb_c[...] = (qb * (scale * LOG2E)).astype(CDT)
    kb = _rope(_head_rms(z[:, 4 * w:5 * w], bd, gn[3:4]), cos, sin)
    kb_ref[...] = kb
    kb_c[...] = kb.astype(CDT)
    vb = z[:, 5 * w:6 * w]
    vb_ref[...] = vb
    vb_c[...] = vb.astype(CDT)
    qi = _rope(z[:, 6 * w:7 * w], cos, sin)
    qi_c[...] = (qi * scale).astype(CDT)
    kw = z[:, 7 * w:7 * w + LANES]
    kwr = _rope(kw, cos1, sin1)
    lane = lax.broadcasted_iota(I32, kw.shape, 1)
    kw_ref[...] = jnp.where(lane < 64, kwr, kw)
    ki2_c[...] = jnp.where(lane < 64, kwr, pltpu.roll(kwr, 64, 1)).astype(CDT)


def _proj_odd_kernel(h_ref, g_ref, w_ref, gn_ref, bf_ref, bd_ref,
                     k_ref, v_ref, lf_ref, q_c, k_c, v_c):
    a = _rms(h_ref[...], g_ref[...]).astype(CDT)
    z = _mm(a, w_ref[...])
    w = 1024
    bd = bd_ref[...]
    gn = gn_ref[...]
    q = _head_rms(z[:, 0:w], bd, gn[0:1])
    q_c[...] = (q * (HEAD_DIM ** -0.5 * LOG2E)).astype(CDT)
    k = _head_rms(z[:, w:2 * w], bd, gn[1:2])
    k_ref[...] = k
    k_c[...] = k.astype(CDT)
    v = z[:, 2 * w:3 * w]
    v_ref[...] = v
    v_c[...] = v.astype(CDT)
    fl = z[:, 3 * w:3 * w + LANES] + bf_ref[...]
    lf_ref[...] = jnp.minimum(fl, 0.0) - jnp.log1p(jnp.exp(-jnp.abs(fl)))


def _const_spec(shape):
    nd = len(shape)
    return pl.BlockSpec(shape, lambda *_: (0,) * nd, pipeline_mode=pl.Buffered(1))


def _row_spec(tm, w):
    return pl.BlockSpec((tm, w), lambda i: (i, 0))


def _params(sem):
    return pltpu.CompilerParams(dimension_semantics=sem, vmem_limit_bytes=VMEM_LIMIT)


def _row_tile(rows, pref):
    return pref if rows % pref == 0 else rows


def _proj_even(h, g, w_all, gn, bd, cos, sin):
    rows, d = h.shape
    tm = _row_tile(rows, 256)
    f = lambda w, dt: jax.ShapeDtypeStruct((rows, w), dt)
    out_shape = [f(512, F32)] * 4 + [f(LANES, F32)] + [f(512, CDT)] * 7 + [f(LANES, CDT)]
    out_specs = [_row_spec(tm, 512)] * 4 + [_row_spec(tm, LANES)] + [_row_spec(tm, 512)] * 7 + [_row_spec(tm, LANES)]
    return pl.pallas_call(
        _proj_even_kernel,
        grid=(rows // tm,),
        in_specs=[_row_spec(tm, d), _const_spec(g.shape), _const_spec(w_all.shape), _const_spec(gn.shape),
                  _const_spec(bd.shape), _row_spec(tm, LANES), _row_spec(tm, LANES)],
        out_specs=out_specs,
        out_shape=out_shape,
        compiler_params=_params(("parallel",)),
        name="proj_even",
    )(h, g, w_all, gn, bd, cos, sin)


def _proj_odd(h, g, w_all, gn, bf, bd):
    rows, d = h.shape
    tm = _row_tile(rows, 256)
    f = lambda w, dt: jax.ShapeDtypeStruct((rows, w), dt)
    out_shape = [f(1024, F32)] * 2 + [f(LANES, F32)] + [f(1024, CDT)] * 3
    out_specs = [_row_spec(tm, 1024)] * 2 + [_row_spec(tm, LANES)] + [_row_spec(tm, 1024)] * 3
    return pl.pallas_call(
        _proj_odd_kernel,
        grid=(rows // tm,),
        in_specs=[_row_spec(tm, d), _const_spec(g.shape), _const_spec(w_all.shape), _const_spec(gn.shape),
                  _const_spec(bf.shape), _const_spec(bd.shape)],
        out_specs=out_specs,
        out_shape=out_shape,
        compiler_params=_params(("parallel",)),
        name="proj_odd",
    )(h, g, w_all, gn, bf, bd)


def _post_kernel(*refs, n_o, ff_chunk):
    h_ref = refs[0]
    o_refs = refs[1:1 + n_o]
    p_ref = refs[1 + n_o]
    wo_refs = refs[2 + n_o:2 + 2 * n_o]
    gm_ref, wu_ref, wd_ref, gp_ref, wg_ref, wp_ref, out_ref = refs[2 + 2 * n_o:]
    mix = None
    for o_ref, wo_ref in zip(o_refs, wo_refs):
        t = _mm(o_ref[...], wo_ref[...])
        mix = t if mix is None else mix + t
    h = h_ref[...] + mix
    m = _rms(h, gm_ref[...]).astype(CDT)
    mlp = None
    d_ff = wu_ref.shape[1]
    for c in range(d_ff // ff_chunk):
        sl = slice(c * ff_chunk, (c + 1) * ff_chunk)
        u = jnp.square(jnp.maximum(_mm(m, wu_ref[:, sl]), 0.0)).astype(CDT)
        t = _mm(u, wd_ref[sl, :])
        mlp = t if mlp is None else mlp + t
    acc = h + mlp
    gate_in = _rms(acc, gp_ref[...]).astype(CDT)
    gate = 1.0 / (1.0 + jnp.exp(-_mm(gate_in, wg_ref[...])))
    out_ref[...] = acc + gate * _mm(p_ref[...].astype(CDT), wp_ref[...])


def _post(h, os_, p, wos, gm, wu, wd, gp, wg, wp):
    rows, d = h.shape
    tm = _row_tile(rows, 512)
    n_o = len(os_)
    in_specs = ([_row_spec(tm, d)] + [_row_spec(tm, o.shape[1]) for o in os_] + [_row_spec(tm, p.shape[1])]
                + [_const_spec(w.shape) for w in wos]
                + [_const_spec(x.shape) for x in (gm, wu, wd, gp, wg, wp)])
    return pl.pallas_call(
        functools.partial(_post_kernel, n_o=n_o, ff_chunk=1024),
        grid=(rows // tm,),
        in_specs=in_specs,
        out_specs=_row_spec(tm, d),
        out_shape=jax.ShapeDtypeStruct((rows, d), F32),
        compiler_params=_params(("parallel",)),
        name="post",
    )(h, *os_, p, *wos, gm, wu, wd, gp, wg, wp)


def _pair_stack(qp, tq):
    lane = lax.broadcasted_iota(I32, qp.shape, 1)
    zero = jnp.zeros_like(qp)
    return jnp.concatenate([jnp.where(lane < 64, qp, zero), jnp.where(lane >= 64, qp, zero)], axis=0)


def _pair_merge(o2, tq):
    lane = lax.broadcasted_iota(I32, (tq, LANES), 1)
    return jnp.where(lane < 64, o2[:tq], o2[tq:])


def _last_tile(i, tq, tk, qoff, length, chunked):
    qend = qoff + i * tq + tq - 1
    kmax = (qend // CHUNK + 1) * CHUNK if chunked else qend + 1
    kmax = jnp.minimum(kmax, length)
    return (kmax - 1) // tk


def _tri_steps(nq, tq, tk, qoff, length, chunked):
    ii, jj = [], []
    for i in range(nq):
        qend = qoff + i * tq + tq - 1
        kmax = min((qend // CHUNK + 1) * CHUNK if chunked else qend + 1, length)
        for j in range((kmax - 1) // tk + 1):
            ii.append(i)
            jj.append(j)
    return jnp.asarray(ii, I32), jnp.asarray(jj, I32)


def _tri_call(kernel_fn, steps, batch, in_specs, out_specs, out_shape, scratch, name):
    ii, jj = steps
    grid_spec = pltpu.PrefetchScalarGridSpec(
        num_scalar_prefetch=2, grid=(batch, ii.shape[0]),
        in_specs=in_specs, out_specs=out_specs, scratch_shapes=scratch)
    call = pl.pallas_call(kernel_fn, grid_spec=grid_spec, out_shape=out_shape,
                          compiler_params=_params(("parallel", "arbitrary")), name=name)
    return functools.partial(call, ii, jj)


def _flash_init(q_ref, aug_ref, qst_ref, m_ref, l_ref, acc_ref, tq):
    q = q_ref[0]
    for g in range(qst_ref.shape[0]):
        st = _pair_stack(q[:, g * LANES:(g + 1) * LANES], tq)
        if aug_ref is not None:
            a = aug_ref[0, 0]
            ext = jnp.concatenate([jnp.broadcast_to(a[2 * g:2 * g + 1], (tq, LANES)),
                                   jnp.broadcast_to(a[2 * g + 1:2 * g + 2], (tq, LANES))], axis=0)
            st = jnp.concatenate([st, ext.astype(CDT)], axis=1)
        qst_ref[g] = st
    m_ref[...] = jnp.full(m_ref.shape, NEG, F32)
    l_ref[...] = jnp.zeros(l_ref.shape, F32)
    acc_ref[...] = jnp.zeros(acc_ref.shape, F32)


def _flash_pairs(score_fn, s_ref, vt_ref, m_ref, l_ref, acc_ref, tq, mxu_sum):
    npair = m_ref.shape[0]
    s_ref[0] = score_fn(0)
    for g in range(npair):
        if g + 1 < npair:
            s_ref[(g + 1) % 2] = score_fn(g + 1)
        _flash_step(s_ref[g % 2], vt_ref, m_ref, l_ref, acc_ref, g, tq, mxu_sum)


def _flash_step(st, vt_ref, m_ref, l_ref, acc_ref, g, tq, mxu_sum):
    m_old = m_ref[g]
    m_new = jnp.maximum(m_old, jnp.max(st, axis=0, keepdims=True))
    alpha = jnp.exp2(m_old - m_new)
    m_ref[g] = m_new
    if mxu_sum:
        pc = jnp.exp2((st - m_new).astype(CDT))
        ones = jnp.ones((16, st.shape[0]), CDT)
    else:
        p = jnp.exp2(st - m_new)
        pc = p.astype(CDT)
    sums = []
    for e in range(2):
        rows = slice(g * LANES + e * HEAD_DIM, g * LANES + (e + 1) * HEAD_DIM)
        cols = slice(e * tq, (e + 1) * tq)
        vt = vt_ref[0, rows, :]
        pv = _mm(jnp.concatenate([vt, ones], axis=0) if mxu_sum else vt, pc[:, cols])
        acc_ref[rows, :] = alpha[:, cols] * acc_ref[rows, :] + pv[:HEAD_DIM]
        if mxu_sum:
            sums.append(pv[HEAD_DIM:HEAD_DIM + 1])
    psum = jnp.concatenate(sums, axis=1) if mxu_sum else jnp.sum(p, axis=0, keepdims=True)
    l_ref[g] = alpha * l_ref[g] + psum


def _flash_finish(o_ref, l_ref, acc_ref, tq):
    for g in range(l_ref.shape[0]):
        l = l_ref[g]
        den = jnp.concatenate([jnp.broadcast_to(l[:, :tq], (HEAD_DIM, tq)),
                               jnp.broadcast_to(l[:, tq:], (HEAD_DIM, tq))], axis=0)
        o = acc_ref[g * LANES:(g + 1) * LANES, :] / den
        o_ref[0, :, g * LANES:(g + 1) * LANES] = o.T.astype(o_ref.dtype)


def _flash_scratch(npair, tq, tk, kd, w):
    return [pltpu.VMEM((npair, 2 * tq, kd), CDT),
            pltpu.VMEM((npair, 1, 2 * tq), F32),
            pltpu.VMEM((npair, 1, 2 * tq), F32),
            pltpu.VMEM((w, tq), F32),
            pltpu.VMEM((2, tk, 2 * tq), F32)]


def _band_kernel(*refs, n_kv, tq, mask_neg):
    q_ref = refs[0]
    k_refs = refs[1:1 + n_kv]
    v_refs = refs[1 + n_kv:1 + 2 * n_kv]
    bias_ref = refs[1 + 2 * n_kv]
    o_ref = refs[2 + 2 * n_kv]
    i = pl.program_id(1)
    q = q_ref[0]
    cat = lambda rs: (rs[0][0].astype(CDT) if len(rs) == 1
                      else jnp.concatenate([r[0].astype(CDT) for r in rs], axis=0))
    k = cat(k_refs)
    v = cat(v_refs)
    tk = bias_ref.shape[2]
    if k.shape[0] < tk:
        zpad = jnp.zeros((tk - k.shape[0], k.shape[1]), CDT)
        k = jnp.concatenate([k, zpad], axis=0)
        v = jnp.concatenate([v, zpad], axis=0)
    if mask_neg:
        kpos = lax.broadcasted_iota(I32, (2 * tq, tk), 1) + (i * tq - A_PAST)
        neg = kpos < 0
    for g in range(q.shape[1] // LANES):
        sl = slice(g * LANES, (g + 1) * LANES)
        s = _mm_nt(_pair_stack(q[:, sl], tq), k[:, sl])
        s = s + jnp.concatenate([bias_ref[2 * g], bias_ref[2 * g + 1]], axis=0)
        if mask_neg:
            s = jnp.where(neg, NEG, s)
        m = jnp.max(s, axis=-1, keepdims=True)
        p = jnp.exp(s - m)
        l = jnp.sum(p, axis=-1, keepdims=True)
        o2 = _mm(p.astype(CDT), v[:, sl]) / l
        o_ref[0, :, sl] = _pair_merge(o2, tq).astype(o_ref.dtype)


def _band(q, k_parts, v_parts, k_specs, bias, tq, mask_neg):
    b, s, w = q.shape
    n_kv = len(k_parts)
    q_spec = pl.BlockSpec((1, tq, w), lambda bb, i: (bb, i, 0))
    return pl.pallas_call(
        functools.partial(_band_kernel, n_kv=n_kv, tq=tq, mask_neg=mask_neg),
        grid=(b, s // tq),
        in_specs=[q_spec] + k_specs + k_specs + [_const_spec(bias.shape)],
        out_specs=q_spec,
        out_shape=jax.ShapeDtypeStruct((b, s, w), CDT),
        compiler_params=_params(("parallel", "parallel")),
        name="band_attn",
    )(q, *k_parts, *v_parts, bias)


def _band_bias(rel_bias, q_pos, k_pos):
    nq, nk = len(q_pos), len(k_pos)
    qc = q_pos[:, None] // CHUNK
    kc = k_pos[None, :] // CHUNK
    ok = (kc <= qc) & (kc >= qc - A_PAST // CHUNK)
    n = nq + nk - 1
    d0 = int(q_pos[0] - k_pos[0])
    m = np.arange(n + 1)
    rel = np.where(m < nk, d0 - m, d0 - m + n + 1)
    table = jnp.take(rel_bias.astype(F32), jnp.asarray(np.clip(rel, -REL_CLIP, REL_CLIP) + REL_CLIP), axis=1)
    bias = jnp.tile(table, (1, nq))[:, :nq * n].reshape(-1, nq, n)[:, :, :nk]
    return jnp.where(jnp.asarray(ok)[None], bias, NEG)


def _dsa_index_kernel(ii_ref, jj_ref, qi_ref, ki2_ref, wt_ref, keys_ref, t_ref, c_ref, qst_ref, kscr_ref, k16_ref,
                      *, tq, tk, nk, unroll, qoff, length, topk, idx_bits, nreal):
    i = ii_ref[pl.program_id(1)]
    j = jj_ref[pl.program_id(1)]
    last = _last_tile(i, tq, tk, qoff, length, True)
    q0 = qoff + i * tq

    @pl.when(j == 0)
    def _():
        q = qi_ref[0]
        for g in range(IDX_HEADS // 2):
            qst_ref[2 * g * tq:(2 * g + 2) * tq, :] = _pair_stack(q[:, g * LANES:(g + 1) * LANES], tq)

    @pl.when(j <= last)
    def _():
        lg = _mm_nt(ki2_ref[0], qst_ref[...])
        wgt = wt_ref[0] * (IDX_HEADS ** -0.5)
        sc = jnp.zeros((tk, tq), F32)
        for h in range(IDX_HEADS):
            sc = sc + jnp.maximum(lg[:, h * tq:(h + 1) * tq], 0.0) * wgt[h:h + 1, :]
        sc = jnp.where(sc == 0.0, 0.0, sc)
        bits = lax.bitcast_convert_type(sc, I32)
        key = bits ^ ((bits >> 31) & 0x7FFFFFFF)
        kpos = j * tk + lax.broadcasted_iota(I32, (tk, tq), 0)
        qpos = q0 + lax.broadcasted_iota(I32, (tk, tq), 1)
        ok = ((kpos >> 6) <= (qpos >> 6)) & (kpos < length)
        key = jnp.where(ok, key, INT_MIN)
        keys_ref[0] = key
        kscr_ref[j] = key
        top = lax.bitcast_convert_type(bits & -65536, F32)
        k16_ref[j] = jnp.where(ok, top, -jnp.inf).astype(jnp.bfloat16)
        if unroll > 1:
            @pl.when(j + 1 < nk)
            def _():
                kscr_ref[j + 1] = jnp.full((tk, tq), INT_MIN, I32)
                k16_ref[j + 1] = jnp.full((tk, tq), -jnp.inf, jnp.bfloat16)

    @pl.when(j == last)
    def _():
        def count(pred):
            def body(t, acc):
                for u in range(unroll):
                    tt = unroll * t + u
                    kpos = tt * tk + lax.broadcasted_iota(I32, (tk, tq), 0)
                    m = jnp.where(pred(kscr_ref[tt], kpos), 1.0, 0.0)
                    acc = acc + jnp.sum(m, axis=0, keepdims=True)
                return acc
            return lax.fori_loop(0, (last + unroll) // unroll, body, jnp.zeros((1, tq), F32))

        kf = float(topk)
        qpos = q0 + lax.broadcasted_iota(I32, (1, tq), 1)
        n_adm = jnp.minimum(((qpos >> 6) + 1) << 6, length).astype(F32)
        take_all = n_adm < kf

        def count16(c16):
            one, zero = jnp.ones((), jnp.bfloat16), jnp.zeros((), jnp.bfloat16)

            def body(t, acc):
                for u in range(unroll):
                    m = jnp.where(k16_ref[unroll * t + u] >= c16, one, zero).reshape(tk // 16, 16, tq)
                    part = m[0]
                    for r in range(1, tk // 16):
                        part = part + m[r]
                    acc = acc + jnp.sum(part.astype(F32), axis=0, keepdims=True)
                return acc
            return lax.fori_loop(0, (last + unroll) // unroll, body, jnp.zeros((1, tq), F32))

        def bit_body(top_half):
            def body(carry):
                it, prefix, nge, _ = carry
                cand = prefix + lax.shift_left(jnp.int32(1), 31 - it)
                if top_half:
                    cbits = cand ^ ((cand >> 31) & 0x7FFFFFFF)
                    cbits = jnp.where((cand > 0) & (cand < 0x00800000), 0x00800000, cbits)
                    cnt = count16(lax.bitcast_convert_type(cbits & -65536, F32).astype(jnp.bfloat16))
                else:
                    cnt = count(lambda kt, _: kt >= cand)
                take = cnt >= kf
                prefix = jnp.where(take, cand, prefix)
                nge = jnp.where(take, cnt, nge)
                settled = jnp.min(jnp.where(take_all | (nge == kf), 1.0, 0.0))
                return it + 1, prefix, nge, (settled > 0.0).astype(I32)
            return body

        carry = (jnp.int32(0), jnp.full((1, tq), INT_MIN, I32), jnp.zeros((1, tq), F32), jnp.int32(0))
        carry = lax.while_loop(lambda c: (c[0] < 16) & (c[3] == 0), bit_body(True), carry)
        _, prefix, nge, _ = lax.while_loop(lambda c: (c[0] < 32) & (c[3] == 0), bit_body(False), carry)
        found = prefix > INT_MIN
        thr = jnp.maximum(prefix, INT_MIN + 1)
        t_ref[0] = thr
        c_ref[0] = jnp.full((1, tq), 2 ** 30, I32)
        real = qpos - qoff < nreal
        straddle = jnp.max(jnp.where(found & (nge > kf) & real, 1.0, 0.0))

        @pl.when(straddle > 0.0)
        def _():
            need = kf - count(lambda kt, _: kt > thr)

            def tie_body(it, x):
                cx = x + lax.shift_left(jnp.int32(1), idx_bits - 1 - it)
                g = count(lambda kt, kpos: jnp.where(kt == thr, kpos, INT_MAX) < cx)
                return jnp.where(g < need, cx, x)
            c_ref[0] = lax.fori_loop(0, idx_bits, tie_body, jnp.zeros((1, tq), I32))


def _dsa_index(qi, ki2, wt, *, tq, tk, qoff, length, topk, nreal):
    b, s, _ = qi.shape
    lp = ki2.shape[1]
    nq, nk = s // tq, lp // tk
    idx_bits = max(1, int(np.ceil(np.log2(lp))))
    qvec = pl.BlockSpec((1, 1, tq), lambda bb, t, ii, jj: (bb, 0, ii[t]))
    return _tri_call(
        functools.partial(_dsa_index_kernel, tq=tq, tk=tk, nk=nk, unroll=2 if nk % 2 == 0 else 1,
                          qoff=qoff, length=length, topk=topk, idx_bits=idx_bits, nreal=nreal),
        _tri_steps(nq, tq, tk, qoff, length, True), b,
        in_specs=[pl.BlockSpec((1, tq, 512), lambda bb, t, ii, jj: (bb, ii[t], 0)),
                  pl.BlockSpec((1, tk, LANES), lambda bb, t, ii, jj: (bb, jj[t], 0)),
                  pl.BlockSpec((1, IDX_HEADS, tq), lambda bb, t, ii, jj: (bb, 0, ii[t]))],
        out_specs=[pl.BlockSpec((1, tk, tq), lambda bb, t, ii, jj: (bb, jj[t], ii[t])), qvec, qvec],
        out_shape=[jax.ShapeDtypeStruct((b, lp, s), I32),
                   jax.ShapeDtypeStruct((b, 1, s), I32),
                   jax.ShapeDtypeStruct((b, 1, s), I32)],
        scratch=[pltpu.VMEM((IDX_HEADS * tq, LANES), CDT),
                 pltpu.VMEM((nk, tk, tq), I32),
                 pltpu.VMEM((nk, tk, tq), jnp.bfloat16)],
        name="dsa_index",
    )(qi, ki2, wt)


def _dsa_attn_kernel(ii_ref, jj_ref, q_ref, k_ref, vt_ref, keys_ref, t_ref, c_ref, o_ref,
                     qst_ref, m_ref, l_ref, acc_ref, s_ref, *, tq, tk, qoff, length):
    i = ii_ref[pl.program_id(1)]
    j = jj_ref[pl.program_id(1)]
    last = _last_tile(i, tq, tk, qoff, length, True)

    @pl.when(j == 0)
    def _():
        _flash_init(q_ref, None, qst_ref, m_ref, l_ref, acc_ref, tq)

    @pl.when(j <= last)
    def _():
        kt = keys_ref[0]
        thr = t_ref[0]
        kpos = j * tk + lax.broadcasted_iota(I32, (tk, tq), 0)
        sel = (kt > thr) | (jnp.where(kt == thr, kpos, INT_MAX) <= c_ref[0])
        mb = jnp.where(sel, 0.0, NEG)
        mb2 = jnp.concatenate([mb, mb], axis=1)
        score = lambda g: _mm_nt(k_ref[0, :, g * LANES:(g + 1) * LANES], qst_ref[g]) + mb2
        _flash_pairs(score, s_ref, vt_ref, m_ref, l_ref, acc_ref, tq, True)

    @pl.when(j == last)
    def _():
        _flash_finish(o_ref, l_ref, acc_ref, tq)


def _dsa_attn(q, k, vt, keys, thr, cut, *, tq, tk, qoff, length):
    b, s, w = q.shape
    qmap = lambda bb, t, ii, jj: (bb, ii[t], 0)
    qvec = pl.BlockSpec((1, 1, tq), lambda bb, t, ii, jj: (bb, 0, ii[t]))
    npair = w // LANES
    return _tri_call(
        functools.partial(_dsa_attn_kernel, tq=tq, tk=tk, qoff=qoff, length=length),
        _tri_steps(s // tq, tq, tk, qoff, length, True), b,
        in_specs=[pl.BlockSpec((1, tq, w), qmap),
                  pl.BlockSpec((1, tk, w), lambda bb, t, ii, jj: (bb, jj[t], 0)),
                  pl.BlockSpec((1, w, tk), lambda bb, t, ii, jj: (bb, 0, jj[t])),
                  pl.BlockSpec((1, tk, tq), lambda bb, t, ii, jj: (bb, jj[t], ii[t])),
                  qvec, qvec],
        out_specs=pl.BlockSpec((1, tq, w), qmap),
        out_shape=jax.ShapeDtypeStruct((b, s, w), CDT),
        scratch=_flash_scratch(npair, tq, tk, LANES, w),
        name="dsa_attn",
    )(q, k, vt, keys, thr, cut)


def _pieces3(x):
    pcs = _pieces(x, 3)
    return pcs + [jnp.zeros_like(pcs[0])] * (3 - len(pcs))


FOX_EXT = 9


def _cumsum_kernel(x_ref, tri_ref, place_ref, ones_ref, kx_ref, carry_ref, *, scale):
    @pl.when(pl.program_id(1) == 0)
    def _():
        carry_ref[...] = jnp.zeros(carry_ref.shape, F32)
    tri = tri_ref[...]
    tot = None
    for p in _pieces(x_ref[0], 3):
        t = _mm(tri, p)
        tot = t if tot is None else tot + t
    acc = carry_ref[...] + tot
    carry_ref[...] = acc[-1:, :]
    ext = None
    for c, p in enumerate(_pieces3(acc * (-scale))):
        t = _mm(p, place_ref[c])
        ext = t if ext is None else ext + t
    kx_ref[0] = (ext + ones_ref[...]).astype(kx_ref.dtype)


def _fox_key_lanes(x, tb, scale, nh):
    b, s, w = x.shape
    npair = nh // 2
    place = np.zeros((3, w, npair * LANES), np.float32)
    ones = np.zeros((1, npair * LANES), np.float32)
    for h in range(nh):
        for c in range(3):
            place[c, h, (h // 2) * LANES + (h % 2) * 6 + c] = 1.0
    for g in range(npair):
        ones[0, g * LANES + 3:g * LANES + 6] = 1.0
    tri = jnp.tril(jnp.ones((tb, tb), F32)).astype(CDT)
    place = jnp.asarray(place, CDT)
    spec = lambda wd: pl.BlockSpec((1, tb, wd), lambda bb, i: (bb, i, 0))
    return pl.pallas_call(
        functools.partial(_cumsum_kernel, scale=scale),
        grid=(b, s // tb),
        in_specs=[spec(w), _const_spec(tri.shape), _const_spec(place.shape), _const_spec(ones.shape)],
        out_specs=spec(npair * LANES),
        out_shape=jax.ShapeDtypeStruct((b, s, npair * LANES), CDT),
        scratch_shapes=[pltpu.VMEM((1, w), F32)],
        compiler_params=_params(("parallel", "arbitrary")),
        name="logf_cumsum",
    )(x, tri, place, jnp.asarray(ones))


def _fox_kernel(ii_ref, jj_ref, q_ref, k_ref, kx_ref, vt_ref, aug_ref, o_ref,
                qst_ref, m_ref, l_ref, acc_ref, s_ref, *, tq, tk, qoff, length):
    i = ii_ref[pl.program_id(1)]
    j = jj_ref[pl.program_id(1)]
    last = _last_tile(i, tq, tk, qoff, length, False)
    q0 = qoff + i * tq

    @pl.when(j == 0)
    def _():
        _flash_init(q_ref, aug_ref, qst_ref, m_ref, l_ref, acc_ref, tq)

    def step(masked):
        if masked:
            kpos = j * tk + lax.broadcasted_iota(I32, (tk, 2 * tq), 0)
            lane = lax.broadcasted_iota(I32, (tk, 2 * tq), 1)
            causal = kpos <= q0 + jnp.where(lane >= tq, lane - tq, lane)
        def score(g):
            sl = slice(g * LANES, (g + 1) * LANES)
            st = _mm_nt(jnp.concatenate([k_ref[0, :, sl], kx_ref[0, :, sl]], axis=1), qst_ref[g])
            return jnp.where(causal, st, NEG) if masked else st
        _flash_pairs(score, s_ref, vt_ref, m_ref, l_ref, acc_ref, tq, False)

    diag = (j + 1) * tk - 1 > q0

    @pl.when((j <= last) & diag)
    def _():
        step(True)

    @pl.when((j <= last) & jnp.logical_not(diag))
    def _():
        step(False)

    @pl.when(j == last)
    def _():
        _flash_finish(o_ref, l_ref, acc_ref, tq)


def _fox(q, k, kx, vt, qaug, *, tq, tk, qoff, length):
    b, s, w = q.shape
    npair = w // LANES
    qmap = lambda bb, t, ii, jj: (bb, ii[t], 0)
    kmap = lambda bb, t, ii, jj: (bb, jj[t], 0)
    return _tri_call(
        functools.partial(_fox_kernel, tq=tq, tk=tk, qoff=qoff, length=length),
        _tri_steps(s // tq, tq, tk, qoff, length, False), b,
        in_specs=[pl.BlockSpec((1, tq, w), qmap),
                  pl.BlockSpec((1, tk, w), kmap),
                  pl.BlockSpec((1, tk, npair * LANES), kmap),
                  pl.BlockSpec((1, w, tk), lambda bb, t, ii, jj: (bb, 0, jj[t])),
                  pl.BlockSpec((1, 1, 2 * npair, LANES), lambda bb, t, ii, jj: (bb, ii[t], 0, 0))],
        out_specs=pl.BlockSpec((1, tq, w), qmap),
        out_shape=jax.ShapeDtypeStruct((b, s, w), CDT),
        scratch=_flash_scratch(npair, tq, tk, 2 * LANES, w),
        name="fox_attn",
    )(q, k, kx, vt, qaug)


def _fox_qaug(kx0):
    b, nq, _ = kx0.shape
    a = kx0.reshape(b, nq, -1, LANES).astype(F32)
    one, zero = jnp.ones_like(a[..., 0:3]), jnp.zeros_like(a[..., 0:3])
    even = jnp.concatenate([one, -a[..., 0:3], zero], axis=-1)
    odd = jnp.concatenate([zero, -a[..., 6:9], one], axis=-1)
    heads = jnp.stack([even, odd], axis=3).reshape(b, nq, -1, FOX_EXT)
    return jnp.pad(heads, ((0, 0), (0, 0), (0, 0), (0, LANES - FOX_EXT)))


def _rope_tables(pos):
    half = HEAD_DIM // 2
    inv = ROPE_THETA ** (-jnp.arange(half, dtype=F32) / half)
    ang = pos.astype(F32)[:, None] * inv[None, :]
    cos, sin = jnp.cos(ang), jnp.sin(ang)
    return (jnp.tile(jnp.concatenate([cos, cos], axis=1), (1, 2)),
            jnp.tile(jnp.concatenate([-sin, sin], axis=1), (1, 2)))


def _pad_rows(x, n):
    return x if x.shape[1] == n else jnp.pad(x, ((0, 0), (0, n - x.shape[1]), (0, 0)))


def _tiles(s):
    tk = 512 if s % 512 == 0 else s
    tq = 512 if s % 512 == 0 else s
    tqi = 256 if s % 256 == 0 else s
    return tq, tqi, tk


def kernel(x_prompt, x_sample, cache_a_k, cache_a_v, cache_b_k, cache_b_v, cache_b_ik, cache_c_k, cache_c_v,
           cache_c_logf, p_prompt, p_sample, g_mix, w_in_even, qn_a, kn_a, rel_bias_a, qn_b, kn_b, w_out_even,
           w_in_odd, b_f, qn_c, kn_c, w_out_odd, g_mlp, w_up, w_down, g_ple, w_ple_gate, w_ple_proj):
    B, S, D = x_prompt.shape
    DB, n, _ = x_sample.shape
    P = cache_b_k.shape[2]
    W_A = cache_a_k.shape[2]
    W_AP = min(A_PAST, S)
    depth = g_mix.shape[0]
    L = P + n
    Lp = -(-L // LANES) * LANES
    nqp = -(-n // LANES) * LANES
    assert B == 1 and S % 128 == 0 and n % 16 == 0 and P % 16 == 0

    row = lambda v: v.astype(F32).reshape(1, -1)
    bd = jnp.asarray(np.kron(np.eye(256 // HEAD_DIM), np.full((HEAD_DIM, HEAD_DIM), 1.0 / HEAD_DIM)), CDT)
    cos_p, sin_p = _rope_tables(jnp.arange(S))
    cos_s, sin_s = (jnp.tile(t, (DB, 1)) for t in _rope_tables(P + jnp.arange(n)))
    tq, tqi, tk = _tiles(S)
    tr = lambda t: jnp.swapaxes(t, 1, 2)
    catp = lambda c, new: _pad_rows(jnp.concatenate([c.astype(CDT), new], axis=1), Lp)

    outs = {name: [] for name in ("a_k_p", "a_v_p", "b_k_p", "b_v_p", "b_ik_p", "c_k_p", "c_v_p", "c_lf_p",
                                  "a_k_s", "a_v_s", "b_k_s", "b_v_s", "b_ik_s", "c_k_s", "c_v_s", "c_lf_s")}
    h_p = x_prompt.reshape(B * S, D)
    h_s = x_sample.reshape(DB * n, D)
    for i in range(depth):
        g = row(g_mix[i])
        if i % 2 == 0:
            e = i // 2
            wsz = 512
            w = w_in_even[e]
            w_all = jnp.concatenate([w[:, :7 * wsz], jnp.pad(w[:, 7 * wsz:], ((0, 0), (0, LANES - 72)))],
                                    axis=1).astype(CDT)
            gn = jnp.stack([jnp.tile(v[e].astype(F32), wsz // HEAD_DIM) for v in (qn_a, kn_a, qn_b, kn_b)])
            (ka, va, kb, vb, kw, qa_c, ka_c, va_c, qb_c, kb_c, vb_c, qi_c, ki2_c) = _proj_even(
                h_p, g, w_all, gn, bd, cos_p, sin_p)
            r3 = lambda t: t.reshape(B, S, t.shape[-1])
            tqa = 256 if S % 256 == 0 else 128
            npc = (A_PAST + tqa) // tqa
            kpad = jnp.pad(r3(ka_c), ((0, 0), (A_PAST, 0), (0, 0)))
            vpad = jnp.pad(r3(va_c), ((0, 0), (A_PAST, 0), (0, 0)))
            k_specs = [pl.BlockSpec((1, tqa, wsz), functools.partial(lambda bb, ii, pp: (bb, ii + pp, 0), pp=pp))
                       for pp in range(npc)]
            bias_p = _band_bias(rel_bias_a[e], np.arange(tqa), np.arange(-A_PAST, tqa))
            oa = _band(r3(qa_c), [kpad] * npc, [vpad] * npc, k_specs, bias_p, tqa, True)
            topk_p = min(TOPK_MAX, S // 4)
            wt = tr(r3(kw)[..., 64:64 + IDX_HEADS])
            keys, thr, cut = _dsa_index(r3(qi_c), r3(ki2_c), wt, tq=tqi, tk=tk, qoff=0, length=S, topk=topk_p,
                                        nreal=S)
            ob = _dsa_attn(r3(qb_c), r3(kb_c), tr(r3(vb_c)), keys, thr, cut, tq=tq, tk=tk, qoff=0, length=S)
            os_p = [oa.reshape(B * S, wsz), ob.reshape(B * S, wsz)]
            hd = lambda t: t.reshape(B, S, -1, HEAD_DIM)
            outs["a_k_p"].append(hd(ka)[:, S - W_AP:]); outs["a_v_p"].append(hd(va)[:, S - W_AP:])
            outs["b_k_p"].append(hd(kb)); outs["b_v_p"].append(hd(vb))
            outs["b_ik_p"].append(kw[:, :64].reshape(B, S, 64))
            (ka, va, kb, vb, kw, qa_c, ka_c, va_c, qb_c, kb_c, vb_c, qi_c, ki2_c) = _proj_even(
                h_s, g, w_all, gn, bd, cos_s, sin_s)
            r3 = lambda t: t.reshape(DB, n, t.shape[-1])
            ca_k = cache_a_k[e].reshape(DB, W_A, wsz)
            ca_v = cache_a_v[e].reshape(DB, W_A, wsz)
            full = lambda rows, wd: pl.BlockSpec((1, rows, wd), lambda bb, ii: (bb, 0, 0))
            bias_s = _band_bias(rel_bias_a[e], P + np.arange(n), P - W_A + np.arange(W_A + n))
            bias_s = jnp.pad(bias_s, ((0, 0), (0, 0), (0, -(W_A + n) % LANES)), constant_values=NEG)
            oa = _band(r3(qa_c), [ca_k, r3(ka)], [ca_v, r3(va)], [full(W_A, wsz), full(n, wsz)], bias_s, n, False)
            ik = cache_b_ik[e].astype(CDT)
            ki2_all = catp(jnp.concatenate([ik, ik], axis=-1), r3(ki2_c))
            kb_all = catp(cache_b_k[e].reshape(DB, P, wsz), r3(kb_c))
            vb_all = catp(cache_b_v[e].reshape(DB, P, wsz), r3(vb_c))
            topk_s = min(TOPK_MAX, L // 4)
            wt = _pad_rows(r3(kw)[..., 64:64 + IDX_HEADS], nqp)
            keys, thr, cut = _dsa_index(_pad_rows(r3(qi_c), nqp), ki2_all, tr(wt),
                                        tq=nqp, tk=Lp, qoff=P, length=L, topk=topk_s, nreal=n)
            ob = _dsa_attn(_pad_rows(r3(qb_c), nqp), kb_all, tr(vb_all), keys, thr, cut,
                           tq=nqp, tk=Lp, qoff=P, length=L)[:, :n]
            os_s = [oa.reshape(DB * n, wsz), ob.reshape(DB * n, wsz)]
            hd = lambda t: t.reshape(DB, n, -1, HEAD_DIM)
            outs["a_k_s"].append(jnp.concatenate([cache_a_k[e], hd(ka)], axis=1)[:, n:])
            outs["a_v_s"].append(jnp.concatenate([cache_a_v[e], hd(va)], axis=1)[:, n:])
            outs["b_k_s"].append(hd(kb)); outs["b_v_s"].append(hd(vb))
            outs["b_ik_s"].append(kw[:, :64].reshape(DB, n, 64))
            w_out = w_out_even[e].astype(CDT)
            wos = [w_out[:wsz], w_out[wsz:]]
        else:
            o = i // 2
            nh = b_f.shape[1]
            wsz = nh * HEAD_DIM
            w = w_in_odd[o]
            w_all = jnp.concatenate([w[:, :3 * wsz], jnp.pad(w[:, 3 * wsz:], ((0, 0), (0, LANES - nh)))],
                                    axis=1).astype(CDT)
            gn = jnp.stack([jnp.tile(v[o].astype(F32), wsz // HEAD_DIM) for v in (qn_c, kn_c)])
            bf = jnp.pad(b_f[o].astype(F32), (0, LANES - nh)).reshape(1, LANES)
            k, v, lf, q_c, k_c, v_c = _proj_odd(h_p, g, w_all, gn, bf, bd)
            r3 = lambda t: t.reshape(B, S, t.shape[-1])
            kx = _fox_key_lanes(r3(lf), 512 if S % 512 == 0 else S, LOG2E, nh)
            oc = _fox(r3(q_c), r3(k_c), kx, tr(r3(v_c)), _fox_qaug(kx[:, ::tq]), tq=tq, tk=tk, qoff=0, length=S)
            os_p = [oc.reshape(B * S, wsz)]
            hd = lambda t: t.reshape(B, S, nh, HEAD_DIM)
            outs["c_k_p"].append(hd(k)); outs["c_v_p"].append(hd(v)); outs["c_lf_p"].append(lf[:, :nh].reshape(B, S, nh))
            k, v, lf, q_c, k_c, v_c = _proj_odd(h_s, g, w_all, gn, bf, bd)
            r3 = lambda t: t.reshape(DB, n, t.shape[-1])
            lf_all = jnp.concatenate([jnp.pad(cache_c_logf[o].astype(F32), ((0, 0), (0, 0), (0, LANES - nh))), r3(lf)],
                                     axis=1)
            kx = _fox_key_lanes(lf_all, L, LOG2E, nh)
            k_all = catp(cache_c_k[o].reshape(DB, P, wsz), r3(k_c))
            v_all = catp(cache_c_v[o].reshape(DB, P, wsz), r3(v_c))
            oc = _fox(_pad_rows(r3(q_c), nqp), k_all, _pad_rows(kx, Lp), tr(v_all),
                      _fox_qaug(kx[:, P:P + 1]), tq=nqp, tk=Lp, qoff=P, length=L)[:, :n]
            os_s = [oc.reshape(DB * n, wsz)]
            hd = lambda t: t.reshape(DB, n, nh, HEAD_DIM)
            outs["c_k_s"].append(hd(k)); outs["c_v_s"].append(hd(v)); outs["c_lf_s"].append(lf[:, :nh].reshape(DB, n, nh))
            w_out = w_out_odd[o].astype(CDT)
            wos = [w_out]
        post_w = (row(g_mlp[i]), w_up[i].astype(CDT), w_down[i].astype(CDT), row(g_ple[i]),
                  w_ple_gate[i].astype(CDT), w_ple_proj[i].astype(CDT))
        h_p = _post(h_p, os_p, p_prompt[i].reshape(B * S, -1), wos, *post_w)
        h_s = _post(h_s, os_s, p_sample[i].reshape(DB * n, -1), wos, *post_w)

    st = jnp.stack
    names = ("a_k_p", "a_v_p", "b_k_p", "b_v_p", "b_ik_p", "c_k_p", "c_v_p", "c_lf_p",
             "a_k_s", "a_v_s", "b_k_s", "b_v_s", "b_ik_s", "c_k_s", "c_v_s", "c_lf_s")
    return (h_p.reshape(B, S, D), h_s.reshape(DB, n, D)) + tuple(st(outs[nm]) for nm in names)
```

```python
import functools

import numpy as np
import jax
import jax.numpy as jnp
from jax import lax
from jax.experimental import pallas as pl
from jax.experimental.pallas import tpu as pltpu

F32 = jnp.float32
I32 = jnp.int32
CDT = jnp.bfloat16

CHUNK = 64
HEAD_DIM = 64
IDX_HEADS = 8
A_PAST = 8 * CHUNK
REL_CLIP = 128
TOPK_MAX = 256
ROPE_THETA = 10000.0
RMS_EPS = 1e-6
LANES = 128
NEG = -1e30
INT_MIN = -(2 ** 31)
INT_MAX = 2 ** 31 - 1
LOG2E = 1.4426950408889634
VMEM_LIMIT = 56 * 1024 * 1024


def _mm(a, b):
    return jnp.dot(a, b, preferred_element_type=F32)


def _mm_nt(a, b):
    return lax.dot_general(a, b, (((1,), (1,)), ((), ())), preferred_element_type=F32)


def _pieces(x, n):
    if CDT == F32:
        return [x]
    out = []
    for _ in range(n - 1):
        p = x.astype(CDT)
        out.append(p)
        x = x - p.astype(F32)
    out.append(x.astype(CDT))
    return out


def _rms(x, g):
    return x * lax.rsqrt(jnp.mean(x * x, axis=-1, keepdims=True) + RMS_EPS) * g


def _head_rms(x, bd, g):
    x2 = x * x
    pcs = _pieces(x2, 2)
    cols = []
    for s in range(x.shape[1] // 256):
        sl = slice(s * 256, (s + 1) * 256)
        ms = _mm(pcs[0][:, sl], bd)
        for p in pcs[1:]:
            ms = ms + _mm(p[:, sl], bd)
        cols.append(ms)
    ms = cols[0] if len(cols) == 1 else jnp.concatenate(cols, axis=1)
    return x * lax.rsqrt(ms + RMS_EPS) * g


def _rope(x, cos, sin):
    w = x.shape[1]
    lane = lax.broadcasted_iota(I32, x.shape, 1)
    first = (lane & 63) < 32
    swapped = jnp.where(first, pltpu.roll(x, w - 32, 1), pltpu.roll(x, 32, 1))
    return x * cos + swapped * sin


def _tile_lanes(t, w):
    return t if w == t.shape[1] else jnp.concatenate([t] * (w // t.shape[1]), axis=1)


def _proj_even_kernel(h_ref, g_ref, w_ref, gn_ref, bd_ref, cos_ref, sin_ref,
                      ka_ref, va_ref, kb_ref, vb_ref, kw_ref,
                      qa_c, ka_c, va_c, qb_c, kb_c, vb_c, qi_c, ki2_c):
    a = _rms(h_ref[...], g_ref[...]).astype(CDT)
    z = _mm(a, w_ref[...])
    w = 512
    bd = bd_ref[...]
    gn = gn_ref[...]
    cos1, sin1 = cos_ref[...], sin_ref[...]
    cos, sin = _tile_lanes(cos1, w), _tile_lanes(sin1, w)
    scale = HEAD_DIM ** -0.5

    qa = _head_rms(z[:, 0:w], bd, gn[0:1])
    qa_c[...] = (qa * scale).astype(CDT)
    ka = _head_rms(z[:, w:2 * w], bd, gn[1:2])
    ka_ref[...] = ka
    ka_c[...] = ka.astype(CDT)
    va = z[:, 2 * w:3 * w]
    va_ref[...] = va
    va_c[...] = va.astype(CDT)
    qb = _rope(_head_rms(z[:, 3 * w:4 * w], bd, gn[2:3]), cos, sin)
    qb_c[...] = (qb * (scale * LOG2E)).astype(CDT)
    kb = _rope(_head_rms(z[:, 4 * w:5 * w], bd, gn[3:4]), cos, sin)
    kb_ref[...] = kb
    kb_c[...] = kb.astype(CDT)
    vb = z[:, 5 * w:6 * w]
    vb_ref[...] = vb
    vb_c[...] = vb.astype(CDT)
    qi = _rope(z[:, 6 * w:7 * w], cos, sin)
    qi_c[...] = (qi * scale).astype(CDT)
    kw = z[:, 7 * w:7 * w + LANES]
    kwr = _rope(kw, cos1, sin1)
    lane = lax.broadcasted_iota(I32, kw.shape, 1)
    kw_ref[...] = jnp.where(lane < 64, kwr, kw)
    ki2_c[...] = jnp.where(lane < 64, kwr, pltpu.roll(kwr, 64, 1)).astype(CDT)


def _proj_odd_kernel(h_ref, g_ref, w_ref, gn_ref, bf_ref, bd_ref,
                     k_ref, v_ref, lf_ref, q_c, k_c, v_c):
    a = _rms(h_ref[...], g_ref[...]).astype(CDT)
    z = _mm(a, w_ref[...])
    w = 1024
    bd = bd_ref[...]
    gn = gn_ref[...]
    q = _head_rms(z[:, 0:w], bd, gn[0:1])
    q_c[...] = (q * (HEAD_DIM ** -0.5 * LOG2E)).astype(CDT)
    k = _head_rms(z[:, w:2 * w], bd, gn[1:2])
    k_ref[...] = k
    k_c[...] = k.astype(CDT)
    v = z[:, 2 * w:3 * w]
    v_ref[...] = v
    v_c[...] = v.astype(CDT)
    fl = z[:, 3 * w:3 * w + LANES] + bf_ref[...]
    lf_ref[...] = jnp.minimum(fl, 0.0) - jnp.log1p(jnp.exp(-jnp.abs(fl)))


def _const_spec(shape):
    nd = len(shape)
    return pl.BlockSpec(shape, lambda *_: (0,) * nd, pipeline_mode=pl.Buffered(1))


def _row_spec(tm, w):
    return pl.BlockSpec((tm, w), lambda i: (i, 0))


def _params(sem):
    return pltpu.CompilerParams(dimension_semantics=sem, vmem_limit_bytes=VMEM_LIMIT)


def _row_tile(rows, pref):
    return pref if rows % pref == 0 else rows


def _proj_even(h, g, w_all, gn, bd, cos, sin):
    rows, d = h.shape
    tm = _row_tile(rows, 256)
    f = lambda w, dt: jax.ShapeDtypeStruct((rows, w), dt)
    out_shape = [f(512, F32)] * 4 + [f(LANES, F32)] + [f(512, CDT)] * 7 + [f(LANES, CDT)]
    out_specs = [_row_spec(tm, 512)] * 4 + [_row_spec(tm, LANES)] + [_row_spec(tm, 512)] * 7 + [_row_spec(tm, LANES)]
    return pl.pallas_call(
        _proj_even_kernel,
        grid=(rows // tm,),
        in_specs=[_row_spec(tm, d), _const_spec(g.shape), _const_spec(w_all.shape), _const_spec(gn.shape),
                  _const_spec(bd.shape), _row_spec(tm, LANES), _row_spec(tm, LANES)],
        out_specs=out_specs,
        out_shape=out_shape,
        compiler_params=_params(("parallel",)),
        name="proj_even",
    )(h, g, w_all, gn, bd, cos, sin)


def _proj_odd(h, g, w_all, gn, bf, bd):
    rows, d = h.shape
    tm = _row_tile(rows, 256)
    f = lambda w, dt: jax.ShapeDtypeStruct((rows, w), dt)
    out_shape = [f(1024, F32)] * 2 + [f(LANES, F32)] + [f(1024, CDT)] * 3
    out_specs = [_row_spec(tm, 1024)] * 2 + [_row_spec(tm, LANES)] + [_row_spec(tm, 1024)] * 3
    return pl.pallas_call(
        _proj_odd_kernel,
        grid=(rows // tm,),
        in_specs=[_row_spec(tm, d), _const_spec(g.shape), _const_spec(w_all.shape), _const_spec(gn.shape),
                  _const_spec(bf.shape), _const_spec(bd.shape)],
        out_specs=out_specs,
        out_shape=out_shape,
        compiler_params=_params(("parallel",)),
        name="proj_odd",
    )(h, g, w_all, gn, bf, bd)


def _post_kernel(*refs, n_o, ff_chunk):
    h_ref = refs[0]
    o_refs = refs[1:1 + n_o]
    p_ref = refs[1 + n_o]
    wo_refs = refs[2 + n_o:2 + 2 * n_o]
    gm_ref, wu_ref, wd_ref, gp_ref, wg_ref, wp_ref, out_ref = refs[2 + 2 * n_o:]
    mix = None
    for o_ref, wo_ref in zip(o_refs, wo_refs):
        t = _mm(o_ref[...], wo_ref[...])
        mix = t if mix is None else mix + t
    h = h_ref[...] + mix
    m = _rms(h, gm_ref[...]).astype(CDT)
    mlp = None
    d_ff = wu_ref.shape[1]
    for c in range(d_ff // ff_chunk):
        sl = slice(c * ff_chunk, (c + 1) * ff_chunk)
        u = jnp.square(jnp.maximum(_mm(m, wu_ref[:, sl]), 0.0)).astype(CDT)
        t = _mm(u, wd_ref[sl, :])
        mlp = t if mlp is None else mlp + t
    acc = h + mlp
    gate_in = _rms(acc, gp_ref[...]).astype(CDT)
    gate = 1.0 / (1.0 + jnp.exp(-_mm(gate_in, wg_ref[...])))
    out_ref[...] = acc + gate * _mm(p_ref[...].astype(CDT), wp_ref[...])


def _post(h, os_, p, wos, gm, wu, wd, gp, wg, wp):
    rows, d = h.shape
    tm = _row_tile(rows, 512)
    n_o = len(os_)
    in_specs = ([_row_spec(tm, d)] + [_row_spec(tm, o.shape[1]) for o in os_] + [_row_spec(tm, p.shape[1])]
                + [_const_spec(w.shape) for w in wos]
                + [_const_spec(x.shape) for x in (gm, wu, wd, gp, wg, wp)])
    return pl.pallas_call(
        functools.partial(_post_kernel, n_o=n_o, ff_chunk=1024),
        grid=(rows // tm,),
        in_specs=in_specs,
        out_specs=_row_spec(tm, d),
        out_shape=jax.ShapeDtypeStruct((rows, d), F32),
        compiler_params=_params(("parallel",)),
        name="post",
    )(h, *os_, p, *wos, gm, wu, wd, gp, wg, wp)


def _pair_stack(qp, tq):
    lane = lax.broadcasted_iota(I32, qp.shape, 1)
    zero = jnp.zeros_like(qp)
    return jnp.concatenate([jnp.where(lane < 64, qp, zero), jnp.where(lane >= 64, qp, zero)], axis=0)


def _pair_merge(o2, tq):
    lane = lax.broadcasted_iota(I32, (tq, LANES), 1)
    return jnp.where(lane < 64, o2[:tq], o2[tq:])


def _last_tile(i, tq, tk, qoff, length, chunked):
    qend = qoff + i * tq + tq - 1
    kmax = (qend // CHUNK + 1) * CHUNK if chunked else qend + 1
    kmax = jnp.minimum(kmax, length)
    return (kmax - 1) // tk


def _tri_steps(nq, tq, tk, qoff, length, chunked):
    ii, jj = [], []
    for i in range(nq):
        qend = qoff + i * tq + tq - 1
        kmax = min((qend // CHUNK + 1) * CHUNK if chunked else qend + 1, length)
        for j in range((kmax - 1) // tk + 1):
            ii.append(i)
            jj.append(j)
    return jnp.asarray(ii, I32), jnp.asarray(jj, I32)


def _tri_call(kernel_fn, steps, batch, in_specs, out_specs, out_shape, scratch, name):
    ii, jj = steps
    grid_spec = pltpu.PrefetchScalarGridSpec(
        num_scalar_prefetch=2, grid=(batch, ii.shape[0]),
        in_specs=in_specs, out_specs=out_specs, scratch_shapes=scratch)
    call = pl.pallas_call(kernel_fn, grid_spec=grid_spec, out_shape=out_shape,
                          compiler_params=_params(("parallel", "arbitrary")), name=name)
    return functools.partial(call, ii, jj)


def _flash_init(q_ref, aug_ref, qst_ref, m_ref, l_ref, acc_ref, tq):
    q = q_ref[0]
    for g in range(qst_ref.shape[0]):
        st = _pair_stack(q[:, g * LANES:(g + 1) * LANES], tq)
        if aug_ref is not None:
            a = aug_ref[0, 0]
            ext = jnp.concatenate([jnp.broadcast_to(a[2 * g:2 * g + 1], (tq, LANES)),
                                   jnp.broadcast_to(a[2 * g + 1:2 * g + 2], (tq, LANES))], axis=0)
            st = jnp.concatenate([st, ext.astype(CDT)], axis=1)
        qst_ref[g] = st
    m_ref[...] = jnp.full(m_ref.shape, NEG, F32)
    l_ref[...] = jnp.zeros(l_ref.shape, F32)
    acc_ref[...] = jnp.zeros(acc_ref.shape, F32)


def _flash_pairs(score_fn, vt_ref, scr, tq, mxu_sum):
    _, m_ref, l_ref, acc_ref, s_ref = scr
    npair = m_ref.shape[0]
    s_ref[0] = score_fn(0)
    for g in range(npair):
        if g + 1 < npair:
            s_ref[(g + 1) % 2] = score_fn(g + 1)
        _flash_step(s_ref[g % 2], vt_ref, m_ref, l_ref, acc_ref, g, tq, mxu_sum)


def _flash_step(st, vt_ref, m_ref, l_ref, acc_ref, g, tq, mxu_sum):
    m_old = m_ref[g]
    m_new = jnp.maximum(m_old, jnp.max(st, axis=0, keepdims=True))
    alpha = jnp.exp2(m_old - m_new)
    m_ref[g] = m_new
    if mxu_sum:
        pc = jnp.exp2((st - m_new).astype(CDT))
        ones = jnp.ones((16, st.shape[0]), CDT)
    else:
        p = jnp.exp2(st - m_new)
        pc = p.astype(CDT)
    sums = []
    for e in range(2):
        rows = slice(g * LANES + e * HEAD_DIM, g * LANES + (e + 1) * HEAD_DIM)
        cols = slice(e * tq, (e + 1) * tq)
        vt = vt_ref[0, rows, :]
        pv = _mm(jnp.concatenate([vt, ones], axis=0) if mxu_sum else vt, pc[:, cols])
        acc_ref[rows, :] = alpha[:, cols] * acc_ref[rows, :] + pv[:HEAD_DIM]
        if mxu_sum:
            sums.append(pv[HEAD_DIM:HEAD_DIM + 1])
    psum = jnp.concatenate(sums, axis=1) if mxu_sum else jnp.sum(p, axis=0, keepdims=True)
    l_ref[g] = alpha * l_ref[g] + psum


def _flash_finish(o_ref, l_ref, acc_ref, tq):
    for g in range(l_ref.shape[0]):
        l = l_ref[g]
        den = jnp.concatenate([jnp.broadcast_to(l[:, :tq], (HEAD_DIM, tq)),
                               jnp.broadcast_to(l[:, tq:], (HEAD_DIM, tq))], axis=0)
        o = acc_ref[g * LANES:(g + 1) * LANES, :] / den
        o_ref[0, :, g * LANES:(g + 1) * LANES] = o.T.astype(o_ref.dtype)


def _flash_scratch(npair, tq, tk, kd, w):
    return [pltpu.VMEM((npair, 2 * tq, kd), CDT),
            pltpu.VMEM((npair, 1, 2 * tq), F32),
            pltpu.VMEM((npair, 1, 2 * tq), F32),
            pltpu.VMEM((w, tq), F32),
            pltpu.VMEM((2, tk, 2 * tq), F32)]


def _band_kernel(*refs, n_kv, tq, mask_neg):
    q_ref = refs[0]
    k_refs = refs[1:1 + n_kv]
    v_refs = refs[1 + n_kv:1 + 2 * n_kv]
    bias_ref = refs[1 + 2 * n_kv]
    o_ref = refs[2 + 2 * n_kv]
    i = pl.program_id(1)
    q = q_ref[0]
    cat = lambda rs: (rs[0][0].astype(CDT) if len(rs) == 1
                      else jnp.concatenate([r[0].astype(CDT) for r in rs], axis=0))
    k = cat(k_refs)
    v = cat(v_refs)
    tk = bias_ref.shape[2]
    if k.shape[0] < tk:
        zpad = jnp.zeros((tk - k.shape[0], k.shape[1]), CDT)
        k = jnp.concatenate([k, zpad], axis=0)
        v = jnp.concatenate([v, zpad], axis=0)
    if mask_neg:
        kpos = lax.broadcasted_iota(I32, (2 * tq, tk), 1) + (i * tq - A_PAST)
        neg = kpos < 0
    for g in range(q.shape[1] // LANES):
        sl = slice(g * LANES, (g + 1) * LANES)
        s = _mm_nt(_pair_stack(q[:, sl], tq), k[:, sl])
        s = s + jnp.concatenate([bias_ref[2 * g], bias_ref[2 * g + 1]], axis=0)
        if mask_neg:
            s = jnp.where(neg, NEG, s)
        m = jnp.max(s, axis=-1, keepdims=True)
        p = jnp.exp(s - m)
        l = jnp.sum(p, axis=-1, keepdims=True)
        o2 = _mm(p.astype(CDT), v[:, sl]) / l
        o_ref[0, :, sl] = _pair_merge(o2, tq).astype(o_ref.dtype)


def _band(q, k_parts, v_parts, k_specs, bias, tq, mask_neg):
    b, s, w = q.shape
    n_kv = len(k_parts)
    q_spec = pl.BlockSpec((1, tq, w), lambda bb, i: (bb, i, 0))
    return pl.pallas_call(
        functools.partial(_band_kernel, n_kv=n_kv, tq=tq, mask_neg=mask_neg),
        grid=(b, s // tq),
        in_specs=[q_spec] + k_specs + k_specs + [_const_spec(bias.shape)],
        out_specs=q_spec,
        out_shape=jax.ShapeDtypeStruct((b, s, w), CDT),
        compiler_params=_params(("parallel", "parallel")),
        name="band_attn",
    )(q, *k_parts, *v_parts, bias)


def _band_bias(rel_bias, q_pos, k_pos):
    nq, nk = len(q_pos), len(k_pos)
    qc = q_pos[:, None] // CHUNK
    kc = k_pos[None, :] // CHUNK
    ok = (kc <= qc) & (kc >= qc - A_PAST // CHUNK)
    n = nq + nk - 1
    d0 = int(q_pos[0] - k_pos[0])
    m = np.arange(n + 1)
    rel = np.where(m < nk, d0 - m, d0 - m + n + 1)
    table = jnp.take(rel_bias.astype(F32), jnp.asarray(np.clip(rel, -REL_CLIP, REL_CLIP) + REL_CLIP), axis=1)
    bias = jnp.tile(table, (1, nq))[:, :nq * n].reshape(-1, nq, n)[:, :, :nk]
    return jnp.where(jnp.asarray(ok)[None], bias, NEG)


def _dsa_index_kernel(ii_ref, jj_ref, qi_ref, ki2_ref, wt_ref, keys_ref, t_ref, c_ref, qst_ref, kscr_ref, k16_ref,
                      *, tq, tk, nk, unroll, qoff, length, topk, idx_bits, nreal):
    i = ii_ref[pl.program_id(1)]
    j = jj_ref[pl.program_id(1)]
    last = _last_tile(i, tq, tk, qoff, length, True)
    q0 = qoff + i * tq

    @pl.when(j == 0)
    def _():
        q = qi_ref[0]
        for g in range(IDX_HEADS // 2):
            qst_ref[2 * g * tq:(2 * g + 2) * tq, :] = _pair_stack(q[:, g * LANES:(g + 1) * LANES], tq)

    @pl.when(j <= last)
    def _():
        lg = _mm_nt(ki2_ref[0], qst_ref[...])
        wgt = wt_ref[0] * (IDX_HEADS ** -0.5)
        sc = jnp.zeros((tk, tq), F32)
        for h in range(IDX_HEADS):
            sc = sc + jnp.maximum(lg[:, h * tq:(h + 1) * tq], 0.0) * wgt[h:h + 1, :]
        sc = jnp.where(sc == 0.0, 0.0, sc)
        bits = lax.bitcast_convert_type(sc, I32)
        key = bits ^ ((bits >> 31) & 0x7FFFFFFF)
        kpos = j * tk + lax.broadcasted_iota(I32, (tk, tq), 0)
        qpos = q0 + lax.broadcasted_iota(I32, (tk, tq), 1)
        ok = ((kpos >> 6) <= (qpos >> 6)) & (kpos < length)
        key = jnp.where(ok, key, INT_MIN)
        keys_ref[0] = key
        kscr_ref[j] = key
        top = lax.bitcast_convert_type(bits & -65536, F32)
        k16_ref[j] = jnp.where(ok, top, -jnp.inf).astype(jnp.bfloat16)
        if unroll > 1:
            @pl.when(j + 1 < nk)
            def _():
                kscr_ref[j + 1] = jnp.full((tk, tq), INT_MIN, I32)
                k16_ref[j + 1] = jnp.full((tk, tq), -jnp.inf, jnp.bfloat16)

    @pl.when(j == last)
    def _():
        def count(pred):
            def body(t, acc):
                for u in range(unroll):
                    tt = unroll * t + u
                    kpos = tt * tk + lax.broadcasted_iota(I32, (tk, tq), 0)
                    m = jnp.where(pred(kscr_ref[tt], kpos), 1.0, 0.0)
                    acc = acc + jnp.sum(m, axis=0, keepdims=True)
                return acc
            return lax.fori_loop(0, (last + unroll) // unroll, body, jnp.zeros((1, tq), F32))

        kf = float(topk)
        qpos = q0 + lax.broadcasted_iota(I32, (1, tq), 1)
        n_adm = jnp.minimum(((qpos >> 6) + 1) << 6, length).astype(F32)
        real = qpos - qoff < nreal
        take_all = (n_adm < kf) | jnp.logical_not(real)

        def count16(c16):
            one, zero = jnp.ones((), jnp.bfloat16), jnp.zeros((), jnp.bfloat16)

            def body(t, acc):
                for u in range(unroll):
                    m = jnp.where(k16_ref[unroll * t + u] >= c16, one, zero).reshape(tk // 16, 16, tq)
                    part = m[0]
                    for r in range(1, tk // 16):
                        part = part + m[r]
                    acc = acc + jnp.sum(part.astype(F32), axis=0, keepdims=True)
                return acc
            return lax.fori_loop(0, (last + unroll) // unroll, body, jnp.zeros((1, tq), F32))

        def bit_body(top_half):
            def body(carry):
                it, prefix, nge, _ = carry
                cand = prefix + lax.shift_left(jnp.int32(1), 31 - it)
                if top_half:
                    cbits = cand ^ ((cand >> 31) & 0x7FFFFFFF)
                    cbits = jnp.where((cand > 0) & (cand < 0x00800000), 0x00800000, cbits)
                    cnt = count16(lax.bitcast_convert_type(cbits & -65536, F32).astype(jnp.bfloat16))
                else:
                    cnt = count(lambda kt, _: kt >= cand)
                take = cnt >= kf
                prefix = jnp.where(take, cand, prefix)
                nge = jnp.where(take, cnt, nge)
                settled = jnp.min(jnp.where(take_all | (nge == kf), 1.0, 0.0))
                return it + 1, prefix, nge, (settled > 0.0).astype(I32)
            return body

        carry = (jnp.int32(0), jnp.full((1, tq), INT_MIN, I32), jnp.zeros((1, tq), F32), jnp.int32(0))
        carry = lax.while_loop(lambda c: (c[0] < 16) & (c[3] == 0), bit_body(True), carry)
        _, prefix, nge, _ = lax.while_loop(lambda c: (c[0] < 32) & (c[3] == 0), bit_body(False), carry)
        found = prefix > INT_MIN
        thr = jnp.maximum(prefix, INT_MIN + 1)
        t_ref[0] = thr
        c_ref[0] = jnp.full((1, tq), 2 ** 30, I32)
        straddle = jnp.max(jnp.where(found & (nge > kf) & real, 1.0, 0.0))

        @pl.when(straddle > 0.0)
        def _():
            need = kf - count(lambda kt, _: kt > thr)

            def tie_body(it, x):
                cx = x + lax.shift_left(jnp.int32(1), idx_bits - 1 - it)
                g = count(lambda kt, kpos: jnp.where(kt == thr, kpos, INT_MAX) < cx)
                return jnp.where(g < need, cx, x)
            c_ref[0] = lax.fori_loop(0, idx_bits, tie_body, jnp.zeros((1, tq), I32))


def _dsa_index(qi, ki2, wt, *, tq, tk, qoff, length, topk, nreal):
    b, s, _ = qi.shape
    lp = ki2.shape[1]
    nq, nk = s // tq, lp // tk
    idx_bits = max(1, int(np.ceil(np.log2(lp))))
    qvec = pl.BlockSpec((1, 1, tq), lambda bb, t, ii, jj: (bb, 0, ii[t]))
    return _tri_call(
        functools.partial(_dsa_index_kernel, tq=tq, tk=tk, nk=nk, unroll=2 if nk % 2 == 0 else 1,
                          qoff=qoff, length=length, topk=topk, idx_bits=idx_bits, nreal=nreal),
        _tri_steps(nq, tq, tk, qoff, length, True), b,
        in_specs=[pl.BlockSpec((1, tq, 512), lambda bb, t, ii, jj: (bb, ii[t], 0)),
                  pl.BlockSpec((1, tk, LANES), lambda bb, t, ii, jj: (bb, jj[t], 0)),
                  pl.BlockSpec((1, IDX_HEADS, tq), lambda bb, t, ii, jj: (bb, 0, ii[t]))],
        out_specs=[pl.BlockSpec((1, tk, tq), lambda bb, t, ii, jj: (bb, jj[t], ii[t])), qvec, qvec],
        out_shape=[jax.ShapeDtypeStruct((b, lp, s), I32),
                   jax.ShapeDtypeStruct((b, 1, s), I32),
                   jax.ShapeDtypeStruct((b, 1, s), I32)],
        scratch=[pltpu.VMEM((IDX_HEADS * tq, LANES), CDT),
                 pltpu.VMEM((nk, tk, tq), I32),
                 pltpu.VMEM((nk, tk, tq), jnp.bfloat16)],
        name="dsa_index",
    )(qi, ki2, wt)


def _dsa_attn_kernel(ii_ref, jj_ref, q_ref, k_ref, vt_ref, keys_ref, t_ref, c_ref, o_ref, *scr,
                     tq, tk, qoff, length):
    qst_ref, m_ref, l_ref, acc_ref = scr[:4]
    i = ii_ref[pl.program_id(1)]
    j = jj_ref[pl.program_id(1)]
    last = _last_tile(i, tq, tk, qoff, length, True)

    @pl.when(j == 0)
    def _():
        _flash_init(q_ref, None, qst_ref, m_ref, l_ref, acc_ref, tq)

    @pl.when(j <= last)
    def _():
        kt = keys_ref[0]
        thr = t_ref[0]
        kpos = j * tk + lax.broadcasted_iota(I32, (tk, tq), 0)
        sel = (kt > thr) | (jnp.where(kt == thr, kpos, INT_MAX) <= c_ref[0])
        mb = jnp.where(sel, 0.0, NEG)
        mb2 = jnp.concatenate([mb, mb], axis=1)
        score = lambda g: _mm_nt(k_ref[0, :, g * LANES:(g + 1) * LANES], qst_ref[g]) + mb2
        _flash_pairs(score, vt_ref, scr, tq, True)

    @pl.when(j == last)
    def _():
        _flash_finish(o_ref, l_ref, acc_ref, tq)


def _dsa_attn(q, k, vt, keys, thr, cut, *, tq, tk, qoff, length):
    b, s, w = q.shape
    qmap = lambda bb, t, ii, jj: (bb, ii[t], 0)
    qvec = pl.BlockSpec((1, 1, tq), lambda bb, t, ii, jj: (bb, 0, ii[t]))
    npair = w // LANES
    return _tri_call(
        functools.partial(_dsa_attn_kernel, tq=tq, tk=tk, qoff=qoff, length=length),
        _tri_steps(s // tq, tq, tk, qoff, length, True), b,
        in_specs=[pl.BlockSpec((1, tq, w), qmap),
                  pl.BlockSpec((1, tk, w), lambda bb, t, ii, jj: (bb, jj[t], 0)),
                  pl.BlockSpec((1, w, tk), lambda bb, t, ii, jj: (bb, 0, jj[t])),
                  pl.BlockSpec((1, tk, tq), lambda bb, t, ii, jj: (bb, jj[t], ii[t])),
                  qvec, qvec],
        out_specs=pl.BlockSpec((1, tq, w), qmap),
        out_shape=jax.ShapeDtypeStruct((b, s, w), CDT),
        scratch=_flash_scratch(npair, tq, tk, LANES, w),
        name="dsa_attn",
    )(q, k, vt, keys, thr, cut)


def _pieces3(x):
    pcs = _pieces(x, 3)
    return pcs + [jnp.zeros_like(pcs[0])] * (3 - len(pcs))


FOX_EXT = 9


def _cumsum_kernel(x_ref, tri_ref, place_ref, ones_ref, kx_ref, carry_ref, *, scale):
    @pl.when(pl.program_id(1) == 0)
    def _():
        carry_ref[...] = jnp.zeros(carry_ref.shape, F32)
    tri = tri_ref[...]
    tot = None
    for p in _pieces(x_ref[0], 3):
        t = _mm(tri, p)
        tot = t if tot is None else tot + t
    acc = carry_ref[...] + tot
    carry_ref[...] = acc[-1:, :]
    ext = None
    for c, p in enumerate(_pieces3(acc * (-scale))):
        t = _mm(p, place_ref[c])
        ext = t if ext is None else ext + t
    kx_ref[0] = (ext + ones_ref[...]).astype(kx_ref.dtype)


def _fox_key_lanes(x, tb, scale, nh):
    b, s, w = x.shape
    npair = nh // 2
    place = np.zeros((3, w, npair * LANES), np.float32)
    ones = np.zeros((1, npair * LANES), np.float32)
    for h in range(nh):
        for c in range(3):
            place[c, h, (h // 2) * LANES + (h % 2) * 6 + c] = 1.0
    for g in range(npair):
        ones[0, g * LANES + 3:g * LANES + 6] = 1.0
    tri = jnp.tril(jnp.ones((tb, tb), F32)).astype(CDT)
    place = jnp.asarray(place, CDT)
    spec = lambda wd: pl.BlockSpec((1, tb, wd), lambda bb, i: (bb, i, 0))
    return pl.pallas_call(
        functools.partial(_cumsum_kernel, scale=scale),
        grid=(b, s // tb),
        in_specs=[spec(w), _const_spec(tri.shape), _const_spec(place.shape), _const_spec(ones.shape)],
        out_specs=spec(npair * LANES),
        out_shape=jax.ShapeDtypeStruct((b, s, npair * LANES), CDT),
        scratch_shapes=[pltpu.VMEM((1, w), F32)],
        compiler_params=_params(("parallel", "arbitrary")),
        name="logf_cumsum",
    )(x, tri, place, jnp.asarray(ones))


def _fox_kernel(ii_ref, jj_ref, q_ref, k_ref, kx_ref, vt_ref, aug_ref, o_ref, *scr, tq, tk, qoff, length):
    qst_ref, m_ref, l_ref, acc_ref = scr[:4]
    i = ii_ref[pl.program_id(1)]
    j = jj_ref[pl.program_id(1)]
    last = _last_tile(i, tq, tk, qoff, length, False)
    q0 = qoff + i * tq

    @pl.when(j == 0)
    def _():
        _flash_init(q_ref, aug_ref, qst_ref, m_ref, l_ref, acc_ref, tq)

    def step(masked):
        if masked:
            kpos = j * tk + lax.broadcasted_iota(I32, (tk, 2 * tq), 0)
            lane = lax.broadcasted_iota(I32, (tk, 2 * tq), 1)
            causal = kpos <= q0 + jnp.where(lane >= tq, lane - tq, lane)

        def score(g):
            sl = slice(g * LANES, (g + 1) * LANES)
            st = _mm_nt(jnp.concatenate([k_ref[0, :, sl], kx_ref[0, :, sl]], axis=1), qst_ref[g])
            return jnp.where(causal, st, NEG) if masked else st
        _flash_pairs(score, vt_ref, scr, tq, False)

    diag = (j + 1) * tk - 1 > q0

    @pl.when((j <= last) & diag)
    def _():
        step(True)

    @pl.when((j <= last) & jnp.logical_not(diag))
    def _():
        step(False)

    @pl.when(j == last)
    def _():
        _flash_finish(o_ref, l_ref, acc_ref, tq)


def _fox(q, k, kx, vt, qaug, *, tq, tk, qoff, length):
    b, s, w = q.shape
    npair = w // LANES
    qmap = lambda bb, t, ii, jj: (bb, ii[t], 0)
    kmap = lambda bb, t, ii, jj: (bb, jj[t], 0)
    return _tri_call(
        functools.partial(_fox_kernel, tq=tq, tk=tk, qoff=qoff, length=length),
        _tri_steps(s // tq, tq, tk, qoff, length, False), b,
        in_specs=[pl.BlockSpec((1, tq, w), qmap),
                  pl.BlockSpec((1, tk, w), kmap),
                  pl.BlockSpec((1, tk, npair * LANES), kmap),
                  pl.BlockSpec((1, w, tk), lambda bb, t, ii, jj: (bb, 0, jj[t])),
                  pl.BlockSpec((1, 1, 2 * npair, LANES), lambda bb, t, ii, jj: (bb, ii[t], 0, 0))],
        out_specs=pl.BlockSpec((1, tq, w), qmap),
        out_shape=jax.ShapeDtypeStruct((b, s, w), CDT),
        scratch=_flash_scratch(npair, tq, tk, 2 * LANES, w),
        name="fox_attn",
    )(q, k, kx, vt, qaug)


def _fox_qaug(kx0):
    b, nq, _ = kx0.shape
    a = kx0.reshape(b, nq, -1, LANES).astype(F32)
    one, zero = jnp.ones_like(a[..., 0:3]), jnp.zeros_like(a[..., 0:3])
    even = jnp.concatenate([one, -a[..., 0:3], zero], axis=-1)
    odd = jnp.concatenate([zero, -a[..., 6:9], one], axis=-1)
    heads = jnp.stack([even, odd], axis=3).reshape(b, nq, -1, FOX_EXT)
    return jnp.pad(heads, ((0, 0), (0, 0), (0, 0), (0, LANES - FOX_EXT)))


def _rope_tables(pos):
    half = HEAD_DIM // 2
    inv = ROPE_THETA ** (-jnp.arange(half, dtype=F32) / half)
    ang = pos.astype(F32)[:, None] * inv[None, :]
    cos, sin = jnp.cos(ang), jnp.sin(ang)
    return (jnp.tile(jnp.concatenate([cos, cos], axis=1), (1, 2)),
            jnp.tile(jnp.concatenate([-sin, sin], axis=1), (1, 2)))


def _pad_rows(x, n):
    return x if x.shape[1] == n else jnp.pad(x, ((0, 0), (0, n - x.shape[1]), (0, 0)))


def _tiles(s):
    tk = 512 if s % 512 == 0 else s
    tq = 512 if s % 512 == 0 else s
    tqi = 256 if s % 256 == 0 else s
    return tq, tqi, tk


def kernel(x_prompt, x_sample, cache_a_k, cache_a_v, cache_b_k, cache_b_v, cache_b_ik, cache_c_k, cache_c_v,
           cache_c_logf, p_prompt, p_sample, g_mix, w_in_even, qn_a, kn_a, rel_bias_a, qn_b, kn_b, w_out_even,
           w_in_odd, b_f, qn_c, kn_c, w_out_odd, g_mlp, w_up, w_down, g_ple, w_ple_gate, w_ple_proj):
    B, S, D = x_prompt.shape
    DB, n, _ = x_sample.shape
    P = cache_b_k.shape[2]
    W_A = cache_a_k.shape[2]
    W_AP = min(A_PAST, S)
    depth = g_mix.shape[0]
    L = P + n
    Lp = -(-L // LANES) * LANES
    nqp = -(-n // LANES) * LANES
    assert B == 1 and S % 128 == 0 and n % 16 == 0 and P % 16 == 0

    row = lambda v: v.astype(F32).reshape(1, -1)
    bd = jnp.asarray(np.kron(np.eye(256 // HEAD_DIM), np.full((HEAD_DIM, HEAD_DIM), 1.0 / HEAD_DIM)), CDT)
    cos_p, sin_p = _rope_tables(jnp.arange(S))
    cos_s, sin_s = (jnp.tile(t, (DB, 1)) for t in _rope_tables(P + jnp.arange(n)))
    tq, tqi, tk = _tiles(S)
    tr = lambda t: jnp.swapaxes(t, 1, 2)
    catp = lambda c, new: _pad_rows(jnp.concatenate([c.astype(CDT), new], axis=1), Lp)

    outs = {name: [] for name in ("a_k_p", "a_v_p", "b_k_p", "b_v_p", "b_ik_p", "c_k_p", "c_v_p", "c_lf_p",
                                  "a_k_s", "a_v_s", "b_k_s", "b_v_s", "b_ik_s", "c_k_s", "c_v_s", "c_lf_s")}
    h_p = x_prompt.reshape(B * S, D)
    h_s = x_sample.reshape(DB * n, D)
    for i in range(depth):
        g = row(g_mix[i])
        if i % 2 == 0:
            e = i // 2
            wsz = 512
            w = w_in_even[e]
            w_all = jnp.concatenate([w[:, :7 * wsz], jnp.pad(w[:, 7 * wsz:], ((0, 0), (0, LANES - 72)))],
                                    axis=1).astype(CDT)
            gn = jnp.stack([jnp.tile(v[e].astype(F32), wsz // HEAD_DIM) for v in (qn_a, kn_a, qn_b, kn_b)])
            (ka, va, kb, vb, kw, qa_c, ka_c, va_c, qb_c, kb_c, vb_c, qi_c, ki2_c) = _proj_even(
                h_p, g, w_all, gn, bd, cos_p, sin_p)
            r3 = lambda t: t.reshape(B, S, t.shape[-1])
            tqa = 256 if S % 256 == 0 else 128
            npc = (A_PAST + tqa) // tqa
            k_specs = [pl.BlockSpec((1, tqa, wsz), functools.partial(
                lambda bb, ii, pp: (bb, jnp.maximum(ii + pp - A_PAST // tqa, 0), 0), pp=pp)) for pp in range(npc)]
            bias_p = _band_bias(rel_bias_a[e], np.arange(tqa), np.arange(-A_PAST, tqa))
            oa = _band(r3(qa_c), [r3(ka_c)] * npc, [r3(va_c)] * npc, k_specs, bias_p, tqa, True)
            topk_p = min(TOPK_MAX, S // 4)
            wt = tr(r3(kw)[..., 64:64 + IDX_HEADS])
            keys, thr, cut = _dsa_index(r3(qi_c), r3(ki2_c), wt, tq=tqi, tk=tk, qoff=0, length=S, topk=topk_p,
                                        nreal=S)
            ob = _dsa_attn(r3(qb_c), r3(kb_c), tr(r3(vb_c)), keys, thr, cut, tq=tq, tk=tk, qoff=0, length=S)
            os_p = [oa.reshape(B * S, wsz), ob.reshape(B * S, wsz)]
            hd = lambda t: t.reshape(B, S, -1, HEAD_DIM)
            outs["a_k_p"].append(hd(ka)[:, S - W_AP:]); outs["a_v_p"].append(hd(va)[:, S - W_AP:])
            outs["b_k_p"].append(hd(kb)); outs["b_v_p"].append(hd(vb))
            outs["b_ik_p"].append(kw[:, :64].reshape(B, S, 64))
            (ka, va, kb, vb, kw, qa_c, ka_c, va_c, qb_c, kb_c, vb_c, qi_c, ki2_c) = _proj_even(
                h_s, g, w_all, gn, bd, cos_s, sin_s)
            r3 = lambda t: t.reshape(DB, n, t.shape[-1])
            ca_k = cache_a_k[e].reshape(DB, W_A, wsz)
            ca_v = cache_a_v[e].reshape(DB, W_A, wsz)
            full = lambda rows, wd: pl.BlockSpec((1, rows, wd), lambda bb, ii: (bb, 0, 0))
            bias_s = _band_bias(rel_bias_a[e], P + np.arange(n), P - W_A + np.arange(W_A + n))
            bias_s = jnp.pad(bias_s, ((0, 0), (0, 0), (0, -(W_A + n) % LANES)), constant_values=NEG)
            oa = _band(r3(qa_c), [ca_k, r3(ka)], [ca_v, r3(va)], [full(W_A, wsz), full(n, wsz)], bias_s, n, False)
            ik = cache_b_ik[e].astype(CDT)
            ki2_all = catp(jnp.concatenate([ik, ik], axis=-1), r3(ki2_c))
            kb_all = catp(cache_b_k[e].reshape(DB, P, wsz), r3(kb_c))
            vb_all = catp(cache_b_v[e].reshape(DB, P, wsz), r3(vb_c))
            topk_s = min(TOPK_MAX, L // 4)
            wt = _pad_rows(r3(kw)[..., 64:64 + IDX_HEADS], nqp)
            keys, thr, cut = _dsa_index(_pad_rows(r3(qi_c), nqp), ki2_all, tr(wt),
                                        tq=nqp, tk=Lp, qoff=P, length=L, topk=topk_s, nreal=n)
            ob = _dsa_attn(_pad_rows(r3(qb_c), nqp), kb_all, tr(vb_all), keys, thr, cut,
                           tq=nqp, tk=Lp, qoff=P, length=L)[:, :n]
            os_s = [oa.reshape(DB * n, wsz), ob.reshape(DB * n, wsz)]
            hd = lambda t: t.reshape(DB, n, -1, HEAD_DIM)
            outs["a_k_s"].append(jnp.concatenate([cache_a_k[e], hd(ka)], axis=1)[:, n:])
            outs["a_v_s"].append(jnp.concatenate([cache_a_v[e], hd(va)], axis=1)[:, n:])
            outs["b_k_s"].append(hd(kb)); outs["b_v_s"].append(hd(vb))
            outs["b_ik_s"].append(kw[:, :64].reshape(DB, n, 64))
            w_out = w_out_even[e].astype(CDT)
            wos = [w_out[:wsz], w_out[wsz:]]
        else:
            o = i // 2
            nh = b_f.shape[1]
            wsz = nh * HEAD_DIM
            w = w_in_odd[o]
            w_all = jnp.concatenate([w[:, :3 * wsz], jnp.pad(w[:, 3 * wsz:], ((0, 0), (0, LANES - nh)))],
                                    axis=1).astype(CDT)
            gn = jnp.stack([jnp.tile(v[o].astype(F32), wsz // HEAD_DIM) for v in (qn_c, kn_c)])
            bf = jnp.pad(b_f[o].astype(F32), (0, LANES - nh)).reshape(1, LANES)
            k, v, lf, q_c, k_c, v_c = _proj_odd(h_p, g, w_all, gn, bf, bd)
            r3 = lambda t: t.reshape(B, S, t.shape[-1])
            kx = _fox_key_lanes(r3(lf), 512 if S % 512 == 0 else S, LOG2E, nh)
            oc = _fox(r3(q_c), r3(k_c), kx, tr(r3(v_c)), _fox_qaug(kx[:, ::tq]), tq=tq, tk=tk, qoff=0, length=S)
            os_p = [oc.reshape(B * S, wsz)]
            hd = lambda t: t.reshape(B, S, nh, HEAD_DIM)
            outs["c_k_p"].append(hd(k)); outs["c_v_p"].append(hd(v)); outs["c_lf_p"].append(lf[:, :nh].reshape(B, S, nh))
            k, v, lf, q_c, k_c, v_c = _proj_odd(h_s, g, w_all, gn, bf, bd)
            r3 = lambda t: t.reshape(DB, n, t.shape[-1])
            lf_all = jnp.concatenate([jnp.pad(cache_c_logf[o].astype(F32), ((0, 0), (0, 0), (0, LANES - nh))), r3(lf)],
                                     axis=1)
            kx = _fox_key_lanes(lf_all, L, LOG2E, nh)
            k_all = catp(cache_c_k[o].reshape(DB, P, wsz), r3(k_c))
            v_all = catp(cache_c_v[o].reshape(DB, P, wsz), r3(v_c))
            oc = _fox(_pad_rows(r3(q_c), nqp), k_all, _pad_rows(kx, Lp), tr(v_all),
                      _fox_qaug(kx[:, P:P + 1]), tq=nqp, tk=Lp, qoff=P, length=L)[:, :n]
            os_s = [oc.reshape(DB * n, wsz)]
            hd = lambda t: t.reshape(DB, n, nh, HEAD_DIM)
            outs["c_k_s"].append(hd(k)); outs["c_v_s"].append(hd(v)); outs["c_lf_s"].append(lf[:, :nh].reshape(DB, n, nh))
            w_out = w_out_odd[o].astype(CDT)
            wos = [w_out]
        post_w = (row(g_mlp[i]), w_up[i].astype(CDT), w_down[i].astype(CDT), row(g_ple[i]),
                  w_ple_gate[i].astype(CDT), w_ple_proj[i].astype(CDT))
        h_p = _post(h_p, os_p, p_prompt[i].reshape(B * S, -1), wos, *post_w)
        h_s = _post(h_s, os_s, p_sample[i].reshape(DB * n, -1), wos, *post_w)

    st = jnp.stack
    names = ("a_k_p", "a_v_p", "b_k_p", "b_v_p", "b_ik_p", "c_k_p", "c_v_p", "c_lf_p",
             "a_k_s", "a_v_s", "b_k_s", "b_v_s", "b_ik_s", "c_k_s", "c_v_s", "c_lf_s")
    return (h_p.reshape(B, S, D), h_s.reshape(DB, n, D)) + tuple(st(outs[nm]) for nm in names)
```

```python
import functools

import numpy as np
import jax
import jax.numpy as jnp
from jax import lax
from jax.experimental import pallas as pl
from jax.experimental.pallas import tpu as pltpu

F32 = jnp.float32
I32 = jnp.int32
CDT = jnp.bfloat16

CHUNK = 64
HEAD_DIM = 64
IDX_HEADS = 8
A_PAST = 8 * CHUNK
REL_CLIP = 128
TOPK_MAX = 256
ROPE_THETA = 10000.0
RMS_EPS = 1e-6
LANES = 128
NEG = -1e30
INT_MIN = -(2 ** 31)
INT_MAX = 2 ** 31 - 1
LOG2E = 1.4426950408889634
VMEM_LIMIT = 56 * 1024 * 1024


def _mm(a, b):
    return jnp.dot(a, b, preferred_element_type=F32)


def _mm_nt(a, b):
    return lax.dot_general(a, b, (((1,), (1,)), ((), ())), preferred_element_type=F32)


def _pieces(x, n):
    if CDT == F32:
        return [x]
    out = []
    for _ in range(n - 1):
        p = x.astype(CDT)
        out.append(p)
        x = x - p.astype(F32)
    out.append(x.astype(CDT))
    return out


def _rms(x, g):
    return x * lax.rsqrt(jnp.mean(x * x, axis=-1, keepdims=True) + RMS_EPS) * g


def _head_rms(x, bd, g):
    x2 = x * x
    pcs = _pieces(x2, 2)
    cols = []
    for s in range(x.shape[1] // 256):
        sl = slice(s * 256, (s + 1) * 256)
        ms = _mm(pcs[0][:, sl], bd)
        for p in pcs[1:]:
            ms = ms + _mm(p[:, sl], bd)
        cols.append(ms)
    ms = cols[0] if len(cols) == 1 else jnp.concatenate(cols, axis=1)
    return x * lax.rsqrt(ms + RMS_EPS) * g


def _rope(x, cos, sin):
    w = x.shape[1]
    lane = lax.broadcasted_iota(I32, x.shape, 1)
    first = (lane & 63) < 32
    swapped = jnp.where(first, pltpu.roll(x, w - 32, 1), pltpu.roll(x, 32, 1))
    return x * cos + swapped * sin


def _tile_lanes(t, w):
    return t if w == t.shape[1] else jnp.concatenate([t] * (w // t.shape[1]), axis=1)


def _proj_even_kernel(h_ref, g_ref, w_ref, gn_ref, bd_ref, cos_ref, sin_ref,
                      ka_ref, va_ref, kb_ref, vb_ref, kw_ref,
                      qa_c, ka_c, va_c, qb_c, kb_c, vb_c, qi_c, ki2_c):
    a = _rms(h_ref[...], g_ref[...]).astype(CDT)
    z = _mm(a, w_ref[...])
    w = 512
    bd = bd_ref[...]
    gn = gn_ref[...]
    cos1, sin1 = cos_ref[...], sin_ref[...]
    cos, sin = _tile_lanes(cos1, w), _tile_lanes(sin1, w)
    scale = HEAD_DIM ** -0.5

    qa = _head_rms(z[:, 0:w], bd, gn[0:1])
    qa_c[...] = (qa * scale).astype(CDT)
    ka = _head_rms(z[:, w:2 * w], bd, gn[1:2])
    ka_ref[...] = ka
    ka_c[...] = ka.astype(CDT)
    va = z[:, 2 * w:3 * w]
    va_ref[...] = va
    va_c[...] = va.astype(CDT)
    qb = _rope(_head_rms(z[:, 3 * w:4 * w], bd, gn[2:3]), cos, sin)
    qb_c[...] = (qb * (scale * LOG2E)).astype(CDT)
    kb = _rope(_head_rms(z[:, 4 * w:5 * w], bd, gn[3:4]), cos, sin)
    kb_ref[...] = kb
    kb_c[...] = kb.astype(CDT)
    vb = z[:, 5 * w:6 * w]
    vb_ref[...] = vb
    vb_c[...] = vb.astype(CDT)
    qi = _rope(z[:, 6 * w:7 * w], cos, sin)
    qi_c[...] = (qi * scale).astype(CDT)
    kw = z[:, 7 * w:7 * w + LANES]
    kwr = _rope(kw, cos1, sin1)
    lane = lax.broadcasted_iota(I32, kw.shape, 1)
    kw_ref[...] = jnp.where(lane < 64, kwr, kw)
    ki2_c[...] = jnp.where(lane < 64, kwr, pltpu.roll(kwr, 64, 1)).astype(CDT)


def _proj_odd_kernel(h_ref, g_ref, w_ref, gn_ref, bf_ref, bd_ref,
                     k_ref, v_ref, lf_ref, q_c, k_c, v_c):
    a = _rms(h_ref[...], g_ref[...]).astype(CDT)
    z = _mm(a, w_ref[...])
    w = 1024
    bd = bd_ref[...]
    gn = gn_ref[...]
    q = _head_rms(z[:, 0:w], bd, gn[0:1])
    q_c[...] = (q * (HEAD_DIM ** -0.5 * LOG2E)).astype(CDT)
    k = _head_rms(z[:, w:2 * w], bd, gn[1:2])
    k_ref[...] = k
    k_c[...] = k.astype(CDT)
    v = z[:, 2 * w:3 * w]
    v_ref[...] = v
    v_c[...] = v.astype(CDT)
    fl = z[:, 3 * w:3 * w + LANES] + bf_ref[...]
    lf_ref[...] = jnp.minimum(fl, 0.0) - jnp.log1p(jnp.exp(-jnp.abs(fl)))


def _const_spec(shape):
    nd = len(shape)
    return pl.BlockSpec(shape, lambda *_: (0,) * nd, pipeline_mode=pl.Buffered(1))


def _row_spec(tm, w):
    return pl.BlockSpec((tm, w), lambda i: (i, 0))


def _params(sem):
    return pltpu.CompilerParams(dimension_semantics=sem, vmem_limit_bytes=VMEM_LIMIT)


def _row_tile(rows, pref):
    return pref if rows % pref == 0 else rows


def _proj_even(h, g, w_all, gn, bd, cos, sin):
    rows, d = h.shape
    tm = _row_tile(rows, 256)
    f = lambda w, dt: jax.ShapeDtypeStruct((rows, w), dt)
    out_shape = [f(512, F32)] * 4 + [f(LANES, F32)] + [f(512, CDT)] * 7 + [f(LANES, CDT)]
    out_specs = [_row_spec(tm, 512)] * 4 + [_row_spec(tm, LANES)] + [_row_spec(tm, 512)] * 7 + [_row_spec(tm, LANES)]
    return pl.pallas_call(
        _proj_even_kernel,
        grid=(rows // tm,),
        in_specs=[_row_spec(tm, d), _const_spec(g.shape), _const_spec(w_all.shape), _const_spec(gn.shape),
                  _const_spec(bd.shape), _row_spec(tm, LANES), _row_spec(tm, LANES)],
        out_specs=out_specs,
        out_shape=out_shape,
        compiler_params=_params(("parallel",)),
        name="proj_even",
    )(h, g, w_all, gn, bd, cos, sin)


def _proj_odd(h, g, w_all, gn, bf, bd):
    rows, d = h.shape
    tm = _row_tile(rows, 256)
    f = lambda w, dt: jax.ShapeDtypeStruct((rows, w), dt)
    out_shape = [f(1024, F32)] * 2 + [f(LANES, F32)] + [f(1024, CDT)] * 3
    out_specs = [_row_spec(tm, 1024)] * 2 + [_row_spec(tm, LANES)] + [_row_spec(tm, 1024)] * 3
    return pl.pallas_call(
        _proj_odd_kernel,
        grid=(rows // tm,),
        in_specs=[_row_spec(tm, d), _const_spec(g.shape), _const_spec(w_all.shape), _const_spec(gn.shape),
                  _const_spec(bf.shape), _const_spec(bd.shape)],
        out_specs=out_specs,
        out_shape=out_shape,
        compiler_params=_params(("parallel",)),
        name="proj_odd",
    )(h, g, w_all, gn, bf, bd)


def _post_kernel(*refs, n_o, ff_chunk):
    h_ref = refs[0]
    o_refs = refs[1:1 + n_o]
    p_ref = refs[1 + n_o]
    wo_refs = refs[2 + n_o:2 + 2 * n_o]
    gm_ref, wu_ref, wd_ref, gp_ref, wg_ref, wp_ref, out_ref = refs[2 + 2 * n_o:]
    mix = None
    for o_ref, wo_ref in zip(o_refs, wo_refs):
        t = _mm(o_ref[...], wo_ref[...])
        mix = t if mix is None else mix + t
    h = h_ref[...] + mix
    m = _rms(h, gm_ref[...]).astype(CDT)
    mlp = None
    d_ff = wu_ref.shape[1]
    for c in range(d_ff // ff_chunk):
        sl = slice(c * ff_chunk, (c + 1) * ff_chunk)
        u = jnp.square(jnp.maximum(_mm(m, wu_ref[:, sl]), 0.0)).astype(CDT)
        t = _mm(u, wd_ref[sl, :])
        mlp = t if mlp is None else mlp + t
    acc = h + mlp
    gate_in = _rms(acc, gp_ref[...]).astype(CDT)
    gate = 1.0 / (1.0 + jnp.exp(-_mm(gate_in, wg_ref[...])))
    out_ref[...] = acc + gate * _mm(p_ref[...].astype(CDT), wp_ref[...])


def _post(h, os_, p, wos, gm, wu, wd, gp, wg, wp):
    rows, d = h.shape
    tm = _row_tile(rows, 512)
    n_o = len(os_)
    in_specs = ([_row_spec(tm, d)] + [_row_spec(tm, o.shape[1]) for o in os_] + [_row_spec(tm, p.shape[1])]
                + [_const_spec(w.shape) for w in wos]
                + [_const_spec(x.shape) for x in (gm, wu, wd, gp, wg, wp)])
    return pl.pallas_call(
        functools.partial(_post_kernel, n_o=n_o, ff_chunk=1024),
        grid=(rows // tm,),
        in_specs=in_specs,
        out_specs=_row_spec(tm, d),
        out_shape=jax.ShapeDtypeStruct((rows, d), F32),
        compiler_params=_params(("parallel",)),
        name="post",
    )(h, *os_, p, *wos, gm, wu, wd, gp, wg, wp)


def _pair_stack(qp, tq):
    lane = lax.broadcasted_iota(I32, qp.shape, 1)
    zero = jnp.zeros_like(qp)
    return jnp.concatenate([jnp.where(lane < 64, qp, zero), jnp.where(lane >= 64, qp, zero)], axis=0)


def _pair_merge(o2, tq):
    lane = lax.broadcasted_iota(I32, (tq, LANES), 1)
    return jnp.where(lane < 64, o2[:tq], o2[tq:])


def _last_tile(i, tq, tk, qoff, length, chunked):
    qend = qoff + i * tq + tq - 1
    kmax = (qend // CHUNK + 1) * CHUNK if chunked else qend + 1
    kmax = jnp.minimum(kmax, length)
    return (kmax - 1) // tk


def _tri_steps(nq, tq, tk, qoff, length, chunked):
    ii, jj = [], []
    for i in range(nq):
        qend = qoff + i * tq + tq - 1
        kmax = min((qend // CHUNK + 1) * CHUNK if chunked else qend + 1, length)
        for j in range((kmax - 1) // tk + 1):
            ii.append(i)
            jj.append(j)
    return jnp.asarray(ii, I32), jnp.asarray(jj, I32)


def _tri_call(kernel_fn, steps, batch, in_specs, out_specs, out_shape, scratch, name):
    ii, jj = steps
    grid_spec = pltpu.PrefetchScalarGridSpec(
        num_scalar_prefetch=2, grid=(batch, ii.shape[0]),
        in_specs=in_specs, out_specs=out_specs, scratch_shapes=scratch)
    call = pl.pallas_call(kernel_fn, grid_spec=grid_spec, out_shape=out_shape,
                          compiler_params=_params(("parallel", "arbitrary")), name=name)
    return functools.partial(call, ii, jj)


def _flash_init(q_ref, aug_ref, qst_ref, m_ref, l_ref, acc_ref, tq):
    q = q_ref[0]
    for g in range(qst_ref.shape[0]):
        st = _pair_stack(q[:, g * LANES:(g + 1) * LANES], tq)
        if aug_ref is not None:
            a = aug_ref[0, 0]
            ext = jnp.concatenate([jnp.broadcast_to(a[2 * g:2 * g + 1], (tq, LANES)),
                                   jnp.broadcast_to(a[2 * g + 1:2 * g + 2], (tq, LANES))], axis=0)
            st = jnp.concatenate([st, ext.astype(CDT)], axis=1)
        qst_ref[g] = st
    m_ref[...] = jnp.full(m_ref.shape, NEG, F32)
    l_ref[...] = jnp.zeros(l_ref.shape, F32)
    acc_ref[...] = jnp.zeros(acc_ref.shape, F32)


def _flash_pairs(score_fn, vt_ref, scr, tq):
    _, m_ref, l_ref, acc_ref, s_ref = scr
    npair = m_ref.shape[0]
    s_ref[0] = score_fn(0)
    for g in range(npair):
        if g + 1 < npair:
            s_ref[(g + 1) % 2] = score_fn(g + 1)
        _flash_step(s_ref[g % 2], vt_ref, m_ref, l_ref, acc_ref, g, tq)


def _flash_step(st, vt_ref, m_ref, l_ref, acc_ref, g, tq):
    m_old = m_ref[g]
    m_new = jnp.maximum(m_old, jnp.max(st, axis=0, keepdims=True))
    alpha = jnp.exp2(m_old - m_new)
    m_ref[g] = m_new
    pc = jnp.exp2(st - m_new).astype(CDT)
    ones = jnp.ones((16, st.shape[0]), CDT)
    sums = []
    for e in range(2):
        rows = slice(g * LANES + e * HEAD_DIM, g * LANES + (e + 1) * HEAD_DIM)
        cols = slice(e * tq, (e + 1) * tq)
        pv = _mm(jnp.concatenate([vt_ref[0, rows, :], ones], axis=0), pc[:, cols])
        acc_ref[rows, :] = alpha[:, cols] * acc_ref[rows, :] + pv[:HEAD_DIM]
        sums.append(pv[HEAD_DIM:HEAD_DIM + 1])
    l_ref[g] = alpha * l_ref[g] + jnp.concatenate(sums, axis=1)


def _flash_finish(o_ref, l_ref, acc_ref, tq):
    for g in range(l_ref.shape[0]):
        l = l_ref[g]
        den = jnp.concatenate([jnp.broadcast_to(l[:, :tq], (HEAD_DIM, tq)),
                               jnp.broadcast_to(l[:, tq:], (HEAD_DIM, tq))], axis=0)
        o = acc_ref[g * LANES:(g + 1) * LANES, :] / den
        o_ref[0, :, g * LANES:(g + 1) * LANES] = o.T.astype(o_ref.dtype)


def _flash_scratch(npair, tq, tk, kd, w):
    return [pltpu.VMEM((npair, 2 * tq, kd), CDT),
            pltpu.VMEM((npair, 1, 2 * tq), F32),
            pltpu.VMEM((npair, 1, 2 * tq), F32),
            pltpu.VMEM((w, tq), F32),
            pltpu.VMEM((2, tk, 2 * tq), F32)]


def _band_kernel(*refs, n_kv, tq, mask_neg):
    q_ref = refs[0]
    k_refs = refs[1:1 + n_kv]
    v_refs = refs[1 + n_kv:1 + 2 * n_kv]
    bias_ref = refs[1 + 2 * n_kv]
    o_ref = refs[2 + 2 * n_kv]
    i = pl.program_id(1)
    q = q_ref[0]
    cat = lambda rs: (rs[0][0].astype(CDT) if len(rs) == 1
                      else jnp.concatenate([r[0].astype(CDT) for r in rs], axis=0))
    k = cat(k_refs)
    v = cat(v_refs)
    tk = bias_ref.shape[2]
    if k.shape[0] < tk:
        zpad = jnp.zeros((tk - k.shape[0], k.shape[1]), CDT)
        k = jnp.concatenate([k, zpad], axis=0)
        v = jnp.concatenate([v, zpad], axis=0)
    if mask_neg:
        kpos = lax.broadcasted_iota(I32, (2 * tq, tk), 1) + (i * tq - A_PAST)
        neg = kpos < 0
    for g in range(q.shape[1] // LANES):
        sl = slice(g * LANES, (g + 1) * LANES)
        s = _mm_nt(_pair_stack(q[:, sl], tq), k[:, sl])
        s = s + jnp.concatenate([bias_ref[2 * g], bias_ref[2 * g + 1]], axis=0)
        if mask_neg:
            s = jnp.where(neg, NEG, s)
        m = jnp.max(s, axis=-1, keepdims=True)
        p = jnp.exp(s - m)
        l = jnp.sum(p, axis=-1, keepdims=True)
        o2 = _mm(p.astype(CDT), v[:, sl]) / l
        o_ref[0, :, sl] = _pair_merge(o2, tq).astype(o_ref.dtype)


def _band(q, k_parts, v_parts, k_specs, bias, tq, mask_neg):
    b, s, w = q.shape
    n_kv = len(k_parts)
    q_spec = pl.BlockSpec((1, tq, w), lambda bb, i: (bb, i, 0))
    return pl.pallas_call(
        functools.partial(_band_kernel, n_kv=n_kv, tq=tq, mask_neg=mask_neg),
        grid=(b, s // tq),
        in_specs=[q_spec] + k_specs + k_specs + [_const_spec(bias.shape)],
        out_specs=q_spec,
        out_shape=jax.ShapeDtypeStruct((b, s, w), CDT),
        compiler_params=_params(("parallel", "parallel")),
        name="band_attn",
    )(q, *k_parts, *v_parts, bias)


def _band_bias(rel_bias, q_pos, k_pos):
    nq, nk = len(q_pos), len(k_pos)
    qc = q_pos[:, None] // CHUNK
    kc = k_pos[None, :] // CHUNK
    ok = (kc <= qc) & (kc >= qc - A_PAST // CHUNK)
    n = nq + nk - 1
    d0 = int(q_pos[0] - k_pos[0])
    m = np.arange(n + 1)
    rel = np.where(m < nk, d0 - m, d0 - m + n + 1)
    table = jnp.take(rel_bias.astype(F32), jnp.asarray(np.clip(rel, -REL_CLIP, REL_CLIP) + REL_CLIP), axis=1)
    bias = jnp.tile(table, (1, nq))[:, :nq * n].reshape(-1, nq, n)[:, :, :nk]
    return jnp.where(jnp.asarray(ok)[None], bias, NEG)


def _dsa_index_kernel(ii_ref, jj_ref, qi_ref, ki2_ref, wt_ref, keys_ref, t_ref, c_ref, qst_ref, kscr_ref, k16_ref,
                      *, tq, tk, nk, unroll, qoff, length, topk, idx_bits, nreal):
    i = ii_ref[pl.program_id(1)]
    j = jj_ref[pl.program_id(1)]
    last = _last_tile(i, tq, tk, qoff, length, True)
    q0 = qoff + i * tq

    @pl.when(j == 0)
    def _():
        q = qi_ref[0]
        for g in range(IDX_HEADS // 2):
            qst_ref[2 * g * tq:(2 * g + 2) * tq, :] = _pair_stack(q[:, g * LANES:(g + 1) * LANES], tq)

    @pl.when(j <= last)
    def _():
        lg = _mm_nt(ki2_ref[0], qst_ref[...])
        wgt = wt_ref[0] * (IDX_HEADS ** -0.5)
        sc = jnp.zeros((tk, tq), F32)
        for h in range(IDX_HEADS):
            sc = sc + jnp.maximum(lg[:, h * tq:(h + 1) * tq], 0.0) * wgt[h:h + 1, :]
        sc = jnp.where(sc == 0.0, 0.0, sc)
        bits = lax.bitcast_convert_type(sc, I32)
        key = bits ^ ((bits >> 31) & 0x7FFFFFFF)
        kpos = j * tk + lax.broadcasted_iota(I32, (tk, tq), 0)
        qpos = q0 + lax.broadcasted_iota(I32, (tk, tq), 1)
        ok = ((kpos >> 6) <= (qpos >> 6)) & (kpos < length)
        key = jnp.where(ok, key, INT_MIN)
        keys_ref[0] = key
        kscr_ref[j] = key
        top = lax.bitcast_convert_type(bits & -65536, F32)
        k16_ref[j] = jnp.where(ok, top, -jnp.inf).astype(jnp.bfloat16)
        if unroll > 1:
            @pl.when(j + 1 < nk)
            def _():
                kscr_ref[j + 1] = jnp.full((tk, tq), INT_MIN, I32)
                k16_ref[j + 1] = jnp.full((tk, tq), -jnp.inf, jnp.bfloat16)

    @pl.when(j == last)
    def _():
        def count(pred):
            def body(t, acc):
                for u in range(unroll):
                    tt = unroll * t + u
                    kpos = tt * tk + lax.broadcasted_iota(I32, (tk, tq), 0)
                    m = jnp.where(pred(kscr_ref[tt], kpos), 1.0, 0.0)
                    acc = acc + jnp.sum(m, axis=0, keepdims=True)
                return acc
            return lax.fori_loop(0, (last + unroll) // unroll, body, jnp.zeros((1, tq), F32))

        kf = float(topk)
        qpos = q0 + lax.broadcasted_iota(I32, (1, tq), 1)
        n_adm = jnp.minimum(((qpos >> 6) + 1) << 6, length).astype(F32)
        real = qpos - qoff < nreal
        take_all = (n_adm < kf) | jnp.logical_not(real)

        def count16(c16):
            one, zero = jnp.ones((), jnp.bfloat16), jnp.zeros((), jnp.bfloat16)

            def body(t, acc):
                for u in range(unroll):
                    m = jnp.where(k16_ref[unroll * t + u] >= c16, one, zero).reshape(tk // 16, 16, tq)
                    part = m[0]
                    for r in range(1, tk // 16):
                        part = part + m[r]
                    acc = acc + jnp.sum(part.astype(F32), axis=0, keepdims=True)
                return acc
            return lax.fori_loop(0, (last + unroll) // unroll, body, jnp.zeros((1, tq), F32))

        def bit_body(top_half):
            def body(carry):
                it, prefix, nge, _ = carry
                cand = prefix + lax.shift_left(jnp.int32(1), 31 - it)
                if top_half:
                    cbits = cand ^ ((cand >> 31) & 0x7FFFFFFF)
                    cbits = jnp.where((cand > 0) & (cand < 0x00800000), 0x00800000, cbits)
                    cnt = count16(lax.bitcast_convert_type(cbits & -65536, F32).astype(jnp.bfloat16))
                else:
                    cnt = count(lambda kt, _: kt >= cand)
                take = cnt >= kf
                prefix = jnp.where(take, cand, prefix)
                nge = jnp.where(take, cnt, nge)
                settled = jnp.min(jnp.where(take_all | (nge == kf), 1.0, 0.0))
                return it + 1, prefix, nge, (settled > 0.0).astype(I32)
            return body

        carry = (jnp.int32(0), jnp.full((1, tq), INT_MIN, I32), jnp.zeros((1, tq), F32), jnp.int32(0))
        carry = lax.while_loop(lambda c: (c[0] < 16) & (c[3] == 0), bit_body(True), carry)
        _, prefix, nge, _ = lax.while_loop(lambda c: (c[0] < 32) & (c[3] == 0), bit_body(False), carry)
        found = prefix > INT_MIN
        thr = jnp.maximum(prefix, INT_MIN + 1)
        t_ref[0] = thr
        c_ref[0] = jnp.full((1, tq), 2 ** 30, I32)
        straddle = jnp.max(jnp.where(found & (nge > kf) & real, 1.0, 0.0))

        @pl.when(straddle > 0.0)
        def _():
            need = kf - count(lambda kt, _: kt > thr)

            def tie_body(it, x):
                cx = x + lax.shift_left(jnp.int32(1), idx_bits - 1 - it)
                g = count(lambda kt, kpos: jnp.where(kt == thr, kpos, INT_MAX) < cx)
                return jnp.where(g < need, cx, x)
            c_ref[0] = lax.fori_loop(0, idx_bits, tie_body, jnp.zeros((1, tq), I32))


def _dsa_index(qi, ki2, wt, *, tq, tk, qoff, length, topk, nreal):
    b, s, _ = qi.shape
    lp = ki2.shape[1]
    nq, nk = s // tq, lp // tk
    idx_bits = max(1, int(np.ceil(np.log2(lp))))
    qvec = pl.BlockSpec((1, 1, tq), lambda bb, t, ii, jj: (bb, 0, ii[t]))
    return _tri_call(
        functools.partial(_dsa_index_kernel, tq=tq, tk=tk, nk=nk, unroll=2 if nk % 2 == 0 else 1,
                          qoff=qoff, length=length, topk=topk, idx_bits=idx_bits, nreal=nreal),
        _tri_steps(nq, tq, tk, qoff, length, True), b,
        in_specs=[pl.BlockSpec((1, tq, 512), lambda bb, t, ii, jj: (bb, ii[t], 0)),
                  pl.BlockSpec((1, tk, LANES), lambda bb, t, ii, jj: (bb, jj[t], 0)),
                  pl.BlockSpec((1, IDX_HEADS, tq), lambda bb, t, ii, jj: (bb, 0, ii[t]))],
        out_specs=[pl.BlockSpec((1, tk, tq), lambda bb, t, ii, jj: (bb, jj[t], ii[t])), qvec, qvec],
        out_shape=[jax.ShapeDtypeStruct((b, lp, s), I32),
                   jax.ShapeDtypeStruct((b, 1, s), I32),
                   jax.ShapeDtypeStruct((b, 1, s), I32)],
        scratch=[pltpu.VMEM((IDX_HEADS * tq, LANES), CDT),
                 pltpu.VMEM((nk, tk, tq), I32),
                 pltpu.VMEM((nk, tk, tq), jnp.bfloat16)],
        name="dsa_index",
    )(qi, ki2, wt)


def _dsa_attn_kernel(ii_ref, jj_ref, q_ref, k_ref, vt_ref, keys_ref, t_ref, c_ref, o_ref, *scr,
                     tq, tk, qoff, length):
    qst_ref, m_ref, l_ref, acc_ref = scr[:4]
    i = ii_ref[pl.program_id(1)]
    j = jj_ref[pl.program_id(1)]
    last = _last_tile(i, tq, tk, qoff, length, True)

    @pl.when(j == 0)
    def _():
        _flash_init(q_ref, None, qst_ref, m_ref, l_ref, acc_ref, tq)

    @pl.when(j <= last)
    def _():
        kt = keys_ref[0]
        thr = t_ref[0]
        kpos = j * tk + lax.broadcasted_iota(I32, (tk, tq), 0)
        sel = (kt > thr) | (jnp.where(kt == thr, kpos, INT_MAX) <= c_ref[0])
        mb = jnp.where(sel, 0.0, NEG)
        mb2 = jnp.concatenate([mb, mb], axis=1)
        score = lambda g: _mm_nt(k_ref[0, :, g * LANES:(g + 1) * LANES], qst_ref[g]) + mb2
        _flash_pairs(score, vt_ref, scr, tq)

    @pl.when(j == last)
    def _():
        _flash_finish(o_ref, l_ref, acc_ref, tq)


def _dsa_attn(q, k, vt, keys, thr, cut, *, tq, tk, qoff, length):
    b, s, w = q.shape
    qmap = lambda bb, t, ii, jj: (bb, ii[t], 0)
    qvec = pl.BlockSpec((1, 1, tq), lambda bb, t, ii, jj: (bb, 0, ii[t]))
    npair = w // LANES
    return _tri_call(
        functools.partial(_dsa_attn_kernel, tq=tq, tk=tk, qoff=qoff, length=length),
        _tri_steps(s // tq, tq, tk, qoff, length, True), b,
        in_specs=[pl.BlockSpec((1, tq, w), qmap),
                  pl.BlockSpec((1, tk, w), lambda bb, t, ii, jj: (bb, jj[t], 0)),
                  pl.BlockSpec((1, w, tk), lambda bb, t, ii, jj: (bb, 0, jj[t])),
                  pl.BlockSpec((1, tk, tq), lambda bb, t, ii, jj: (bb, jj[t], ii[t])),
                  qvec, qvec],
        out_specs=pl.BlockSpec((1, tq, w), qmap),
        out_shape=jax.ShapeDtypeStruct((b, s, w), CDT),
        scratch=_flash_scratch(npair, tq, tk, LANES, w),
        name="dsa_attn",
    )(q, k, vt, keys, thr, cut)


def _pieces3(x):
    pcs = _pieces(x, 3)
    return pcs + [jnp.zeros_like(pcs[0])] * (3 - len(pcs))


FOX_EXT = 9


def _cumsum_kernel(x_ref, tri_ref, place_ref, ones_ref, kx_ref, carry_ref, *, scale):
    @pl.when(pl.program_id(1) == 0)
    def _():
        carry_ref[...] = jnp.zeros(carry_ref.shape, F32)
    tri = tri_ref[...]
    tot = None
    for p in _pieces(x_ref[0], 3):
        t = _mm(tri, p)
        tot = t if tot is None else tot + t
    acc = carry_ref[...] + tot
    carry_ref[...] = acc[-1:, :]
    ext = None
    for c, p in enumerate(_pieces3(acc * (-scale))):
        t = _mm(p, place_ref[c])
        ext = t if ext is None else ext + t
    kx_ref[0] = (ext + ones_ref[...]).astype(kx_ref.dtype)


def _fox_key_lanes(x, tb, scale, nh):
    b, s, w = x.shape
    npair = nh // 2
    place = np.zeros((3, w, npair * LANES), np.float32)
    ones = np.zeros((1, npair * LANES), np.float32)
    for h in range(nh):
        for c in range(3):
            place[c, h, (h // 2) * LANES + (h % 2) * 6 + c] = 1.0
    for g in range(npair):
        ones[0, g * LANES + 3:g * LANES + 6] = 1.0
    tri = jnp.tril(jnp.ones((tb, tb), F32)).astype(CDT)
    place = jnp.asarray(place, CDT)
    spec = lambda wd: pl.BlockSpec((1, tb, wd), lambda bb, i: (bb, i, 0))
    return pl.pallas_call(
        functools.partial(_cumsum_kernel, scale=scale),
        grid=(b, s // tb),
        in_specs=[spec(w), _const_spec(tri.shape), _const_spec(place.shape), _const_spec(ones.shape)],
        out_specs=spec(npair * LANES),
        out_shape=jax.ShapeDtypeStruct((b, s, npair * LANES), CDT),
        scratch_shapes=[pltpu.VMEM((1, w), F32)],
        compiler_params=_params(("parallel", "arbitrary")),
        name="logf_cumsum",
    )(x, tri, place, jnp.asarray(ones))


def _fox_kernel(ii_ref, jj_ref, q_ref, k_ref, kx_ref, vt_ref, aug_ref, o_ref, *scr, tq, tk, qoff, length):
    qst_ref, m_ref, l_ref, acc_ref = scr[:4]
    i = ii_ref[pl.program_id(1)]
    j = jj_ref[pl.program_id(1)]
    last = _last_tile(i, tq, tk, qoff, length, False)
    q0 = qoff + i * tq

    @pl.when(j == 0)
    def _():
        _flash_init(q_ref, aug_ref, qst_ref, m_ref, l_ref, acc_ref, tq)

    def step(masked):
        if masked:
            kpos = j * tk + lax.broadcasted_iota(I32, (tk, 2 * tq), 0)
            lane = lax.broadcasted_iota(I32, (tk, 2 * tq), 1)
            causal = kpos <= q0 + jnp.where(lane >= tq, lane - tq, lane)

        def score(g):
            sl = slice(g * LANES, (g + 1) * LANES)
            st = _mm_nt(jnp.concatenate([k_ref[0, :, sl], kx_ref[0, :, sl]], axis=1), qst_ref[g])
            return jnp.where(causal, st, NEG) if masked else st
        _flash_pairs(score, vt_ref, scr, tq)

    diag = (j + 1) * tk - 1 > q0

    @pl.when((j <= last) & diag)
    def _():
        step(True)

    @pl.when((j <= last) & jnp.logical_not(diag))
    def _():
        step(False)

    @pl.when(j == last)
    def _():
        _flash_finish(o_ref, l_ref, acc_ref, tq)


def _fox(q, k, kx, vt, qaug, *, tq, tk, qoff, length):
    b, s, w = q.shape
    npair = w // LANES
    qmap = lambda bb, t, ii, jj: (bb, ii[t], 0)
    kmap = lambda bb, t, ii, jj: (bb, jj[t], 0)
    return _tri_call(
        functools.partial(_fox_kernel, tq=tq, tk=tk, qoff=qoff, length=length),
        _tri_steps(s // tq, tq, tk, qoff, length, False), b,
        in_specs=[pl.BlockSpec((1, tq, w), qmap),
                  pl.BlockSpec((1, tk, w), kmap),
                  pl.BlockSpec((1, tk, npair * LANES), kmap),
                  pl.BlockSpec((1, w, tk), lambda bb, t, ii, jj: (bb, 0, jj[t])),
                  pl.BlockSpec((1, 1, 2 * npair, LANES), lambda bb, t, ii, jj: (bb, ii[t], 0, 0))],
        out_specs=pl.BlockSpec((1, tq, w), qmap),
        out_shape=jax.ShapeDtypeStruct((b, s, w), CDT),
        scratch=_flash_scratch(npair, tq, tk, 2 * LANES, w),
        name="fox_attn",
    )(q, k, kx, vt, qaug)


def _fox_qaug(kx0):
    b, nq, _ = kx0.shape
    a = kx0.reshape(b, nq, -1, LANES).astype(F32)
    one, zero = jnp.ones_like(a[..., 0:3]), jnp.zeros_like(a[..., 0:3])
    even = jnp.concatenate([one, -a[..., 0:3], zero], axis=-1)
    odd = jnp.concatenate([zero, -a[..., 6:9], one], axis=-1)
    heads = jnp.stack([even, odd], axis=3).reshape(b, nq, -1, FOX_EXT)
    return jnp.pad(heads, ((0, 0), (0, 0), (0, 0), (0, LANES - FOX_EXT)))


def _rope_tables(pos):
    half = HEAD_DIM // 2
    inv = ROPE_THETA ** (-jnp.arange(half, dtype=F32) / half)
    ang = pos.astype(F32)[:, None] * inv[None, :]
    cos, sin = jnp.cos(ang), jnp.sin(ang)
    return (jnp.tile(jnp.concatenate([cos, cos], axis=1), (1, 2)),
            jnp.tile(jnp.concatenate([-sin, sin], axis=1), (1, 2)))


def _pad_rows(x, n):
    return x if x.shape[1] == n else jnp.pad(x, ((0, 0), (0, n - x.shape[1]), (0, 0)))


def _tiles(s):
    tk = 512 if s % 512 == 0 else s
    tq = 512 if s % 512 == 0 else s
    tqi = 256 if s % 256 == 0 else s
    return tq, tqi, tk


def kernel(x_prompt, x_sample, cache_a_k, cache_a_v, cache_b_k, cache_b_v, cache_b_ik, cache_c_k, cache_c_v,
           cache_c_logf, p_prompt, p_sample, g_mix, w_in_even, qn_a, kn_a, rel_bias_a, qn_b, kn_b, w_out_even,
           w_in_odd, b_f, qn_c, kn_c, w_out_odd, g_mlp, w_up, w_down, g_ple, w_ple_gate, w_ple_proj):
    B, S, D = x_prompt.shape
    DB, n, _ = x_sample.shape
    P = cache_b_k.shape[2]
    W_A = cache_a_k.shape[2]
    W_AP = min(A_PAST, S)
    depth = g_mix.shape[0]
    L = P + n
    Lp = -(-L // LANES) * LANES
    nqp = -(-n // LANES) * LANES
    assert B == 1 and S % 128 == 0 and n % 16 == 0 and P % 16 == 0

    row = lambda v: v.astype(F32).reshape(1, -1)
    bd = jnp.asarray(np.kron(np.eye(256 // HEAD_DIM), np.full((HEAD_DIM, HEAD_DIM), 1.0 / HEAD_DIM)), CDT)
    cos_p, sin_p = _rope_tables(jnp.arange(S))
    cos_s, sin_s = (jnp.tile(t, (DB, 1)) for t in _rope_tables(P + jnp.arange(n)))
    tq, tqi, tk = _tiles(S)
    tr = lambda t: jnp.swapaxes(t, 1, 2)
    catp = lambda c, new: _pad_rows(jnp.concatenate([c.astype(CDT), new], axis=1), Lp)

    outs = {name: [] for name in ("a_k_p", "a_v_p", "b_k_p", "b_v_p", "b_ik_p", "c_k_p", "c_v_p", "c_lf_p",
                                  "a_k_s", "a_v_s", "b_k_s", "b_v_s", "b_ik_s", "c_k_s", "c_v_s", "c_lf_s")}
    h_p = x_prompt.reshape(B * S, D)
    h_s = x_sample.reshape(DB * n, D)
    for i in range(depth):
        g = row(g_mix[i])
        if i % 2 == 0:
            e = i // 2
            wsz = 512
            w = w_in_even[e]
            w_all = jnp.concatenate([w[:, :7 * wsz], jnp.pad(w[:, 7 * wsz:], ((0, 0), (0, LANES - 72)))],
                                    axis=1).astype(CDT)
            gn = jnp.stack([jnp.tile(v[e].astype(F32), wsz // HEAD_DIM) for v in (qn_a, kn_a, qn_b, kn_b)])
            (ka, va, kb, vb, kw, qa_c, ka_c, va_c, qb_c, kb_c, vb_c, qi_c, ki2_c) = _proj_even(
                h_p, g, w_all, gn, bd, cos_p, sin_p)
            r3 = lambda t: t.reshape(B, S, t.shape[-1])
            tqa = 256 if S % 256 == 0 else 128
            npc = (A_PAST + tqa) // tqa
            k_specs = [pl.BlockSpec((1, tqa, wsz), functools.partial(
                lambda bb, ii, pp: (bb, jnp.maximum(ii + pp - A_PAST // tqa, 0), 0), pp=pp)) for pp in range(npc)]
            bias_p = _band_bias(rel_bias_a[e], np.arange(tqa), np.arange(-A_PAST, tqa))
            oa = _band(r3(qa_c), [r3(ka_c)] * npc, [r3(va_c)] * npc, k_specs, bias_p, tqa, True)
            topk_p = min(TOPK_MAX, S // 4)
            wt = tr(r3(kw)[..., 64:64 + IDX_HEADS])
            keys, thr, cut = _dsa_index(r3(qi_c), r3(ki2_c), wt, tq=tqi, tk=tk, qoff=0, length=S, topk=topk_p,
                                        nreal=S)
            ob = _dsa_attn(r3(qb_c), r3(kb_c), tr(r3(vb_c)), keys, thr, cut, tq=tq, tk=tk, qoff=0, length=S)
            os_p = [oa.reshape(B * S, wsz), ob.reshape(B * S, wsz)]
            hd = lambda t: t.reshape(B, S, -1, HEAD_DIM)
            outs["a_k_p"].append(hd(ka)[:, S - W_AP:]); outs["a_v_p"].append(hd(va)[:, S - W_AP:])
            outs["b_k_p"].append(hd(kb)); outs["b_v_p"].append(hd(vb))
            outs["b_ik_p"].append(kw[:, :64].reshape(B, S, 64))
            (ka, va, kb, vb, kw, qa_c, ka_c, va_c, qb_c, kb_c, vb_c, qi_c, ki2_c) = _proj_even(
                h_s, g, w_all, gn, bd, cos_s, sin_s)
            r3 = lambda t: t.reshape(DB, n, t.shape[-1])
            ca_k = cache_a_k[e].reshape(DB, W_A, wsz)
            ca_v = cache_a_v[e].reshape(DB, W_A, wsz)
            full = lambda rows, wd: pl.BlockSpec((1, rows, wd), lambda bb, ii: (bb, 0, 0))
            bias_s = _band_bias(rel_bias_a[e], P + np.arange(n), P - W_A + np.arange(W_A + n))
            bias_s = jnp.pad(bias_s, ((0, 0), (0, 0), (0, -(W_A + n) % LANES)), constant_values=NEG)
            oa = _band(r3(qa_c), [ca_k, r3(ka)], [ca_v, r3(va)], [full(W_A, wsz), full(n, wsz)], bias_s, n, False)
            ik = cache_b_ik[e].astype(CDT)
            ki2_all = catp(jnp.concatenate([ik, ik], axis=-1), r3(ki2_c))
            kb_all = catp(cache_b_k[e].reshape(DB, P, wsz), r3(kb_c))
            vb_all = catp(cache_b_v[e].reshape(DB, P, wsz), r3(vb_c))
            topk_s = min(TOPK_MAX, L // 4)
            wt = _pad_rows(r3(kw)[..., 64:64 + IDX_HEADS], nqp)
            keys, thr, cut = _dsa_index(_pad_rows(r3(qi_c), nqp), ki2_all, tr(wt),
                                        tq=nqp, tk=Lp, qoff=P, length=L, topk=topk_s, nreal=n)
            ob = _dsa_attn(_pad_rows(r3(qb_c), nqp), kb_all, tr(vb_all), keys, thr, cut,
                           tq=nqp, tk=Lp, qoff=P, length=L)[:, :n]
            os_s = [oa.reshape(DB * n, wsz), ob.reshape(DB * n, wsz)]
            hd = lambda t: t.reshape(DB, n, -1, HEAD_DIM)
            outs["a_k_s"].append(jnp.concatenate([cache_a_k[e], hd(ka)], axis=1)[:, n:])
            outs["a_v_s"].append(jnp.concatenate([cache_a_v[e], hd(va)], axis=1)[:, n:])
            outs["b_k_s"].append(hd(kb)); outs["b_v_s"].append(hd(vb))
            outs["b_ik_s"].append(kw[:, :64].reshape(DB, n, 64))
            w_out = w_out_even[e].astype(CDT)
            wos = [w_out[:wsz], w_out[wsz:]]
        else:
            o = i // 2
            nh = b_f.shape[1]
            wsz = nh * HEAD_DIM
            w = w_in_odd[o]
            w_all = jnp.concatenate([w[:, :3 * wsz], jnp.pad(w[:, 3 * wsz:], ((0, 0), (0, LANES - nh)))],
                                    axis=1).astype(CDT)
            gn = jnp.stack([jnp.tile(v[o].astype(F32), wsz // HEAD_DIM) for v in (qn_c, kn_c)])
            bf = jnp.pad(b_f[o].astype(F32), (0, LANES - nh)).reshape(1, LANES)
            k, v, lf, q_c, k_c, v_c = _proj_odd(h_p, g, w_all, gn, bf, bd)
            r3 = lambda t: t.reshape(B, S, t.shape[-1])
            kx = _fox_key_lanes(r3(lf), 512 if S % 512 == 0 else S, LOG2E, nh)
            oc = _fox(r3(q_c), r3(k_c), kx, tr(r3(v_c)), _fox_qaug(kx[:, ::tq]), tq=tq, tk=tk, qoff=0, length=S)
            os_p = [oc.reshape(B * S, wsz)]
            hd = lambda t: t.reshape(B, S, nh, HEAD_DIM)
            outs["c_k_p"].append(hd(k)); outs["c_v_p"].append(hd(v)); outs["c_lf_p"].append(lf[:, :nh].reshape(B, S, nh))
            k, v, lf, q_c, k_c, v_c = _proj_odd(h_s, g, w_all, gn, bf, bd)
            r3 = lambda t: t.reshape(DB, n, t.shape[-1])
            lf_all = jnp.concatenate([jnp.pad(cache_c_logf[o].astype(F32), ((0, 0), (0, 0), (0, LANES - nh))), r3(lf)],
                                     axis=1)
            kx = _fox_key_lanes(lf_all, L, LOG2E, nh)
            k_all = catp(cache_c_k[o].reshape(DB, P, wsz), r3(k_c))
            v_all = catp(cache_c_v[o].reshape(DB, P, wsz), r3(v_c))
            oc = _fox(_pad_rows(r3(q_c), nqp), k_all, _pad_rows(kx, Lp), tr(v_all),
                      _fox_qaug(kx[:, P:P + 1]), tq=nqp, tk=Lp, qoff=P, length=L)[:, :n]
            os_s = [oc.reshape(DB * n, wsz)]
            hd = lambda t: t.reshape(DB, n, nh, HEAD_DIM)
            outs["c_k_s"].append(hd(k)); outs["c_v_s"].append(hd(v)); outs["c_lf_s"].append(lf[:, :nh].reshape(DB, n, nh))
            w_out = w_out_odd[o].astype(CDT)
            wos = [w_out]
        post_w = (row(g_mlp[i]), w_up[i].astype(CDT), w_down[i].astype(CDT), row(g_ple[i]),
                  w_ple_gate[i].astype(CDT), w_ple_proj[i].astype(CDT))
        h_p = _post(h_p, os_p, p_prompt[i].reshape(B * S, -1), wos, *post_w)
        h_s = _post(h_s, os_s, p_sample[i].reshape(DB * n, -1), wos, *post_w)

    st = jnp.stack
    names = ("a_k_p", "a_v_p", "b_k_p", "b_v_p", "b_ik_p", "c_k_p", "c_v_p", "c_lf_p",
             "a_k_s", "a_v_s", "b_k_s", "b_v_s", "b_ik_s", "c_k_s", "c_v_s", "c_lf_s")
    return (h_p.reshape(B, S, D), h_s.reshape(DB, n, D)) + tuple(st(outs[nm]) for nm in names)
```

```python
import functools

import numpy as np
import jax
import jax.numpy as jnp
from jax import lax
from jax.experimental import pallas as pl
from jax.experimental.pallas import tpu as pltpu

F32 = jnp.float32
I32 = jnp.int32
CDT = jnp.bfloat16

CHUNK = 64
HEAD_DIM = 64
IDX_HEADS = 8
A_PAST = 8 * CHUNK
REL_CLIP = 128
TOPK_MAX = 256
ROPE_THETA = 10000.0
RMS_EPS = 1e-6
LANES = 128
NEG = -1e30
INT_MIN = -(2 ** 31)
INT_MAX = 2 ** 31 - 1
LOG2E = 1.4426950408889634
VMEM_LIMIT = 56 * 1024 * 1024


def _mm(a, b):
    return jnp.dot(a, b, preferred_element_type=F32)


def _mm_nt(a, b):
    return lax.dot_general(a, b, (((1,), (1,)), ((), ())), preferred_element_type=F32)


def _pieces(x, n):
    if CDT == F32:
        return [x]
    out = []
    for _ in range(n - 1):
        p = x.astype(CDT)
        out.append(p)
        x = x - p.astype(F32)
    out.append(x.astype(CDT))
    return out


def _rms(x, g):
    return x * lax.rsqrt(jnp.mean(x * x, axis=-1, keepdims=True) + RMS_EPS) * g


def _head_rms(x, bd, g):
    x2 = x * x
    pcs = _pieces(x2, 2)
    cols = []
    for s in range(x.shape[1] // 256):
        sl = slice(s * 256, (s + 1) * 256)
        ms = _mm(pcs[0][:, sl], bd)
        for p in pcs[1:]:
            ms = ms + _mm(p[:, sl], bd)
        cols.append(ms)
    ms = cols[0] if len(cols) == 1 else jnp.concatenate(cols, axis=1)
    return x * lax.rsqrt(ms + RMS_EPS) * g


def _rope(x, cos, sin):
    w = x.shape[1]
    lane = lax.broadcasted_iota(I32, x.shape, 1)
    first = (lane & 63) < 32
    swapped = jnp.where(first, pltpu.roll(x, w - 32, 1), pltpu.roll(x, 32, 1))
    return x * cos + swapped * sin


def _tile_lanes(t, w):
    return t if w == t.shape[1] else jnp.concatenate([t] * (w // t.shape[1]), axis=1)


def _proj_even_kernel(h_ref, g_ref, w_ref, gn_ref, bd_ref, cos_ref, sin_ref,
                      ka_ref, va_ref, kb_ref, vb_ref, kw_ref,
                      qa_c, ka_c, va_c, qb_c, kb_c, vb_c, qi_c, ki2_c):
    a = _rms(h_ref[...], g_ref[...]).astype(CDT)
    z = _mm(a, w_ref[...])
    w = 512
    bd = bd_ref[...]
    gn = gn_ref[...]
    cos1, sin1 = cos_ref[...], sin_ref[...]
    cos, sin = _tile_lanes(cos1, w), _tile_lanes(sin1, w)
    scale = HEAD_DIM ** -0.5

    qa = _head_rms(z[:, 0:w], bd, gn[0:1])
    qa_c[...] = (qa * scale).astype(CDT)
    ka = _head_rms(z[:, w:2 * w], bd, gn[1:2])
    ka_ref[...] = ka
    ka_c[...] = ka.astype(CDT)
    va = z[:, 2 * w:3 * w]
    va_ref[...] = va
    va_c[...] = va.astype(CDT)
    qb = _rope(_head_rms(z[:, 3 * w:4 * w], bd, gn[2:3]), cos, sin)
    qb_c[...] = (qb * (scale * LOG2E)).astype(CDT)
    kb = _rope(_head_rms(z[:, 4 * w:5 * w], bd, gn[3:4]), cos, sin)
    kb_ref[...] = kb
    kb_c[...] = kb.astype(CDT)
    vb = z[:, 5 * w:6 * w]
    vb_ref[...] = vb
    vb_c[...] = vb.astype(CDT)
    qi = _rope(z[:, 6 * w:7 * w], cos, sin)
    qi_c[...] = (qi * scale).astype(CDT)
    kw = z[:, 7 * w:7 * w + LANES]
    kwr = _rope(kw, cos1, sin1)
    lane = lax.broadcasted_iota(I32, kw.shape, 1)
    kw_ref[...] = jnp.where(lane < 64, kwr, kw)
    ki2_c[...] = jnp.where(lane < 64, kwr, pltpu.roll(kwr, 64, 1)).astype(CDT)


def _proj_odd_kernel(h_ref, g_ref, w_ref, gn_ref, bf_ref, bd_ref,
                     k_ref, v_ref, lf_ref, q_c, k_c, v_c):
    a = _rms(h_ref[...], g_ref[...]).astype(CDT)
    z = _mm(a, w_ref[...])
    w = 1024
    bd = bd_ref[...]
    gn = gn_ref[...]
    q = _head_rms(z[:, 0:w], bd, gn[0:1])
    q_c[...] = (q * (HEAD_DIM ** -0.5 * LOG2E)).astype(CDT)
    k = _head_rms(z[:, w:2 * w], bd, gn[1:2])
    k_ref[...] = k
    k_c[...] = k.astype(CDT)
    v = z[:, 2 * w:3 * w]
    v_ref[...] = v
    v_c[...] = v.astype(CDT)
    fl = z[:, 3 * w:3 * w + LANES] + bf_ref[...]
    lf_ref[...] = jnp.minimum(fl, 0.0) - jnp.log1p(jnp.exp(-jnp.abs(fl)))


def _const_spec(shape):
    nd = len(shape)
    return pl.BlockSpec(shape, lambda *_: (0,) * nd, pipeline_mode=pl.Buffered(1))


def _row_spec(tm, w):
    return pl.BlockSpec((tm, w), lambda i: (i, 0))


def _params(sem):
    return pltpu.CompilerParams(dimension_semantics=sem, vmem_limit_bytes=VMEM_LIMIT)


def _row_tile(rows, pref):
    return pref if rows % pref == 0 else rows


def _proj_even(h, g, w_all, gn, bd, cos, sin):
    rows, d = h.shape
    tm = _row_tile(rows, 256)
    f = lambda w, dt: jax.ShapeDtypeStruct((rows, w), dt)
    out_shape = [f(512, F32)] * 4 + [f(LANES, F32)] + [f(512, CDT)] * 7 + [f(LANES, CDT)]
    out_specs = [_row_spec(tm, 512)] * 4 + [_row_spec(tm, LANES)] + [_row_spec(tm, 512)] * 7 + [_row_spec(tm, LANES)]
    return pl.pallas_call(
        _proj_even_kernel,
        grid=(rows // tm,),
        in_specs=[_row_spec(tm, d), _const_spec(g.shape), _const_spec(w_all.shape), _const_spec(gn.shape),
                  _const_spec(bd.shape), _row_spec(tm, LANES), _row_spec(tm, LANES)],
        out_specs=out_specs,
        out_shape=out_shape,
        compiler_params=_params(("parallel",)),
        name="proj_even",
    )(h, g, w_all, gn, bd, cos, sin)


def _proj_odd(h, g, w_all, gn, bf, bd):
    rows, d = h.shape
    tm = _row_tile(rows, 256)
    f = lambda w, dt: jax.ShapeDtypeStruct((rows, w), dt)
    out_shape = [f(1024, F32)] * 2 + [f(LANES, F32)] + [f(1024, CDT)] * 3
    out_specs = [_row_spec(tm, 1024)] * 2 + [_row_spec(tm, LANES)] + [_row_spec(tm, 1024)] * 3
    return pl.pallas_call(
        _proj_odd_kernel,
        grid=(rows // tm,),
        in_specs=[_row_spec(tm, d), _const_spec(g.shape), _const_spec(w_all.shape), _const_spec(gn.shape),
                  _const_spec(bf.shape), _const_spec(bd.shape)],
        out_specs=out_specs,
        out_shape=out_shape,
        compiler_params=_params(("parallel",)),
        name="proj_odd",
    )(h, g, w_all, gn, bf, bd)


def _post_kernel(*refs, n_o, ff_chunk):
    h_ref = refs[0]
    o_refs = refs[1:1 + n_o]
    p_ref = refs[1 + n_o]
    wo_refs = refs[2 + n_o:2 + 2 * n_o]
    gm_ref, wu_ref, wd_ref, gp_ref, wg_ref, wp_ref, out_ref = refs[2 + 2 * n_o:]
    mix = None
    for o_ref, wo_ref in zip(o_refs, wo_refs):
        t = _mm(o_ref[...], wo_ref[...])
        mix = t if mix is None else mix + t
    h = h_ref[...] + mix
    m = _rms(h, gm_ref[...]).astype(CDT)
    mlp = None
    d_ff = wu_ref.shape[1]
    for c in range(d_ff // ff_chunk):
        sl = slice(c * ff_chunk, (c + 1) * ff_chunk)
        u = jnp.square(jnp.maximum(_mm(m, wu_ref[:, sl]), 0.0)).astype(CDT)
        t = _mm(u, wd_ref[sl, :])
        mlp = t if mlp is None else mlp + t
    acc = h + mlp
    gate_in = _rms(acc, gp_ref[...]).astype(CDT)
    gate = 1.0 / (1.0 + jnp.exp(-_mm(gate_in, wg_ref[...])))
    out_ref[...] = acc + gate * _mm(p_ref[...].astype(CDT), wp_ref[...])


def _post(h, os_, p, wos, gm, wu, wd, gp, wg, wp):
    rows, d = h.shape
    tm = _row_tile(rows, 512)
    n_o = len(os_)
    in_specs = ([_row_spec(tm, d)] + [_row_spec(tm, o.shape[1]) for o in os_] + [_row_spec(tm, p.shape[1])]
                + [_const_spec(w.shape) for w in wos]
                + [_const_spec(x.shape) for x in (gm, wu, wd, gp, wg, wp)])
    return pl.pallas_call(
        functools.partial(_post_kernel, n_o=n_o, ff_chunk=1024),
        grid=(rows // tm,),
        in_specs=in_specs,
        out_specs=_row_spec(tm, d),
        out_shape=jax.ShapeDtypeStruct((rows, d), F32),
        compiler_params=_params(("parallel",)),
        name="post",
    )(h, *os_, p, *wos, gm, wu, wd, gp, wg, wp)


def _pair_stack(qp, tq):
    lane = lax.broadcasted_iota(I32, qp.shape, 1)
    zero = jnp.zeros_like(qp)
    return jnp.concatenate([jnp.where(lane < 64, qp, zero), jnp.where(lane >= 64, qp, zero)], axis=0)


def _pair_merge(o2, tq):
    lane = lax.broadcasted_iota(I32, (tq, LANES), 1)
    return jnp.where(lane < 64, o2[:tq], o2[tq:])


def _last_tile(i, tq, tk, qoff, length, chunked):
    qend = qoff + i * tq + tq - 1
    kmax = (qend // CHUNK + 1) * CHUNK if chunked else qend + 1
    kmax = jnp.minimum(kmax, length)
    return (kmax - 1) // tk


def _tri_steps(nq, tq, tk, qoff, length, chunked):
    ii, jj = [], []
    for i in range(nq):
        qend = qoff + i * tq + tq - 1
        kmax = min((qend // CHUNK + 1) * CHUNK if chunked else qend + 1, length)
        for j in range((kmax - 1) // tk + 1):
            ii.append(i)
            jj.append(j)
    return jnp.asarray(ii, I32), jnp.asarray(jj, I32)


def _tri_call(kernel_fn, steps, batch, in_specs, out_specs, out_shape, scratch, name):
    ii, jj = steps
    grid_spec = pltpu.PrefetchScalarGridSpec(
        num_scalar_prefetch=2, grid=(batch, ii.shape[0]),
        in_specs=in_specs, out_specs=out_specs, scratch_shapes=scratch)
    call = pl.pallas_call(kernel_fn, grid_spec=grid_spec, out_shape=out_shape,
                          compiler_params=_params(("parallel", "arbitrary")), name=name)
    return functools.partial(call, ii, jj)


def _flash_init(q_ref, aug_ref, qst_ref, m_ref, l_ref, acc_ref, tq):
    q = q_ref[0]
    for g in range(qst_ref.shape[0]):
        st = _pair_stack(q[:, g * LANES:(g + 1) * LANES], tq)
        if aug_ref is not None:
            a = aug_ref[0, 0]
            ext = jnp.concatenate([jnp.broadcast_to(a[2 * g:2 * g + 1], (tq, LANES)),
                                   jnp.broadcast_to(a[2 * g + 1:2 * g + 2], (tq, LANES))], axis=0)
            st = jnp.concatenate([st, ext.astype(CDT)], axis=1)
        qst_ref[g] = st
    m_ref[...] = jnp.full(m_ref.shape, NEG, F32)
    l_ref[...] = jnp.zeros(l_ref.shape, F32)
    acc_ref[...] = jnp.zeros(acc_ref.shape, F32)


def _flash_pairs(score_fn, vt_ref, scr, tq):
    _, m_ref, l_ref, acc_ref, s_ref = scr
    npair = m_ref.shape[0]
    s_ref[0] = score_fn(0)
    for g in range(npair):
        if g + 1 < npair:
            s_ref[(g + 1) % 2] = score_fn(g + 1)
        _flash_step(s_ref[g % 2], vt_ref, m_ref, l_ref, acc_ref, g, tq)


def _flash_step(st, vt_ref, m_ref, l_ref, acc_ref, g, tq):
    m_old = m_ref[g]
    m_new = jnp.maximum(m_old, jnp.max(st, axis=0, keepdims=True))
    alpha = jnp.exp2(m_old - m_new)
    m_ref[g] = m_new
    pc = jnp.exp2(st - m_new).astype(CDT)
    ones = jnp.ones((16, st.shape[0]), CDT)
    sums = []
    for e in range(2):
        rows = slice(g * LANES + e * HEAD_DIM, g * LANES + (e + 1) * HEAD_DIM)
        cols = slice(e * tq, (e + 1) * tq)
        pv = _mm(jnp.concatenate([vt_ref[0, rows, :], ones], axis=0), pc[:, cols])
        acc_ref[rows, :] = alpha[:, cols] * acc_ref[rows, :] + pv[:HEAD_DIM]
        sums.append(pv[HEAD_DIM:HEAD_DIM + 1])
    l_ref[g] = alpha * l_ref[g] + jnp.concatenate(sums, axis=1)


def _flash_finish(o_ref, l_ref, acc_ref, tq):
    for g in range(l_ref.shape[0]):
        l = l_ref[g]
        den = jnp.concatenate([jnp.broadcast_to(l[:, :tq], (HEAD_DIM, tq)),
                               jnp.broadcast_to(l[:, tq:], (HEAD_DIM, tq))], axis=0)
        o = acc_ref[g * LANES:(g + 1) * LANES, :] / den
        o_ref[0, :, g * LANES:(g + 1) * LANES] = o.T.astype(o_ref.dtype)


def _flash_scratch(npair, tq, tk, kd, w):
    return [pltpu.VMEM((npair, 2 * tq, kd), CDT),
            pltpu.VMEM((npair, 1, 2 * tq), F32),
            pltpu.VMEM((npair, 1, 2 * tq), F32),
            pltpu.VMEM((w, tq), F32),
            pltpu.VMEM((2, tk, 2 * tq), F32)]


def _band_kernel(*refs, n_kv, tq, mask_neg):
    q_ref = refs[0]
    k_refs = refs[1:1 + n_kv]
    v_refs = refs[1 + n_kv:1 + 2 * n_kv]
    bias_ref = refs[1 + 2 * n_kv]
    o_ref = refs[2 + 2 * n_kv]
    i = pl.program_id(1)
    q = q_ref[0]
    cat = lambda rs: (rs[0][0].astype(CDT) if len(rs) == 1
                      else jnp.concatenate([r[0].astype(CDT) for r in rs], axis=0))
    k = cat(k_refs)
    v = cat(v_refs)
    tk = bias_ref.shape[2]
    if k.shape[0] < tk:
        zpad = jnp.zeros((tk - k.shape[0], k.shape[1]), CDT)
        k = jnp.concatenate([k, zpad], axis=0)
        v = jnp.concatenate([v, zpad], axis=0)
    if mask_neg:
        kpos = lax.broadcasted_iota(I32, (2 * tq, tk), 1) + (i * tq - A_PAST)
        neg = kpos < 0
    for g in range(q.shape[1] // LANES):
        sl = slice(g * LANES, (g + 1) * LANES)
        s = _mm_nt(_pair_stack(q[:, sl], tq), k[:, sl])
        s = s + jnp.concatenate([bias_ref[2 * g], bias_ref[2 * g + 1]], axis=0)
        if mask_neg:
            s = jnp.where(neg, NEG, s)
        m = jnp.max(s, axis=-1, keepdims=True)
        p = jnp.exp(s - m)
        l = jnp.sum(p, axis=-1, keepdims=True)
        o2 = _mm(p.astype(CDT), v[:, sl]) / l
        o_ref[0, :, sl] = _pair_merge(o2, tq).astype(o_ref.dtype)


def _band(q, k_parts, v_parts, k_specs, bias, tq, mask_neg):
    b, s, w = q.shape
    n_kv = len(k_parts)
    q_spec = pl.BlockSpec((1, tq, w), lambda bb, i: (bb, i, 0))
    return pl.pallas_call(
        functools.partial(_band_kernel, n_kv=n_kv, tq=tq, mask_neg=mask_neg),
        grid=(b, s // tq),
        in_specs=[q_spec] + k_specs + k_specs + [_const_spec(bias.shape)],
        out_specs=q_spec,
        out_shape=jax.ShapeDtypeStruct((b, s, w), CDT),
        compiler_params=_params(("parallel", "parallel")),
        name="band_attn",
    )(q, *k_parts, *v_parts, bias)


def _band_bias(rel_bias, q_pos, k_pos):
    nq, nk = len(q_pos), len(k_pos)
    qc = q_pos[:, None] // CHUNK
    kc = k_pos[None, :] // CHUNK
    ok = (kc <= qc) & (kc >= qc - A_PAST // CHUNK)
    n = nq + nk - 1
    d0 = int(q_pos[0] - k_pos[0])
    m = np.arange(n + 1)
    rel = np.where(m < nk, d0 - m, d0 - m + n + 1)
    table = jnp.take(rel_bias.astype(F32), jnp.asarray(np.clip(rel, -REL_CLIP, REL_CLIP) + REL_CLIP), axis=1)
    bias = jnp.tile(table, (1, nq))[:, :nq * n].reshape(-1, nq, n)[:, :, :nk]
    return jnp.where(jnp.asarray(ok)[None], bias, NEG)


def _dsa_index_kernel(ii_ref, jj_ref, qi_ref, ki2_ref, wt_ref, keys_ref, t_ref, c_ref, qst_ref, kscr_ref, k16_ref,
                      *, tq, tk, nk, unroll, qoff, length, topk, idx_bits, nreal):
    i = ii_ref[pl.program_id(1)]
    j = jj_ref[pl.program_id(1)]
    last = _last_tile(i, tq, tk, qoff, length, True)
    q0 = qoff + i * tq

    @pl.when(j == 0)
    def _():
        q = qi_ref[0]
        for g in range(IDX_HEADS // 2):
            qst_ref[2 * g * tq:(2 * g + 2) * tq, :] = _pair_stack(q[:, g * LANES:(g + 1) * LANES], tq)

    @pl.when(j <= last)
    def _():
        lg = _mm_nt(ki2_ref[0], qst_ref[...])
        wgt = wt_ref[0] * (IDX_HEADS ** -0.5)
        sc = jnp.zeros((tk, tq), F32)
        for h in range(IDX_HEADS):
            sc = sc + jnp.maximum(lg[:, h * tq:(h + 1) * tq], 0.0) * wgt[h:h + 1, :]
        sc = jnp.where(sc == 0.0, 0.0, sc)
        bits = lax.bitcast_convert_type(sc, I32)
        key = bits ^ ((bits >> 31) & 0x7FFFFFFF)
        kpos = j * tk + lax.broadcasted_iota(I32, (tk, tq), 0)
        qpos = q0 + lax.broadcasted_iota(I32, (tk, tq), 1)
        ok = ((kpos >> 6) <= (qpos >> 6)) & (kpos < length)
        key = jnp.where(ok, key, INT_MIN)
        keys_ref[0] = key
        kscr_ref[j] = key
        top = lax.bitcast_convert_type(bits & -65536, F32)
        k16_ref[j] = jnp.where(ok, top, -jnp.inf).astype(jnp.bfloat16)
        if unroll > 1:
            @pl.when(j + 1 < nk)
            def _():
                kscr_ref[j + 1] = jnp.full((tk, tq), INT_MIN, I32)
                k16_ref[j + 1] = jnp.full((tk, tq), -jnp.inf, jnp.bfloat16)

    @pl.when(j == last)
    def _():
        def count(pred):
            def body(t, acc):
                for u in range(unroll):
                    tt = unroll * t + u
                    kpos = tt * tk + lax.broadcasted_iota(I32, (tk, tq), 0)
                    m = jnp.where(pred(kscr_ref[tt], kpos), 1.0, 0.0)
                    acc = acc + jnp.sum(m, axis=0, keepdims=True)
                return acc
            return lax.fori_loop(0, (last + unroll) // unroll, body, jnp.zeros((1, tq), F32))

        kf = float(topk)
        qpos = q0 + lax.broadcasted_iota(I32, (1, tq), 1)
        n_adm = jnp.minimum(((qpos >> 6) + 1) << 6, length).astype(F32)
        real = qpos - qoff < nreal
        take_all = (n_adm < kf) | jnp.logical_not(real)

        def count16(c16):
            one, zero = jnp.ones((), jnp.bfloat16), jnp.zeros((), jnp.bfloat16)

            def body(t, acc):
                for u in range(unroll):
                    m = jnp.where(k16_ref[unroll * t + u] >= c16, one, zero).reshape(tk // 16, 16, tq)
                    part = m[0]
                    for r in range(1, tk // 16):
                        part = part + m[r]
                    acc = acc + jnp.sum(part.astype(F32), axis=0, keepdims=True)
                return acc
            return lax.fori_loop(0, (last + unroll) // unroll, body, jnp.zeros((1, tq), F32))

        def bit_step(it, prefix, nge, top_half):
            cand = prefix + lax.shift_left(jnp.int32(1), 31 - it)
            if top_half:
                cbits = cand ^ ((cand >> 31) & 0x7FFFFFFF)
                cbits = jnp.where((cand > 0) & (cand < 0x00800000), 0x00800000, cbits)
                cnt = count16(lax.bitcast_convert_type(cbits & -65536, F32).astype(jnp.bfloat16))
            else:
                cnt = count(lambda kt, _: kt >= cand)
            take = cnt >= kf
            return jnp.where(take, cand, prefix), jnp.where(take, cnt, nge)

        def all_settled(nge):
            return (jnp.min(jnp.where(take_all | (nge == kf), 1.0, 0.0)) > 0.0).astype(I32)

        def low_body(carry):
            it, prefix, nge, _ = carry
            prefix, nge = bit_step(it, prefix, nge, False)
            return it + 1, prefix, nge, all_settled(nge)

        prefix, nge = lax.fori_loop(0, 16, lambda it, c: bit_step(it, c[0], c[1], True),
                                    (jnp.full((1, tq), INT_MIN, I32), jnp.zeros((1, tq), F32)))
        _, prefix, nge, _ = lax.while_loop(lambda c: (c[0] < 32) & (c[3] == 0), low_body,
                                           (jnp.int32(16), prefix, nge, all_settled(nge)))
        found = prefix > INT_MIN
        thr = jnp.maximum(prefix, INT_MIN + 1)
        t_ref[0] = thr
        c_ref[0] = jnp.full((1, tq), 2 ** 30, I32)
        straddle = jnp.max(jnp.where(found & (nge > kf) & real, 1.0, 0.0))

        @pl.when(straddle > 0.0)
        def _():
            need = kf - count(lambda kt, _: kt > thr)

            def tie_body(it, x):
                cx = x + lax.shift_left(jnp.int32(1), idx_bits - 1 - it)
                g = count(lambda kt, kpos: jnp.where(kt == thr, kpos, INT_MAX) < cx)
                return jnp.where(g < need, cx, x)
            c_ref[0] = lax.fori_loop(0, idx_bits, tie_body, jnp.zeros((1, tq), I32))


def _dsa_index(qi, ki2, wt, *, tq, tk, qoff, length, topk, nreal):
    b, s, _ = qi.shape
    lp = ki2.shape[1]
    nq, nk = s // tq, lp // tk
    idx_bits = max(1, int(np.ceil(np.log2(lp))))
    qvec = pl.BlockSpec((1, 1, tq), lambda bb, t, ii, jj: (bb, 0, ii[t]))
    return _tri_call(
        functools.partial(_dsa_index_kernel, tq=tq, tk=tk, nk=nk, unroll=2 if nk % 2 == 0 else 1,
                          qoff=qoff, length=length, topk=topk, idx_bits=idx_bits, nreal=nreal),
        _tri_steps(nq, tq, tk, qoff, length, True), b,
        in_specs=[pl.BlockSpec((1, tq, 512), lambda bb, t, ii, jj: (bb, ii[t], 0)),
                  pl.BlockSpec((1, tk, LANES), lambda bb, t, ii, jj: (bb, jj[t], 0)),
                  pl.BlockSpec((1, IDX_HEADS, tq), lambda bb, t, ii, jj: (bb, 0, ii[t]))],
        out_specs=[pl.BlockSpec((1, tk, tq), lambda bb, t, ii, jj: (bb, jj[t], ii[t])), qvec, qvec],
        out_shape=[jax.ShapeDtypeStruct((b, lp, s), I32),
                   jax.ShapeDtypeStruct((b, 1, s), I32),
                   jax.ShapeDtypeStruct((b, 1, s), I32)],
        scratch=[pltpu.VMEM((IDX_HEADS * tq, LANES), CDT),
                 pltpu.VMEM((nk, tk, tq), I32),
                 pltpu.VMEM((nk, tk, tq), jnp.bfloat16)],
        name="dsa_index",
    )(qi, ki2, wt)


def _dsa_attn_kernel(ii_ref, jj_ref, q_ref, k_ref, vt_ref, keys_ref, t_ref, c_ref, o_ref, *scr,
                     tq, tk, qoff, length):
    qst_ref, m_ref, l_ref, acc_ref = scr[:4]
    i = ii_ref[pl.program_id(1)]
    j = jj_ref[pl.program_id(1)]
    last = _last_tile(i, tq, tk, qoff, length, True)

    @pl.when(j == 0)
    def _():
        _flash_init(q_ref, None, qst_ref, m_ref, l_ref, acc_ref, tq)

    @pl.when(j <= last)
    def _():
        kt = keys_ref[0]
        thr = t_ref[0]
        kpos = j * tk + lax.broadcasted_iota(I32, (tk, tq), 0)
        sel = (kt > thr) | (jnp.where(kt == thr, kpos, INT_MAX) <= c_ref[0])
        mb = jnp.where(sel, 0.0, NEG)
        mb2 = jnp.concatenate([mb, mb], axis=1)
        score = lambda g: _mm_nt(k_ref[0, :, g * LANES:(g + 1) * LANES], qst_ref[g]) + mb2
        _flash_pairs(score, vt_ref, scr, tq)

    @pl.when(j == last)
    def _():
        _flash_finish(o_ref, l_ref, acc_ref, tq)


def _dsa_attn(q, k, vt, keys, thr, cut, *, tq, tk, qoff, length):
    b, s, w = q.shape
    qmap = lambda bb, t, ii, jj: (bb, ii[t], 0)
    qvec = pl.BlockSpec((1, 1, tq), lambda bb, t, ii, jj: (bb, 0, ii[t]))
    npair = w // LANES
    return _tri_call(
        functools.partial(_dsa_attn_kernel, tq=tq, tk=tk, qoff=qoff, length=length),
        _tri_steps(s // tq, tq, tk, qoff, length, True), b,
        in_specs=[pl.BlockSpec((1, tq, w), qmap),
                  pl.BlockSpec((1, tk, w), lambda bb, t, ii, jj: (bb, jj[t], 0)),
                  pl.BlockSpec((1, w, tk), lambda bb, t, ii, jj: (bb, 0, jj[t])),
                  pl.BlockSpec((1, tk, tq), lambda bb, t, ii, jj: (bb, jj[t], ii[t])),
                  qvec, qvec],
        out_specs=pl.BlockSpec((1, tq, w), qmap),
        out_shape=jax.ShapeDtypeStruct((b, s, w), CDT),
        scratch=_flash_scratch(npair, tq, tk, LANES, w),
        name="dsa_attn",
    )(q, k, vt, keys, thr, cut)


def _pieces3(x):
    pcs = _pieces(x, 3)
    return pcs + [jnp.zeros_like(pcs[0])] * (3 - len(pcs))


FOX_EXT = 9


def _cumsum_kernel(x_ref, tri_ref, place_ref, ones_ref, kx_ref, carry_ref, *, scale):
    @pl.when(pl.program_id(1) == 0)
    def _():
        carry_ref[...] = jnp.zeros(carry_ref.shape, F32)
    tri = tri_ref[...]
    tot = None
    for p in _pieces(x_ref[0], 3):
        t = _mm(tri, p)
        tot = t if tot is None else tot + t
    acc = carry_ref[...] + tot
    carry_ref[...] = acc[-1:, :]
    ext = None
    for c, p in enumerate(_pieces3(acc * (-scale))):
        t = _mm(p, place_ref[c])
        ext = t if ext is None else ext + t
    kx_ref[0] = (ext + ones_ref[...]).astype(kx_ref.dtype)


def _fox_key_lanes(x, tb, scale, nh):
    b, s, w = x.shape
    npair = nh // 2
    place = np.zeros((3, w, npair * LANES), np.float32)
    ones = np.zeros((1, npair * LANES), np.float32)
    for h in range(nh):
        for c in range(3):
            place[c, h, (h // 2) * LANES + (h % 2) * 6 + c] = 1.0
    for g in range(npair):
        ones[0, g * LANES + 3:g * LANES + 6] = 1.0
    tri = jnp.tril(jnp.ones((tb, tb), F32)).astype(CDT)
    place = jnp.asarray(place, CDT)
    spec = lambda wd: pl.BlockSpec((1, tb, wd), lambda bb, i: (bb, i, 0))
    return pl.pallas_call(
        functools.partial(_cumsum_kernel, scale=scale),
        grid=(b, s // tb),
        in_specs=[spec(w), _const_spec(tri.shape), _const_spec(place.shape), _const_spec(ones.shape)],
        out_specs=spec(npair * LANES),
        out_shape=jax.ShapeDtypeStruct((b, s, npair * LANES), CDT),
        scratch_shapes=[pltpu.VMEM((1, w), F32)],
        compiler_params=_params(("parallel", "arbitrary")),
        name="logf_cumsum",
    )(x, tri, place, jnp.asarray(ones))


def _fox_kernel(ii_ref, jj_ref, q_ref, k_ref, kx_ref, vt_ref, aug_ref, o_ref, *scr, tq, tk, qoff, length):
    qst_ref, m_ref, l_ref, acc_ref = scr[:4]
    i = ii_ref[pl.program_id(1)]
    j = jj_ref[pl.program_id(1)]
    last = _last_tile(i, tq, tk, qoff, length, False)
    q0 = qoff + i * tq

    @pl.when(j == 0)
    def _():
        _flash_init(q_ref, aug_ref, qst_ref, m_ref, l_ref, acc_ref, tq)

    def step(masked):
        if masked:
            kpos = j * tk + lax.broadcasted_iota(I32, (tk, 2 * tq), 0)
            lane = lax.broadcasted_iota(I32, (tk, 2 * tq), 1)
            causal = kpos <= q0 + jnp.where(lane >= tq, lane - tq, lane)

        def score(g):
            sl = slice(g * LANES, (g + 1) * LANES)
            st = _mm_nt(jnp.concatenate([k_ref[0, :, sl], kx_ref[0, :, sl]], axis=1), qst_ref[g])
            return jnp.where(causal, st, NEG) if masked else st
        _flash_pairs(score, vt_ref, scr, tq)

    diag = (j + 1) * tk - 1 > q0

    @pl.when((j <= last) & diag)
    def _():
        step(True)

    @pl.when((j <= last) & jnp.logical_not(diag))
    def _():
        step(False)

    @pl.when(j == last)
    def _():
        _flash_finish(o_ref, l_ref, acc_ref, tq)


def _fox(q, k, kx, vt, qaug, *, tq, tk, qoff, length):
    b, s, w = q.shape
    npair = w // LANES
    qmap = lambda bb, t, ii, jj: (bb, ii[t], 0)
    kmap = lambda bb, t, ii, jj: (bb, jj[t], 0)
    return _tri_call(
        functools.partial(_fox_kernel, tq=tq, tk=tk, qoff=qoff, length=length),
        _tri_steps(s // tq, tq, tk, qoff, length, False), b,
        in_specs=[pl.BlockSpec((1, tq, w), qmap),
                  pl.BlockSpec((1, tk, w), kmap),
                  pl.BlockSpec((1, tk, npair * LANES), kmap),
                  pl.BlockSpec((1, w, tk), lambda bb, t, ii, jj: (bb, 0, jj[t])),
                  pl.BlockSpec((1, 1, 2 * npair, LANES), lambda bb, t, ii, jj: (bb, ii[t], 0, 0))],
        out_specs=pl.BlockSpec((1, tq, w), qmap),
        out_shape=jax.ShapeDtypeStruct((b, s, w), CDT),
        scratch=_flash_scratch(npair, tq, tk, 2 * LANES, w),
        name="fox_attn",
    )(q, k, kx, vt, qaug)


def _fox_qaug(kx0):
    b, nq, _ = kx0.shape
    a = kx0.reshape(b, nq, -1, LANES).astype(F32)
    one, zero = jnp.ones_like(a[..., 0:3]), jnp.zeros_like(a[..., 0:3])
    even = jnp.concatenate([one, -a[..., 0:3], zero], axis=-1)
    odd = jnp.concatenate([zero, -a[..., 6:9], one], axis=-1)
    heads = jnp.stack([even, odd], axis=3).reshape(b, nq, -1, FOX_EXT)
    return jnp.pad(heads, ((0, 0), (0, 0), (0, 0), (0, LANES - FOX_EXT)))


def _rope_tables(pos):
    half = HEAD_DIM // 2
    inv = ROPE_THETA ** (-jnp.arange(half, dtype=F32) / half)
    ang = pos.astype(F32)[:, None] * inv[None, :]
    cos, sin = jnp.cos(ang), jnp.sin(ang)
    return (jnp.tile(jnp.concatenate([cos, cos], axis=1), (1, 2)),
            jnp.tile(jnp.concatenate([-sin, sin], axis=1), (1, 2)))


def _pad_rows(x, n):
    return x if x.shape[1] == n else jnp.pad(x, ((0, 0), (0, n - x.shape[1]), (0, 0)))


def _tiles(s):
    tk = 512 if s % 512 == 0 else s
    tq = 512 if s % 512 == 0 else s
    tqi = 256 if s % 256 == 0 else s
    return tq, tqi, tk


def kernel(x_prompt, x_sample, cache_a_k, cache_a_v, cache_b_k, cache_b_v, cache_b_ik, cache_c_k, cache_c_v,
           cache_c_logf, p_prompt, p_sample, g_mix, w_in_even, qn_a, kn_a, rel_bias_a, qn_b, kn_b, w_out_even,
           w_in_odd, b_f, qn_c, kn_c, w_out_odd, g_mlp, w_up, w_down, g_ple, w_ple_gate, w_ple_proj):
    B, S, D = x_prompt.shape
    DB, n, _ = x_sample.shape
    P = cache_b_k.shape[2]
    W_A = cache_a_k.shape[2]
    W_AP = min(A_PAST, S)
    depth = g_mix.shape[0]
    L = P + n
    Lp = -(-L // LANES) * LANES
    nqp = -(-n // LANES) * LANES
    assert B == 1 and S % 128 == 0 and n % 16 == 0 and P % 16 == 0

    row = lambda v: v.astype(F32).reshape(1, -1)
    bd = jnp.asarray(np.kron(np.eye(256 // HEAD_DIM), np.full((HEAD_DIM, HEAD_DIM), 1.0 / HEAD_DIM)), CDT)
    cos_p, sin_p = _rope_tables(jnp.arange(S))
    cos_s, sin_s = (jnp.tile(t, (DB, 1)) for t in _rope_tables(P + jnp.arange(n)))
    tq, tqi, tk = _tiles(S)
    tr = lambda t: jnp.swapaxes(t, 1, 2)
    catp = lambda c, new: _pad_rows(jnp.concatenate([c.astype(CDT), new], axis=1), Lp)

    outs = {name: [] for name in ("a_k_p", "a_v_p", "b_k_p", "b_v_p", "b_ik_p", "c_k_p", "c_v_p", "c_lf_p",
                                  "a_k_s", "a_v_s", "b_k_s", "b_v_s", "b_ik_s", "c_k_s", "c_v_s", "c_lf_s")}
    h_p = x_prompt.reshape(B * S, D)
    h_s = x_sample.reshape(DB * n, D)
    for i in range(depth):
        g = row(g_mix[i])
        if i % 2 == 0:
            e = i // 2
            wsz = 512
            w = w_in_even[e]
            w_all = jnp.concatenate([w[:, :7 * wsz], jnp.pad(w[:, 7 * wsz:], ((0, 0), (0, LANES - 72)))],
                                    axis=1).astype(CDT)
            gn = jnp.stack([jnp.tile(v[e].astype(F32), wsz // HEAD_DIM) for v in (qn_a, kn_a, qn_b, kn_b)])
            (ka, va, kb, vb, kw, qa_c, ka_c, va_c, qb_c, kb_c, vb_c, qi_c, ki2_c) = _proj_even(
                h_p, g, w_all, gn, bd, cos_p, sin_p)
            r3 = lambda t: t.reshape(B, S, t.shape[-1])
            tqa = 256 if S % 256 == 0 else 128
            npc = (A_PAST + tqa) // tqa
            k_specs = [pl.BlockSpec((1, tqa, wsz), functools.partial(
                lambda bb, ii, pp: (bb, jnp.maximum(ii + pp - A_PAST // tqa, 0), 0), pp=pp)) for pp in range(npc)]
            bias_p = _band_bias(rel_bias_a[e], np.arange(tqa), np.arange(-A_PAST, tqa))
            oa = _band(r3(qa_c), [r3(ka_c)] * npc, [r3(va_c)] * npc, k_specs, bias_p, tqa, True)
            topk_p = min(TOPK_MAX, S // 4)
            wt = tr(r3(kw)[..., 64:64 + IDX_HEADS])
            keys, thr, cut = _dsa_index(r3(qi_c), r3(ki2_c), wt, tq=tqi, tk=tk, qoff=0, length=S, topk=topk_p,
                                        nreal=S)
            ob = _dsa_attn(r3(qb_c), r3(kb_c), tr(r3(vb_c)), keys, thr, cut, tq=tq, tk=tk, qoff=0, length=S)
            os_p = [oa.reshape(B * S, wsz), ob.reshape(B * S, wsz)]
            hd = lambda t: t.reshape(B, S, -1, HEAD_DIM)
            outs["a_k_p"].append(hd(ka)[:, S - W_AP:]); outs["a_v_p"].append(hd(va)[:, S - W_AP:])
            outs["b_k_p"].append(hd(kb)); outs["b_v_p"].append(hd(vb))
            outs["b_ik_p"].append(kw[:, :64].reshape(B, S, 64))
            (ka, va, kb, vb, kw, qa_c, ka_c, va_c, qb_c, kb_c, vb_c, qi_c, ki2_c) = _proj_even(
                h_s, g, w_all, gn, bd, cos_s, sin_s)
            r3 = lambda t: t.reshape(DB, n, t.shape[-1])
            ca_k = cache_a_k[e].reshape(DB, W_A, wsz)
            ca_v = cache_a_v[e].reshape(DB, W_A, wsz)
            full = lambda rows, wd: pl.BlockSpec((1, rows, wd), lambda bb, ii: (bb, 0, 0))
            bias_s = _band_bias(rel_bias_a[e], P + np.arange(n), P - W_A + np.arange(W_A + n))
            bias_s = jnp.pad(bias_s, ((0, 0), (0, 0), (0, -(W_A + n) % LANES)), constant_values=NEG)
            oa = _band(r3(qa_c), [ca_k, r3(ka)], [ca_v, r3(va)], [full(W_A, wsz), full(n, wsz)], bias_s, n, False)
            ik = cache_b_ik[e].astype(CDT)
            ki2_all = catp(jnp.concatenate([ik, ik], axis=-1), r3(ki2_c))
            kb_all = catp(cache_b_k[e].reshape(DB, P, wsz), r3(kb_c))
            vb_all = catp(cache_b_v[e].reshape(DB, P, wsz), r3(vb_c))
            topk_s = min(TOPK_MAX, L // 4)
            wt = _pad_rows(r3(kw)[..., 64:64 + IDX_HEADS], nqp)
            keys, thr, cut = _dsa_index(_pad_rows(r3(qi_c), nqp), ki2_all, tr(wt),
                                        tq=nqp, tk=Lp, qoff=P, length=L, topk=topk_s, nreal=n)
            ob = _dsa_attn(_pad_rows(r3(qb_c), nqp), kb_all, tr(vb_all), keys, thr, cut,
                           tq=nqp, tk=Lp, qoff=P, length=L)[:, :n]
            os_s = [oa.reshape(DB * n, wsz), ob.reshape(DB * n, wsz)]
            hd = lambda t: t.reshape(DB, n, -1, HEAD_DIM)
            outs["a_k_s"].append(jnp.concatenate([cache_a_k[e], hd(ka)], axis=1)[:, n:])
            outs["a_v_s"].append(jnp.concatenate([cache_a_v[e], hd(va)], axis=1)[:, n:])
            outs["b_k_s"].append(hd(kb)); outs["b_v_s"].append(hd(vb))
            outs["b_ik_s"].append(kw[:, :64].reshape(DB, n, 64))
            w_out = w_out_even[e].astype(CDT)
            wos = [w_out[:wsz], w_out[wsz:]]
        else:
            o = i // 2
            nh = b_f.shape[1]
            wsz = nh * HEAD_DIM
            w = w_in_odd[o]
            w_all = jnp.concatenate([w[:, :3 * wsz], jnp.pad(w[:, 3 * wsz:], ((0, 0), (0, LANES - nh)))],
                                    axis=1).astype(CDT)
            gn = jnp.stack([jnp.tile(v[o].astype(F32), wsz // HEAD_DIM) for v in (qn_c, kn_c)])
            bf = jnp.pad(b_f[o].astype(F32), (0, LANES - nh)).reshape(1, LANES)
            k, v, lf, q_c, k_c, v_c = _proj_odd(h_p, g, w_all, gn, bf, bd)
            r3 = lambda t: t.reshape(B, S, t.shape[-1])
            kx = _fox_key_lanes(r3(lf), 512 if S % 512 == 0 else S, LOG2E, nh)
            oc = _fox(r3(q_c), r3(k_c), kx, tr(r3(v_c)), _fox_qaug(kx[:, ::tq]), tq=tq, tk=tk, qoff=0, length=S)
            os_p = [oc.reshape(B * S, wsz)]
            hd = lambda t: t.reshape(B, S, nh, HEAD_DIM)
            outs["c_k_p"].append(hd(k)); outs["c_v_p"].append(hd(v)); outs["c_lf_p"].append(lf[:, :nh].reshape(B, S, nh))
            k, v, lf, q_c, k_c, v_c = _proj_odd(h_s, g, w_all, gn, bf, bd)
            r3 = lambda t: t.reshape(DB, n, t.shape[-1])
            lf_all = jnp.concatenate([jnp.pad(cache_c_logf[o].astype(F32), ((0, 0), (0, 0), (0, LANES - nh))), r3(lf)],
                                     axis=1)
            kx = _fox_key_lanes(lf_all, L, LOG2E, nh)
            k_all = catp(cache_c_k[o].reshape(DB, P, wsz), r3(k_c))
            v_all = catp(cache_c_v[o].reshape(DB, P, wsz), r3(v_c))
            oc = _fox(_pad_rows(r3(q_c), nqp), k_all, _pad_rows(kx, Lp), tr(v_all),
                      _fox_qaug(kx[:, P:P + 1]), tq=nqp, tk=Lp, qoff=P, length=L)[:, :n]
            os_s = [oc.reshape(DB * n, wsz)]
            hd = lambda t: t.reshape(DB, n, nh, HEAD_DIM)
            outs["c_k_s"].append(hd(k)); outs["c_v_s"].append(hd(v)); outs["c_lf_s"].append(lf[:, :nh].reshape(DB, n, nh))
            w_out = w_out_odd[o].astype(CDT)
            wos = [w_out]
        post_w = (row(g_mlp[i]), w_up[i].astype(CDT), w_down[i].astype(CDT), row(g_ple[i]),
                  w_ple_gate[i].astype(CDT), w_ple_proj[i].astype(CDT))
        h_p = _post(h_p, os_p, p_prompt[i].reshape(B * S, -1), wos, *post_w)
        h_s = _post(h_s, os_s, p_sample[i].reshape(DB * n, -1), wos, *post_w)

    st = jnp.stack
    names = ("a_k_p", "a_v_p", "b_k_p", "b_v_p", "b_ik_p", "c_k_p", "c_v_p", "c_lf_p",
             "a_k_s", "a_v_s", "b_k_s", "b_v_s", "b_ik_s", "c_k_s", "c_v_s", "c_lf_s")
    return (h_p.reshape(B, S, D), h_s.reshape(DB, n, D)) + tuple(st(outs[nm]) for nm in names)
```

```python
import functools

import numpy as np
import jax
import jax.numpy as jnp
from jax import lax
from jax.experimental import pallas as pl
from jax.experimental.pallas import tpu as pltpu

F32 = jnp.float32
I32 = jnp.int32
CDT = jnp.bfloat16

CHUNK = 64
HEAD_DIM = 64
IDX_HEADS = 8
A_PAST = 8 * CHUNK
REL_CLIP = 128
TOPK_MAX = 256
ROPE_THETA = 10000.0
RMS_EPS = 1e-6
LANES = 128
MXU_TILE = 256
PACKED_ROWS = 16
CHUNK_SHIFT = CHUNK.bit_length() - 1
NEG = -1e30
INT_MIN = -(2 ** 31)
INT_MAX = 2 ** 31 - 1
LOG2E = 1.4426950408889634
VMEM_LIMIT = 56 * 1024 * 1024


def _mm(a, b):
    return jnp.dot(a, b, preferred_element_type=F32)


def _mm_nt(a, b):
    return lax.dot_general(a, b, (((1,), (1,)), ((), ())), preferred_element_type=F32)


def _pieces(x, n):
    if CDT == F32:
        return [x]
    out = []
    for _ in range(n - 1):
        p = x.astype(CDT)
        out.append(p)
        x = x - p.astype(F32)
    out.append(x.astype(CDT))
    return out


def _rms(x, g):
    return x * lax.rsqrt(jnp.mean(x * x, axis=-1, keepdims=True) + RMS_EPS) * g


def _head_rms(x, bd, g):
    x2 = x * x
    pcs = _pieces(x2, 2)
    cols = []
    for s in range(x.shape[1] // MXU_TILE):
        sl = slice(s * MXU_TILE, (s + 1) * MXU_TILE)
        ms = _mm(pcs[0][:, sl], bd)
        for p in pcs[1:]:
            ms = ms + _mm(p[:, sl], bd)
        cols.append(ms)
    ms = cols[0] if len(cols) == 1 else jnp.concatenate(cols, axis=1)
    return x * lax.rsqrt(ms + RMS_EPS) * g


def _rope(x, cos, sin):
    w = x.shape[1]
    lane = lax.broadcasted_iota(I32, x.shape, 1)
    first = (lane & 63) < 32
    swapped = jnp.where(first, pltpu.roll(x, w - 32, 1), pltpu.roll(x, 32, 1))
    return x * cos + swapped * sin


def _tile_lanes(t, w):
    return t if w == t.shape[1] else jnp.concatenate([t] * (w // t.shape[1]), axis=1)


def _proj_even_kernel(h_ref, g_ref, w_ref, gn_ref, bd_ref, cos_ref, sin_ref,
                      ka_ref, va_ref, kb_ref, vb_ref, kw_ref,
                      qa_c, ka_c, va_c, qb_c, kb_c, vb_c, qi_c, ki2_c):
    a = _rms(h_ref[...], g_ref[...]).astype(CDT)
    z = _mm(a, w_ref[...])
    w = 512
    bd = bd_ref[...]
    gn = gn_ref[...]
    cos1, sin1 = cos_ref[...], sin_ref[...]
    cos, sin = _tile_lanes(cos1, w), _tile_lanes(sin1, w)
    scale = HEAD_DIM ** -0.5

    qa = _head_rms(z[:, 0:w], bd, gn[0:1])
    qa_c[...] = (qa * scale).astype(CDT)
    ka = _head_rms(z[:, w:2 * w], bd, gn[1:2])
    ka_ref[...] = ka
    ka_c[...] = ka.astype(CDT)
    va = z[:, 2 * w:3 * w]
    va_ref[...] = va
    va_c[...] = va.astype(CDT)
    qb = _rope(_head_rms(z[:, 3 * w:4 * w], bd, gn[2:3]), cos, sin)
    qb_c[...] = (qb * (scale * LOG2E)).astype(CDT)
    kb = _rope(_head_rms(z[:, 4 * w:5 * w], bd, gn[3:4]), cos, sin)
    kb_ref[...] = kb
    kb_c[...] = kb.astype(CDT)
    vb = z[:, 5 * w:6 * w]
    vb_ref[...] = vb
    vb_c[...] = vb.astype(CDT)
    qi = _rope(z[:, 6 * w:7 * w], cos, sin)
    qi_c[...] = (qi * scale).astype(CDT)
    kw = z[:, 7 * w:7 * w + LANES]
    kwr = _rope(kw, cos1, sin1)
    lane = lax.broadcasted_iota(I32, kw.shape, 1)
    kw_ref[...] = jnp.where(lane < 64, kwr, kw)
    ki2_c[...] = jnp.where(lane < 64, kwr, pltpu.roll(kwr, 64, 1)).astype(CDT)


def _proj_odd_kernel(h_ref, g_ref, w_ref, gn_ref, bf_ref, bd_ref,
                     k_ref, v_ref, lf_ref, q_c, k_c, v_c):
    a = _rms(h_ref[...], g_ref[...]).astype(CDT)
    z = _mm(a, w_ref[...])
    w = 1024
    bd = bd_ref[...]
    gn = gn_ref[...]
    q = _head_rms(z[:, 0:w], bd, gn[0:1])
    q_c[...] = (q * (HEAD_DIM ** -0.5 * LOG2E)).astype(CDT)
    k = _head_rms(z[:, w:2 * w], bd, gn[1:2])
    k_ref[...] = k
    k_c[...] = k.astype(CDT)
    v = z[:, 2 * w:3 * w]
    v_ref[...] = v
    v_c[...] = v.astype(CDT)
    fl = z[:, 3 * w:3 * w + LANES] + bf_ref[...]
    lf_ref[...] = jnp.minimum(fl, 0.0) - jnp.log1p(jnp.exp(-jnp.abs(fl)))


def _const_spec(shape):
    nd = len(shape)
    return pl.BlockSpec(shape, lambda *_: (0,) * nd, pipeline_mode=pl.Buffered(1))


def _row_spec(tm, w):
    return pl.BlockSpec((tm, w), lambda i: (i, 0))


def _params(sem):
    return pltpu.CompilerParams(dimension_semantics=sem, vmem_limit_bytes=VMEM_LIMIT)


def _row_tile(rows, pref):
    return pref if rows % pref == 0 else rows


def _proj_even(h, g, w_all, gn, bd, cos, sin):
    rows, d = h.shape
    tm = _row_tile(rows, 256)
    f = lambda w, dt: jax.ShapeDtypeStruct((rows, w), dt)
    out_shape = [f(512, F32)] * 4 + [f(LANES, F32)] + [f(512, CDT)] * 7 + [f(LANES, CDT)]
    out_specs = [_row_spec(tm, 512)] * 4 + [_row_spec(tm, LANES)] + [_row_spec(tm, 512)] * 7 + [_row_spec(tm, LANES)]
    return pl.pallas_call(
        _proj_even_kernel,
        grid=(rows // tm,),
        in_specs=[_row_spec(tm, d), _const_spec(g.shape), _const_spec(w_all.shape), _const_spec(gn.shape),
                  _const_spec(bd.shape), _row_spec(tm, LANES), _row_spec(tm, LANES)],
        out_specs=out_specs,
        out_shape=out_shape,
        compiler_params=_params(("parallel",)),
        name="proj_even",
    )(h, g, w_all, gn, bd, cos, sin)


def _proj_odd(h, g, w_all, gn, bf, bd):
    rows, d = h.shape
    tm = _row_tile(rows, 256)
    f = lambda w, dt: jax.ShapeDtypeStruct((rows, w), dt)
    out_shape = [f(1024, F32)] * 2 + [f(LANES, F32)] + [f(1024, CDT)] * 3
    out_specs = [_row_spec(tm, 1024)] * 2 + [_row_spec(tm, LANES)] + [_row_spec(tm, 1024)] * 3
    return pl.pallas_call(
        _proj_odd_kernel,
        grid=(rows // tm,),
        in_specs=[_row_spec(tm, d), _const_spec(g.shape), _const_spec(w_all.shape), _const_spec(gn.shape),
                  _const_spec(bf.shape), _const_spec(bd.shape)],
        out_specs=out_specs,
        out_shape=out_shape,
        compiler_params=_params(("parallel",)),
        name="proj_odd",
    )(h, g, w_all, gn, bf, bd)


def _post_kernel(*refs, n_o, ff_chunk):
    h_ref = refs[0]
    o_refs = refs[1:1 + n_o]
    p_ref = refs[1 + n_o]
    wo_refs = refs[2 + n_o:2 + 2 * n_o]
    gm_ref, wu_ref, wd_ref, gp_ref, wg_ref, wp_ref, out_ref = refs[2 + 2 * n_o:]
    mix = None
    for o_ref, wo_ref in zip(o_refs, wo_refs):
        t = _mm(o_ref[...], wo_ref[...])
        mix = t if mix is None else mix + t
    h = h_ref[...] + mix
    m = _rms(h, gm_ref[...]).astype(CDT)
    mlp = None
    d_ff = wu_ref.shape[1]
    for c in range(d_ff // ff_chunk):
        sl = slice(c * ff_chunk, (c + 1) * ff_chunk)
        u = jnp.square(jnp.maximum(_mm(m, wu_ref[:, sl]), 0.0)).astype(CDT)
        t = _mm(u, wd_ref[sl, :])
        mlp = t if mlp is None else mlp + t
    acc = h + mlp
    gate_in = _rms(acc, gp_ref[...]).astype(CDT)
    gate = 1.0 / (1.0 + jnp.exp(-_mm(gate_in, wg_ref[...])))
    out_ref[...] = acc + gate * _mm(p_ref[...].astype(CDT), wp_ref[...])


def _post(h, os_, p, wos, gm, wu, wd, gp, wg, wp):
    rows, d = h.shape
    tm = _row_tile(rows, 512)
    n_o = len(os_)
    in_specs = ([_row_spec(tm, d)] + [_row_spec(tm, o.shape[1]) for o in os_] + [_row_spec(tm, p.shape[1])]
                + [_const_spec(w.shape) for w in wos]
                + [_const_spec(x.shape) for x in (gm, wu, wd, gp, wg, wp)])
    return pl.pallas_call(
        functools.partial(_post_kernel, n_o=n_o, ff_chunk=1024),
        grid=(rows // tm,),
        in_specs=in_specs,
        out_specs=_row_spec(tm, d),
        out_shape=jax.ShapeDtypeStruct((rows, d), F32),
        compiler_params=_params(("parallel",)),
        name="post",
    )(h, *os_, p, *wos, gm, wu, wd, gp, wg, wp)


def _pair_stack(qp, tq):
    lane = lax.broadcasted_iota(I32, qp.shape, 1)
    zero = jnp.zeros_like(qp)
    return jnp.concatenate([jnp.where(lane < 64, qp, zero), jnp.where(lane >= 64, qp, zero)], axis=0)


def _pair_merge(o2, tq):
    lane = lax.broadcasted_iota(I32, (tq, LANES), 1)
    return jnp.where(lane < 64, o2[:tq], o2[tq:])


def _last_tile(i, tq, tk, qoff, length, chunked):
    qend = qoff + i * tq + tq - 1
    kmax = (qend // CHUNK + 1) * CHUNK if chunked else qend + 1
    kmax = jnp.minimum(kmax, length)
    return (kmax - 1) // tk


def _tri_steps(nq, tq, tk, qoff, length, chunked):
    ii, jj = [], []
    for i in range(nq):
        qend = qoff + i * tq + tq - 1
        kmax = min((qend // CHUNK + 1) * CHUNK if chunked else qend + 1, length)
        for j in range((kmax - 1) // tk + 1):
            ii.append(i)
            jj.append(j)
    return jnp.asarray(ii, I32), jnp.asarray(jj, I32)


def _tri_call(kernel_fn, steps, batch, in_specs, out_specs, out_shape, scratch, name):
    ii, jj = steps
    grid_spec = pltpu.PrefetchScalarGridSpec(
        num_scalar_prefetch=2, grid=(batch, ii.shape[0]),
        in_specs=in_specs, out_specs=out_specs, scratch_shapes=scratch)
    call = pl.pallas_call(kernel_fn, grid_spec=grid_spec, out_shape=out_shape,
                          compiler_params=_params(("parallel", "arbitrary")), name=name)
    return functools.partial(call, ii, jj)


def _flash_init(q_ref, aug_ref, qst_ref, m_ref, l_ref, acc_ref, tq):
    q = q_ref[0]
    for g in range(qst_ref.shape[0]):
        st = _pair_stack(q[:, g * LANES:(g + 1) * LANES], tq)
        if aug_ref is not None:
            a = aug_ref[0, 0]
            ext = jnp.concatenate([jnp.broadcast_to(a[2 * g:2 * g + 1], (tq, LANES)),
                                   jnp.broadcast_to(a[2 * g + 1:2 * g + 2], (tq, LANES))], axis=0)
            st = jnp.concatenate([st, ext.astype(CDT)], axis=1)
        qst_ref[g] = st
    m_ref[...] = jnp.full(m_ref.shape, NEG, F32)
    l_ref[...] = jnp.zeros(l_ref.shape, F32)
    acc_ref[...] = jnp.zeros(acc_ref.shape, F32)


def _flash_pairs(score_fn, vt_ref, scr, tq):
    _, m_ref, l_ref, acc_ref, s_ref = scr
    npair = m_ref.shape[0]
    s_ref[0] = score_fn(0)
    for g in range(npair):
        if g + 1 < npair:
            s_ref[(g + 1) % 2] = score_fn(g + 1)
        _flash_step(s_ref[g % 2], vt_ref, m_ref, l_ref, acc_ref, g, tq)


def _flash_step(st, vt_ref, m_ref, l_ref, acc_ref, g, tq):
    m_old = m_ref[g]
    m_new = jnp.maximum(m_old, jnp.max(st, axis=0, keepdims=True))
    alpha = jnp.exp2(m_old - m_new)
    m_ref[g] = m_new
    pc = jnp.exp2(st - m_new).astype(CDT)
    ones = jnp.ones((PACKED_ROWS, st.shape[0]), CDT)
    sums = []
    for e in range(2):
        rows = slice(g * LANES + e * HEAD_DIM, g * LANES + (e + 1) * HEAD_DIM)
        cols = slice(e * tq, (e + 1) * tq)
        pv = _mm(jnp.concatenate([vt_ref[0, rows, :], ones], axis=0), pc[:, cols])
        acc_ref[rows, :] = alpha[:, cols] * acc_ref[rows, :] + pv[:HEAD_DIM]
        sums.append(pv[HEAD_DIM:HEAD_DIM + 1])
    l_ref[g] = alpha * l_ref[g] + jnp.concatenate(sums, axis=1)


def _flash_finish(o_ref, l_ref, acc_ref, tq):
    for g in range(l_ref.shape[0]):
        l = l_ref[g]
        den = jnp.concatenate([jnp.broadcast_to(l[:, :tq], (HEAD_DIM, tq)),
                               jnp.broadcast_to(l[:, tq:], (HEAD_DIM, tq))], axis=0)
        o = acc_ref[g * LANES:(g + 1) * LANES, :] / den
        o_ref[0, :, g * LANES:(g + 1) * LANES] = o.T.astype(o_ref.dtype)


def _flash_scratch(npair, tq, tk, kd, w):
    return [pltpu.VMEM((npair, 2 * tq, kd), CDT),
            pltpu.VMEM((npair, 1, 2 * tq), F32),
            pltpu.VMEM((npair, 1, 2 * tq), F32),
            pltpu.VMEM((w, tq), F32),
            pltpu.VMEM((2, tk, 2 * tq), F32)]


def _band_kernel(*refs, n_kv, tq, mask_neg):
    q_ref = refs[0]
    k_refs = refs[1:1 + n_kv]
    v_refs = refs[1 + n_kv:1 + 2 * n_kv]
    bias_ref = refs[1 + 2 * n_kv]
    o_ref = refs[2 + 2 * n_kv]
    i = pl.program_id(1)
    q = q_ref[0]
    cat = lambda rs: (rs[0][0].astype(CDT) if len(rs) == 1
                      else jnp.concatenate([r[0].astype(CDT) for r in rs], axis=0))
    k = cat(k_refs)
    v = cat(v_refs)
    tk = bias_ref.shape[2]
    if k.shape[0] < tk:
        zpad = jnp.zeros((tk - k.shape[0], k.shape[1]), CDT)
        k = jnp.concatenate([k, zpad], axis=0)
        v = jnp.concatenate([v, zpad], axis=0)
    if mask_neg:
        kpos = lax.broadcasted_iota(I32, (2 * tq, tk), 1) + (i * tq - A_PAST)
        neg = kpos < 0
    for g in range(q.shape[1] // LANES):
        sl = slice(g * LANES, (g + 1) * LANES)
        s = _mm_nt(_pair_stack(q[:, sl], tq), k[:, sl])
        s = s + jnp.concatenate([bias_ref[2 * g], bias_ref[2 * g + 1]], axis=0)
        if mask_neg:
            s = jnp.where(neg, NEG, s)
        m = jnp.max(s, axis=-1, keepdims=True)
        p = jnp.exp(s - m)
        l = jnp.sum(p, axis=-1, keepdims=True)
        o2 = _mm(p.astype(CDT), v[:, sl]) / l
        o_ref[0, :, sl] = _pair_merge(o2, tq).astype(o_ref.dtype)


def _band(q, k_parts, v_parts, k_specs, bias, tq, mask_neg):
    b, s, w = q.shape
    n_kv = len(k_parts)
    q_spec = pl.BlockSpec((1, tq, w), lambda bb, i: (bb, i, 0))
    return pl.pallas_call(
        functools.partial(_band_kernel, n_kv=n_kv, tq=tq, mask_neg=mask_neg),
        grid=(b, s // tq),
        in_specs=[q_spec] + k_specs + k_specs + [_const_spec(bias.shape)],
        out_specs=q_spec,
        out_shape=jax.ShapeDtypeStruct((b, s, w), CDT),
        compiler_params=_params(("parallel", "parallel")),
        name="band_attn",
    )(q, *k_parts, *v_parts, bias)


def _band_bias(rel_bias, q_pos, k_pos):
    nq, nk = len(q_pos), len(k_pos)
    qc = q_pos[:, None] // CHUNK
    kc = k_pos[None, :] // CHUNK
    ok = (kc <= qc) & (kc >= qc - A_PAST // CHUNK)
    n = nq + nk - 1
    d0 = int(q_pos[0] - k_pos[0])
    m = np.arange(n + 1)
    rel = np.where(m < nk, d0 - m, d0 - m + n + 1)
    table = jnp.take(rel_bias.astype(F32), jnp.asarray(np.clip(rel, -REL_CLIP, REL_CLIP) + REL_CLIP), axis=1)
    bias = jnp.tile(table, (1, nq))[:, :nq * n].reshape(-1, nq, n)[:, :, :nk]
    return jnp.where(jnp.asarray(ok)[None], bias, NEG)


def _dsa_index_kernel(ii_ref, jj_ref, qi_ref, ki2_ref, wt_ref, keys_ref, t_ref, c_ref, qst_ref, kscr_ref, k16_ref,
                      *, tq, tk, nk, unroll, qoff, length, topk, idx_bits, nreal):
    i = ii_ref[pl.program_id(1)]
    j = jj_ref[pl.program_id(1)]
    last = _last_tile(i, tq, tk, qoff, length, True)
    q0 = qoff + i * tq

    @pl.when(j == 0)
    def _():
        q = qi_ref[0]
        for g in range(IDX_HEADS // 2):
            qst_ref[2 * g * tq:(2 * g + 2) * tq, :] = _pair_stack(q[:, g * LANES:(g + 1) * LANES], tq)

    @pl.when(j <= last)
    def _():
        lg = _mm_nt(ki2_ref[0], qst_ref[...])
        wgt = wt_ref[0] * (IDX_HEADS ** -0.5)
        sc = jnp.zeros((tk, tq), F32)
        for h in range(IDX_HEADS):
            sc = sc + jnp.maximum(lg[:, h * tq:(h + 1) * tq], 0.0) * wgt[h:h + 1, :]
        sc = jnp.where(sc == 0.0, 0.0, sc)
        bits = lax.bitcast_convert_type(sc, I32)
        key = bits ^ ((bits >> 31) & 0x7FFFFFFF)
        kpos = j * tk + lax.broadcasted_iota(I32, (tk, tq), 0)
        qpos = q0 + lax.broadcasted_iota(I32, (tk, tq), 1)
        ok = ((kpos >> CHUNK_SHIFT) <= (qpos >> CHUNK_SHIFT)) & (kpos < length)
        key = jnp.where(ok, key, INT_MIN)
        keys_ref[0] = key
        kscr_ref[j] = key
        top = lax.bitcast_convert_type(bits & -65536, F32)
        k16_ref[j] = jnp.where(ok, top, -jnp.inf).astype(jnp.bfloat16)
        if unroll > 1:
            @pl.when(j + 1 < nk)
            def _():
                kscr_ref[j + 1] = jnp.full((tk, tq), INT_MIN, I32)
                k16_ref[j + 1] = jnp.full((tk, tq), -jnp.inf, jnp.bfloat16)

    @pl.when(j == last)
    def _():
        def count(pred):
            def body(t, acc):
                for u in range(unroll):
                    tt = unroll * t + u
                    kpos = tt * tk + lax.broadcasted_iota(I32, (tk, tq), 0)
                    m = jnp.where(pred(kscr_ref[tt], kpos), 1.0, 0.0)
                    acc = acc + jnp.sum(m, axis=0, keepdims=True)
                return acc
            return lax.fori_loop(0, (last + unroll) // unroll, body, jnp.zeros((1, tq), F32))

        kf = float(topk)
        qpos = q0 + lax.broadcasted_iota(I32, (1, tq), 1)
        n_adm = jnp.minimum(((qpos >> CHUNK_SHIFT) + 1) << CHUNK_SHIFT, length).astype(F32)
        real = qpos - qoff < nreal
        take_all = (n_adm < kf) | jnp.logical_not(real)

        def count16(c16):
            one, zero = jnp.ones((), jnp.bfloat16), jnp.zeros((), jnp.bfloat16)
            nslab = tk // PACKED_ROWS
            assert nslab <= 256

            def body(t, acc):
                for u in range(unroll):
                    m = jnp.where(k16_ref[unroll * t + u] >= c16, one, zero).reshape(nslab, PACKED_ROWS, tq)
                    part = m[0]
                    for r in range(1, nslab):
                        part = part + m[r]
                    acc = acc + jnp.sum(part.astype(F32), axis=0, keepdims=True)
                return acc
            return lax.fori_loop(0, (last + unroll) // unroll, body, jnp.zeros((1, tq), F32))

        def bit_step(it, prefix, nge, top_half):
            cand = prefix + lax.shift_left(jnp.int32(1), 31 - it)
            if top_half:
                cbits = cand ^ ((cand >> 31) & 0x7FFFFFFF)
                cbits = jnp.where((cand > 0) & (cand < 0x00800000), 0x00800000, cbits)
                cnt = count16(lax.bitcast_convert_type(cbits & -65536, F32).astype(jnp.bfloat16))
            else:
                cnt = count(lambda kt, _: kt >= cand)
            take = cnt >= kf
            return jnp.where(take, cand, prefix), jnp.where(take, cnt, nge)

        def all_settled(nge):
            return (jnp.min(jnp.where(take_all | (nge == kf), 1.0, 0.0)) > 0.0).astype(I32)

        def low_body(carry):
            it, prefix, nge, _ = carry
            prefix, nge = bit_step(it, prefix, nge, False)
            return it + 1, prefix, nge, all_settled(nge)

        prefix, nge = lax.fori_loop(0, 16, lambda it, c: bit_step(it, c[0], c[1], True),
                                    (jnp.full((1, tq), INT_MIN, I32), jnp.zeros((1, tq), F32)))
        _, prefix, nge, _ = lax.while_loop(lambda c: (c[0] < 32) & (c[3] == 0), low_body,
                                           (jnp.int32(16), prefix, nge, all_settled(nge)))
        found = prefix > INT_MIN
        thr = jnp.maximum(prefix, INT_MIN + 1)
        t_ref[0] = thr
        c_ref[0] = jnp.full((1, tq), 2 ** 30, I32)
        straddle = jnp.max(jnp.where(found & (nge > kf) & real, 1.0, 0.0))

        @pl.when(straddle > 0.0)
        def _():
            need = kf - count(lambda kt, _: kt > thr)

            def tie_body(it, x):
                cx = x + lax.shift_left(jnp.int32(1), idx_bits - 1 - it)
                g = count(lambda kt, kpos: jnp.where(kt == thr, kpos, INT_MAX) < cx)
                return jnp.where(g < need, cx, x)
            c_ref[0] = lax.fori_loop(0, idx_bits, tie_body, jnp.zeros((1, tq), I32))


def _dsa_index(qi, ki2, wt, *, tq, tk, qoff, length, topk, nreal):
    b, s, _ = qi.shape
    lp = ki2.shape[1]
    nq, nk = s // tq, lp // tk
    idx_bits = max(1, int(np.ceil(np.log2(lp))))
    qvec = pl.BlockSpec((1, 1, tq), lambda bb, t, ii, jj: (bb, 0, ii[t]))
    return _tri_call(
        functools.partial(_dsa_index_kernel, tq=tq, tk=tk, nk=nk, unroll=2 if nk % 2 == 0 else 1,
                          qoff=qoff, length=length, topk=topk, idx_bits=idx_bits, nreal=nreal),
        _tri_steps(nq, tq, tk, qoff, length, True), b,
        in_specs=[pl.BlockSpec((1, tq, 512), lambda bb, t, ii, jj: (bb, ii[t], 0)),
                  pl.BlockSpec((1, tk, LANES), lambda bb, t, ii, jj: (bb, jj[t], 0)),
                  pl.BlockSpec((1, IDX_HEADS, tq), lambda bb, t, ii, jj: (bb, 0, ii[t]))],
        out_specs=[pl.BlockSpec((1, tk, tq), lambda bb, t, ii, jj: (bb, jj[t], ii[t])), qvec, qvec],
        out_shape=[jax.ShapeDtypeStruct((b, lp, s), I32),
                   jax.ShapeDtypeStruct((b, 1, s), I32),
                   jax.ShapeDtypeStruct((b, 1, s), I32)],
        scratch=[pltpu.VMEM((IDX_HEADS * tq, LANES), CDT),
                 pltpu.VMEM((nk, tk, tq), I32),
                 pltpu.VMEM((nk, tk, tq), jnp.bfloat16)],
        name="dsa_index",
    )(qi, ki2, wt)


def _dsa_attn_kernel(ii_ref, jj_ref, q_ref, k_ref, vt_ref, keys_ref, t_ref, c_ref, o_ref, *scr,
                     tq, tk, qoff, length):
    qst_ref, m_ref, l_ref, acc_ref = scr[:4]
    i = ii_ref[pl.program_id(1)]
    j = jj_ref[pl.program_id(1)]
    last = _last_tile(i, tq, tk, qoff, length, True)

    @pl.when(j == 0)
    def _():
        _flash_init(q_ref, None, qst_ref, m_ref, l_ref, acc_ref, tq)

    @pl.when(j <= last)
    def _():
        kt = keys_ref[0]
        thr = t_ref[0]
        kpos = j * tk + lax.broadcasted_iota(I32, (tk, tq), 0)
        sel = (kt > thr) | (jnp.where(kt == thr, kpos, INT_MAX) <= c_ref[0])
        mb = jnp.where(sel, 0.0, NEG)
        mb2 = jnp.concatenate([mb, mb], axis=1)
        score = lambda g: _mm_nt(k_ref[0, :, g * LANES:(g + 1) * LANES], qst_ref[g]) + mb2
        _flash_pairs(score, vt_ref, scr, tq)

    @pl.when(j == last)
    def _():
        _flash_finish(o_ref, l_ref, acc_ref, tq)


def _dsa_attn(q, k, vt, keys, thr, cut, *, tq, tk, qoff, length):
    b, s, w = q.shape
    qmap = lambda bb, t, ii, jj: (bb, ii[t], 0)
    qvec = pl.BlockSpec((1, 1, tq), lambda bb, t, ii, jj: (bb, 0, ii[t]))
    npair = w // LANES
    return _tri_call(
        functools.partial(_dsa_attn_kernel, tq=tq, tk=tk, qoff=qoff, length=length),
        _tri_steps(s // tq, tq, tk, qoff, length, True), b,
        in_specs=[pl.BlockSpec((1, tq, w), qmap),
                  pl.BlockSpec((1, tk, w), lambda bb, t, ii, jj: (bb, jj[t], 0)),
                  pl.BlockSpec((1, w, tk), lambda bb, t, ii, jj: (bb, 0, jj[t])),
                  pl.BlockSpec((1, tk, tq), lambda bb, t, ii, jj: (bb, jj[t], ii[t])),
                  qvec, qvec],
        out_specs=pl.BlockSpec((1, tq, w), qmap),
        out_shape=jax.ShapeDtypeStruct((b, s, w), CDT),
        scratch=_flash_scratch(npair, tq, tk, LANES, w),
        name="dsa_attn",
    )(q, k, vt, keys, thr, cut)


def _pieces3(x):
    pcs = _pieces(x, 3)
    return pcs + [jnp.zeros_like(pcs[0])] * (3 - len(pcs))


FOX_EXT = 9


def _cumsum_kernel(x_ref, tri_ref, place_ref, ones_ref, kx_ref, carry_ref, *, scale):
    @pl.when(pl.program_id(1) == 0)
    def _():
        carry_ref[...] = jnp.zeros(carry_ref.shape, F32)
    tri = tri_ref[...]
    tot = None
    for p in _pieces(x_ref[0], 3):
        t = _mm(tri, p)
        tot = t if tot is None else tot + t
    acc = carry_ref[...] + tot
    carry_ref[...] = acc[-1:, :]
    ext = None
    for c, p in enumerate(_pieces3(acc * (-scale))):
        t = _mm(p, place_ref[c])
        ext = t if ext is None else ext + t
    kx_ref[0] = (ext + ones_ref[...]).astype(kx_ref.dtype)


def _fox_key_lanes(x, tb, scale, nh):
    b, s, w = x.shape
    npair = nh // 2
    place = np.zeros((3, w, npair * LANES), np.float32)
    ones = np.zeros((1, npair * LANES), np.float32)
    for h in range(nh):
        for c in range(3):
            place[c, h, (h // 2) * LANES + (h % 2) * 6 + c] = 1.0
    for g in range(npair):
        ones[0, g * LANES + 3:g * LANES + 6] = 1.0
    tri = jnp.tril(jnp.ones((tb, tb), F32)).astype(CDT)
    place = jnp.asarray(place, CDT)
    spec = lambda wd: pl.BlockSpec((1, tb, wd), lambda bb, i: (bb, i, 0))
    return pl.pallas_call(
        functools.partial(_cumsum_kernel, scale=scale),
        grid=(b, s // tb),
        in_specs=[spec(w), _const_spec(tri.shape), _const_spec(place.shape), _const_spec(ones.shape)],
        out_specs=spec(npair * LANES),
        out_shape=jax.ShapeDtypeStruct((b, s, npair * LANES), CDT),
        scratch_shapes=[pltpu.VMEM((1, w), F32)],
        compiler_params=_params(("parallel", "arbitrary")),
        name="logf_cumsum",
    )(x, tri, place, jnp.asarray(ones))


def _fox_kernel(ii_ref, jj_ref, q_ref, k_ref, kx_ref, vt_ref, aug_ref, o_ref, *scr, tq, tk, qoff, length):
    qst_ref, m_ref, l_ref, acc_ref = scr[:4]
    i = ii_ref[pl.program_id(1)]
    j = jj_ref[pl.program_id(1)]
    last = _last_tile(i, tq, tk, qoff, length, False)
    q0 = qoff + i * tq

    @pl.when(j == 0)
    def _():
        _flash_init(q_ref, aug_ref, qst_ref, m_ref, l_ref, acc_ref, tq)

    def step(masked):
        if masked:
            kpos = j * tk + lax.broadcasted_iota(I32, (tk, 2 * tq), 0)
            lane = lax.broadcasted_iota(I32, (tk, 2 * tq), 1)
            causal = kpos <= q0 + jnp.where(lane >= tq, lane - tq, lane)

        def score(g):
            sl = slice(g * LANES, (g + 1) * LANES)
            st = _mm_nt(jnp.concatenate([k_ref[0, :, sl], kx_ref[0, :, sl]], axis=1), qst_ref[g])
            return jnp.where(causal, st, NEG) if masked else st
        _flash_pairs(score, vt_ref, scr, tq)

    diag = (j + 1) * tk - 1 > q0

    @pl.when((j <= last) & diag)
    def _():
        step(True)

    @pl.when((j <= last) & jnp.logical_not(diag))
    def _():
        step(False)

    @pl.when(j == last)
    def _():
        _flash_finish(o_ref, l_ref, acc_ref, tq)


def _fox(q, k, kx, vt, qaug, *, tq, tk, qoff, length):
    b, s, w = q.shape
    npair = w // LANES
    qmap = lambda bb, t, ii, jj: (bb, ii[t], 0)
    kmap = lambda bb, t, ii, jj: (bb, jj[t], 0)
    return _tri_call(
        functools.partial(_fox_kernel, tq=tq, tk=tk, qoff=qoff, length=length),
        _tri_steps(s // tq, tq, tk, qoff, length, False), b,
        in_specs=[pl.BlockSpec((1, tq, w), qmap),
                  pl.BlockSpec((1, tk, w), kmap),
                  pl.BlockSpec((1, tk, npair * LANES), kmap),
                  pl.BlockSpec((1, w, tk), lambda bb, t, ii, jj: (bb, 0, jj[t])),
                  pl.BlockSpec((1, 1, 2 * npair, LANES), lambda bb, t, ii, jj: (bb, ii[t], 0, 0))],
        out_specs=pl.BlockSpec((1, tq, w), qmap),
        out_shape=jax.ShapeDtypeStruct((b, s, w), CDT),
        scratch=_flash_scratch(npair, tq, tk, 2 * LANES, w),
        name="fox_attn",
    )(q, k, kx, vt, qaug)


def _fox_qaug(kx0):
    b, nq, _ = kx0.shape
    a = kx0.reshape(b, nq, -1, LANES).astype(F32)
    one, zero = jnp.ones_like(a[..., 0:3]), jnp.zeros_like(a[..., 0:3])
    even = jnp.concatenate([one, -a[..., 0:3], zero], axis=-1)
    odd = jnp.concatenate([zero, -a[..., 6:9], one], axis=-1)
    heads = jnp.stack([even, odd], axis=3).reshape(b, nq, -1, FOX_EXT)
    return jnp.pad(heads, ((0, 0), (0, 0), (0, 0), (0, LANES - FOX_EXT)))


def _rope_tables(pos):
    half = HEAD_DIM // 2
    inv = ROPE_THETA ** (-jnp.arange(half, dtype=F32) / half)
    ang = pos.astype(F32)[:, None] * inv[None, :]
    cos, sin = jnp.cos(ang), jnp.sin(ang)
    return (jnp.tile(jnp.concatenate([cos, cos], axis=1), (1, 2)),
            jnp.tile(jnp.concatenate([-sin, sin], axis=1), (1, 2)))


def _pad_rows(x, n):
    return x if x.shape[1] == n else jnp.pad(x, ((0, 0), (0, n - x.shape[1]), (0, 0)))


def _tiles(s):
    tk = 512 if s % 512 == 0 else s
    tq = 512 if s % 512 == 0 else s
    tqi = 256 if s % 256 == 0 else s
    return tq, tqi, tk


def kernel(x_prompt, x_sample, cache_a_k, cache_a_v, cache_b_k, cache_b_v, cache_b_ik, cache_c_k, cache_c_v,
           cache_c_logf, p_prompt, p_sample, g_mix, w_in_even, qn_a, kn_a, rel_bias_a, qn_b, kn_b, w_out_even,
           w_in_odd, b_f, qn_c, kn_c, w_out_odd, g_mlp, w_up, w_down, g_ple, w_ple_gate, w_ple_proj):
    B, S, D = x_prompt.shape
    DB, n, _ = x_sample.shape
    P = cache_b_k.shape[2]
    W_A = cache_a_k.shape[2]
    W_AP = min(A_PAST, S)
    depth = g_mix.shape[0]
    L = P + n
    Lp = -(-L // LANES) * LANES
    nqp = -(-n // LANES) * LANES
    assert B == 1 and S % 128 == 0 and n % 16 == 0 and P % 16 == 0

    row = lambda v: v.astype(F32).reshape(1, -1)
    bd = jnp.asarray(np.kron(np.eye(MXU_TILE // HEAD_DIM), np.full((HEAD_DIM, HEAD_DIM), 1.0 / HEAD_DIM)), CDT)
    cos_p, sin_p = _rope_tables(jnp.arange(S))
    cos_s, sin_s = (jnp.tile(t, (DB, 1)) for t in _rope_tables(P + jnp.arange(n)))
    tq, tqi, tk = _tiles(S)
    tr = lambda t: jnp.swapaxes(t, 1, 2)
    catp = lambda c, new: _pad_rows(jnp.concatenate([c.astype(CDT), new], axis=1), Lp)

    outs = {name: [] for name in ("a_k_p", "a_v_p", "b_k_p", "b_v_p", "b_ik_p", "c_k_p", "c_v_p", "c_lf_p",
                                  "a_k_s", "a_v_s", "b_k_s", "b_v_s", "b_ik_s", "c_k_s", "c_v_s", "c_lf_s")}
    h_p = x_prompt.reshape(B * S, D)
    h_s = x_sample.reshape(DB * n, D)
    for i in range(depth):
        g = row(g_mix[i])
        if i % 2 == 0:
            e = i // 2
            wsz = 512
            w = w_in_even[e]
            w_all = jnp.concatenate([w[:, :7 * wsz], jnp.pad(w[:, 7 * wsz:], ((0, 0), (0, LANES - 72)))],
                                    axis=1).astype(CDT)
            gn = jnp.stack([jnp.tile(v[e].astype(F32), wsz // HEAD_DIM) for v in (qn_a, kn_a, qn_b, kn_b)])
            (ka, va, kb, vb, kw, qa_c, ka_c, va_c, qb_c, kb_c, vb_c, qi_c, ki2_c) = _proj_even(
                h_p, g, w_all, gn, bd, cos_p, sin_p)
            r3 = lambda t: t.reshape(B, S, t.shape[-1])
            tqa = 256 if S % 256 == 0 else 128
            npc = (A_PAST + tqa) // tqa
            k_specs = [pl.BlockSpec((1, tqa, wsz), functools.partial(
                lambda bb, ii, pp: (bb, jnp.maximum(ii + pp - A_PAST // tqa, 0), 0), pp=pp)) for pp in range(npc)]
            bias_p = _band_bias(rel_bias_a[e], np.arange(tqa), np.arange(-A_PAST, tqa))
            oa = _band(r3(qa_c), [r3(ka_c)] * npc, [r3(va_c)] * npc, k_specs, bias_p, tqa, True)
            topk_p = min(TOPK_MAX, S // 4)
            wt = tr(r3(kw)[..., 64:64 + IDX_HEADS])
            keys, thr, cut = _dsa_index(r3(qi_c), r3(ki2_c), wt, tq=tqi, tk=tk, qoff=0, length=S, topk=topk_p,
                                        nreal=S)
            ob = _dsa_attn(r3(qb_c), r3(kb_c), tr(r3(vb_c)), keys, thr, cut, tq=tq, tk=tk, qoff=0, length=S)
            os_p = [oa.reshape(B * S, wsz), ob.reshape(B * S, wsz)]
            hd = lambda t: t.reshape(B, S, -1, HEAD_DIM)
            outs["a_k_p"].append(hd(ka)[:, S - W_AP:]); outs["a_v_p"].append(hd(va)[:, S - W_AP:])
            outs["b_k_p"].append(hd(kb)); outs["b_v_p"].append(hd(vb))
            outs["b_ik_p"].append(kw[:, :64].reshape(B, S, 64))
            (ka, va, kb, vb, kw, qa_c, ka_c, va_c, qb_c, kb_c, vb_c, qi_c, ki2_c) = _proj_even(
                h_s, g, w_all, gn, bd, cos_s, sin_s)
            r3 = lambda t: t.reshape(DB, n, t.shape[-1])
            ca_k = cache_a_k[e].reshape(DB, W_A, wsz)
            ca_v = cache_a_v[e].reshape(DB, W_A, wsz)
            full = lambda rows, wd: pl.BlockSpec((1, rows, wd), lambda bb, ii: (bb, 0, 0))
            bias_s = _band_bias(rel_bias_a[e], P + np.arange(n), P - W_A + np.arange(W_A + n))
            bias_s = jnp.pad(bias_s, ((0, 0), (0, 0), (0, -(W_A + n) % LANES)), constant_values=NEG)
            oa = _band(r3(qa_c), [ca_k, r3(ka)], [ca_v, r3(va)], [full(W_A, wsz), full(n, wsz)], bias_s, n, False)
            ik = cache_b_ik[e].astype(CDT)
            ki2_all = catp(jnp.concatenate([ik, ik], axis=-1), r3(ki2_c))
            kb_all = catp(cache_b_k[e].reshape(DB, P, wsz), r3(kb_c))
            vb_all = catp(cache_b_v[e].reshape(DB, P, wsz), r3(vb_c))
            topk_s = min(TOPK_MAX, L // 4)
            wt = _pad_rows(r3(kw)[..., 64:64 + IDX_HEADS], nqp)
            keys, thr, cut = _dsa_index(_pad_rows(r3(qi_c), nqp), ki2_all, tr(wt),
                                        tq=nqp, tk=Lp, qoff=P, length=L, topk=topk_s, nreal=n)
            ob = _dsa_attn(_pad_rows(r3(qb_c), nqp), kb_all, tr(vb_all), keys, thr, cut,
                           tq=nqp, tk=Lp, qoff=P, length=L)[:, :n]
            os_s = [oa.reshape(DB * n, wsz), ob.reshape(DB * n, wsz)]
            hd = lambda t: t.reshape(DB, n, -1, HEAD_DIM)
            outs["a_k_s"].append(jnp.concatenate([cache_a_k[e], hd(ka)], axis=1)[:, n:])
            outs["a_v_s"].append(jnp.concatenate([cache_a_v[e], hd(va)], axis=1)[:, n:])
            outs["b_k_s"].append(hd(kb)); outs["b_v_s"].append(hd(vb))
            outs["b_ik_s"].append(kw[:, :64].reshape(DB, n, 64))
            w_out = w_out_even[e].astype(CDT)
            wos = [w_out[:wsz], w_out[wsz:]]
        else:
            o = i // 2
            nh = b_f.shape[1]
            wsz = nh * HEAD_DIM
            w = w_in_odd[o]
            w_all = jnp.concatenate([w[:, :3 * wsz], jnp.pad(w[:, 3 * wsz:], ((0, 0), (0, LANES - nh)))],
                                    axis=1).astype(CDT)
            gn = jnp.stack([jnp.tile(v[o].astype(F32), wsz // HEAD_DIM) for v in (qn_c, kn_c)])
            bf = jnp.pad(b_f[o].astype(F32), (0, LANES - nh)).reshape(1, LANES)
            k, v, lf, q_c, k_c, v_c = _proj_odd(h_p, g, w_all, gn, bf, bd)
            r3 = lambda t: t.reshape(B, S, t.shape[-1])
            kx = _fox_key_lanes(r3(lf), 512 if S % 512 == 0 else S, LOG2E, nh)
            oc = _fox(r3(q_c), r3(k_c), kx, tr(r3(v_c)), _fox_qaug(kx[:, ::tq]), tq=tq, tk=tk, qoff=0, length=S)
            os_p = [oc.reshape(B * S, wsz)]
            hd = lambda t: t.reshape(B, S, nh, HEAD_DIM)
            outs["c_k_p"].append(hd(k)); outs["c_v_p"].append(hd(v)); outs["c_lf_p"].append(lf[:, :nh].reshape(B, S, nh))
            k, v, lf, q_c, k_c, v_c = _proj_odd(h_s, g, w_all, gn, bf, bd)
            r3 = lambda t: t.reshape(DB, n, t.shape[-1])
            lf_all = jnp.concatenate([jnp.pad(cache_c_logf[o].astype(F32), ((0, 0), (0, 0), (0, LANES - nh))), r3(lf)],
                                     axis=1)
            kx = _fox_key_lanes(lf_all, L, LOG2E, nh)
            k_all = catp(cache_c_k[o].reshape(DB, P, wsz), r3(k_c))
            v_all = catp(cache_c_v[o].reshape(DB, P, wsz), r3(v_c))
            oc = _fox(_pad_rows(r3(q_c), nqp), k_all, _pad_rows(kx, Lp), tr(v_all),
                      _fox_qaug(kx[:, P:P + 1]), tq=nqp, tk=Lp, qoff=P, length=L)[:, :n]
            os_s = [oc.reshape(DB * n, wsz)]
            hd = lambda t: t.reshape(DB, n, nh, HEAD_DIM)
            outs["c_k_s"].append(hd(k)); outs["c_v_s"].append(hd(v)); outs["c_lf_s"].append(lf[:, :nh].reshape(DB, n, nh))
            w_out = w_out_odd[o].astype(CDT)
            wos = [w_out]
        post_w = (row(g_mlp[i]), w_up[i].astype(CDT), w_down[i].astype(CDT), row(g_ple[i]),
                  w_ple_gate[i].astype(CDT), w_ple_proj[i].astype(CDT))
        h_p = _post(h_p, os_p, p_prompt[i].reshape(B * S, -1), wos, *post_w)
        h_s = _post(h_s, os_s, p_sample[i].reshape(DB * n, -1), wos, *post_w)

    st = jnp.stack
    names = ("a_k_p", "a_v_p", "b_k_p", "b_v_p", "b_ik_p", "c_k_p", "c_v_p", "c_lf_p",
             "a_k_s", "a_v_s", "b_k_s", "b_v_s", "b_ik_s", "c_k_s", "c_v_s", "c_lf_s")
    return (h_p.reshape(B, S, D), h_s.reshape(DB, n, D)) + tuple(st(outs[nm]) for nm in names)
```

```python
import functools

import numpy as np
import jax
import jax.numpy as jnp
from jax import lax
from jax.experimental import pallas as pl
from jax.experimental.pallas import tpu as pltpu

F32 = jnp.float32
I32 = jnp.int32
CDT = jnp.bfloat16

CHUNK = 64
HEAD_DIM = 64
IDX_HEADS = 8
A_PAST = 8 * CHUNK
REL_CLIP = 128
TOPK_MAX = 256
ROPE_THETA = 10000.0
RMS_EPS = 1e-6
LANES = 128
MXU_TILE = 256
PACKED_ROWS = 16
CHUNK_SHIFT = CHUNK.bit_length() - 1
NEG = -1e30
INT_MIN = -(2 ** 31)
INT_MAX = 2 ** 31 - 1
LOG2E = 1.4426950408889634
VMEM_LIMIT = 56 * 1024 * 1024


def _mm(a, b):
    return jnp.dot(a, b, preferred_element_type=F32)


def _mm_nt(a, b):
    return lax.dot_general(a, b, (((1,), (1,)), ((), ())), preferred_element_type=F32)


def _pieces(x, n):
    if CDT == F32:
        return [x]
    out = []
    for _ in range(n - 1):
        p = x.astype(CDT)
        out.append(p)
        x = x - p.astype(F32)
    out.append(x.astype(CDT))
    return out


def _rms(x, g):
    return x * lax.rsqrt(jnp.mean(x * x, axis=-1, keepdims=True) + RMS_EPS) * g


def _head_rms(x, bd, g):
    x2 = x * x
    pcs = _pieces(x2, 2)
    cols = []
    for s in range(x.shape[1] // MXU_TILE):
        sl = slice(s * MXU_TILE, (s + 1) * MXU_TILE)
        ms = _mm(pcs[0][:, sl], bd)
        for p in pcs[1:]:
            ms = ms + _mm(p[:, sl], bd)
        cols.append(ms)
    ms = cols[0] if len(cols) == 1 else jnp.concatenate(cols, axis=1)
    return x * lax.rsqrt(ms + RMS_EPS) * g


def _rope(x, cos, sin):
    w = x.shape[1]
    lane = lax.broadcasted_iota(I32, x.shape, 1)
    first = (lane & 63) < 32
    swapped = jnp.where(first, pltpu.roll(x, w - 32, 1), pltpu.roll(x, 32, 1))
    return x * cos + swapped * sin


def _tile_lanes(t, w):
    return t if w == t.shape[1] else jnp.concatenate([t] * (w // t.shape[1]), axis=1)


def _proj_even_kernel(h_ref, g_ref, w_ref, gn_ref, bd_ref, cos_ref, sin_ref,
                      ka_ref, va_ref, kb_ref, vb_ref, kw_ref,
                      qa_c, ka_c, va_c, qb_c, kb_c, vb_c, qi_c, ki2_c):
    a = _rms(h_ref[...], g_ref[...]).astype(CDT)
    z = _mm(a, w_ref[...])
    w = 512
    bd = bd_ref[...]
    gn = gn_ref[...]
    cos1, sin1 = cos_ref[...], sin_ref[...]
    cos, sin = _tile_lanes(cos1, w), _tile_lanes(sin1, w)
    scale = HEAD_DIM ** -0.5

    qa = _head_rms(z[:, 0:w], bd, gn[0:1])
    qa_c[...] = (qa * scale).astype(CDT)
    ka = _head_rms(z[:, w:2 * w], bd, gn[1:2])
    ka_ref[...] = ka
    ka_c[...] = ka.astype(CDT)
    va = z[:, 2 * w:3 * w]
    va_ref[...] = va
    va_c[...] = va.astype(CDT)
    qb = _rope(_head_rms(z[:, 3 * w:4 * w], bd, gn[2:3]), cos, sin)
    qb_c[...] = (qb * (scale * LOG2E)).astype(CDT)
    kb = _rope(_head_rms(z[:, 4 * w:5 * w], bd, gn[3:4]), cos, sin)
    kb_ref[...] = kb
    kb_c[...] = kb.astype(CDT)
    vb = z[:, 5 * w:6 * w]
    vb_ref[...] = vb
    vb_c[...] = vb.astype(CDT)
    qi = _rope(z[:, 6 * w:7 * w], cos, sin)
    qi_c[...] = (qi * scale).astype(CDT)
    kw = z[:, 7 * w:7 * w + LANES]
    kwr = _rope(kw, cos1, sin1)
    lane = lax.broadcasted_iota(I32, kw.shape, 1)
    kw_ref[...] = jnp.where(lane < 64, kwr, kw)
    ki2_c[...] = jnp.where(lane < 64, kwr, pltpu.roll(kwr, 64, 1)).astype(CDT)


def _proj_odd_kernel(h_ref, g_ref, w_ref, gn_ref, bf_ref, bd_ref,
                     k_ref, v_ref, lf_ref, q_c, k_c, v_c):
    a = _rms(h_ref[...], g_ref[...]).astype(CDT)
    z = _mm(a, w_ref[...])
    w = 1024
    bd = bd_ref[...]
    gn = gn_ref[...]
    q = _head_rms(z[:, 0:w], bd, gn[0:1])
    q_c[...] = (q * (HEAD_DIM ** -0.5 * LOG2E)).astype(CDT)
    k = _head_rms(z[:, w:2 * w], bd, gn[1:2])
    k_ref[...] = k
    k_c[...] = k.astype(CDT)
    v = z[:, 2 * w:3 * w]
    v_ref[...] = v
    v_c[...] = v.astype(CDT)
    fl = z[:, 3 * w:3 * w + LANES] + bf_ref[...]
    lf_ref[...] = jnp.minimum(fl, 0.0) - jnp.log1p(jnp.exp(-jnp.abs(fl)))


def _const_spec(shape):
    nd = len(shape)
    return pl.BlockSpec(shape, lambda *_: (0,) * nd, pipeline_mode=pl.Buffered(1))


def _row_spec(tm, w):
    return pl.BlockSpec((tm, w), lambda i: (i, 0))


def _params(sem):
    return pltpu.CompilerParams(dimension_semantics=sem, vmem_limit_bytes=VMEM_LIMIT)


def _row_tile(rows, pref):
    return pref if rows % pref == 0 else rows


def _proj_even(h, g, w_all, gn, bd, cos, sin):
    rows, d = h.shape
    tm = _row_tile(rows, 256)
    f = lambda w, dt: jax.ShapeDtypeStruct((rows, w), dt)
    out_shape = [f(512, F32)] * 4 + [f(LANES, F32)] + [f(512, CDT)] * 7 + [f(LANES, CDT)]
    out_specs = [_row_spec(tm, 512)] * 4 + [_row_spec(tm, LANES)] + [_row_spec(tm, 512)] * 7 + [_row_spec(tm, LANES)]
    return pl.pallas_call(
        _proj_even_kernel,
        grid=(rows // tm,),
        in_specs=[_row_spec(tm, d), _const_spec(g.shape), _const_spec(w_all.shape), _const_spec(gn.shape),
                  _const_spec(bd.shape), _row_spec(tm, LANES), _row_spec(tm, LANES)],
        out_specs=out_specs,
        out_shape=out_shape,
        compiler_params=_params(("parallel",)),
        name="proj_even",
    )(h, g, w_all, gn, bd, cos, sin)


def _proj_odd(h, g, w_all, gn, bf, bd):
    rows, d = h.shape
    tm = _row_tile(rows, 256)
    f = lambda w, dt: jax.ShapeDtypeStruct((rows, w), dt)
    out_shape = [f(1024, F32)] * 2 + [f(LANES, F32)] + [f(1024, CDT)] * 3
    out_specs = [_row_spec(tm, 1024)] * 2 + [_row_spec(tm, LANES)] + [_row_spec(tm, 1024)] * 3
    return pl.pallas_call(
        _proj_odd_kernel,
        grid=(rows // tm,),
        in_specs=[_row_spec(tm, d), _const_spec(g.shape), _const_spec(w_all.shape), _const_spec(gn.shape),
                  _const_spec(bf.shape), _const_spec(bd.shape)],
        out_specs=out_specs,
        out_shape=out_shape,
        compiler_params=_params(("parallel",)),
        name="proj_odd",
    )(h, g, w_all, gn, bf, bd)


def _post_kernel(*refs, n_o, ff_chunk):
    h_ref = refs[0]
    o_refs = refs[1:1 + n_o]
    p_ref = refs[1 + n_o]
    wo_refs = refs[2 + n_o:2 + 2 * n_o]
    gm_ref, wu_ref, wd_ref, gp_ref, wg_ref, wp_ref, out_ref = refs[2 + 2 * n_o:]
    mix = None
    for o_ref, wo_ref in zip(o_refs, wo_refs):
        t = _mm(o_ref[...], wo_ref[...])
        mix = t if mix is None else mix + t
    h = h_ref[...] + mix
    m = _rms(h, gm_ref[...]).astype(CDT)
    mlp = None
    d_ff = wu_ref.shape[1]
    for c in range(d_ff // ff_chunk):
        sl = slice(c * ff_chunk, (c + 1) * ff_chunk)
        u = jnp.square(jnp.maximum(_mm(m, wu_ref[:, sl]), 0.0)).astype(CDT)
        t = _mm(u, wd_ref[sl, :])
        mlp = t if mlp is None else mlp + t
    acc = h + mlp
    gate_in = _rms(acc, gp_ref[...]).astype(CDT)
    gate = 1.0 / (1.0 + jnp.exp(-_mm(gate_in, wg_ref[...])))
    out_ref[...] = acc + gate * _mm(p_ref[...].astype(CDT), wp_ref[...])


def _post(h, os_, p, wos, gm, wu, wd, gp, wg, wp):
    rows, d = h.shape
    tm = _row_tile(rows, 512)
    n_o = len(os_)
    in_specs = ([_row_spec(tm, d)] + [_row_spec(tm, o.shape[1]) for o in os_] + [_row_spec(tm, p.shape[1])]
                + [_const_spec(w.shape) for w in wos]
                + [_const_spec(x.shape) for x in (gm, wu, wd, gp, wg, wp)])
    return pl.pallas_call(
        functools.partial(_post_kernel, n_o=n_o, ff_chunk=1024),
        grid=(rows // tm,),
        in_specs=in_specs,
        out_specs=_row_spec(tm, d),
        out_shape=jax.ShapeDtypeStruct((rows, d), F32),
        compiler_params=_params(("parallel",)),
        name="post",
    )(h, *os_, p, *wos, gm, wu, wd, gp, wg, wp)


def _pair_stack(qp, tq):
    lane = lax.broadcasted_iota(I32, qp.shape, 1)
    zero = jnp.zeros_like(qp)
    return jnp.concatenate([jnp.where(lane < 64, qp, zero), jnp.where(lane >= 64, qp, zero)], axis=0)


def _pair_merge(o2, tq):
    lane = lax.broadcasted_iota(I32, (tq, LANES), 1)
    return jnp.where(lane < 64, o2[:tq], o2[tq:])


def _last_tile(i, tq, tk, qoff, length, chunked):
    qend = qoff + i * tq + tq - 1
    kmax = (qend // CHUNK + 1) * CHUNK if chunked else qend + 1
    kmax = jnp.minimum(kmax, length)
    return (kmax - 1) // tk


def _tri_steps(nq, tq, tk, qoff, length, chunked):
    ii, jj = [], []
    for i in range(nq):
        qend = qoff + i * tq + tq - 1
        kmax = min((qend // CHUNK + 1) * CHUNK if chunked else qend + 1, length)
        for j in range((kmax - 1) // tk + 1):
            ii.append(i)
            jj.append(j)
    return jnp.asarray(ii, I32), jnp.asarray(jj, I32)


def _tri_call(kernel_fn, steps, batch, in_specs, out_specs, out_shape, scratch, name):
    ii, jj = steps
    grid_spec = pltpu.PrefetchScalarGridSpec(
        num_scalar_prefetch=2, grid=(batch, ii.shape[0]),
        in_specs=in_specs, out_specs=out_specs, scratch_shapes=scratch)
    call = pl.pallas_call(kernel_fn, grid_spec=grid_spec, out_shape=out_shape,
                          compiler_params=_params(("parallel", "arbitrary")), name=name)
    return functools.partial(call, ii, jj)


def _flash_init(q_ref, aug_ref, qst_ref, m_ref, l_ref, acc_ref, tq):
    q = q_ref[0]
    for g in range(qst_ref.shape[0]):
        st = _pair_stack(q[:, g * LANES:(g + 1) * LANES], tq)
        if aug_ref is not None:
            a = aug_ref[0, 0]
            ext = jnp.concatenate([jnp.broadcast_to(a[2 * g:2 * g + 1], (tq, LANES)),
                                   jnp.broadcast_to(a[2 * g + 1:2 * g + 2], (tq, LANES))], axis=0)
            st = jnp.concatenate([st, ext.astype(CDT)], axis=1)
        qst_ref[g] = st
    m_ref[...] = jnp.full(m_ref.shape, NEG, F32)
    l_ref[...] = jnp.zeros(l_ref.shape, F32)
    acc_ref[...] = jnp.zeros(acc_ref.shape, F32)


def _flash_pairs(score_fn, vt_ref, scr, tq):
    _, m_ref, l_ref, acc_ref, s_ref = scr
    npair = m_ref.shape[0]
    s_ref[0] = score_fn(0)
    for g in range(npair):
        if g + 1 < npair:
            s_ref[(g + 1) % 2] = score_fn(g + 1)
        _flash_step(s_ref[g % 2], vt_ref, m_ref, l_ref, acc_ref, g, tq)


def _flash_step(st, vt_ref, m_ref, l_ref, acc_ref, g, tq):
    m_old = m_ref[g]
    m_new = jnp.maximum(m_old, jnp.max(st, axis=0, keepdims=True))
    alpha = jnp.exp2(m_old - m_new)
    m_ref[g] = m_new
    pc = jnp.exp2(st - m_new).astype(CDT)
    ones = jnp.ones((PACKED_ROWS, st.shape[0]), CDT)
    sums = []
    for e in range(2):
        rows = slice(g * LANES + e * HEAD_DIM, g * LANES + (e + 1) * HEAD_DIM)
        cols = slice(e * tq, (e + 1) * tq)
        pv = _mm(jnp.concatenate([vt_ref[0, rows, :], ones], axis=0), pc[:, cols])
        acc_ref[rows, :] = alpha[:, cols] * acc_ref[rows, :] + pv[:HEAD_DIM]
        sums.append(pv[HEAD_DIM:HEAD_DIM + 1])
    l_ref[g] = alpha * l_ref[g] + jnp.concatenate(sums, axis=1)


def _flash_finish(o_ref, l_ref, acc_ref, tq):
    for g in range(l_ref.shape[0]):
        l = l_ref[g]
        den = jnp.concatenate([jnp.broadcast_to(l[:, :tq], (HEAD_DIM, tq)),
                               jnp.broadcast_to(l[:, tq:], (HEAD_DIM, tq))], axis=0)
        o = acc_ref[g * LANES:(g + 1) * LANES, :] / den
        o_ref[0, :, g * LANES:(g + 1) * LANES] = o.T.astype(o_ref.dtype)


def _flash_scratch(npair, tq, tk, kd, w):
    return [pltpu.VMEM((npair, 2 * tq, kd), CDT),
            pltpu.VMEM((npair, 1, 2 * tq), F32),
            pltpu.VMEM((npair, 1, 2 * tq), F32),
            pltpu.VMEM((w, tq), F32),
            pltpu.VMEM((2, tk, 2 * tq), F32)]


def _band_kernel(*refs, n_kv, tq, mask_neg):
    q_ref = refs[0]
    k_refs = refs[1:1 + n_kv]
    v_refs = refs[1 + n_kv:1 + 2 * n_kv]
    bias_ref = refs[1 + 2 * n_kv]
    o_ref = refs[2 + 2 * n_kv]
    i = pl.program_id(1)
    q = q_ref[0]
    cat = lambda rs: (rs[0][0].astype(CDT) if len(rs) == 1
                      else jnp.concatenate([r[0].astype(CDT) for r in rs], axis=0))
    k = cat(k_refs)
    v = cat(v_refs)
    tk = bias_ref.shape[2]
    if k.shape[0] < tk:
        zpad = jnp.zeros((tk - k.shape[0], k.shape[1]), CDT)
        k = jnp.concatenate([k, zpad], axis=0)
        v = jnp.concatenate([v, zpad], axis=0)
    if mask_neg:
        kpos = lax.broadcasted_iota(I32, (2 * tq, tk), 1) + (i * tq - A_PAST)
        neg = kpos < 0
    for g in range(q.shape[1] // LANES):
        sl = slice(g * LANES, (g + 1) * LANES)
        s = _mm_nt(_pair_stack(q[:, sl], tq), k[:, sl])
        s = s + jnp.concatenate([bias_ref[2 * g], bias_ref[2 * g + 1]], axis=0)
        if mask_neg:
            s = jnp.where(neg, NEG, s)
        m = jnp.max(s, axis=-1, keepdims=True)
        p = jnp.exp(s - m)
        l = jnp.sum(p, axis=-1, keepdims=True)
        o2 = _mm(p.astype(CDT), v[:, sl]) / l
        o_ref[0, :, sl] = _pair_merge(o2, tq).astype(o_ref.dtype)


def _band(q, k_parts, v_parts, k_specs, bias, tq, mask_neg):
    b, s, w = q.shape
    n_kv = len(k_parts)
    q_spec = pl.BlockSpec((1, tq, w), lambda bb, i: (bb, i, 0))
    return pl.pallas_call(
        functools.partial(_band_kernel, n_kv=n_kv, tq=tq, mask_neg=mask_neg),
        grid=(b, s // tq),
        in_specs=[q_spec] + k_specs + k_specs + [_const_spec(bias.shape)],
        out_specs=q_spec,
        out_shape=jax.ShapeDtypeStruct((b, s, w), CDT),
        compiler_params=_params(("parallel", "parallel")),
        name="band_attn",
    )(q, *k_parts, *v_parts, bias)


def _band_bias(rel_bias, q_pos, k_pos):
    nq, nk = len(q_pos), len(k_pos)
    qc = q_pos[:, None] // CHUNK
    kc = k_pos[None, :] // CHUNK
    ok = (kc <= qc) & (kc >= qc - A_PAST // CHUNK)
    n = nq + nk - 1
    d0 = int(q_pos[0] - k_pos[0])
    m = np.arange(n + 1)
    rel = np.where(m < nk, d0 - m, d0 - m + n + 1)
    table = jnp.take(rel_bias.astype(F32), jnp.asarray(np.clip(rel, -REL_CLIP, REL_CLIP) + REL_CLIP), axis=1)
    bias = jnp.tile(table, (1, nq))[:, :nq * n].reshape(-1, nq, n)[:, :, :nk]
    return jnp.where(jnp.asarray(ok)[None], bias, NEG)


def _dsa_index_kernel(ii_ref, jj_ref, qi_ref, ki2_ref, wt_ref, keys_ref, t_ref, c_ref, qst_ref, kscr_ref, k16_ref,
                      *, tq, tk, nk, unroll, qoff, length, topk, idx_bits, nreal):
    i = ii_ref[pl.program_id(1)]
    j = jj_ref[pl.program_id(1)]
    last = _last_tile(i, tq, tk, qoff, length, True)
    q0 = qoff + i * tq

    @pl.when(j == 0)
    def _():
        q = qi_ref[0]
        for g in range(IDX_HEADS // 2):
            qst_ref[2 * g * tq:(2 * g + 2) * tq, :] = _pair_stack(q[:, g * LANES:(g + 1) * LANES], tq)

    @pl.when(j <= last)
    def _():
        lg = _mm_nt(ki2_ref[0], qst_ref[...])
        wgt = wt_ref[0] * (IDX_HEADS ** -0.5)
        sc = jnp.zeros((tk, tq), F32)
        for h in range(IDX_HEADS):
            sc = sc + jnp.maximum(lg[:, h * tq:(h + 1) * tq], 0.0) * wgt[h:h + 1, :]
        sc = jnp.where(sc == 0.0, 0.0, sc)
        bits = lax.bitcast_convert_type(sc, I32)
        key = bits ^ ((bits >> 31) & 0x7FFFFFFF)
        kpos = j * tk + lax.broadcasted_iota(I32, (tk, tq), 0)
        qpos = q0 + lax.broadcasted_iota(I32, (tk, tq), 1)
        ok = ((kpos >> CHUNK_SHIFT) <= (qpos >> CHUNK_SHIFT)) & (kpos < length)
        key = jnp.where(ok, key, INT_MIN)
        keys_ref[0] = key
        kscr_ref[j] = key
        top = lax.bitcast_convert_type(bits & -65536, F32)
        k16_ref[j] = jnp.where(ok, top, -jnp.inf).astype(jnp.bfloat16)
        if unroll > 1:
            @pl.when(j + 1 < nk)
            def _():
                kscr_ref[j + 1] = jnp.full((tk, tq), INT_MIN, I32)
                k16_ref[j + 1] = jnp.full((tk, tq), -jnp.inf, jnp.bfloat16)

    @pl.when(j == last)
    def _():
        def count(pred):
            def body(t, acc):
                for u in range(unroll):
                    tt = unroll * t + u
                    kpos = tt * tk + lax.broadcasted_iota(I32, (tk, tq), 0)
                    m = jnp.where(pred(kscr_ref[tt], kpos), 1.0, 0.0)
                    acc = acc + jnp.sum(m, axis=0, keepdims=True)
                return acc
            return lax.fori_loop(0, (last + unroll) // unroll, body, jnp.zeros((1, tq), F32))

        kf = float(topk)
        qpos = q0 + lax.broadcasted_iota(I32, (1, tq), 1)
        n_adm = jnp.minimum(((qpos >> CHUNK_SHIFT) + 1) << CHUNK_SHIFT, length).astype(F32)
        real = qpos - qoff < nreal
        take_all = (n_adm < kf) | jnp.logical_not(real)

        def count16(c16):
            one, zero = jnp.ones((), jnp.bfloat16), jnp.zeros((), jnp.bfloat16)
            nslab = tk // PACKED_ROWS
            assert nslab <= 256

            def body(t, acc):
                for u in range(unroll):
                    m = jnp.where(k16_ref[unroll * t + u] >= c16, one, zero).reshape(nslab, PACKED_ROWS, tq)
                    part = m[0]
                    for r in range(1, nslab):
                        part = part + m[r]
                    acc = acc + jnp.sum(part.astype(F32), axis=0, keepdims=True)
                return acc
            return lax.fori_loop(0, (last + unroll) // unroll, body, jnp.zeros((1, tq), F32))

        def bit_step(it, prefix, nge, top_half):
            cand = prefix + lax.shift_left(jnp.int32(1), 31 - it)
            if top_half:
                cbits = cand ^ ((cand >> 31) & 0x7FFFFFFF)
                cbits = jnp.where((cand > 0) & (cand < 0x00800000), 0x00800000, cbits)
                cnt = count16(lax.bitcast_convert_type(cbits & -65536, F32).astype(jnp.bfloat16))
            else:
                cnt = count(lambda kt, _: kt >= cand)
            take = cnt >= kf
            return jnp.where(take, cand, prefix), jnp.where(take, cnt, nge)

        def all_settled(nge):
            return (jnp.min(jnp.where(take_all | (nge == kf), 1.0, 0.0)) > 0.0).astype(I32)

        def low_body(carry):
            it, prefix, nge, _ = carry
            prefix, nge = bit_step(it, prefix, nge, False)
            return it + 1, prefix, nge, all_settled(nge)

        prefix, nge = lax.fori_loop(0, 16, lambda it, c: bit_step(it, c[0], c[1], True),
                                    (jnp.full((1, tq), INT_MIN, I32), jnp.zeros((1, tq), F32)))
        _, prefix, nge, _ = lax.while_loop(lambda c: (c[0] < 32) & (c[3] == 0), low_body,
                                           (jnp.int32(16), prefix, nge, all_settled(nge)))
        found = prefix > INT_MIN
        thr = jnp.maximum(prefix, INT_MIN + 1)
        t_ref[0] = thr
        c_ref[0] = jnp.full((1, tq), 2 ** 30, I32)
        straddle = jnp.max(jnp.where(found & (nge > kf) & real, 1.0, 0.0))

        @pl.when(straddle > 0.0)
        def _():
            need = kf - count(lambda kt, _: kt > thr)

            def tie_body(it, x):
                cx = x + lax.shift_left(jnp.int32(1), idx_bits - 1 - it)
                g = count(lambda kt, kpos: jnp.where(kt == thr, kpos, INT_MAX) < cx)
                return jnp.where(g < need, cx, x)
            c_ref[0] = lax.fori_loop(0, idx_bits, tie_body, jnp.zeros((1, tq), I32))


def _dsa_index(qi, ki2, wt, *, tq, tk, qoff, length, topk, nreal):
    b, s, _ = qi.shape
    lp = ki2.shape[1]
    nq, nk = s // tq, lp // tk
    idx_bits = max(1, int(np.ceil(np.log2(lp))))
    qvec = pl.BlockSpec((1, 1, tq), lambda bb, t, ii, jj: (bb, 0, ii[t]))
    return _tri_call(
        functools.partial(_dsa_index_kernel, tq=tq, tk=tk, nk=nk, unroll=2 if nk % 2 == 0 else 1,
                          qoff=qoff, length=length, topk=topk, idx_bits=idx_bits, nreal=nreal),
        _tri_steps(nq, tq, tk, qoff, length, True), b,
        in_specs=[pl.BlockSpec((1, tq, 512), lambda bb, t, ii, jj: (bb, ii[t], 0)),
                  pl.BlockSpec((1, tk, LANES), lambda bb, t, ii, jj: (bb, jj[t], 0)),
                  pl.BlockSpec((1, IDX_HEADS, tq), lambda bb, t, ii, jj: (bb, 0, ii[t]))],
        out_specs=[pl.BlockSpec((1, tk, tq), lambda bb, t, ii, jj: (bb, jj[t], ii[t])), qvec, qvec],
        out_shape=[jax.ShapeDtypeStruct((b, lp, s), I32),
                   jax.ShapeDtypeStruct((b, 1, s), I32),
                   jax.ShapeDtypeStruct((b, 1, s), I32)],
        scratch=[pltpu.VMEM((IDX_HEADS * tq, LANES), CDT),
                 pltpu.VMEM((nk, tk, tq), I32),
                 pltpu.VMEM((nk, tk, tq), jnp.bfloat16)],
        name="dsa_index",
    )(qi, ki2, wt)


def _dsa_attn_kernel(ii_ref, jj_ref, q_ref, k_ref, vt_ref, keys_ref, t_ref, c_ref, o_ref, *scr,
                     tq, tk, qoff, length):
    qst_ref, m_ref, l_ref, acc_ref = scr[:4]
    i = ii_ref[pl.program_id(1)]
    j = jj_ref[pl.program_id(1)]
    last = _last_tile(i, tq, tk, qoff, length, True)

    @pl.when(j == 0)
    def _():
        _flash_init(q_ref, None, qst_ref, m_ref, l_ref, acc_ref, tq)

    @pl.when(j <= last)
    def _():
        kt = keys_ref[0]
        thr = t_ref[0]
        kpos = j * tk + lax.broadcasted_iota(I32, (tk, tq), 0)
        sel = (kt > thr) | (jnp.where(kt == thr, kpos, INT_MAX) <= c_ref[0])
        mb = jnp.where(sel, 0.0, NEG)
        mb2 = jnp.concatenate([mb, mb], axis=1)
        score = lambda g: _mm_nt(k_ref[0, :, g * LANES:(g + 1) * LANES], qst_ref[g]) + mb2
        _flash_pairs(score, vt_ref, scr, tq)

    @pl.when(j == last)
    def _():
        _flash_finish(o_ref, l_ref, acc_ref, tq)


def _dsa_attn(q, k, vt, keys, thr, cut, *, tq, tk, qoff, length):
    b, s, w = q.shape
    qmap = lambda bb, t, ii, jj: (bb, ii[t], 0)
    qvec = pl.BlockSpec((1, 1, tq), lambda bb, t, ii, jj: (bb, 0, ii[t]))
    npair = w // LANES
    return _tri_call(
        functools.partial(_dsa_attn_kernel, tq=tq, tk=tk, qoff=qoff, length=length),
        _tri_steps(s // tq, tq, tk, qoff, length, True), b,
        in_specs=[pl.BlockSpec((1, tq, w), qmap),
                  pl.BlockSpec((1, tk, w), lambda bb, t, ii, jj: (bb, jj[t], 0)),
                  pl.BlockSpec((1, w, tk), lambda bb, t, ii, jj: (bb, 0, jj[t])),
                  pl.BlockSpec((1, tk, tq), lambda bb, t, ii, jj: (bb, jj[t], ii[t])),
                  qvec, qvec],
        out_specs=pl.BlockSpec((1, tq, w), qmap),
        out_shape=jax.ShapeDtypeStruct((b, s, w), CDT),
        scratch=_flash_scratch(npair, tq, tk, LANES, w),
        name="dsa_attn",
    )(q, k, vt, keys, thr, cut)


def _pieces3(x):
    pcs = _pieces(x, 3)
    return pcs + [jnp.zeros_like(pcs[0])] * (3 - len(pcs))


FOX_EXT = 9


def _cumsum_kernel(x_ref, tri_ref, place_ref, ones_ref, kx_ref, carry_ref, *, scale):
    @pl.when(pl.program_id(1) == 0)
    def _():
        carry_ref[...] = jnp.zeros(carry_ref.shape, F32)
    tri = tri_ref[...]
    tot = None
    for p in _pieces(x_ref[0], 3):
        t = _mm(tri, p)
        tot = t if tot is None else tot + t
    acc = carry_ref[...] + tot
    carry_ref[...] = acc[-1:, :]
    ext = None
    for c, p in enumerate(_pieces3(acc * (-scale))):
        t = _mm(p, place_ref[c])
        ext = t if ext is None else ext + t
    kx_ref[0] = (ext + ones_ref[...]).astype(kx_ref.dtype)


def _fox_key_lanes(x, tb, scale, nh):
    b, s, w = x.shape
    npair = nh // 2
    place = np.zeros((3, w, npair * LANES), np.float32)
    ones = np.zeros((1, npair * LANES), np.float32)
    for h in range(nh):
        for c in range(3):
            place[c, h, (h // 2) * LANES + (h % 2) * 6 + c] = 1.0
    for g in range(npair):
        ones[0, g * LANES + 3:g * LANES + 6] = 1.0
    tri = jnp.tril(jnp.ones((tb, tb), F32)).astype(CDT)
    place = jnp.asarray(place, CDT)
    spec = lambda wd: pl.BlockSpec((1, tb, wd), lambda bb, i: (bb, i, 0))
    return pl.pallas_call(
        functools.partial(_cumsum_kernel, scale=scale),
        grid=(b, s // tb),
        in_specs=[spec(w), _const_spec(tri.shape), _const_spec(place.shape), _const_spec(ones.shape)],
        out_specs=spec(npair * LANES),
        out_shape=jax.ShapeDtypeStruct((b, s, npair * LANES), CDT),
        scratch_shapes=[pltpu.VMEM((1, w), F32)],
        compiler_params=_params(("parallel", "arbitrary")),
        name="logf_cumsum",
    )(x, tri, place, jnp.asarray(ones))


def _fox_kernel(ii_ref, jj_ref, q_ref, k_ref, kx_ref, vt_ref, aug_ref, o_ref, *scr, tq, tk, qoff, length):
    qst_ref, m_ref, l_ref, acc_ref = scr[:4]
    i = ii_ref[pl.program_id(1)]
    j = jj_ref[pl.program_id(1)]
    last = _last_tile(i, tq, tk, qoff, length, False)
    q0 = qoff + i * tq

    @pl.when(j == 0)
    def _():
        _flash_init(q_ref, aug_ref, qst_ref, m_ref, l_ref, acc_ref, tq)

    def step(masked):
        if masked:
            kpos = j * tk + lax.broadcasted_iota(I32, (tk, 2 * tq), 0)
            lane = lax.broadcasted_iota(I32, (tk, 2 * tq), 1)
            causal = kpos <= q0 + jnp.where(lane >= tq, lane - tq, lane)

        def score(g):
            sl = slice(g * LANES, (g + 1) * LANES)
            st = _mm_nt(jnp.concatenate([k_ref[0, :, sl], kx_ref[0, :, sl]], axis=1), qst_ref[g])
            return jnp.where(causal, st, NEG) if masked else st
        _flash_pairs(score, vt_ref, scr, tq)

    diag = (j + 1) * tk - 1 > q0

    @pl.when((j <= last) & diag)
    def _():
        step(True)

    @pl.when((j <= last) & jnp.logical_not(diag))
    def _():
        step(False)

    @pl.when(j == last)
    def _():
        _flash_finish(o_ref, l_ref, acc_ref, tq)


def _fox(q, k, kx, vt, qaug, *, tq, tk, qoff, length):
    b, s, w = q.shape
    npair = w // LANES
    qmap = lambda bb, t, ii, jj: (bb, ii[t], 0)
    kmap = lambda bb, t, ii, jj: (bb, jj[t], 0)
    return _tri_call(
        functools.partial(_fox_kernel, tq=tq, tk=tk, qoff=qoff, length=length),
        _tri_steps(s // tq, tq, tk, qoff, length, False), b,
        in_specs=[pl.BlockSpec((1, tq, w), qmap),
                  pl.BlockSpec((1, tk, w), kmap),
                  pl.BlockSpec((1, tk, npair * LANES), kmap),
                  pl.BlockSpec((1, w, tk), lambda bb, t, ii, jj: (bb, 0, jj[t])),
                  pl.BlockSpec((1, 1, 2 * npair, LANES), lambda bb, t, ii, jj: (bb, ii[t], 0, 0))],
        out_specs=pl.BlockSpec((1, tq, w), qmap),
        out_shape=jax.ShapeDtypeStruct((b, s, w), CDT),
        scratch=_flash_scratch(npair, tq, tk, 2 * LANES, w),
        name="fox_attn",
    )(q, k, kx, vt, qaug)


def _fox_qaug(kx0):
    b, nq, _ = kx0.shape
    a = kx0.reshape(b, nq, -1, LANES).astype(F32)
    one, zero = jnp.ones_like(a[..., 0:3]), jnp.zeros_like(a[..., 0:3])
    even = jnp.concatenate([one, -a[..., 0:3], zero], axis=-1)
    odd = jnp.concatenate([zero, -a[..., 6:9], one], axis=-1)
    heads = jnp.stack([even, odd], axis=3).reshape(b, nq, -1, FOX_EXT)
    return jnp.pad(heads, ((0, 0), (0, 0), (0, 0), (0, LANES - FOX_EXT)))


def _rope_tables(pos):
    half = HEAD_DIM // 2
    inv = ROPE_THETA ** (-jnp.arange(half, dtype=F32) / half)
    ang = pos.astype(F32)[:, None] * inv[None, :]
    cos, sin = jnp.cos(ang), jnp.sin(ang)
    return (jnp.tile(jnp.concatenate([cos, cos], axis=1), (1, 2)),
            jnp.tile(jnp.concatenate([-sin, sin], axis=1), (1, 2)))


def _pad_rows(x, n):
    return x if x.shape[1] == n else jnp.pad(x, ((0, 0), (0, n - x.shape[1]), (0, 0)))


def _tiles(s):
    tk = 512 if s % 512 == 0 else s
    tq = 256 if s % 512 == 0 else s
    tqi = 256 if s % 256 == 0 else s
    return tq, tqi, tk


def kernel(x_prompt, x_sample, cache_a_k, cache_a_v, cache_b_k, cache_b_v, cache_b_ik, cache_c_k, cache_c_v,
           cache_c_logf, p_prompt, p_sample, g_mix, w_in_even, qn_a, kn_a, rel_bias_a, qn_b, kn_b, w_out_even,
           w_in_odd, b_f, qn_c, kn_c, w_out_odd, g_mlp, w_up, w_down, g_ple, w_ple_gate, w_ple_proj):
    B, S, D = x_prompt.shape
    DB, n, _ = x_sample.shape
    P = cache_b_k.shape[2]
    W_A = cache_a_k.shape[2]
    W_AP = min(A_PAST, S)
    depth = g_mix.shape[0]
    L = P + n
    Lp = -(-L // LANES) * LANES
    nqp = -(-n // LANES) * LANES
    assert B == 1 and S % 128 == 0 and n % 16 == 0 and P % 16 == 0

    row = lambda v: v.astype(F32).reshape(1, -1)
    bd = jnp.asarray(np.kron(np.eye(MXU_TILE // HEAD_DIM), np.full((HEAD_DIM, HEAD_DIM), 1.0 / HEAD_DIM)), CDT)
    cos_p, sin_p = _rope_tables(jnp.arange(S))
    cos_s, sin_s = (jnp.tile(t, (DB, 1)) for t in _rope_tables(P + jnp.arange(n)))
    tq, tqi, tk = _tiles(S)
    tr = lambda t: jnp.swapaxes(t, 1, 2)
    catp = lambda c, new: _pad_rows(jnp.concatenate([c.astype(CDT), new], axis=1), Lp)

    outs = {name: [] for name in ("a_k_p", "a_v_p", "b_k_p", "b_v_p", "b_ik_p", "c_k_p", "c_v_p", "c_lf_p",
                                  "a_k_s", "a_v_s", "b_k_s", "b_v_s", "b_ik_s", "c_k_s", "c_v_s", "c_lf_s")}
    h_p = x_prompt.reshape(B * S, D)
    h_s = x_sample.reshape(DB * n, D)
    for i in range(depth):
        g = row(g_mix[i])
        if i % 2 == 0:
            e = i // 2
            wsz = 512
            w = w_in_even[e]
            w_all = jnp.concatenate([w[:, :7 * wsz], jnp.pad(w[:, 7 * wsz:], ((0, 0), (0, LANES - 72)))],
                                    axis=1).astype(CDT)
            gn = jnp.stack([jnp.tile(v[e].astype(F32), wsz // HEAD_DIM) for v in (qn_a, kn_a, qn_b, kn_b)])
            (ka, va, kb, vb, kw, qa_c, ka_c, va_c, qb_c, kb_c, vb_c, qi_c, ki2_c) = _proj_even(
                h_p, g, w_all, gn, bd, cos_p, sin_p)
            r3 = lambda t: t.reshape(B, S, t.shape[-1])
            tqa = 256 if S % 256 == 0 else 128
            npc = (A_PAST + tqa) // tqa
            k_specs = [pl.BlockSpec((1, tqa, wsz), functools.partial(
                lambda bb, ii, pp: (bb, jnp.maximum(ii + pp - A_PAST // tqa, 0), 0), pp=pp)) for pp in range(npc)]
            bias_p = _band_bias(rel_bias_a[e], np.arange(tqa), np.arange(-A_PAST, tqa))
            oa = _band(r3(qa_c), [r3(ka_c)] * npc, [r3(va_c)] * npc, k_specs, bias_p, tqa, True)
            topk_p = min(TOPK_MAX, S // 4)
            wt = tr(r3(kw)[..., 64:64 + IDX_HEADS])
            keys, thr, cut = _dsa_index(r3(qi_c), r3(ki2_c), wt, tq=tqi, tk=tk, qoff=0, length=S, topk=topk_p,
                                        nreal=S)
            ob = _dsa_attn(r3(qb_c), r3(kb_c), tr(r3(vb_c)), keys, thr, cut, tq=tq, tk=tk, qoff=0, length=S)
            os_p = [oa.reshape(B * S, wsz), ob.reshape(B * S, wsz)]
            hd = lambda t: t.reshape(B, S, -1, HEAD_DIM)
            outs["a_k_p"].append(hd(ka)[:, S - W_AP:]); outs["a_v_p"].append(hd(va)[:, S - W_AP:])
            outs["b_k_p"].append(hd(kb)); outs["b_v_p"].append(hd(vb))
            outs["b_ik_p"].append(kw[:, :64].reshape(B, S, 64))
            (ka, va, kb, vb, kw, qa_c, ka_c, va_c, qb_c, kb_c, vb_c, qi_c, ki2_c) = _proj_even(
                h_s, g, w_all, gn, bd, cos_s, sin_s)
            r3 = lambda t: t.reshape(DB, n, t.shape[-1])
            ca_k = cache_a_k[e].reshape(DB, W_A, wsz)
            ca_v = cache_a_v[e].reshape(DB, W_A, wsz)
            full = lambda rows, wd: pl.BlockSpec((1, rows, wd), lambda bb, ii: (bb, 0, 0))
            bias_s = _band_bias(rel_bias_a[e], P + np.arange(n), P - W_A + np.arange(W_A + n))
            bias_s = jnp.pad(bias_s, ((0, 0), (0, 0), (0, -(W_A + n) % LANES)), constant_values=NEG)
            oa = _band(r3(qa_c), [ca_k, r3(ka)], [ca_v, r3(va)], [full(W_A, wsz), full(n, wsz)], bias_s, n, False)
            ik = cache_b_ik[e].astype(CDT)
            ki2_all = catp(jnp.concatenate([ik, ik], axis=-1), r3(ki2_c))
            kb_all = catp(cache_b_k[e].reshape(DB, P, wsz), r3(kb_c))
            vb_all = catp(cache_b_v[e].reshape(DB, P, wsz), r3(vb_c))
            topk_s = min(TOPK_MAX, L // 4)
            wt = _pad_rows(r3(kw)[..., 64:64 + IDX_HEADS], nqp)
            keys, thr, cut = _dsa_index(_pad_rows(r3(qi_c), nqp), ki2_all, tr(wt),
                                        tq=nqp, tk=Lp, qoff=P, length=L, topk=topk_s, nreal=n)
            ob = _dsa_attn(_pad_rows(r3(qb_c), nqp), kb_all, tr(vb_all), keys, thr, cut,
                           tq=nqp, tk=Lp, qoff=P, length=L)[:, :n]
            os_s = [oa.reshape(DB * n, wsz), ob.reshape(DB * n, wsz)]
            hd = lambda t: t.reshape(DB, n, -1, HEAD_DIM)
            outs["a_k_s"].append(jnp.concatenate([cache_a_k[e], hd(ka)], axis=1)[:, n:])
            outs["a_v_s"].append(jnp.concatenate([cache_a_v[e], hd(va)], axis=1)[:, n:])
            outs["b_k_s"].append(hd(kb)); outs["b_v_s"].append(hd(vb))
            outs["b_ik_s"].append(kw[:, :64].reshape(DB, n, 64))
            w_out = w_out_even[e].astype(CDT)
            wos = [w_out[:wsz], w_out[wsz:]]
        else:
            o = i // 2
            nh = b_f.shape[1]
            wsz = nh * HEAD_DIM
            w = w_in_odd[o]
            w_all = jnp.concatenate([w[:, :3 * wsz], jnp.pad(w[:, 3 * wsz:], ((0, 0), (0, LANES - nh)))],
                                    axis=1).astype(CDT)
            gn = jnp.stack([jnp.tile(v[o].astype(F32), wsz // HEAD_DIM) for v in (qn_c, kn_c)])
            bf = jnp.pad(b_f[o].astype(F32), (0, LANES - nh)).reshape(1, LANES)
            k, v, lf, q_c, k_c, v_c = _proj_odd(h_p, g, w_all, gn, bf, bd)
            r3 = lambda t: t.reshape(B, S, t.shape[-1])
            kx = _fox_key_lanes(r3(lf), 512 if S % 512 == 0 else S, LOG2E, nh)
            oc = _fox(r3(q_c), r3(k_c), kx, tr(r3(v_c)), _fox_qaug(kx[:, ::tq]), tq=tq, tk=tk, qoff=0, length=S)
            os_p = [oc.reshape(B * S, wsz)]
            hd = lambda t: t.reshape(B, S, nh, HEAD_DIM)
            outs["c_k_p"].append(hd(k)); outs["c_v_p"].append(hd(v)); outs["c_lf_p"].append(lf[:, :nh].reshape(B, S, nh))
            k, v, lf, q_c, k_c, v_c = _proj_odd(h_s, g, w_all, gn, bf, bd)
            r3 = lambda t: t.reshape(DB, n, t.shape[-1])
            lf_all = jnp.concatenate([jnp.pad(cache_c_logf[o].astype(F32), ((0, 0), (0, 0), (0, LANES - nh))), r3(lf)],
                                     axis=1)
            kx = _fox_key_lanes(lf_all, L, LOG2E, nh)
            k_all = catp(cache_c_k[o].reshape(DB, P, wsz), r3(k_c))
            v_all = catp(cache_c_v[o].reshape(DB, P, wsz), r3(v_c))
            oc = _fox(_pad_rows(r3(q_c), nqp), k_all, _pad_rows(kx, Lp), tr(v_all),
                      _fox_qaug(kx[:, P:P + 1]), tq=nqp, tk=Lp, qoff=P, length=L)[:, :n]
            os_s = [oc.reshape(DB * n, wsz)]
            hd = lambda t: t.reshape(DB, n, nh, HEAD_DIM)
            outs["c_k_s"].append(hd(k)); outs["c_v_s"].append(hd(v)); outs["c_lf_s"].append(lf[:, :nh].reshape(DB, n, nh))
            w_out = w_out_odd[o].astype(CDT)
            wos = [w_out]
        post_w = (row(g_mlp[i]), w_up[i].astype(CDT), w_down[i].astype(CDT), row(g_ple[i]),
                  w_ple_gate[i].astype(CDT), w_ple_proj[i].astype(CDT))
        h_p = _post(h_p, os_p, p_prompt[i].reshape(B * S, -1), wos, *post_w)
        h_s = _post(h_s, os_s, p_sample[i].reshape(DB * n, -1), wos, *post_w)

    st = jnp.stack
    names = ("a_k_p", "a_v_p", "b_k_p", "b_v_p", "b_ik_p", "c_k_p", "c_v_p", "c_lf_p",
             "a_k_s", "a_v_s", "b_k_s", "b_v_s", "b_ik_s", "c_k_s", "c_v_s", "c_lf_s")
    return (h_p.reshape(B, S, D), h_s.reshape(DB, n, D)) + tuple(st(outs[nm]) for nm in names)
```

```python
import functools

import numpy as np
import jax
import jax.numpy as jnp
from jax import lax
from jax.experimental import pallas as pl
from jax.experimental.pallas import tpu as pltpu

F32 = jnp.float32
I32 = jnp.int32
CDT = jnp.bfloat16

CHUNK = 64
HEAD_DIM = 64
IDX_HEADS = 8
A_PAST = 8 * CHUNK
REL_CLIP = 128
TOPK_MAX = 256
ROPE_THETA = 10000.0
RMS_EPS = 1e-6
LANES = 128
MXU_TILE = 256
PACKED_ROWS = 16
CHUNK_SHIFT = CHUNK.bit_length() - 1
NEG = -1e30
INT_MIN = -(2 ** 31)
INT_MAX = 2 ** 31 - 1
LOG2E = 1.4426950408889634
VMEM_LIMIT = 56 * 1024 * 1024


def _mm(a, b):
    return jnp.dot(a, b, preferred_element_type=F32)


def _mm_nt(a, b):
    return lax.dot_general(a, b, (((1,), (1,)), ((), ())), preferred_element_type=F32)


def _pieces(x, n):
    if CDT == F32:
        return [x]
    out = []
    for _ in range(n - 1):
        p = x.astype(CDT)
        out.append(p)
        x = x - p.astype(F32)
    out.append(x.astype(CDT))
    return out


def _rms(x, g):
    return x * lax.rsqrt(jnp.mean(x * x, axis=-1, keepdims=True) + RMS_EPS) * g


def _head_rms(x, bd, g):
    x2 = x * x
    pcs = _pieces(x2, 2)
    cols = []
    for s in range(x.shape[1] // MXU_TILE):
        sl = slice(s * MXU_TILE, (s + 1) * MXU_TILE)
        ms = _mm(pcs[0][:, sl], bd)
        for p in pcs[1:]:
            ms = ms + _mm(p[:, sl], bd)
        cols.append(ms)
    ms = cols[0] if len(cols) == 1 else jnp.concatenate(cols, axis=1)
    return x * lax.rsqrt(ms + RMS_EPS) * g


def _rope(x, cos, sin):
    w = x.shape[1]
    lane = lax.broadcasted_iota(I32, x.shape, 1)
    first = (lane & 63) < 32
    swapped = jnp.where(first, pltpu.roll(x, w - 32, 1), pltpu.roll(x, 32, 1))
    return x * cos + swapped * sin


def _tile_lanes(t, w):
    return t if w == t.shape[1] else jnp.concatenate([t] * (w // t.shape[1]), axis=1)


def _proj_even_kernel(h_ref, g_ref, w_ref, gn_ref, bd_ref, cos_ref, sin_ref,
                      ka_ref, va_ref, kb_ref, vb_ref, kw_ref,
                      qa_c, ka_c, va_c, qb_c, kb_c, vb_c, qi_c, ki2_c):
    a = _rms(h_ref[...], g_ref[...]).astype(CDT)
    z = _mm(a, w_ref[...])
    w = 512
    bd = bd_ref[...]
    gn = gn_ref[...]
    cos1, sin1 = cos_ref[...], sin_ref[...]
    cos, sin = _tile_lanes(cos1, w), _tile_lanes(sin1, w)
    scale = HEAD_DIM ** -0.5

    qa = _head_rms(z[:, 0:w], bd, gn[0:1])
    qa_c[...] = (qa * scale).astype(CDT)
    ka = _head_rms(z[:, w:2 * w], bd, gn[1:2])
    ka_ref[...] = ka
    ka_c[...] = ka.astype(CDT)
    va = z[:, 2 * w:3 * w]
    va_ref[...] = va
    va_c[...] = va.astype(CDT)
    qb = _rope(_head_rms(z[:, 3 * w:4 * w], bd, gn[2:3]), cos, sin)
    qb_c[...] = (qb * (scale * LOG2E)).astype(CDT)
    kb = _rope(_head_rms(z[:, 4 * w:5 * w], bd, gn[3:4]), cos, sin)
    kb_ref[...] = kb
    kb_c[...] = kb.astype(CDT)
    vb = z[:, 5 * w:6 * w]
    vb_ref[...] = vb
    vb_c[...] = vb.astype(CDT)
    qi = _rope(z[:, 6 * w:7 * w], cos, sin)
    qi_c[...] = (qi * scale).astype(CDT)
    kw = z[:, 7 * w:7 * w + LANES]
    kwr = _rope(kw, cos1, sin1)
    lane = lax.broadcasted_iota(I32, kw.shape, 1)
    kw_ref[...] = jnp.where(lane < 64, kwr, kw)
    ki2_c[...] = jnp.where(lane < 64, kwr, pltpu.roll(kwr, 64, 1)).astype(CDT)


def _proj_odd_kernel(h_ref, g_ref, w_ref, gn_ref, bf_ref, bd_ref,
                     k_ref, v_ref, lf_ref, q_c, k_c, v_c):
    a = _rms(h_ref[...], g_ref[...]).astype(CDT)
    z = _mm(a, w_ref[...])
    w = 1024
    bd = bd_ref[...]
    gn = gn_ref[...]
    q = _head_rms(z[:, 0:w], bd, gn[0:1])
    q_c[...] = (q * (HEAD_DIM ** -0.5 * LOG2E)).astype(CDT)
    k = _head_rms(z[:, w:2 * w], bd, gn[1:2])
    k_ref[...] = k
    k_c[...] = k.astype(CDT)
    v = z[:, 2 * w:3 * w]
    v_ref[...] = v
    v_c[...] = v.astype(CDT)
    fl = z[:, 3 * w:3 * w + LANES] + bf_ref[...]
    lf_ref[...] = jnp.minimum(fl, 0.0) - jnp.log1p(jnp.exp(-jnp.abs(fl)))


def _const_spec(shape):
    nd = len(shape)
    return pl.BlockSpec(shape, lambda *_: (0,) * nd, pipeline_mode=pl.Buffered(1))


def _row_spec(tm, w):
    return pl.BlockSpec((tm, w), lambda i: (i, 0))


def _params(sem):
    return pltpu.CompilerParams(dimension_semantics=sem, vmem_limit_bytes=VMEM_LIMIT)


def _row_tile(rows, pref):
    return pref if rows % pref == 0 else rows


def _proj_even(h, g, w_all, gn, bd, cos, sin):
    rows, d = h.shape
    tm = _row_tile(rows, 256)
    f = lambda w, dt: jax.ShapeDtypeStruct((rows, w), dt)
    out_shape = [f(512, F32)] * 4 + [f(LANES, F32)] + [f(512, CDT)] * 7 + [f(LANES, CDT)]
    out_specs = [_row_spec(tm, 512)] * 4 + [_row_spec(tm, LANES)] + [_row_spec(tm, 512)] * 7 + [_row_spec(tm, LANES)]
    return pl.pallas_call(
        _proj_even_kernel,
        grid=(rows // tm,),
        in_specs=[_row_spec(tm, d), _const_spec(g.shape), _const_spec(w_all.shape), _const_spec(gn.shape),
                  _const_spec(bd.shape), _row_spec(tm, LANES), _row_spec(tm, LANES)],
        out_specs=out_specs,
        out_shape=out_shape,
        compiler_params=_params(("parallel",)),
        name="proj_even",
    )(h, g, w_all, gn, bd, cos, sin)


def _proj_odd(h, g, w_all, gn, bf, bd):
    rows, d = h.shape
    tm = _row_tile(rows, 256)
    f = lambda w, dt: jax.ShapeDtypeStruct((rows, w), dt)
    out_shape = [f(1024, F32)] * 2 + [f(LANES, F32)] + [f(1024, CDT)] * 3
    out_specs = [_row_spec(tm, 1024)] * 2 + [_row_spec(tm, LANES)] + [_row_spec(tm, 1024)] * 3
    return pl.pallas_call(
        _proj_odd_kernel,
        grid=(rows // tm,),
        in_specs=[_row_spec(tm, d), _const_spec(g.shape), _const_spec(w_all.shape), _const_spec(gn.shape),
                  _const_spec(bf.shape), _const_spec(bd.shape)],
        out_specs=out_specs,
        out_shape=out_shape,
        compiler_params=_params(("parallel",)),
        name="proj_odd",
    )(h, g, w_all, gn, bf, bd)


def _post_kernel(*refs, n_o, ff_chunk):
    h_ref = refs[0]
    o_refs = refs[1:1 + n_o]
    p_ref = refs[1 + n_o]
    wo_refs = refs[2 + n_o:2 + 2 * n_o]
    gm_ref, wu_ref, wd_ref, gp_ref, wg_ref, wp_ref, out_ref = refs[2 + 2 * n_o:]
    mix = None
    for o_ref, wo_ref in zip(o_refs, wo_refs):
        t = _mm(o_ref[...], wo_ref[...])
        mix = t if mix is None else mix + t
    h = h_ref[...] + mix
    m = _rms(h, gm_ref[...]).astype(CDT)
    mlp = None
    d_ff = wu_ref.shape[1]
    for c in range(d_ff // ff_chunk):
        sl = slice(c * ff_chunk, (c + 1) * ff_chunk)
        u = jnp.square(jnp.maximum(_mm(m, wu_ref[:, sl]), 0.0)).astype(CDT)
        t = _mm(u, wd_ref[sl, :])
        mlp = t if mlp is None else mlp + t
    acc = h + mlp
    gate_in = _rms(acc, gp_ref[...]).astype(CDT)
    gate = 1.0 / (1.0 + jnp.exp(-_mm(gate_in, wg_ref[...])))
    out_ref[...] = acc + gate * _mm(p_ref[...].astype(CDT), wp_ref[...])


def _post(h, os_, p, wos, gm, wu, wd, gp, wg, wp):
    rows, d = h.shape
    tm = _row_tile(rows, 512)
    n_o = len(os_)
    in_specs = ([_row_spec(tm, d)] + [_row_spec(tm, o.shape[1]) for o in os_] + [_row_spec(tm, p.shape[1])]
                + [_const_spec(w.shape) for w in wos]
                + [_const_spec(x.shape) for x in (gm, wu, wd, gp, wg, wp)])
    return pl.pallas_call(
        functools.partial(_post_kernel, n_o=n_o, ff_chunk=1024),
        grid=(rows // tm,),
        in_specs=in_specs,
        out_specs=_row_spec(tm, d),
        out_shape=jax.ShapeDtypeStruct((rows, d), F32),
        compiler_params=_params(("parallel",)),
        name="post",
    )(h, *os_, p, *wos, gm, wu, wd, gp, wg, wp)


def _pair_stack(qp, tq):
    lane = lax.broadcasted_iota(I32, qp.shape, 1)
    zero = jnp.zeros_like(qp)
    return jnp.concatenate([jnp.where(lane < 64, qp, zero), jnp.where(lane >= 64, qp, zero)], axis=0)


def _pair_merge(o2, tq):
    lane = lax.broadcasted_iota(I32, (tq, LANES), 1)
    return jnp.where(lane < 64, o2[:tq], o2[tq:])


def _last_tile(i, tq, tk, qoff, length, chunked):
    qend = qoff + i * tq + tq - 1
    kmax = (qend // CHUNK + 1) * CHUNK if chunked else qend + 1
    kmax = jnp.minimum(kmax, length)
    return (kmax - 1) // tk


def _tri_steps(nq, tq, tk, qoff, length, chunked):
    ii, jj = [], []
    for i in range(nq):
        qend = qoff + i * tq + tq - 1
        kmax = min((qend // CHUNK + 1) * CHUNK if chunked else qend + 1, length)
        for j in range((kmax - 1) // tk + 1):
            ii.append(i)
            jj.append(j)
    return jnp.asarray(ii, I32), jnp.asarray(jj, I32)


def _tri_call(kernel_fn, steps, batch, in_specs, out_specs, out_shape, scratch, name):
    ii, jj = steps
    grid_spec = pltpu.PrefetchScalarGridSpec(
        num_scalar_prefetch=2, grid=(batch, ii.shape[0]),
        in_specs=in_specs, out_specs=out_specs, scratch_shapes=scratch)
    call = pl.pallas_call(kernel_fn, grid_spec=grid_spec, out_shape=out_shape,
                          compiler_params=_params(("parallel", "arbitrary")), name=name)
    return functools.partial(call, ii, jj)


def _flash_init(q_ref, aug_ref, qst_ref, m_ref, l_ref, acc_ref, tq):
    q = q_ref[0]
    for g in range(qst_ref.shape[0]):
        st = _pair_stack(q[:, g * LANES:(g + 1) * LANES], tq)
        if aug_ref is not None:
            a = aug_ref[0, 0]
            ext = jnp.concatenate([jnp.broadcast_to(a[2 * g:2 * g + 1], (tq, LANES)),
                                   jnp.broadcast_to(a[2 * g + 1:2 * g + 2], (tq, LANES))], axis=0)
            st = jnp.concatenate([st, ext.astype(CDT)], axis=1)
        qst_ref[g] = st
    m_ref[...] = jnp.full(m_ref.shape, NEG, F32)
    l_ref[...] = jnp.zeros(l_ref.shape, F32)
    acc_ref[...] = jnp.zeros(acc_ref.shape, F32)


def _flash_pairs(score_fn, vt_ref, scr, tq):
    _, m_ref, l_ref, acc_ref, s_ref = scr
    npair = m_ref.shape[0]
    s_ref[0] = score_fn(0)
    for g in range(npair):
        if g + 1 < npair:
            s_ref[(g + 1) % 2] = score_fn(g + 1)
        _flash_step(s_ref[g % 2], vt_ref, m_ref, l_ref, acc_ref, g, tq)


def _flash_step(st, vt_ref, m_ref, l_ref, acc_ref, g, tq):
    m_old = m_ref[g]
    m_new = jnp.maximum(m_old, jnp.max(st, axis=0, keepdims=True))
    alpha = jnp.exp2(m_old - m_new)
    m_ref[g] = m_new
    pc = jnp.exp2(st - m_new).astype(CDT)
    ones = jnp.ones((PACKED_ROWS, st.shape[0]), CDT)
    sums = []
    for e in range(2):
        rows = slice(g * LANES + e * HEAD_DIM, g * LANES + (e + 1) * HEAD_DIM)
        cols = slice(e * tq, (e + 1) * tq)
        pv = _mm(jnp.concatenate([vt_ref[0, rows, :], ones], axis=0), pc[:, cols])
        acc_ref[rows, :] = alpha[:, cols] * acc_ref[rows, :] + pv[:HEAD_DIM]
        sums.append(pv[HEAD_DIM:HEAD_DIM + 1])
    l_ref[g] = alpha * l_ref[g] + jnp.concatenate(sums, axis=1)


def _flash_finish(o_ref, l_ref, acc_ref, tq):
    for g in range(l_ref.shape[0]):
        l = l_ref[g]
        den = jnp.concatenate([jnp.broadcast_to(l[:, :tq], (HEAD_DIM, tq)),
                               jnp.broadcast_to(l[:, tq:], (HEAD_DIM, tq))], axis=0)
        o = acc_ref[g * LANES:(g + 1) * LANES, :] / den
        o_ref[0, :, g * LANES:(g + 1) * LANES] = o.T.astype(o_ref.dtype)


def _flash_scratch(npair, tq, tk, kd, w):
    return [pltpu.VMEM((npair, 2 * tq, kd), CDT),
            pltpu.VMEM((npair, 1, 2 * tq), F32),
            pltpu.VMEM((npair, 1, 2 * tq), F32),
            pltpu.VMEM((w, tq), F32),
            pltpu.VMEM((2, tk, 2 * tq), F32)]


def _band_kernel(*refs, n_kv, tq, mask_neg):
    q_ref = refs[0]
    k_refs = refs[1:1 + n_kv]
    v_refs = refs[1 + n_kv:1 + 2 * n_kv]
    bias_ref = refs[1 + 2 * n_kv]
    o_ref = refs[2 + 2 * n_kv]
    i = pl.program_id(1)
    q = q_ref[0]
    cat = lambda rs: (rs[0][0].astype(CDT) if len(rs) == 1
                      else jnp.concatenate([r[0].astype(CDT) for r in rs], axis=0))
    k = cat(k_refs)
    v = cat(v_refs)
    tk = bias_ref.shape[2]
    if k.shape[0] < tk:
        zpad = jnp.zeros((tk - k.shape[0], k.shape[1]), CDT)
        k = jnp.concatenate([k, zpad], axis=0)
        v = jnp.concatenate([v, zpad], axis=0)
    if mask_neg:
        kpos = lax.broadcasted_iota(I32, (2 * tq, tk), 1) + (i * tq - A_PAST)
        neg = kpos < 0
    for g in range(q.shape[1] // LANES):
        sl = slice(g * LANES, (g + 1) * LANES)
        s = _mm_nt(_pair_stack(q[:, sl], tq), k[:, sl])
        s = s + jnp.concatenate([bias_ref[2 * g], bias_ref[2 * g + 1]], axis=0)
        if mask_neg:
            s = jnp.where(neg, NEG, s)
        m = jnp.max(s, axis=-1, keepdims=True)
        p = jnp.exp(s - m)
        l = jnp.sum(p, axis=-1, keepdims=True)
        o2 = _mm(p.astype(CDT), v[:, sl]) / l
        o_ref[0, :, sl] = _pair_merge(o2, tq).astype(o_ref.dtype)


def _band(q, k_parts, v_parts, k_specs, bias, tq, mask_neg):
    b, s, w = q.shape
    n_kv = len(k_parts)
    q_spec = pl.BlockSpec((1, tq, w), lambda bb, i: (bb, i, 0))
    return pl.pallas_call(
        functools.partial(_band_kernel, n_kv=n_kv, tq=tq, mask_neg=mask_neg),
        grid=(b, s // tq),
        in_specs=[q_spec] + k_specs + k_specs + [_const_spec(bias.shape)],
        out_specs=q_spec,
        out_shape=jax.ShapeDtypeStruct((b, s, w), CDT),
        compiler_params=_params(("parallel", "parallel")),
        name="band_attn",
    )(q, *k_parts, *v_parts, bias)


def _band_bias(rel_bias, q_pos, k_pos):
    nq, nk = len(q_pos), len(k_pos)
    qc = q_pos[:, None] // CHUNK
    kc = k_pos[None, :] // CHUNK
    ok = (kc <= qc) & (kc >= qc - A_PAST // CHUNK)
    n = nq + nk - 1
    d0 = int(q_pos[0] - k_pos[0])
    m = np.arange(n + 1)
    rel = np.where(m < nk, d0 - m, d0 - m + n + 1)
    table = jnp.take(rel_bias.astype(F32), jnp.asarray(np.clip(rel, -REL_CLIP, REL_CLIP) + REL_CLIP), axis=1)
    bias = jnp.tile(table, (1, nq))[:, :nq * n].reshape(-1, nq, n)[:, :, :nk]
    return jnp.where(jnp.asarray(ok)[None], bias, NEG)


def _dsa_index_kernel(ii_ref, jj_ref, qi_ref, ki2_ref, wt_ref, keys_ref, t_ref, c_ref, qst_ref, kscr_ref, k16_ref,
                      *, tq, tk, nk, unroll, qoff, length, topk, idx_bits, nreal):
    i = ii_ref[pl.program_id(1)]
    j = jj_ref[pl.program_id(1)]
    last = _last_tile(i, tq, tk, qoff, length, True)
    q0 = qoff + i * tq

    @pl.when(j == 0)
    def _():
        q = qi_ref[0]
        for g in range(IDX_HEADS // 2):
            qst_ref[2 * g * tq:(2 * g + 2) * tq, :] = _pair_stack(q[:, g * LANES:(g + 1) * LANES], tq)

    @pl.when(j <= last)
    def _():
        lg = _mm_nt(ki2_ref[0], qst_ref[...])
        wgt = wt_ref[0] * (IDX_HEADS ** -0.5)
        sc = jnp.zeros((tk, tq), F32)
        for h in range(IDX_HEADS):
            sc = sc + jnp.maximum(lg[:, h * tq:(h + 1) * tq], 0.0) * wgt[h:h + 1, :]
        sc = jnp.where(sc == 0.0, 0.0, sc)
        bits = lax.bitcast_convert_type(sc, I32)
        key = bits ^ ((bits >> 31) & 0x7FFFFFFF)
        kpos = j * tk + lax.broadcasted_iota(I32, (tk, tq), 0)
        qpos = q0 + lax.broadcasted_iota(I32, (tk, tq), 1)
        ok = ((kpos >> CHUNK_SHIFT) <= (qpos >> CHUNK_SHIFT)) & (kpos < length)
        key = jnp.where(ok, key, INT_MIN)
        keys_ref[0] = key
        kscr_ref[j] = key
        top = lax.bitcast_convert_type(bits & -65536, F32)
        k16_ref[j] = jnp.where(ok, top, -jnp.inf).astype(jnp.bfloat16)
        if unroll > 1:
            @pl.when(j + 1 < nk)
            def _():
                kscr_ref[j + 1] = jnp.full((tk, tq), INT_MIN, I32)
                k16_ref[j + 1] = jnp.full((tk, tq), -jnp.inf, jnp.bfloat16)

    @pl.when(j == last)
    def _():
        def count(pred):
            def body(t, acc):
                for u in range(unroll):
                    tt = unroll * t + u
                    kpos = tt * tk + lax.broadcasted_iota(I32, (tk, tq), 0)
                    m = jnp.where(pred(kscr_ref[tt], kpos), 1.0, 0.0)
                    acc = acc + jnp.sum(m, axis=0, keepdims=True)
                return acc
            return lax.fori_loop(0, (last + unroll) // unroll, body, jnp.zeros((1, tq), F32))

        kf = float(topk)
        qpos = q0 + lax.broadcasted_iota(I32, (1, tq), 1)
        n_adm = jnp.minimum(((qpos >> CHUNK_SHIFT) + 1) << CHUNK_SHIFT, length).astype(F32)
        real = qpos - qoff < nreal
        take_all = (n_adm < kf) | jnp.logical_not(real)

        def count16(c16):
            one, zero = jnp.ones((), jnp.bfloat16), jnp.zeros((), jnp.bfloat16)
            nslab = tk // PACKED_ROWS
            assert nslab <= 256

            def body(t, acc):
                for u in range(unroll):
                    m = jnp.where(k16_ref[unroll * t + u] >= c16, one, zero).reshape(nslab, PACKED_ROWS, tq)
                    part = m[0]
                    for r in range(1, nslab):
                        part = part + m[r]
                    acc = acc + jnp.sum(part.astype(F32), axis=0, keepdims=True)
                return acc
            return lax.fori_loop(0, (last + unroll) // unroll, body, jnp.zeros((1, tq), F32))

        def bit_step(it, prefix, nge, top_half):
            cand = prefix + lax.shift_left(jnp.int32(1), 31 - it)
            if top_half:
                cbits = cand ^ ((cand >> 31) & 0x7FFFFFFF)
                cbits = jnp.where((cand > 0) & (cand < 0x00800000), 0x00800000, cbits)
                cnt = count16(lax.bitcast_convert_type(cbits & -65536, F32).astype(jnp.bfloat16))
            else:
                cnt = count(lambda kt, _: kt >= cand)
            take = cnt >= kf
            return jnp.where(take, cand, prefix), jnp.where(take, cnt, nge)

        def all_settled(nge):
            return (jnp.min(jnp.where(take_all | (nge == kf), 1.0, 0.0)) > 0.0).astype(I32)

        def low_body(carry):
            it, prefix, nge, _ = carry
            prefix, nge = bit_step(it, prefix, nge, False)
            return it + 1, prefix, nge, all_settled(nge)

        prefix, nge = lax.fori_loop(0, 16, lambda it, c: bit_step(it, c[0], c[1], True),
                                    (jnp.full((1, tq), INT_MIN, I32), jnp.zeros((1, tq), F32)))
        _, prefix, nge, _ = lax.while_loop(lambda c: (c[0] < 32) & (c[3] == 0), low_body,
                                           (jnp.int32(16), prefix, nge, all_settled(nge)))
        found = prefix > INT_MIN
        thr = jnp.maximum(prefix, INT_MIN + 1)
        t_ref[0] = thr
        c_ref[0] = jnp.full((1, tq), 2 ** 30, I32)
        straddle = jnp.max(jnp.where(found & (nge > kf) & real, 1.0, 0.0))

        @pl.when(straddle > 0.0)
        def _():
            need = kf - count(lambda kt, _: kt > thr)

            def tie_body(it, x):
                cx = x + lax.shift_left(jnp.int32(1), idx_bits - 1 - it)
                g = count(lambda kt, kpos: jnp.where(kt == thr, kpos, INT_MAX) < cx)
                return jnp.where(g < need, cx, x)
            c_ref[0] = lax.fori_loop(0, idx_bits, tie_body, jnp.zeros((1, tq), I32))


def _dsa_index(qi, ki2, wt, *, tq, tk, qoff, length, topk, nreal):
    b, s, _ = qi.shape
    lp = ki2.shape[1]
    nq, nk = s // tq, lp // tk
    idx_bits = max(1, int(np.ceil(np.log2(lp))))
    qvec = pl.BlockSpec((1, 1, tq), lambda bb, t, ii, jj: (bb, 0, ii[t]))
    return _tri_call(
        functools.partial(_dsa_index_kernel, tq=tq, tk=tk, nk=nk, unroll=2 if nk % 2 == 0 else 1,
                          qoff=qoff, length=length, topk=topk, idx_bits=idx_bits, nreal=nreal),
        _tri_steps(nq, tq, tk, qoff, length, True), b,
        in_specs=[pl.BlockSpec((1, tq, 512), lambda bb, t, ii, jj: (bb, ii[t], 0)),
                  pl.BlockSpec((1, tk, LANES), lambda bb, t, ii, jj: (bb, jj[t], 0)),
                  pl.BlockSpec((1, IDX_HEADS, tq), lambda bb, t, ii, jj: (bb, 0, ii[t]))],
        out_specs=[pl.BlockSpec((1, tk, tq), lambda bb, t, ii, jj: (bb, jj[t], ii[t])), qvec, qvec],
        out_shape=[jax.ShapeDtypeStruct((b, lp, s), I32),
                   jax.ShapeDtypeStruct((b, 1, s), I32),
                   jax.ShapeDtypeStruct((b, 1, s), I32)],
        scratch=[pltpu.VMEM((IDX_HEADS * tq, LANES), CDT),
                 pltpu.VMEM((nk, tk, tq), I32),
                 pltpu.VMEM((nk, tk, tq), jnp.bfloat16)],
        name="dsa_index",
    )(qi, ki2, wt)


def _dsa_attn_kernel(ii_ref, jj_ref, q_ref, k_ref, vt_ref, keys_ref, t_ref, c_ref, o_ref, *scr,
                     tq, tk, qoff, length):
    qst_ref, m_ref, l_ref, acc_ref = scr[:4]
    i = ii_ref[pl.program_id(1)]
    j = jj_ref[pl.program_id(1)]
    last = _last_tile(i, tq, tk, qoff, length, True)

    @pl.when(j == 0)
    def _():
        _flash_init(q_ref, None, qst_ref, m_ref, l_ref, acc_ref, tq)

    @pl.when(j <= last)
    def _():
        kt = keys_ref[0]
        thr = t_ref[0]
        kpos = j * tk + lax.broadcasted_iota(I32, (tk, tq), 0)
        sel = (kt > thr) | (jnp.where(kt == thr, kpos, INT_MAX) <= c_ref[0])
        mb = jnp.where(sel, 0.0, NEG)
        mb2 = jnp.concatenate([mb, mb], axis=1)
        score = lambda g: _mm_nt(k_ref[0, :, g * LANES:(g + 1) * LANES], qst_ref[g]) + mb2
        _flash_pairs(score, vt_ref, scr, tq)

    @pl.when(j == last)
    def _():
        _flash_finish(o_ref, l_ref, acc_ref, tq)


def _dsa_attn(q, k, vt, keys, thr, cut, *, tq, tk, qoff, length):
    b, s, w = q.shape
    qmap = lambda bb, t, ii, jj: (bb, ii[t], 0)
    qvec = pl.BlockSpec((1, 1, tq), lambda bb, t, ii, jj: (bb, 0, ii[t]))
    npair = w // LANES
    return _tri_call(
        functools.partial(_dsa_attn_kernel, tq=tq, tk=tk, qoff=qoff, length=length),
        _tri_steps(s // tq, tq, tk, qoff, length, True), b,
        in_specs=[pl.BlockSpec((1, tq, w), qmap),
                  pl.BlockSpec((1, tk, w), lambda bb, t, ii, jj: (bb, jj[t], 0)),
                  pl.BlockSpec((1, w, tk), lambda bb, t, ii, jj: (bb, 0, jj[t])),
                  pl.BlockSpec((1, tk, tq), lambda bb, t, ii, jj: (bb, jj[t], ii[t])),
                  qvec, qvec],
        out_specs=pl.BlockSpec((1, tq, w), qmap),
        out_shape=jax.ShapeDtypeStruct((b, s, w), CDT),
        scratch=_flash_scratch(npair, tq, tk, LANES, w),
        name="dsa_attn",
    )(q, k, vt, keys, thr, cut)


def _pieces3(x):
    pcs = _pieces(x, 3)
    return pcs + [jnp.zeros_like(pcs[0])] * (3 - len(pcs))


FOX_EXT = 9


def _cumsum_kernel(x_ref, tri_ref, place_ref, ones_ref, kx_ref, carry_ref, *, scale):
    @pl.when(pl.program_id(1) == 0)
    def _():
        carry_ref[...] = jnp.zeros(carry_ref.shape, F32)
    tri = tri_ref[...]
    tot = None
    for p in _pieces(x_ref[0], 3):
        t = _mm(tri, p)
        tot = t if tot is None else tot + t
    acc = carry_ref[...] + tot
    carry_ref[...] = acc[-1:, :]
    ext = None
    for c, p in enumerate(_pieces3(acc * (-scale))):
        t = _mm(p, place_ref[c])
        ext = t if ext is None else ext + t
    kx_ref[0] = (ext + ones_ref[...]).astype(kx_ref.dtype)


def _fox_key_lanes(x, tb, scale, nh):
    b, s, w = x.shape
    npair = nh // 2
    place = np.zeros((3, w, npair * LANES), np.float32)
    ones = np.zeros((1, npair * LANES), np.float32)
    for h in range(nh):
        for c in range(3):
            place[c, h, (h // 2) * LANES + (h % 2) * 6 + c] = 1.0
    for g in range(npair):
        ones[0, g * LANES + 3:g * LANES + 6] = 1.0
    tri = jnp.tril(jnp.ones((tb, tb), F32)).astype(CDT)
    place = jnp.asarray(place, CDT)
    spec = lambda wd: pl.BlockSpec((1, tb, wd), lambda bb, i: (bb, i, 0))
    return pl.pallas_call(
        functools.partial(_cumsum_kernel, scale=scale),
        grid=(b, s // tb),
        in_specs=[spec(w), _const_spec(tri.shape), _const_spec(place.shape), _const_spec(ones.shape)],
        out_specs=spec(npair * LANES),
        out_shape=jax.ShapeDtypeStruct((b, s, npair * LANES), CDT),
        scratch_shapes=[pltpu.VMEM((1, w), F32)],
        compiler_params=_params(("parallel", "arbitrary")),
        name="logf_cumsum",
    )(x, tri, place, jnp.asarray(ones))


def _fox_kernel(ii_ref, jj_ref, q_ref, k_ref, kx_ref, vt_ref, aug_ref, o_ref, *scr, tq, tk, qoff, length):
    qst_ref, m_ref, l_ref, acc_ref = scr[:4]
    i = ii_ref[pl.program_id(1)]
    j = jj_ref[pl.program_id(1)]
    last = _last_tile(i, tq, tk, qoff, length, False)
    q0 = qoff + i * tq

    @pl.when(j == 0)
    def _():
        _flash_init(q_ref, aug_ref, qst_ref, m_ref, l_ref, acc_ref, tq)

    def step(masked):
        if masked:
            kpos = j * tk + lax.broadcasted_iota(I32, (tk, 2 * tq), 0)
            lane = lax.broadcasted_iota(I32, (tk, 2 * tq), 1)
            causal = kpos <= q0 + jnp.where(lane >= tq, lane - tq, lane)

        def score(g):
            sl = slice(g * LANES, (g + 1) * LANES)
            st = _mm_nt(jnp.concatenate([k_ref[0, :, sl], kx_ref[0, :, sl]], axis=1), qst_ref[g])
            return jnp.where(causal, st, NEG) if masked else st
        _flash_pairs(score, vt_ref, scr, tq)

    diag = (j + 1) * tk - 1 > q0

    @pl.when((j <= last) & diag)
    def _():
        step(True)

    @pl.when((j <= last) & jnp.logical_not(diag))
    def _():
        step(False)

    @pl.when(j == last)
    def _():
        _flash_finish(o_ref, l_ref, acc_ref, tq)


def _fox(q, k, kx, vt, qaug, *, tq, tk, qoff, length):
    b, s, w = q.shape
    npair = w // LANES
    qmap = lambda bb, t, ii, jj: (bb, ii[t], 0)
    kmap = lambda bb, t, ii, jj: (bb, jj[t], 0)
    return _tri_call(
        functools.partial(_fox_kernel, tq=tq, tk=tk, qoff=qoff, length=length),
        _tri_steps(s // tq, tq, tk, qoff, length, False), b,
        in_specs=[pl.BlockSpec((1, tq, w), qmap),
                  pl.BlockSpec((1, tk, w), kmap),
                  pl.BlockSpec((1, tk, npair * LANES), kmap),
                  pl.BlockSpec((1, w, tk), lambda bb, t, ii, jj: (bb, 0, jj[t])),
                  pl.BlockSpec((1, 1, 2 * npair, LANES), lambda bb, t, ii, jj: (bb, ii[t], 0, 0))],
        out_specs=pl.BlockSpec((1, tq, w), qmap),
        out_shape=jax.ShapeDtypeStruct((b, s, w), CDT),
        scratch=_flash_scratch(npair, tq, tk, 2 * LANES, w),
        name="fox_attn",
    )(q, k, kx, vt, qaug)


def _fox_qaug(kx0):
    b, nq, _ = kx0.shape
    a = kx0.reshape(b, nq, -1, LANES).astype(F32)
    one, zero = jnp.ones_like(a[..., 0:3]), jnp.zeros_like(a[..., 0:3])
    even = jnp.concatenate([one, -a[..., 0:3], zero], axis=-1)
    odd = jnp.concatenate([zero, -a[..., 6:9], one], axis=-1)
    heads = jnp.stack([even, odd], axis=3).reshape(b, nq, -1, FOX_EXT)
    return jnp.pad(heads, ((0, 0), (0, 0), (0, 0), (0, LANES - FOX_EXT)))


def _rope_tables(pos):
    half = HEAD_DIM // 2
    inv = ROPE_THETA ** (-jnp.arange(half, dtype=F32) / half)
    ang = pos.astype(F32)[:, None] * inv[None, :]
    cos, sin = jnp.cos(ang), jnp.sin(ang)
    return (jnp.tile(jnp.concatenate([cos, cos], axis=1), (1, 2)),
            jnp.tile(jnp.concatenate([-sin, sin], axis=1), (1, 2)))


def _pad_rows(x, n):
    return x if x.shape[1] == n else jnp.pad(x, ((0, 0), (0, n - x.shape[1]), (0, 0)))


def _tiles(s):
    tk = 512 if s % 512 == 0 else s
    tq = 512 if s % 512 == 0 else s
    tqi = 256 if s % 256 == 0 else s
    tqf = 1024 if s % 1024 == 0 else tq
    return tq, tqi, tk, tqf


def kernel(x_prompt, x_sample, cache_a_k, cache_a_v, cache_b_k, cache_b_v, cache_b_ik, cache_c_k, cache_c_v,
           cache_c_logf, p_prompt, p_sample, g_mix, w_in_even, qn_a, kn_a, rel_bias_a, qn_b, kn_b, w_out_even,
           w_in_odd, b_f, qn_c, kn_c, w_out_odd, g_mlp, w_up, w_down, g_ple, w_ple_gate, w_ple_proj):
    B, S, D = x_prompt.shape
    DB, n, _ = x_sample.shape
    P = cache_b_k.shape[2]
    W_A = cache_a_k.shape[2]
    W_AP = min(A_PAST, S)
    depth = g_mix.shape[0]
    L = P + n
    Lp = -(-L // LANES) * LANES
    nqp = -(-n // LANES) * LANES
    assert B == 1 and S % 128 == 0 and n % 16 == 0 and P % 16 == 0

    row = lambda v: v.astype(F32).reshape(1, -1)
    bd = jnp.asarray(np.kron(np.eye(MXU_TILE // HEAD_DIM), np.full((HEAD_DIM, HEAD_DIM), 1.0 / HEAD_DIM)), CDT)
    cos_p, sin_p = _rope_tables(jnp.arange(S))
    cos_s, sin_s = (jnp.tile(t, (DB, 1)) for t in _rope_tables(P + jnp.arange(n)))
    tq, tqi, tk, tqf = _tiles(S)
    tr = lambda t: jnp.swapaxes(t, 1, 2)
    catp = lambda c, new: _pad_rows(jnp.concatenate([c.astype(CDT), new], axis=1), Lp)

    outs = {name: [] for name in ("a_k_p", "a_v_p", "b_k_p", "b_v_p", "b_ik_p", "c_k_p", "c_v_p", "c_lf_p",
                                  "a_k_s", "a_v_s", "b_k_s", "b_v_s", "b_ik_s", "c_k_s", "c_v_s", "c_lf_s")}
    h_p = x_prompt.reshape(B * S, D)
    h_s = x_sample.reshape(DB * n, D)
    for i in range(depth):
        g = row(g_mix[i])
        if i % 2 == 0:
            e = i // 2
            wsz = 512
            w = w_in_even[e]
            w_all = jnp.concatenate([w[:, :7 * wsz], jnp.pad(w[:, 7 * wsz:], ((0, 0), (0, LANES - 72)))],
                                    axis=1).astype(CDT)
            gn = jnp.stack([jnp.tile(v[e].astype(F32), wsz // HEAD_DIM) for v in (qn_a, kn_a, qn_b, kn_b)])
            (ka, va, kb, vb, kw, qa_c, ka_c, va_c, qb_c, kb_c, vb_c, qi_c, ki2_c) = _proj_even(
                h_p, g, w_all, gn, bd, cos_p, sin_p)
            r3 = lambda t: t.reshape(B, S, t.shape[-1])
            tqa = 256 if S % 256 == 0 else 128
            npc = (A_PAST + tqa) // tqa
            k_specs = [pl.BlockSpec((1, tqa, wsz), functools.partial(
                lambda bb, ii, pp: (bb, jnp.maximum(ii + pp - A_PAST // tqa, 0), 0), pp=pp)) for pp in range(npc)]
            bias_p = _band_bias(rel_bias_a[e], np.arange(tqa), np.arange(-A_PAST, tqa))
            oa = _band(r3(qa_c), [r3(ka_c)] * npc, [r3(va_c)] * npc, k_specs, bias_p, tqa, True)
            topk_p = min(TOPK_MAX, S // 4)
            wt = tr(r3(kw)[..., 64:64 + IDX_HEADS])
            keys, thr, cut = _dsa_index(r3(qi_c), r3(ki2_c), wt, tq=tqi, tk=tk, qoff=0, length=S, topk=topk_p,
                                        nreal=S)
            ob = _dsa_attn(r3(qb_c), r3(kb_c), tr(r3(vb_c)), keys, thr, cut, tq=tq, tk=tk, qoff=0, length=S)
            os_p = [oa.reshape(B * S, wsz), ob.reshape(B * S, wsz)]
            hd = lambda t: t.reshape(B, S, -1, HEAD_DIM)
            outs["a_k_p"].append(hd(ka)[:, S - W_AP:]); outs["a_v_p"].append(hd(va)[:, S - W_AP:])
            outs["b_k_p"].append(hd(kb)); outs["b_v_p"].append(hd(vb))
            outs["b_ik_p"].append(kw[:, :64].reshape(B, S, 64))
            (ka, va, kb, vb, kw, qa_c, ka_c, va_c, qb_c, kb_c, vb_c, qi_c, ki2_c) = _proj_even(
                h_s, g, w_all, gn, bd, cos_s, sin_s)
            r3 = lambda t: t.reshape(DB, n, t.shape[-1])
            ca_k = cache_a_k[e].reshape(DB, W_A, wsz)
            ca_v = cache_a_v[e].reshape(DB, W_A, wsz)
            full = lambda rows, wd: pl.BlockSpec((1, rows, wd), lambda bb, ii: (bb, 0, 0))
            bias_s = _band_bias(rel_bias_a[e], P + np.arange(n), P - W_A + np.arange(W_A + n))
            bias_s = jnp.pad(bias_s, ((0, 0), (0, 0), (0, -(W_A + n) % LANES)), constant_values=NEG)
            oa = _band(r3(qa_c), [ca_k, r3(ka)], [ca_v, r3(va)], [full(W_A, wsz), full(n, wsz)], bias_s, n, False)
            ik = cache_b_ik[e].astype(CDT)
            ki2_all = catp(jnp.concatenate([ik, ik], axis=-1), r3(ki2_c))
            kb_all = catp(cache_b_k[e].reshape(DB, P, wsz), r3(kb_c))
            vb_all = catp(cache_b_v[e].reshape(DB, P, wsz), r3(vb_c))
            topk_s = min(TOPK_MAX, L // 4)
            wt = _pad_rows(r3(kw)[..., 64:64 + IDX_HEADS], nqp)
            keys, thr, cut = _dsa_index(_pad_rows(r3(qi_c), nqp), ki2_all, tr(wt),
                                        tq=nqp, tk=Lp, qoff=P, length=L, topk=topk_s, nreal=n)
            ob = _dsa_attn(_pad_rows(r3(qb_c), nqp), kb_all, tr(vb_all), keys, thr, cut,
                           tq=nqp, tk=Lp, qoff=P, length=L)[:, :n]
            os_s = [oa.reshape(DB * n, wsz), ob.reshape(DB * n, wsz)]
            hd = lambda t: t.reshape(DB, n, -1, HEAD_DIM)
            outs["a_k_s"].append(jnp.concatenate([cache_a_k[e], hd(ka)], axis=1)[:, n:])
            outs["a_v_s"].append(jnp.concatenate([cache_a_v[e], hd(va)], axis=1)[:, n:])
            outs["b_k_s"].append(hd(kb)); outs["b_v_s"].append(hd(vb))
            outs["b_ik_s"].append(kw[:, :64].reshape(DB, n, 64))
            w_out = w_out_even[e].astype(CDT)
            wos = [w_out[:wsz], w_out[wsz:]]
        else:
            o = i // 2
            nh = b_f.shape[1]
            wsz = nh * HEAD_DIM
            w = w_in_odd[o]
            w_all = jnp.concatenate([w[:, :3 * wsz], jnp.pad(w[:, 3 * wsz:], ((0, 0), (0, LANES - nh)))],
                                    axis=1).astype(CDT)
            gn = jnp.stack([jnp.tile(v[o].astype(F32), wsz // HEAD_DIM) for v in (qn_c, kn_c)])
            bf = jnp.pad(b_f[o].astype(F32), (0, LANES - nh)).reshape(1, LANES)
            k, v, lf, q_c, k_c, v_c = _proj_odd(h_p, g, w_all, gn, bf, bd)
            r3 = lambda t: t.reshape(B, S, t.shape[-1])
            kx = _fox_key_lanes(r3(lf), 512 if S % 512 == 0 else S, LOG2E, nh)
            oc = _fox(r3(q_c), r3(k_c), kx, tr(r3(v_c)), _fox_qaug(kx[:, ::tqf]), tq=tqf, tk=tk, qoff=0, length=S)
            os_p = [oc.reshape(B * S, wsz)]
            hd = lambda t: t.reshape(B, S, nh, HEAD_DIM)
            outs["c_k_p"].append(hd(k)); outs["c_v_p"].append(hd(v)); outs["c_lf_p"].append(lf[:, :nh].reshape(B, S, nh))
            k, v, lf, q_c, k_c, v_c = _proj_odd(h_s, g, w_all, gn, bf, bd)
            r3 = lambda t: t.reshape(DB, n, t.shape[-1])
            lf_all = jnp.concatenate([jnp.pad(cache_c_logf[o].astype(F32), ((0, 0), (0, 0), (0, LANES - nh))), r3(lf)],
                                     axis=1)
            kx = _fox_key_lanes(lf_all, L, LOG2E, nh)
            k_all = catp(cache_c_k[o].reshape(DB, P, wsz), r3(k_c))
            v_all = catp(cache_c_v[o].reshape(DB, P, wsz), r3(v_c))
            oc = _fox(_pad_rows(r3(q_c), nqp), k_all, _pad_rows(kx, Lp), tr(v_all),
                      _fox_qaug(kx[:, P:P + 1]), tq=nqp, tk=Lp, qoff=P, length=L)[:, :n]
            os_s = [oc.reshape(DB * n, wsz)]
            hd = lambda t: t.reshape(DB, n, nh, HEAD_DIM)
            outs["c_k_s"].append(hd(k)); outs["c_v_s"].append(hd(v)); outs["c_lf_s"].append(lf[:, :nh].reshape(DB, n, nh))
            w_out = w_out_odd[o].astype(CDT)
            wos = [w_out]
        post_w = (row(g_mlp[i]), w_up[i].astype(CDT), w_down[i].astype(CDT), row(g_ple[i]),
                  w_ple_gate[i].astype(CDT), w_ple_proj[i].astype(CDT))
        h_p = _post(h_p, os_p, p_prompt[i].reshape(B * S, -1), wos, *post_w)
        h_s = _post(h_s, os_s, p_sample[i].reshape(DB * n, -1), wos, *post_w)

    st = jnp.stack
    names = ("a_k_p", "a_v_p", "b_k_p", "b_v_p", "b_ik_p", "c_k_p", "c_v_p", "c_lf_p",
             "a_k_s", "a_v_s", "b_k_s", "b_v_s", "b_ik_s", "c_k_s", "c_v_s", "c_lf_s")
    return (h_p.reshape(B, S, D), h_s.reshape(DB, n, D)) + tuple(st(outs[nm]) for nm in names)
```

```python
import functools

import numpy as np
import jax
import jax.numpy as jnp
from jax import lax
from jax.experimental import pallas as pl
from jax.experimental.pallas import tpu as pltpu

F32 = jnp.float32
I32 = jnp.int32
CDT = jnp.bfloat16

CHUNK = 64
HEAD_DIM = 64
IDX_HEADS = 8
A_PAST = 8 * CHUNK
REL_CLIP = 128
TOPK_MAX = 256
ROPE_THETA = 10000.0
RMS_EPS = 1e-6
LANES = 128
MXU_TILE = 256
PACKED_ROWS = 16
CHUNK_SHIFT = CHUNK.bit_length() - 1
NEG = -1e30
INT_MIN = -(2 ** 31)
INT_MAX = 2 ** 31 - 1
LOG2E = 1.4426950408889634
VMEM_LIMIT = 56 * 1024 * 1024


def _mm(a, b):
    return jnp.dot(a, b, preferred_element_type=F32)


def _mm_nt(a, b):
    return lax.dot_general(a, b, (((1,), (1,)), ((), ())), preferred_element_type=F32)


def _pieces(x, n):
    if CDT == F32:
        return [x]
    out = []
    for _ in range(n - 1):
        p = x.astype(CDT)
        out.append(p)
        x = x - p.astype(F32)
    out.append(x.astype(CDT))
    return out


def _rms(x, g):
    return x * lax.rsqrt(jnp.mean(x * x, axis=-1, keepdims=True) + RMS_EPS) * g


def _head_rms(x, bd, g):
    x2 = x * x
    pcs = _pieces(x2, 2)
    cols = []
    for s in range(x.shape[1] // MXU_TILE):
        sl = slice(s * MXU_TILE, (s + 1) * MXU_TILE)
        ms = _mm(pcs[0][:, sl], bd)
        for p in pcs[1:]:
            ms = ms + _mm(p[:, sl], bd)
        cols.append(ms)
    ms = cols[0] if len(cols) == 1 else jnp.concatenate(cols, axis=1)
    return x * lax.rsqrt(ms + RMS_EPS) * g


def _rope(x, cos, sin):
    w = x.shape[1]
    lane = lax.broadcasted_iota(I32, x.shape, 1)
    first = (lane & 63) < 32
    swapped = jnp.where(first, pltpu.roll(x, w - 32, 1), pltpu.roll(x, 32, 1))
    return x * cos + swapped * sin


def _tile_lanes(t, w):
    return t if w == t.shape[1] else jnp.concatenate([t] * (w // t.shape[1]), axis=1)


def _proj_even_kernel(h_ref, g_ref, w_ref, gn_ref, bd_ref, cos_ref, sin_ref,
                      ka_ref, va_ref, kb_ref, vb_ref, kw_ref,
                      qa_c, ka_c, va_c, qb_c, kb_c, vb_c, qi_c, ki2_c):
    a = _rms(h_ref[...], g_ref[...]).astype(CDT)
    z = _mm(a, w_ref[...])
    w = 512
    bd = bd_ref[...]
    gn = gn_ref[...]
    cos1, sin1 = cos_ref[...], sin_ref[...]
    cos, sin = _tile_lanes(cos1, w), _tile_lanes(sin1, w)
    scale = HEAD_DIM ** -0.5

    qa = _head_rms(z[:, 0:w], bd, gn[0:1])
    qa_c[...] = (qa * scale).astype(CDT)
    ka = _head_rms(z[:, w:2 * w], bd, gn[1:2])
    ka_ref[...] = ka
    ka_c[...] = ka.astype(CDT)
    va = z[:, 2 * w:3 * w]
    va_ref[...] = va
    va_c[...] = va.astype(CDT)
    qb = _rope(_head_rms(z[:, 3 * w:4 * w], bd, gn[2:3]), cos, sin)
    qb_c[...] = (qb * (scale * LOG2E)).astype(CDT)
    kb = _rope(_head_rms(z[:, 4 * w:5 * w], bd, gn[3:4]), cos, sin)
    kb_ref[...] = kb
    kb_c[...] = kb.astype(CDT)
    vb = z[:, 5 * w:6 * w]
    vb_ref[...] = vb
    vb_c[...] = vb.astype(CDT)
    qi = _rope(z[:, 6 * w:7 * w], cos, sin)
    qi_c[...] = (qi * scale).astype(CDT)
    kw = z[:, 7 * w:7 * w + LANES]
    kwr = _rope(kw, cos1, sin1)
    lane = lax.broadcasted_iota(I32, kw.shape, 1)
    kw_ref[...] = jnp.where(lane < 64, kwr, kw)
    ki2_c[...] = jnp.where(lane < 64, kwr, pltpu.roll(kwr, 64, 1)).astype(CDT)


def _proj_odd_kernel(h_ref, g_ref, w_ref, gn_ref, bf_ref, bd_ref,
                     k_ref, v_ref, lf_ref, q_c, k_c, v_c):
    a = _rms(h_ref[...], g_ref[...]).astype(CDT)
    z = _mm(a, w_ref[...])
    w = 1024
    bd = bd_ref[...]
    gn = gn_ref[...]
    q = _head_rms(z[:, 0:w], bd, gn[0:1])
    q_c[...] = (q * (HEAD_DIM ** -0.5 * LOG2E)).astype(CDT)
    k = _head_rms(z[:, w:2 * w], bd, gn[1:2])
    k_ref[...] = k
    k_c[...] = k.astype(CDT)
    v = z[:, 2 * w:3 * w]
    v_ref[...] = v
    v_c[...] = v.astype(CDT)
    fl = z[:, 3 * w:3 * w + LANES] + bf_ref[...]
    lf_ref[...] = jnp.minimum(fl, 0.0) - jnp.log1p(jnp.exp(-jnp.abs(fl)))


def _const_spec(shape):
    nd = len(shape)
    return pl.BlockSpec(shape, lambda *_: (0,) * nd, pipeline_mode=pl.Buffered(1))


def _row_spec(tm, w):
    return pl.BlockSpec((tm, w), lambda i: (i, 0))


def _params(sem):
    return pltpu.CompilerParams(dimension_semantics=sem, vmem_limit_bytes=VMEM_LIMIT)


def _row_tile(rows, pref):
    return pref if rows % pref == 0 else rows


def _proj_even(h, g, w_all, gn, bd, cos, sin):
    rows, d = h.shape
    tm = _row_tile(rows, 256)
    f = lambda w, dt: jax.ShapeDtypeStruct((rows, w), dt)
    out_shape = [f(512, F32)] * 4 + [f(LANES, F32)] + [f(512, CDT)] * 7 + [f(LANES, CDT)]
    out_specs = [_row_spec(tm, 512)] * 4 + [_row_spec(tm, LANES)] + [_row_spec(tm, 512)] * 7 + [_row_spec(tm, LANES)]
    return pl.pallas_call(
        _proj_even_kernel,
        grid=(rows // tm,),
        in_specs=[_row_spec(tm, d), _const_spec(g.shape), _const_spec(w_all.shape), _const_spec(gn.shape),
                  _const_spec(bd.shape), _row_spec(tm, LANES), _row_spec(tm, LANES)],
        out_specs=out_specs,
        out_shape=out_shape,
        compiler_params=_params(("parallel",)),
        name="proj_even",
    )(h, g, w_all, gn, bd, cos, sin)


def _proj_odd(h, g, w_all, gn, bf, bd):
    rows, d = h.shape
    tm = _row_tile(rows, 256)
    f = lambda w, dt: jax.ShapeDtypeStruct((rows, w), dt)
    out_shape = [f(1024, F32)] * 2 + [f(LANES, F32)] + [f(1024, CDT)] * 3
    out_specs = [_row_spec(tm, 1024)] * 2 + [_row_spec(tm, LANES)] + [_row_spec(tm, 1024)] * 3
    return pl.pallas_call(
        _proj_odd_kernel,
        grid=(rows // tm,),
        in_specs=[_row_spec(tm, d), _const_spec(g.shape), _const_spec(w_all.shape), _const_spec(gn.shape),
                  _const_spec(bf.shape), _const_spec(bd.shape)],
        out_specs=out_specs,
        out_shape=out_shape,
        compiler_params=_params(("parallel",)),
        name="proj_odd",
    )(h, g, w_all, gn, bf, bd)


def _post_kernel(*refs, n_o, ff_chunk):
    h_ref = refs[0]
    o_refs = refs[1:1 + n_o]
    p_ref = refs[1 + n_o]
    wo_refs = refs[2 + n_o:2 + 2 * n_o]
    gm_ref, wu_ref, wd_ref, gp_ref, wg_ref, wp_ref, out_ref = refs[2 + 2 * n_o:]
    mix = None
    for o_ref, wo_ref in zip(o_refs, wo_refs):
        t = _mm(o_ref[...], wo_ref[...])
        mix = t if mix is None else mix + t
    h = h_ref[...] + mix
    m = _rms(h, gm_ref[...]).astype(CDT)
    mlp = None
    d_ff = wu_ref.shape[1]
    for c in range(d_ff // ff_chunk):
        sl = slice(c * ff_chunk, (c + 1) * ff_chunk)
        u = jnp.square(jnp.maximum(_mm(m, wu_ref[:, sl]), 0.0)).astype(CDT)
        t = _mm(u, wd_ref[sl, :])
        mlp = t if mlp is None else mlp + t
    acc = h + mlp
    gate_in = _rms(acc, gp_ref[...]).astype(CDT)
    gate = 1.0 / (1.0 + jnp.exp(-_mm(gate_in, wg_ref[...])))
    out_ref[...] = acc + gate * _mm(p_ref[...].astype(CDT), wp_ref[...])


def _post(h, os_, p, wos, gm, wu, wd, gp, wg, wp):
    rows, d = h.shape
    tm = _row_tile(rows, 512)
    n_o = len(os_)
    in_specs = ([_row_spec(tm, d)] + [_row_spec(tm, o.shape[1]) for o in os_] + [_row_spec(tm, p.shape[1])]
                + [_const_spec(w.shape) for w in wos]
                + [_const_spec(x.shape) for x in (gm, wu, wd, gp, wg, wp)])
    return pl.pallas_call(
        functools.partial(_post_kernel, n_o=n_o, ff_chunk=1024),
        grid=(rows // tm,),
        in_specs=in_specs,
        out_specs=_row_spec(tm, d),
        out_shape=jax.ShapeDtypeStruct((rows, d), F32),
        compiler_params=_params(("parallel",)),
        name="post",
    )(h, *os_, p, *wos, gm, wu, wd, gp, wg, wp)


def _pair_stack(qp, tq):
    lane = lax.broadcasted_iota(I32, qp.shape, 1)
    zero = jnp.zeros_like(qp)
    return jnp.concatenate([jnp.where(lane < 64, qp, zero), jnp.where(lane >= 64, qp, zero)], axis=0)


def _pair_merge(o2, tq):
    lane = lax.broadcasted_iota(I32, (tq, LANES), 1)
    return jnp.where(lane < 64, o2[:tq], o2[tq:])


def _last_tile(i, tq, tk, qoff, length, chunked):
    qend = qoff + i * tq + tq - 1
    kmax = (qend // CHUNK + 1) * CHUNK if chunked else qend + 1
    kmax = jnp.minimum(kmax, length)
    return (kmax - 1) // tk


def _tri_steps(nq, tq, tk, qoff, length, chunked):
    ii, jj = [], []
    for i in range(nq):
        qend = qoff + i * tq + tq - 1
        kmax = min((qend // CHUNK + 1) * CHUNK if chunked else qend + 1, length)
        for j in range((kmax - 1) // tk + 1):
            ii.append(i)
            jj.append(j)
    return np.asarray(ii, np.int32), np.asarray(jj, np.int32)


def _tri_call(kernel_fn, steps, batch, in_specs, out_specs, out_shape, scratch, name, extra=()):
    ii, jj = steps
    grid_spec = pltpu.PrefetchScalarGridSpec(
        num_scalar_prefetch=2 + len(extra), grid=(batch, ii.shape[0]),
        in_specs=in_specs, out_specs=out_specs, scratch_shapes=scratch)
    call = pl.pallas_call(kernel_fn, grid_spec=grid_spec, out_shape=out_shape,
                          compiler_params=_params(("parallel", "arbitrary")), name=name)
    return functools.partial(call, jnp.asarray(ii), jnp.asarray(jj), *extra)


def _flash_init(q_ref, aug_ref, qst_ref, m_ref, l_ref, acc_ref, tq):
    q = q_ref[0]
    for g in range(qst_ref.shape[0]):
        st = _pair_stack(q[:, g * LANES:(g + 1) * LANES], tq)
        if aug_ref is not None:
            a = aug_ref[0, 0]
            ext = jnp.concatenate([jnp.broadcast_to(a[2 * g:2 * g + 1], (tq, LANES)),
                                   jnp.broadcast_to(a[2 * g + 1:2 * g + 2], (tq, LANES))], axis=0)
            st = jnp.concatenate([st, ext.astype(CDT)], axis=1)
        qst_ref[g] = st
    m_ref[...] = jnp.full(m_ref.shape, NEG, F32)
    l_ref[...] = jnp.zeros(l_ref.shape, F32)
    acc_ref[...] = jnp.zeros(acc_ref.shape, F32)


def _flash_pairs(score_fn, vt_ref, scr, tq):
    _, m_ref, l_ref, acc_ref, s_ref = scr
    npair = m_ref.shape[0]
    s_ref[0] = score_fn(0)
    for g in range(npair):
        if g + 1 < npair:
            s_ref[(g + 1) % 2] = score_fn(g + 1)
        _flash_step(s_ref[g % 2], vt_ref, m_ref, l_ref, acc_ref, g, tq)


def _flash_step(st, vt_ref, m_ref, l_ref, acc_ref, g, tq):
    m_old = m_ref[g]
    m_new = jnp.maximum(m_old, jnp.max(st, axis=0, keepdims=True))
    alpha = jnp.exp2(m_old - m_new)
    m_ref[g] = m_new
    pc = jnp.exp2(st - m_new).astype(CDT)
    ones = jnp.ones((PACKED_ROWS, st.shape[0]), CDT)
    sums = []
    for e in range(2):
        rows = slice(g * LANES + e * HEAD_DIM, g * LANES + (e + 1) * HEAD_DIM)
        cols = slice(e * tq, (e + 1) * tq)
        pv = _mm(jnp.concatenate([vt_ref[0, rows, :], ones], axis=0), pc[:, cols])
        acc_ref[rows, :] = alpha[:, cols] * acc_ref[rows, :] + pv[:HEAD_DIM]
        sums.append(pv[HEAD_DIM:HEAD_DIM + 1])
    l_ref[g] = alpha * l_ref[g] + jnp.concatenate(sums, axis=1)


def _flash_finish(o_ref, l_ref, acc_ref, tq):
    for g in range(l_ref.shape[0]):
        l = l_ref[g]
        den = jnp.concatenate([jnp.broadcast_to(l[:, :tq], (HEAD_DIM, tq)),
                               jnp.broadcast_to(l[:, tq:], (HEAD_DIM, tq))], axis=0)
        o = acc_ref[g * LANES:(g + 1) * LANES, :] / den
        o_ref[0, :, g * LANES:(g + 1) * LANES] = o.T.astype(o_ref.dtype)


def _flash_scratch(npair, tq, tk, kd, w):
    return [pltpu.VMEM((npair, 2 * tq, kd), CDT),
            pltpu.VMEM((npair, 1, 2 * tq), F32),
            pltpu.VMEM((npair, 1, 2 * tq), F32),
            pltpu.VMEM((w, tq), F32),
            pltpu.VMEM((2, tk, 2 * tq), F32)]


def _band_kernel(*refs, n_kv, tq, mask_neg):
    q_ref = refs[0]
    k_refs = refs[1:1 + n_kv]
    v_refs = refs[1 + n_kv:1 + 2 * n_kv]
    bias_ref = refs[1 + 2 * n_kv]
    o_ref = refs[2 + 2 * n_kv]
    i = pl.program_id(1)
    q = q_ref[0]
    cat = lambda rs: (rs[0][0].astype(CDT) if len(rs) == 1
                      else jnp.concatenate([r[0].astype(CDT) for r in rs], axis=0))
    k = cat(k_refs)
    v = cat(v_refs)
    tk = bias_ref.shape[2]
    if k.shape[0] < tk:
        zpad = jnp.zeros((tk - k.shape[0], k.shape[1]), CDT)
        k = jnp.concatenate([k, zpad], axis=0)
        v = jnp.concatenate([v, zpad], axis=0)
    if mask_neg:
        kpos = lax.broadcasted_iota(I32, (2 * tq, tk), 1) + (i * tq - A_PAST)
        neg = kpos < 0
    for g in range(q.shape[1] // LANES):
        sl = slice(g * LANES, (g + 1) * LANES)
        s = _mm_nt(_pair_stack(q[:, sl], tq), k[:, sl])
        s = s + jnp.concatenate([bias_ref[2 * g], bias_ref[2 * g + 1]], axis=0)
        if mask_neg:
            s = jnp.where(neg, NEG, s)
        m = jnp.max(s, axis=-1, keepdims=True)
        p = jnp.exp(s - m)
        l = jnp.sum(p, axis=-1, keepdims=True)
        o2 = _mm(p.astype(CDT), v[:, sl]) / l
        o_ref[0, :, sl] = _pair_merge(o2, tq).astype(o_ref.dtype)


def _band(q, k_parts, v_parts, k_specs, bias, tq, mask_neg):
    b, s, w = q.shape
    n_kv = len(k_parts)
    q_spec = pl.BlockSpec((1, tq, w), lambda bb, i: (bb, i, 0))
    return pl.pallas_call(
        functools.partial(_band_kernel, n_kv=n_kv, tq=tq, mask_neg=mask_neg),
        grid=(b, s // tq),
        in_specs=[q_spec] + k_specs + k_specs + [_const_spec(bias.shape)],
        out_specs=q_spec,
        out_shape=jax.ShapeDtypeStruct((b, s, w), CDT),
        compiler_params=_params(("parallel", "parallel")),
        name="band_attn",
    )(q, *k_parts, *v_parts, bias)


def _band_bias(rel_bias, q_pos, k_pos):
    nq, nk = len(q_pos), len(k_pos)
    qc = q_pos[:, None] // CHUNK
    kc = k_pos[None, :] // CHUNK
    ok = (kc <= qc) & (kc >= qc - A_PAST // CHUNK)
    n = nq + nk - 1
    d0 = int(q_pos[0] - k_pos[0])
    m = np.arange(n + 1)
    rel = np.where(m < nk, d0 - m, d0 - m + n + 1)
    table = jnp.take(rel_bias.astype(F32), jnp.asarray(np.clip(rel, -REL_CLIP, REL_CLIP) + REL_CLIP), axis=1)
    bias = jnp.tile(table, (1, nq))[:, :nq * n].reshape(-1, nq, n)[:, :, :nk]
    return jnp.where(jnp.asarray(ok)[None], bias, NEG)


def _dsa_index_kernel(ii_ref, jj_ref, qi_ref, ki2_ref, wt_ref, keys_ref, t_ref, c_ref, qst_ref, kscr_ref, k16_ref,
                      *, tq, tk, nk, unroll, qoff, length, topk, idx_bits, nreal):
    i = ii_ref[pl.program_id(1)]
    j = jj_ref[pl.program_id(1)]
    last = _last_tile(i, tq, tk, qoff, length, True)
    q0 = qoff + i * tq

    @pl.when(j == 0)
    def _():
        q = qi_ref[0]
        for g in range(IDX_HEADS // 2):
            qst_ref[2 * g * tq:(2 * g + 2) * tq, :] = _pair_stack(q[:, g * LANES:(g + 1) * LANES], tq)

    @pl.when(j <= last)
    def _():
        lg = _mm_nt(ki2_ref[0], qst_ref[...])
        wgt = wt_ref[0] * (IDX_HEADS ** -0.5)
        sc = jnp.zeros((tk, tq), F32)
        for h in range(IDX_HEADS):
            sc = sc + jnp.maximum(lg[:, h * tq:(h + 1) * tq], 0.0) * wgt[h:h + 1, :]
        sc = jnp.where(sc == 0.0, 0.0, sc)
        bits = lax.bitcast_convert_type(sc, I32)
        key = bits ^ ((bits >> 31) & 0x7FFFFFFF)
        kpos = j * tk + lax.broadcasted_iota(I32, (tk, tq), 0)
        qpos = q0 + lax.broadcasted_iota(I32, (tk, tq), 1)
        ok = ((kpos >> CHUNK_SHIFT) <= (qpos >> CHUNK_SHIFT)) & (kpos < length)
        key = jnp.where(ok, key, INT_MIN)
        keys_ref[0] = key
        kscr_ref[j] = key
        top = lax.bitcast_convert_type(bits & -65536, F32)
        k16_ref[j] = jnp.where(ok, top, -jnp.inf).astype(jnp.bfloat16)
        if unroll > 1:
            @pl.when(j + 1 < nk)
            def _():
                kscr_ref[j + 1] = jnp.full((tk, tq), INT_MIN, I32)
                k16_ref[j + 1] = jnp.full((tk, tq), -jnp.inf, jnp.bfloat16)

    @pl.when(j == last)
    def _():
        def count(pred):
            def body(t, acc):
                for u in range(unroll):
                    tt = unroll * t + u
                    kpos = tt * tk + lax.broadcasted_iota(I32, (tk, tq), 0)
                    m = jnp.where(pred(kscr_ref[tt], kpos), 1.0, 0.0)
                    acc = acc + jnp.sum(m, axis=0, keepdims=True)
                return acc
            return lax.fori_loop(0, (last + unroll) // unroll, body, jnp.zeros((1, tq), F32))

        kf = float(topk)
        qpos = q0 + lax.broadcasted_iota(I32, (1, tq), 1)
        n_adm = jnp.minimum(((qpos >> CHUNK_SHIFT) + 1) << CHUNK_SHIFT, length).astype(F32)
        real = qpos - qoff < nreal
        take_all = (n_adm < kf) | jnp.logical_not(real)

        def count16(c16):
            one, zero = jnp.ones((), jnp.bfloat16), jnp.zeros((), jnp.bfloat16)
            nslab = tk // PACKED_ROWS
            assert nslab <= 256

            def body(t, acc):
                for u in range(unroll):
                    m = jnp.where(k16_ref[unroll * t + u] >= c16, one, zero).reshape(nslab, PACKED_ROWS, tq)
                    part = m[0]
                    for r in range(1, nslab):
                        part = part + m[r]
                    acc = acc + jnp.sum(part.astype(F32), axis=0, keepdims=True)
                return acc
            return lax.fori_loop(0, (last + unroll) // unroll, body, jnp.zeros((1, tq), F32))

        def bit_step(it, prefix, nge, top_half):
            cand = prefix + lax.shift_left(jnp.int32(1), 31 - it)
            if top_half:
                cbits = cand ^ ((cand >> 31) & 0x7FFFFFFF)
                cbits = jnp.where((cand > 0) & (cand < 0x00800000), 0x00800000, cbits)
                cnt = count16(lax.bitcast_convert_type(cbits & -65536, F32).astype(jnp.bfloat16))
            else:
                cnt = count(lambda kt, _: kt >= cand)
            take = cnt >= kf
            return jnp.where(take, cand, prefix), jnp.where(take, cnt, nge)

        def all_settled(nge):
            return (jnp.min(jnp.where(take_all | (nge == kf), 1.0, 0.0)) > 0.0).astype(I32)

        def low_body(carry):
            it, prefix, nge, _ = carry
            prefix, nge = bit_step(it, prefix, nge, False)
            return it + 1, prefix, nge, all_settled(nge)

        prefix, nge = lax.fori_loop(0, 16, lambda it, c: bit_step(it, c[0], c[1], True),
                                    (jnp.full((1, tq), INT_MIN, I32), jnp.zeros((1, tq), F32)))
        _, prefix, nge, _ = lax.while_loop(lambda c: (c[0] < 32) & (c[3] == 0), low_body,
                                           (jnp.int32(16), prefix, nge, all_settled(nge)))
        found = prefix > INT_MIN
        thr = jnp.maximum(prefix, INT_MIN + 1)
        t_ref[0] = thr
        c_ref[0] = jnp.full((1, tq), 2 ** 30, I32)
        straddle = jnp.max(jnp.where(found & (nge > kf) & real, 1.0, 0.0))

        @pl.when(straddle > 0.0)
        def _():
            need = kf - count(lambda kt, _: kt > thr)

            def tie_body(it, x):
                cx = x + lax.shift_left(jnp.int32(1), idx_bits - 1 - it)
                g = count(lambda kt, kpos: jnp.where(kt == thr, kpos, INT_MAX) < cx)
                return jnp.where(g < need, cx, x)
            c_ref[0] = lax.fori_loop(0, idx_bits, tie_body, jnp.zeros((1, tq), I32))


def _dsa_index(qi, ki2, wt, *, tq, tk, qoff, length, topk, nreal):
    b, s, _ = qi.shape
    lp = ki2.shape[1]
    nq, nk = s // tq, lp // tk
    idx_bits = max(1, int(np.ceil(np.log2(lp))))
    qvec = pl.BlockSpec((1, 1, tq), lambda bb, t, ii, jj: (bb, 0, ii[t]))
    return _tri_call(
        functools.partial(_dsa_index_kernel, tq=tq, tk=tk, nk=nk, unroll=2 if nk % 2 == 0 else 1,
                          qoff=qoff, length=length, topk=topk, idx_bits=idx_bits, nreal=nreal),
        _tri_steps(nq, tq, tk, qoff, length, True), b,
        in_specs=[pl.BlockSpec((1, tq, 512), lambda bb, t, ii, jj: (bb, ii[t], 0)),
                  pl.BlockSpec((1, tk, LANES), lambda bb, t, ii, jj: (bb, jj[t], 0)),
                  pl.BlockSpec((1, IDX_HEADS, tq), lambda bb, t, ii, jj: (bb, 0, ii[t]))],
        out_specs=[pl.BlockSpec((1, tk, tq), lambda bb, t, ii, jj: (bb, jj[t], ii[t])), qvec, qvec],
        out_shape=[jax.ShapeDtypeStruct((b, lp, s), I32),
                   jax.ShapeDtypeStruct((b, 1, s), I32),
                   jax.ShapeDtypeStruct((b, 1, s), I32)],
        scratch=[pltpu.VMEM((IDX_HEADS * tq, LANES), CDT),
                 pltpu.VMEM((nk, tk, tq), I32),
                 pltpu.VMEM((nk, tk, tq), jnp.bfloat16)],
        name="dsa_index",
    )(qi, ki2, wt)


def _dsa_attn_kernel(ii_ref, jj_ref, q_ref, k_ref, vt_ref, keys_ref, t_ref, c_ref, o_ref, *scr,
                     tq, tk, qoff, length):
    qst_ref, m_ref, l_ref, acc_ref = scr[:4]
    i = ii_ref[pl.program_id(1)]
    j = jj_ref[pl.program_id(1)]
    last = _last_tile(i, tq, tk, qoff, length, True)

    @pl.when(j == 0)
    def _():
        _flash_init(q_ref, None, qst_ref, m_ref, l_ref, acc_ref, tq)

    @pl.when(j <= last)
    def _():
        kt = keys_ref[0]
        thr = t_ref[0]
        kpos = j * tk + lax.broadcasted_iota(I32, (tk, tq), 0)
        sel = (kt > thr) | (jnp.where(kt == thr, kpos, INT_MAX) <= c_ref[0])
        mb = jnp.where(sel, 0.0, NEG)
        mb2 = jnp.concatenate([mb, mb], axis=1)
        score = lambda g: _mm_nt(k_ref[0, :, g * LANES:(g + 1) * LANES], qst_ref[g]) + mb2
        _flash_pairs(score, vt_ref, scr, tq)

    @pl.when(j == last)
    def _():
        _flash_finish(o_ref, l_ref, acc_ref, tq)


def _dsa_attn(q, k, vt, keys, thr, cut, *, tq, tk, qoff, length):
    b, s, w = q.shape
    qmap = lambda bb, t, ii, jj: (bb, ii[t], 0)
    qvec = pl.BlockSpec((1, 1, tq), lambda bb, t, ii, jj: (bb, 0, ii[t]))
    npair = w // LANES
    return _tri_call(
        functools.partial(_dsa_attn_kernel, tq=tq, tk=tk, qoff=qoff, length=length),
        _tri_steps(s // tq, tq, tk, qoff, length, True), b,
        in_specs=[pl.BlockSpec((1, tq, w), qmap),
                  pl.BlockSpec((1, tk, w), lambda bb, t, ii, jj: (bb, jj[t], 0)),
                  pl.BlockSpec((1, w, tk), lambda bb, t, ii, jj: (bb, 0, jj[t])),
                  pl.BlockSpec((1, tk, tq), lambda bb, t, ii, jj: (bb, jj[t], ii[t])),
                  qvec, qvec],
        out_specs=pl.BlockSpec((1, tq, w), qmap),
        out_shape=jax.ShapeDtypeStruct((b, s, w), CDT),
        scratch=_flash_scratch(npair, tq, tk, LANES, w),
        name="dsa_attn",
    )(q, k, vt, keys, thr, cut)


def _pieces3(x):
    pcs = _pieces(x, 3)
    return pcs + [jnp.zeros_like(pcs[0])] * (3 - len(pcs))


FOX_EXT = 9


def _cumsum_kernel(x_ref, tri_ref, place_ref, ones_ref, kx_ref, carry_ref, *, scale):
    @pl.when(pl.program_id(1) == 0)
    def _():
        carry_ref[...] = jnp.zeros(carry_ref.shape, F32)
    tri = tri_ref[...]
    tot = None
    for p in _pieces(x_ref[0], 3):
        t = _mm(tri, p)
        tot = t if tot is None else tot + t
    acc = carry_ref[...] + tot
    carry_ref[...] = acc[-1:, :]
    ext = None
    for c, p in enumerate(_pieces3(acc * (-scale))):
        t = _mm(p, place_ref[c])
        ext = t if ext is None else ext + t
    kx_ref[0] = (ext + ones_ref[...]).astype(kx_ref.dtype)


def _fox_key_lanes(x, tb, scale, nh):
    b, s, w = x.shape
    npair = nh // 2
    place = np.zeros((3, w, npair * LANES), np.float32)
    ones = np.zeros((1, npair * LANES), np.float32)
    for h in range(nh):
        for c in range(3):
            place[c, h, (h // 2) * LANES + (h % 2) * 6 + c] = 1.0
    for g in range(npair):
        ones[0, g * LANES + 3:g * LANES + 6] = 1.0
    tri = jnp.tril(jnp.ones((tb, tb), F32)).astype(CDT)
    place = jnp.asarray(place, CDT)
    spec = lambda wd: pl.BlockSpec((1, tb, wd), lambda bb, i: (bb, i, 0))
    return pl.pallas_call(
        functools.partial(_cumsum_kernel, scale=scale),
        grid=(b, s // tb),
        in_specs=[spec(w), _const_spec(tri.shape), _const_spec(place.shape), _const_spec(ones.shape)],
        out_specs=spec(npair * LANES),
        out_shape=jax.ShapeDtypeStruct((b, s, npair * LANES), CDT),
        scratch_shapes=[pltpu.VMEM((1, w), F32)],
        compiler_params=_params(("parallel", "arbitrary")),
        name="logf_cumsum",
    )(x, tri, place, jnp.asarray(ones))


def _fox_kernel(ii_ref, jj_ref, dead_ref, jeff_ref, q_ref, k_ref, kx_ref, vt_ref, aug_ref, o_ref, *scr,
                tq, tk, qoff, length):
    qst_ref, m_ref, l_ref, acc_ref = scr[:4]
    i = ii_ref[pl.program_id(1)]
    j = jj_ref[pl.program_id(1)]
    live = dead_ref[pl.program_id(1)] == 0
    last = _last_tile(i, tq, tk, qoff, length, False)
    q0 = qoff + i * tq

    @pl.when(j == 0)
    def _():
        _flash_init(q_ref, aug_ref, qst_ref, m_ref, l_ref, acc_ref, tq)

    def step(masked):
        if masked:
            kpos = j * tk + lax.broadcasted_iota(I32, (tk, 2 * tq), 0)
            lane = lax.broadcasted_iota(I32, (tk, 2 * tq), 1)
            causal = kpos <= q0 + jnp.where(lane >= tq, lane - tq, lane)

        def score(g):
            sl = slice(g * LANES, (g + 1) * LANES)
            st = _mm_nt(jnp.concatenate([k_ref[0, :, sl], kx_ref[0, :, sl]], axis=1), qst_ref[g])
            return jnp.where(causal, st, NEG) if masked else st
        _flash_pairs(score, vt_ref, scr, tq)

    diag = (j + 1) * tk - 1 > q0

    @pl.when(live & diag)
    def _():
        step(True)

    @pl.when(live & jnp.logical_not(diag))
    def _():
        step(False)

    @pl.when(j == last)
    def _():
        _flash_finish(o_ref, l_ref, acc_ref, tq)


FORGOTTEN_LOG2 = 150.0


def _fox_dead_tiles(kx, steps, tq, tk, qoff, qk_bound):
    ii, jj = steps
    nq, nk = int(ii.max()) + 1, int(jj.max()) + 1

    def neg_f(rows):
        a = kx[0, rows].reshape(len(rows), -1, LANES).astype(F32)
        return jnp.stack([a[..., 0:3].sum(-1), a[..., 6:9].sum(-1)], axis=-1).reshape(len(rows), -1)

    q0 = qoff + np.arange(nq) * tq
    ends = (np.arange(nk) + 1) * tk - 1
    drop = neg_f(q0)[:, None, :] - neg_f(ends)[None, :, :]
    dead = (jnp.min(drop, axis=-1) > 2.0 * qk_bound + FORGOTTEN_LOG2) & jnp.asarray(ends[None, :] < q0[:, None])
    lead = jnp.cumprod(dead.astype(I32), axis=1)
    first_live = jnp.sum(lead, axis=1)
    return lead[ii, jj], jnp.maximum(jnp.asarray(jj), first_live[ii])


def _fox(q, k, kx, vt, qaug, *, tq, tk, qoff, length, qk_bound=None):
    b, s, w = q.shape
    npair = w // LANES
    steps = _tri_steps(s // tq, tq, tk, qoff, length, False)
    if qk_bound is None or b != 1:
        tables = (jnp.zeros(steps[0].shape, I32), jnp.asarray(steps[1]))
    else:
        tables = _fox_dead_tiles(kx, steps, tq, tk, qoff, qk_bound)
    qmap = lambda bb, t, ii, jj, dead, jeff: (bb, ii[t], 0)
    kmap = lambda bb, t, ii, jj, dead, jeff: (bb, jeff[t], 0)
    return _tri_call(
        functools.partial(_fox_kernel, tq=tq, tk=tk, qoff=qoff, length=length),
        steps, b,
        in_specs=[pl.BlockSpec((1, tq, w), qmap),
                  pl.BlockSpec((1, tk, w), kmap),
                  pl.BlockSpec((1, tk, npair * LANES), kmap),
                  pl.BlockSpec((1, w, tk), lambda bb, t, ii, jj, dead, jeff: (bb, 0, jeff[t])),
                  pl.BlockSpec((1, 1, 2 * npair, LANES), lambda bb, t, ii, jj, dead, jeff: (bb, ii[t], 0, 0))],
        out_specs=pl.BlockSpec((1, tq, w), qmap),
        out_shape=jax.ShapeDtypeStruct((b, s, w), CDT),
        scratch=_flash_scratch(npair, tq, tk, 2 * LANES, w),
        name="fox_attn", extra=tables,
    )(q, k, kx, vt, qaug)


def _fox_qaug(kx0):
    b, nq, _ = kx0.shape
    a = kx0.reshape(b, nq, -1, LANES).astype(F32)
    one, zero = jnp.ones_like(a[..., 0:3]), jnp.zeros_like(a[..., 0:3])
    even = jnp.concatenate([one, -a[..., 0:3], zero], axis=-1)
    odd = jnp.concatenate([zero, -a[..., 6:9], one], axis=-1)
    heads = jnp.stack([even, odd], axis=3).reshape(b, nq, -1, FOX_EXT)
    return jnp.pad(heads, ((0, 0), (0, 0), (0, 0), (0, LANES - FOX_EXT)))


def _rope_tables(pos):
    half = HEAD_DIM // 2
    inv = ROPE_THETA ** (-jnp.arange(half, dtype=F32) / half)
    ang = pos.astype(F32)[:, None] * inv[None, :]
    cos, sin = jnp.cos(ang), jnp.sin(ang)
    return (jnp.tile(jnp.concatenate([cos, cos], axis=1), (1, 2)),
            jnp.tile(jnp.concatenate([-sin, sin], axis=1), (1, 2)))


def _pad_rows(x, n):
    return x if x.shape[1] == n else jnp.pad(x, ((0, 0), (0, n - x.shape[1]), (0, 0)))


def _tiles(s):
    tk = 512 if s % 512 == 0 else s
    tq = 512 if s % 512 == 0 else s
    tqi = 256 if s % 256 == 0 else s
    tqf = 1024 if s % 1024 == 0 else tq
    return tq, tqi, tk, tqf


def kernel(x_prompt, x_sample, cache_a_k, cache_a_v, cache_b_k, cache_b_v, cache_b_ik, cache_c_k, cache_c_v,
           cache_c_logf, p_prompt, p_sample, g_mix, w_in_even, qn_a, kn_a, rel_bias_a, qn_b, kn_b, w_out_even,
           w_in_odd, b_f, qn_c, kn_c, w_out_odd, g_mlp, w_up, w_down, g_ple, w_ple_gate, w_ple_proj):
    B, S, D = x_prompt.shape
    DB, n, _ = x_sample.shape
    P = cache_b_k.shape[2]
    W_A = cache_a_k.shape[2]
    W_AP = min(A_PAST, S)
    depth = g_mix.shape[0]
    L = P + n
    Lp = -(-L // LANES) * LANES
    nqp = -(-n // LANES) * LANES
    assert B == 1 and S % 128 == 0 and n % 16 == 0 and P % 16 == 0

    row = lambda v: v.astype(F32).reshape(1, -1)
    bd = jnp.asarray(np.kron(np.eye(MXU_TILE // HEAD_DIM), np.full((HEAD_DIM, HEAD_DIM), 1.0 / HEAD_DIM)), CDT)
    cos_p, sin_p = _rope_tables(jnp.arange(S))
    cos_s, sin_s = (jnp.tile(t, (DB, 1)) for t in _rope_tables(P + jnp.arange(n)))
    tq, tqi, tk, tqf = _tiles(S)
    tr = lambda t: jnp.swapaxes(t, 1, 2)
    catp = lambda c, new: _pad_rows(jnp.concatenate([c.astype(CDT), new], axis=1), Lp)

    outs = {name: [] for name in ("a_k_p", "a_v_p", "b_k_p", "b_v_p", "b_ik_p", "c_k_p", "c_v_p", "c_lf_p",
                                  "a_k_s", "a_v_s", "b_k_s", "b_v_s", "b_ik_s", "c_k_s", "c_v_s", "c_lf_s")}
    h_p = x_prompt.reshape(B * S, D)
    h_s = x_sample.reshape(DB * n, D)
    for i in range(depth):
        g = row(g_mix[i])
        if i % 2 == 0:
            e = i // 2
            wsz = 512
            w = w_in_even[e]
            w_all = jnp.concatenate([w[:, :7 * wsz], jnp.pad(w[:, 7 * wsz:], ((0, 0), (0, LANES - 72)))],
                                    axis=1).astype(CDT)
            gn = jnp.stack([jnp.tile(v[e].astype(F32), wsz // HEAD_DIM) for v in (qn_a, kn_a, qn_b, kn_b)])
            (ka, va, kb, vb, kw, qa_c, ka_c, va_c, qb_c, kb_c, vb_c, qi_c, ki2_c) = _proj_even(
                h_p, g, w_all, gn, bd, cos_p, sin_p)
            r3 = lambda t: t.reshape(B, S, t.shape[-1])
            tqa = 256 if S % 256 == 0 else 128
            npc = (A_PAST + tqa) // tqa
            k_specs = [pl.BlockSpec((1, tqa, wsz), functools.partial(
                lambda bb, ii, pp: (bb, jnp.maximum(ii + pp - A_PAST // tqa, 0), 0), pp=pp)) for pp in range(npc)]
            bias_p = _band_bias(rel_bias_a[e], np.arange(tqa), np.arange(-A_PAST, tqa))
            oa = _band(r3(qa_c), [r3(ka_c)] * npc, [r3(va_c)] * npc, k_specs, bias_p, tqa, True)
            topk_p = min(TOPK_MAX, S // 4)
            wt = tr(r3(kw)[..., 64:64 + IDX_HEADS])
            keys, thr, cut = _dsa_index(r3(qi_c), r3(ki2_c), wt, tq=tqi, tk=tk, qoff=0, length=S, topk=topk_p,
                                        nreal=S)
            ob = _dsa_attn(r3(qb_c), r3(kb_c), tr(r3(vb_c)), keys, thr, cut, tq=tq, tk=tk, qoff=0, length=S)
            os_p = [oa.reshape(B * S, wsz), ob.reshape(B * S, wsz)]
            hd = lambda t: t.reshape(B, S, -1, HEAD_DIM)
            outs["a_k_p"].append(hd(ka)[:, S - W_AP:]); outs["a_v_p"].append(hd(va)[:, S - W_AP:])
            outs["b_k_p"].append(hd(kb)); outs["b_v_p"].append(hd(vb))
            outs["b_ik_p"].append(kw[:, :64].reshape(B, S, 64))
            (ka, va, kb, vb, kw, qa_c, ka_c, va_c, qb_c, kb_c, vb_c, qi_c, ki2_c) = _proj_even(
                h_s, g, w_all, gn, bd, cos_s, sin_s)
            r3 = lambda t: t.reshape(DB, n, t.shape[-1])
            ca_k = cache_a_k[e].reshape(DB, W_A, wsz)
            ca_v = cache_a_v[e].reshape(DB, W_A, wsz)
            full = lambda rows, wd: pl.BlockSpec((1, rows, wd), lambda bb, ii: (bb, 0, 0))
            bias_s = _band_bias(rel_bias_a[e], P + np.arange(n), P - W_A + np.arange(W_A + n))
            bias_s = jnp.pad(bias_s, ((0, 0), (0, 0), (0, -(W_A + n) % LANES)), constant_values=NEG)
            oa = _band(r3(qa_c), [ca_k, r3(ka)], [ca_v, r3(va)], [full(W_A, wsz), full(n, wsz)], bias_s, n, False)
            ik = cache_b_ik[e].astype(CDT)
            ki2_all = catp(jnp.concatenate([ik, ik], axis=-1), r3(ki2_c))
            kb_all = catp(cache_b_k[e].reshape(DB, P, wsz), r3(kb_c))
            vb_all = catp(cache_b_v[e].reshape(DB, P, wsz), r3(vb_c))
            topk_s = min(TOPK_MAX, L // 4)
            wt = _pad_rows(r3(kw)[..., 64:64 + IDX_HEADS], nqp)
            keys, thr, cut = _dsa_index(_pad_rows(r3(qi_c), nqp), ki2_all, tr(wt),
                                        tq=nqp, tk=Lp, qoff=P, length=L, topk=topk_s, nreal=n)
            ob = _dsa_attn(_pad_rows(r3(qb_c), nqp), kb_all, tr(vb_all), keys, thr, cut,
                           tq=nqp, tk=Lp, qoff=P, length=L)[:, :n]
            os_s = [oa.reshape(DB * n, wsz), ob.reshape(DB * n, wsz)]
            hd = lambda t: t.reshape(DB, n, -1, HEAD_DIM)
            outs["a_k_s"].append(jnp.concatenate([cache_a_k[e], hd(ka)], axis=1)[:, n:])
            outs["a_v_s"].append(jnp.concatenate([cache_a_v[e], hd(va)], axis=1)[:, n:])
            outs["b_k_s"].append(hd(kb)); outs["b_v_s"].append(hd(vb))
            outs["b_ik_s"].append(kw[:, :64].reshape(DB, n, 64))
            w_out = w_out_even[e].astype(CDT)
            wos = [w_out[:wsz], w_out[wsz:]]
        else:
            o = i // 2
            nh = b_f.shape[1]
            wsz = nh * HEAD_DIM
            w = w_in_odd[o]
            w_all = jnp.concatenate([w[:, :3 * wsz], jnp.pad(w[:, 3 * wsz:], ((0, 0), (0, LANES - nh)))],
                                    axis=1).astype(CDT)
            gn = jnp.stack([jnp.tile(v[o].astype(F32), wsz // HEAD_DIM) for v in (qn_c, kn_c)])
            bf = jnp.pad(b_f[o].astype(F32), (0, LANES - nh)).reshape(1, LANES)
            k, v, lf, q_c, k_c, v_c = _proj_odd(h_p, g, w_all, gn, bf, bd)
            r3 = lambda t: t.reshape(B, S, t.shape[-1])
            kx = _fox_key_lanes(r3(lf), 512 if S % 512 == 0 else S, LOG2E, nh)
            qk_bound = (1.05 * HEAD_DIM * (HEAD_DIM ** -0.5 * LOG2E)
                        * jnp.max(jnp.abs(qn_c[o].astype(F32))) * jnp.max(jnp.abs(kn_c[o].astype(F32))))
            oc = _fox(r3(q_c), r3(k_c), kx, tr(r3(v_c)), _fox_qaug(kx[:, ::tqf]), tq=tqf, tk=tk, qoff=0, length=S,
                      qk_bound=qk_bound)
            os_p = [oc.reshape(B * S, wsz)]
            hd = lambda t: t.reshape(B, S, nh, HEAD_DIM)
            outs["c_k_p"].append(hd(k)); outs["c_v_p"].append(hd(v)); outs["c_lf_p"].append(lf[:, :nh].reshape(B, S, nh))
            k, v, lf, q_c, k_c, v_c = _proj_odd(h_s, g, w_all, gn, bf, bd)
            r3 = lambda t: t.reshape(DB, n, t.shape[-1])
            lf_all = jnp.concatenate([jnp.pad(cache_c_logf[o].astype(F32), ((0, 0), (0, 0), (0, LANES - nh))), r3(lf)],
                                     axis=1)
            kx = _fox_key_lanes(lf_all, L, LOG2E, nh)
            k_all = catp(cache_c_k[o].reshape(DB, P, wsz), r3(k_c))
            v_all = catp(cache_c_v[o].reshape(DB, P, wsz), r3(v_c))
            oc = _fox(_pad_rows(r3(q_c), nqp), k_all, _pad_rows(kx, Lp), tr(v_all),
                      _fox_qaug(kx[:, P:P + 1]), tq=nqp, tk=Lp, qoff=P, length=L)[:, :n]
            os_s = [oc.reshape(DB * n, wsz)]
            hd = lambda t: t.reshape(DB, n, nh, HEAD_DIM)
            outs["c_k_s"].append(hd(k)); outs["c_v_s"].append(hd(v)); outs["c_lf_s"].append(lf[:, :nh].reshape(DB, n, nh))
            w_out = w_out_odd[o].astype(CDT)
            wos = [w_out]
        post_w = (row(g_mlp[i]), w_up[i].astype(CDT), w_down[i].astype(CDT), row(g_ple[i]),
                  w_ple_gate[i].astype(CDT), w_ple_proj[i].astype(CDT))
        h_p = _post(h_p, os_p, p_prompt[i].reshape(B * S, -1), wos, *post_w)
        h_s = _post(h_s, os_s, p_sample[i].reshape(DB * n, -1), wos, *post_w)

    st = jnp.stack
    names = ("a_k_p", "a_v_p", "b_k_p", "b_v_p", "b_ik_p", "c_k_p", "c_v_p", "c_lf_p",
             "a_k_s", "a_v_s", "b_k_s", "b_v_s", "b_ik_s", "c_k_s", "c_v_s", "c_lf_s")
    return (h_p.reshape(B, S, D), h_s.reshape(DB, n, D)) + tuple(st(outs[nm]) for nm in names)
```

```python
import functools

import numpy as np
import jax
import jax.numpy as jnp
from jax import lax
from jax.experimental import pallas as pl
from jax.experimental.pallas import tpu as pltpu

F32 = jnp.float32
I32 = jnp.int32
CDT = jnp.bfloat16

CHUNK = 64
HEAD_DIM = 64
IDX_HEADS = 8
A_PAST = 8 * CHUNK
REL_CLIP = 128
TOPK_MAX = 256
ROPE_THETA = 10000.0
RMS_EPS = 1e-6
LANES = 128
MXU_TILE = 256
PACKED_ROWS = 16
CHUNK_SHIFT = CHUNK.bit_length() - 1
NEG = -1e30
INT_MIN = -(2 ** 31)
INT_MAX = 2 ** 31 - 1
LOG2E = 1.4426950408889634
VMEM_LIMIT = 56 * 1024 * 1024


def _mm(a, b):
    return jnp.dot(a, b, preferred_element_type=F32)


def _mm_nt(a, b):
    return lax.dot_general(a, b, (((1,), (1,)), ((), ())), preferred_element_type=F32)


def _pieces(x, n):
    if CDT == F32:
        return [x]
    out = []
    for _ in range(n - 1):
        p = x.astype(CDT)
        out.append(p)
        x = x - p.astype(F32)
    out.append(x.astype(CDT))
    return out


def _rms(x, g):
    return x * lax.rsqrt(jnp.mean(x * x, axis=-1, keepdims=True) + RMS_EPS) * g


def _head_rms(x, bd, g):
    x2 = x * x
    pcs = _pieces(x2, 2)
    cols = []
    for s in range(x.shape[1] // MXU_TILE):
        sl = slice(s * MXU_TILE, (s + 1) * MXU_TILE)
        ms = _mm(pcs[0][:, sl], bd)
        for p in pcs[1:]:
            ms = ms + _mm(p[:, sl], bd)
        cols.append(ms)
    ms = cols[0] if len(cols) == 1 else jnp.concatenate(cols, axis=1)
    return x * lax.rsqrt(ms + RMS_EPS) * g


def _rope(x, cos, sin):
    w = x.shape[1]
    lane = lax.broadcasted_iota(I32, x.shape, 1)
    first = (lane & 63) < 32
    swapped = jnp.where(first, pltpu.roll(x, w - 32, 1), pltpu.roll(x, 32, 1))
    return x * cos + swapped * sin


def _tile_lanes(t, w):
    return t if w == t.shape[1] else jnp.concatenate([t] * (w // t.shape[1]), axis=1)


def _proj_even_kernel(h_ref, g_ref, w_ref, gn_ref, bd_ref, cos_ref, sin_ref,
                      ka_ref, va_ref, kb_ref, vb_ref, kw_ref,
                      qa_c, ka_c, va_c, qb_c, kb_c, vb_c, qi_c, ki2_c):
    a = _rms(h_ref[...], g_ref[...]).astype(CDT)
    z = _mm(a, w_ref[...])
    w = 512
    bd = bd_ref[...]
    gn = gn_ref[...]
    cos1, sin1 = cos_ref[...], sin_ref[...]
    cos, sin = _tile_lanes(cos1, w), _tile_lanes(sin1, w)
    scale = HEAD_DIM ** -0.5

    qa = _head_rms(z[:, 0:w], bd, gn[0:1])
    qa_c[...] = (qa * scale).astype(CDT)
    ka = _head_rms(z[:, w:2 * w], bd, gn[1:2])
    ka_ref[...] = ka
    ka_c[...] = ka.astype(CDT)
    va = z[:, 2 * w:3 * w]
    va_ref[...] = va
    va_c[...] = va.astype(CDT)
    qb = _rope(_head_rms(z[:, 3 * w:4 * w], bd, gn[2:3]), cos, sin)
    qb_c[...] = (qb * (scale * LOG2E)).astype(CDT)
    kb = _rope(_head_rms(z[:, 4 * w:5 * w], bd, gn[3:4]), cos, sin)
    kb_ref[...] = kb
    kb_c[...] = kb.astype(CDT)
    vb = z[:, 5 * w:6 * w]
    vb_ref[...] = vb
    vb_c[...] = vb.astype(CDT)
    qi = _rope(z[:, 6 * w:7 * w], cos, sin)
    qi_c[...] = (qi * scale).astype(CDT)
    kw = z[:, 7 * w:7 * w + LANES]
    kwr = _rope(kw, cos1, sin1)
    lane = lax.broadcasted_iota(I32, kw.shape, 1)
    kw_ref[...] = jnp.where(lane < 64, kwr, kw)
    ki2_c[...] = jnp.where(lane < 64, kwr, pltpu.roll(kwr, 64, 1)).astype(CDT)


def _proj_odd_kernel(h_ref, g_ref, w_ref, gn_ref, bf_ref, bd_ref,
                     k_ref, v_ref, lf_ref, q_c, k_c, v_c):
    a = _rms(h_ref[...], g_ref[...]).astype(CDT)
    z = _mm(a, w_ref[...])
    w = 1024
    bd = bd_ref[...]
    gn = gn_ref[...]
    q = _head_rms(z[:, 0:w], bd, gn[0:1])
    q_c[...] = (q * (HEAD_DIM ** -0.5 * LOG2E)).astype(CDT)
    k = _head_rms(z[:, w:2 * w], bd, gn[1:2])
    k_ref[...] = k
    k_c[...] = k.astype(CDT)
    v = z[:, 2 * w:3 * w]
    v_ref[...] = v
    v_c[...] = v.astype(CDT)
    fl = z[:, 3 * w:3 * w + LANES] + bf_ref[...]
    lf_ref[...] = jnp.minimum(fl, 0.0) - jnp.log1p(jnp.exp(-jnp.abs(fl)))


def _const_spec(shape):
    nd = len(shape)
    return pl.BlockSpec(shape, lambda *_: (0,) * nd, pipeline_mode=pl.Buffered(1))


def _row_spec(tm, w):
    return pl.BlockSpec((tm, w), lambda i: (i, 0))


def _params(sem):
    return pltpu.CompilerParams(dimension_semantics=sem, vmem_limit_bytes=VMEM_LIMIT)


def _row_tile(rows, pref):
    return pref if rows % pref == 0 else rows


def _proj_even(h, g, w_all, gn, bd, cos, sin):
    rows, d = h.shape
    tm = _row_tile(rows, 256)
    f = lambda w, dt: jax.ShapeDtypeStruct((rows, w), dt)
    out_shape = [f(512, F32)] * 4 + [f(LANES, F32)] + [f(512, CDT)] * 7 + [f(LANES, CDT)]
    out_specs = [_row_spec(tm, 512)] * 4 + [_row_spec(tm, LANES)] + [_row_spec(tm, 512)] * 7 + [_row_spec(tm, LANES)]
    return pl.pallas_call(
        _proj_even_kernel,
        grid=(rows // tm,),
        in_specs=[_row_spec(tm, d), _const_spec(g.shape), _const_spec(w_all.shape), _const_spec(gn.shape),
                  _const_spec(bd.shape), _row_spec(tm, LANES), _row_spec(tm, LANES)],
        out_specs=out_specs,
        out_shape=out_shape,
        compiler_params=_params(("parallel",)),
        name="proj_even",
    )(h, g, w_all, gn, bd, cos, sin)


def _proj_odd(h, g, w_all, gn, bf, bd):
    rows, d = h.shape
    tm = _row_tile(rows, 256)
    f = lambda w, dt: jax.ShapeDtypeStruct((rows, w), dt)
    out_shape = [f(1024, F32)] * 2 + [f(LANES, F32)] + [f(1024, CDT)] * 3
    out_specs = [_row_spec(tm, 1024)] * 2 + [_row_spec(tm, LANES)] + [_row_spec(tm, 1024)] * 3
    return pl.pallas_call(
        _proj_odd_kernel,
        grid=(rows // tm,),
        in_specs=[_row_spec(tm, d), _const_spec(g.shape), _const_spec(w_all.shape), _const_spec(gn.shape),
                  _const_spec(bf.shape), _const_spec(bd.shape)],
        out_specs=out_specs,
        out_shape=out_shape,
        compiler_params=_params(("parallel",)),
        name="proj_odd",
    )(h, g, w_all, gn, bf, bd)


def _post_kernel(*refs, n_o, ff_chunk):
    h_ref = refs[0]
    o_refs = refs[1:1 + n_o]
    p_ref = refs[1 + n_o]
    wo_refs = refs[2 + n_o:2 + 2 * n_o]
    gm_ref, wu_ref, wd_ref, gp_ref, wg_ref, wp_ref, out_ref = refs[2 + 2 * n_o:]
    mix = None
    for o_ref, wo_ref in zip(o_refs, wo_refs):
        t = _mm(o_ref[...], wo_ref[...])
        mix = t if mix is None else mix + t
    h = h_ref[...] + mix
    m = _rms(h, gm_ref[...]).astype(CDT)
    mlp = None
    d_ff = wu_ref.shape[1]
    for c in range(d_ff // ff_chunk):
        sl = slice(c * ff_chunk, (c + 1) * ff_chunk)
        u = jnp.square(jnp.maximum(_mm(m, wu_ref[:, sl]), 0.0)).astype(CDT)
        t = _mm(u, wd_ref[sl, :])
        mlp = t if mlp is None else mlp + t
    acc = h + mlp
    gate_in = _rms(acc, gp_ref[...]).astype(CDT)
    gate = 1.0 / (1.0 + jnp.exp(-_mm(gate_in, wg_ref[...])))
    out_ref[...] = acc + gate * _mm(p_ref[...].astype(CDT), wp_ref[...])


def _post(h, os_, p, wos, gm, wu, wd, gp, wg, wp):
    rows, d = h.shape
    tm = _row_tile(rows, 512)
    n_o = len(os_)
    in_specs = ([_row_spec(tm, d)] + [_row_spec(tm, o.shape[1]) for o in os_] + [_row_spec(tm, p.shape[1])]
                + [_const_spec(w.shape) for w in wos]
                + [_const_spec(x.shape) for x in (gm, wu, wd, gp, wg, wp)])
    return pl.pallas_call(
        functools.partial(_post_kernel, n_o=n_o, ff_chunk=1024),
        grid=(rows // tm,),
        in_specs=in_specs,
        out_specs=_row_spec(tm, d),
        out_shape=jax.ShapeDtypeStruct((rows, d), F32),
        compiler_params=_params(("parallel",)),
        name="post",
    )(h, *os_, p, *wos, gm, wu, wd, gp, wg, wp)


def _pair_stack(qp, tq):
    lane = lax.broadcasted_iota(I32, qp.shape, 1)
    zero = jnp.zeros_like(qp)
    return jnp.concatenate([jnp.where(lane < 64, qp, zero), jnp.where(lane >= 64, qp, zero)], axis=0)


def _pair_merge(o2, tq):
    lane = lax.broadcasted_iota(I32, (tq, LANES), 1)
    return jnp.where(lane < 64, o2[:tq], o2[tq:])


def _last_tile(i, tq, tk, qoff, length, chunked):
    qend = qoff + i * tq + tq - 1
    kmax = (qend // CHUNK + 1) * CHUNK if chunked else qend + 1
    kmax = jnp.minimum(kmax, length)
    return (kmax - 1) // tk


def _tri_steps(nq, tq, tk, qoff, length, chunked):
    ii, jj = [], []
    for i in range(nq):
        qend = qoff + i * tq + tq - 1
        kmax = min((qend // CHUNK + 1) * CHUNK if chunked else qend + 1, length)
        for j in range((kmax - 1) // tk + 1):
            ii.append(i)
            jj.append(j)
    return np.asarray(ii, np.int32), np.asarray(jj, np.int32)


def _tri_call(kernel_fn, steps, batch, in_specs, out_specs, out_shape, scratch, name, extra=()):
    ii, jj = steps
    grid_spec = pltpu.PrefetchScalarGridSpec(
        num_scalar_prefetch=2 + len(extra), grid=(batch, ii.shape[0]),
        in_specs=in_specs, out_specs=out_specs, scratch_shapes=scratch)
    call = pl.pallas_call(kernel_fn, grid_spec=grid_spec, out_shape=out_shape,
                          compiler_params=_params(("parallel", "arbitrary")), name=name)
    return functools.partial(call, jnp.asarray(ii), jnp.asarray(jj), *extra)


def _flash_init(q_ref, aug_ref, qst_ref, m_ref, l_ref, acc_ref, tq):
    q = q_ref[0]
    for g in range(qst_ref.shape[0]):
        st = _pair_stack(q[:, g * LANES:(g + 1) * LANES], tq)
        if aug_ref is not None:
            a = aug_ref[0, 0]
            ext = jnp.concatenate([jnp.broadcast_to(a[2 * g:2 * g + 1], (tq, LANES)),
                                   jnp.broadcast_to(a[2 * g + 1:2 * g + 2], (tq, LANES))], axis=0)
            st = jnp.concatenate([st, ext.astype(CDT)], axis=1)
        qst_ref[g] = st
    m_ref[...] = jnp.full(m_ref.shape, NEG, F32)
    l_ref[...] = jnp.zeros(l_ref.shape, F32)
    acc_ref[...] = jnp.zeros(acc_ref.shape, F32)


def _flash_pairs(score_fn, vt_ref, scr, tq):
    _, m_ref, l_ref, acc_ref, s_ref = scr
    npair = m_ref.shape[0]
    s_ref[0] = score_fn(0)
    for g in range(npair):
        if g + 1 < npair:
            s_ref[(g + 1) % 2] = score_fn(g + 1)
        _flash_step(s_ref[g % 2], vt_ref, m_ref, l_ref, acc_ref, g, tq)


def _flash_step(st, vt_ref, m_ref, l_ref, acc_ref, g, tq):
    m_old = m_ref[g]
    m_new = jnp.maximum(m_old, jnp.max(st, axis=0, keepdims=True))
    alpha = jnp.exp2(m_old - m_new)
    m_ref[g] = m_new
    pc = jnp.exp2(st - m_new).astype(CDT)
    ones = jnp.ones((PACKED_ROWS, st.shape[0]), CDT)
    sums = []
    for e in range(2):
        rows = slice(g * LANES + e * HEAD_DIM, g * LANES + (e + 1) * HEAD_DIM)
        cols = slice(e * tq, (e + 1) * tq)
        pv = _mm(jnp.concatenate([vt_ref[0, rows, :], ones], axis=0), pc[:, cols])
        acc_ref[rows, :] = alpha[:, cols] * acc_ref[rows, :] + pv[:HEAD_DIM]
        sums.append(pv[HEAD_DIM:HEAD_DIM + 1])
    l_ref[g] = alpha * l_ref[g] + jnp.concatenate(sums, axis=1)


def _flash_finish(o_ref, l_ref, acc_ref, tq):
    for g in range(l_ref.shape[0]):
        l = l_ref[g]
        den = jnp.concatenate([jnp.broadcast_to(l[:, :tq], (HEAD_DIM, tq)),
                               jnp.broadcast_to(l[:, tq:], (HEAD_DIM, tq))], axis=0)
        o = acc_ref[g * LANES:(g + 1) * LANES, :] / den
        o_ref[0, :, g * LANES:(g + 1) * LANES] = o.T.astype(o_ref.dtype)


def _flash_scratch(npair, tq, tk, kd, w):
    return [pltpu.VMEM((npair, 2 * tq, kd), CDT),
            pltpu.VMEM((npair, 1, 2 * tq), F32),
            pltpu.VMEM((npair, 1, 2 * tq), F32),
            pltpu.VMEM((w, tq), F32),
            pltpu.VMEM((2, tk, 2 * tq), F32)]


def _band_kernel(*refs, n_kv, tq, mask_neg):
    q_ref = refs[0]
    k_refs = refs[1:1 + n_kv]
    v_refs = refs[1 + n_kv:1 + 2 * n_kv]
    bias_ref = refs[1 + 2 * n_kv]
    o_ref = refs[2 + 2 * n_kv]
    i = pl.program_id(1)
    q = q_ref[0]
    cat = lambda rs: (rs[0][0].astype(CDT) if len(rs) == 1
                      else jnp.concatenate([r[0].astype(CDT) for r in rs], axis=0))
    k = cat(k_refs)
    v = cat(v_refs)
    tk = bias_ref.shape[2]
    if k.shape[0] < tk:
        zpad = jnp.zeros((tk - k.shape[0], k.shape[1]), CDT)
        k = jnp.concatenate([k, zpad], axis=0)
        v = jnp.concatenate([v, zpad], axis=0)
    if mask_neg:
        kpos = lax.broadcasted_iota(I32, (2 * tq, tk), 1) + (i * tq - A_PAST)
        neg = kpos < 0
    for g in range(q.shape[1] // LANES):
        sl = slice(g * LANES, (g + 1) * LANES)
        s = _mm_nt(_pair_stack(q[:, sl], tq), k[:, sl])
        s = s + jnp.concatenate([bias_ref[2 * g], bias_ref[2 * g + 1]], axis=0)
        if mask_neg:
            s = jnp.where(neg, NEG, s)
        m = jnp.max(s, axis=-1, keepdims=True)
        p = jnp.exp(s - m)
        l = jnp.sum(p, axis=-1, keepdims=True)
        o2 = _mm(p.astype(CDT), v[:, sl]) / l
        o_ref[0, :, sl] = _pair_merge(o2, tq).astype(o_ref.dtype)


def _band(q, k_parts, v_parts, k_specs, bias, tq, mask_neg):
    b, s, w = q.shape
    n_kv = len(k_parts)
    q_spec = pl.BlockSpec((1, tq, w), lambda bb, i: (bb, i, 0))
    return pl.pallas_call(
        functools.partial(_band_kernel, n_kv=n_kv, tq=tq, mask_neg=mask_neg),
        grid=(b, s // tq),
        in_specs=[q_spec] + k_specs + k_specs + [_const_spec(bias.shape)],
        out_specs=q_spec,
        out_shape=jax.ShapeDtypeStruct((b, s, w), CDT),
        compiler_params=_params(("parallel", "parallel")),
        name="band_attn",
    )(q, *k_parts, *v_parts, bias)


def _band_bias(rel_bias, q_pos, k_pos):
    nq, nk = len(q_pos), len(k_pos)
    qc = q_pos[:, None] // CHUNK
    kc = k_pos[None, :] // CHUNK
    ok = (kc <= qc) & (kc >= qc - A_PAST // CHUNK)
    n = nq + nk - 1
    d0 = int(q_pos[0] - k_pos[0])
    m = np.arange(n + 1)
    rel = np.where(m < nk, d0 - m, d0 - m + n + 1)
    table = jnp.take(rel_bias.astype(F32), jnp.asarray(np.clip(rel, -REL_CLIP, REL_CLIP) + REL_CLIP), axis=1)
    bias = jnp.tile(table, (1, nq))[:, :nq * n].reshape(-1, nq, n)[:, :, :nk]
    return jnp.where(jnp.asarray(ok)[None], bias, NEG)


def _dsa_index_kernel(ii_ref, jj_ref, qi_ref, ki2_ref, wt_ref, keys_ref, t_ref, c_ref, qst_ref, kscr_ref, k16_ref,
                      *, tq, tk, nk, unroll, qoff, length, topk, idx_bits, nreal):
    i = ii_ref[pl.program_id(1)]
    j = jj_ref[pl.program_id(1)]
    last = _last_tile(i, tq, tk, qoff, length, True)
    q0 = qoff + i * tq

    @pl.when(j == 0)
    def _():
        q = qi_ref[0]
        for g in range(IDX_HEADS // 2):
            qst_ref[2 * g * tq:(2 * g + 2) * tq, :] = _pair_stack(q[:, g * LANES:(g + 1) * LANES], tq)

    @pl.when(j <= last)
    def _():
        lg = _mm_nt(ki2_ref[0], qst_ref[...])
        wgt = wt_ref[0] * (IDX_HEADS ** -0.5)
        sc = jnp.zeros((tk, tq), F32)
        for h in range(IDX_HEADS):
            sc = sc + jnp.maximum(lg[:, h * tq:(h + 1) * tq], 0.0) * wgt[h:h + 1, :]
        sc = jnp.where(sc == 0.0, 0.0, sc)
        bits = lax.bitcast_convert_type(sc, I32)
        key = bits ^ ((bits >> 31) & 0x7FFFFFFF)
        kpos = j * tk + lax.broadcasted_iota(I32, (tk, tq), 0)
        qpos = q0 + lax.broadcasted_iota(I32, (tk, tq), 1)
        ok = ((kpos >> CHUNK_SHIFT) <= (qpos >> CHUNK_SHIFT)) & (kpos < length)
        key = jnp.where(ok, key, INT_MIN)
        keys_ref[0] = key
        kscr_ref[j] = key
        top = lax.bitcast_convert_type(bits & -65536, F32)
        k16_ref[j] = jnp.where(ok, top, -jnp.inf).astype(jnp.bfloat16)
        if unroll > 1:
            @pl.when(j + 1 < nk)
            def _():
                kscr_ref[j + 1] = jnp.full((tk, tq), INT_MIN, I32)
                k16_ref[j + 1] = jnp.full((tk, tq), -jnp.inf, jnp.bfloat16)

    @pl.when(j == last)
    def _():
        def count(pred):
            def body(t, acc):
                for u in range(unroll):
                    tt = unroll * t + u
                    kpos = tt * tk + lax.broadcasted_iota(I32, (tk, tq), 0)
                    m = jnp.where(pred(kscr_ref[tt], kpos), 1.0, 0.0)
                    acc = acc + jnp.sum(m, axis=0, keepdims=True)
                return acc
            return lax.fori_loop(0, (last + unroll) // unroll, body, jnp.zeros((1, tq), F32))

        kf = float(topk)
        qpos = q0 + lax.broadcasted_iota(I32, (1, tq), 1)
        n_adm = jnp.minimum(((qpos >> CHUNK_SHIFT) + 1) << CHUNK_SHIFT, length).astype(F32)
        real = qpos - qoff < nreal
        take_all = (n_adm < kf) | jnp.logical_not(real)

        def count16(c16):
            one, zero = jnp.ones((), jnp.bfloat16), jnp.zeros((), jnp.bfloat16)
            nslab = tk // PACKED_ROWS
            assert nslab <= 256

            def body(t, acc):
                for u in range(unroll):
                    m = jnp.where(k16_ref[unroll * t + u] >= c16, one, zero).reshape(nslab, PACKED_ROWS, tq)
                    part = m[0]
                    for r in range(1, nslab):
                        part = part + m[r]
                    acc = acc + jnp.sum(part.astype(F32), axis=0, keepdims=True)
                return acc
            return lax.fori_loop(0, (last + unroll) // unroll, body, jnp.zeros((1, tq), F32))

        def bit_step(it, prefix, nge, top_half):
            cand = prefix + lax.shift_left(jnp.int32(1), 31 - it)
            if top_half:
                cbits = cand ^ ((cand >> 31) & 0x7FFFFFFF)
                cbits = jnp.where((cand > 0) & (cand < 0x00800000), 0x00800000, cbits)
                cnt = count16(lax.bitcast_convert_type(cbits & -65536, F32).astype(jnp.bfloat16))
            else:
                cnt = count(lambda kt, _: kt >= cand)
            take = cnt >= kf
            return jnp.where(take, cand, prefix), jnp.where(take, cnt, nge)

        def all_settled(nge):
            return (jnp.min(jnp.where(take_all | (nge == kf), 1.0, 0.0)) > 0.0).astype(I32)

        def low_body(carry):
            it, prefix, nge, _ = carry
            prefix, nge = bit_step(it, prefix, nge, False)
            return it + 1, prefix, nge, all_settled(nge)

        prefix, nge = lax.fori_loop(0, 16, lambda it, c: bit_step(it, c[0], c[1], True),
                                    (jnp.full((1, tq), INT_MIN, I32), jnp.zeros((1, tq), F32)))
        _, prefix, nge, _ = lax.while_loop(lambda c: (c[0] < 32) & (c[3] == 0), low_body,
                                           (jnp.int32(16), prefix, nge, all_settled(nge)))
        found = prefix > INT_MIN
        thr = jnp.maximum(prefix, INT_MIN + 1)
        t_ref[0] = thr
        c_ref[0] = jnp.full((1, tq), 2 ** 30, I32)
        straddle = jnp.max(jnp.where(found & (nge > kf) & real, 1.0, 0.0))

        @pl.when(straddle > 0.0)
        def _():
            need = kf - count(lambda kt, _: kt > thr)

            def tie_body(it, x):
                cx = x + lax.shift_left(jnp.int32(1), idx_bits - 1 - it)
                g = count(lambda kt, kpos: jnp.where(kt == thr, kpos, INT_MAX) < cx)
                return jnp.where(g < need, cx, x)
            c_ref[0] = lax.fori_loop(0, idx_bits, tie_body, jnp.zeros((1, tq), I32))


def _dsa_index(qi, ki2, wt, *, tq, tk, qoff, length, topk, nreal):
    b, s, _ = qi.shape
    lp = ki2.shape[1]
    nq, nk = s // tq, lp // tk
    idx_bits = max(1, int(np.ceil(np.log2(lp))))
    qvec = pl.BlockSpec((1, 1, tq), lambda bb, t, ii, jj: (bb, 0, ii[t]))
    return _tri_call(
        functools.partial(_dsa_index_kernel, tq=tq, tk=tk, nk=nk, unroll=2 if nk % 2 == 0 else 1,
                          qoff=qoff, length=length, topk=topk, idx_bits=idx_bits, nreal=nreal),
        _tri_steps(nq, tq, tk, qoff, length, True), b,
        in_specs=[pl.BlockSpec((1, tq, 512), lambda bb, t, ii, jj: (bb, ii[t], 0)),
                  pl.BlockSpec((1, tk, LANES), lambda bb, t, ii, jj: (bb, jj[t], 0)),
                  pl.BlockSpec((1, IDX_HEADS, tq), lambda bb, t, ii, jj: (bb, 0, ii[t]))],
        out_specs=[pl.BlockSpec((1, tk, tq), lambda bb, t, ii, jj: (bb, jj[t], ii[t])), qvec, qvec],
        out_shape=[jax.ShapeDtypeStruct((b, lp, s), I32),
                   jax.ShapeDtypeStruct((b, 1, s), I32),
                   jax.ShapeDtypeStruct((b, 1, s), I32)],
        scratch=[pltpu.VMEM((IDX_HEADS * tq, LANES), CDT),
                 pltpu.VMEM((nk, tk, tq), I32),
                 pltpu.VMEM((nk, tk, tq), jnp.bfloat16)],
        name="dsa_index",
    )(qi, ki2, wt)


def _dsa_attn_kernel(ii_ref, jj_ref, q_ref, k_ref, vt_ref, keys_ref, t_ref, c_ref, o_ref, *scr,
                     tq, tk, qoff, length):
    qst_ref, m_ref, l_ref, acc_ref = scr[:4]
    i = ii_ref[pl.program_id(1)]
    j = jj_ref[pl.program_id(1)]
    last = _last_tile(i, tq, tk, qoff, length, True)

    @pl.when(j == 0)
    def _():
        _flash_init(q_ref, None, qst_ref, m_ref, l_ref, acc_ref, tq)

    @pl.when(j <= last)
    def _():
        kt = keys_ref[0]
        thr = t_ref[0]
        kpos = j * tk + lax.broadcasted_iota(I32, (tk, tq), 0)
        sel = (kt > thr) | (jnp.where(kt == thr, kpos, INT_MAX) <= c_ref[0])
        mb = jnp.where(sel, 0.0, NEG)
        mb2 = jnp.concatenate([mb, mb], axis=1)
        score = lambda g: _mm_nt(k_ref[0, :, g * LANES:(g + 1) * LANES], qst_ref[g]) + mb2
        _flash_pairs(score, vt_ref, scr, tq)

    @pl.when(j == last)
    def _():
        _flash_finish(o_ref, l_ref, acc_ref, tq)


def _dsa_attn(q, k, vt, keys, thr, cut, *, tq, tk, qoff, length):
    b, s, w = q.shape
    qmap = lambda bb, t, ii, jj: (bb, ii[t], 0)
    qvec = pl.BlockSpec((1, 1, tq), lambda bb, t, ii, jj: (bb, 0, ii[t]))
    npair = w // LANES
    return _tri_call(
        functools.partial(_dsa_attn_kernel, tq=tq, tk=tk, qoff=qoff, length=length),
        _tri_steps(s // tq, tq, tk, qoff, length, True), b,
        in_specs=[pl.BlockSpec((1, tq, w), qmap),
                  pl.BlockSpec((1, tk, w), lambda bb, t, ii, jj: (bb, jj[t], 0)),
                  pl.BlockSpec((1, w, tk), lambda bb, t, ii, jj: (bb, 0, jj[t])),
                  pl.BlockSpec((1, tk, tq), lambda bb, t, ii, jj: (bb, jj[t], ii[t])),
                  qvec, qvec],
        out_specs=pl.BlockSpec((1, tq, w), qmap),
        out_shape=jax.ShapeDtypeStruct((b, s, w), CDT),
        scratch=_flash_scratch(npair, tq, tk, LANES, w),
        name="dsa_attn",
    )(q, k, vt, keys, thr, cut)


def _pieces3(x):
    pcs = _pieces(x, 3)
    return pcs + [jnp.zeros_like(pcs[0])] * (3 - len(pcs))


FOX_EXT = 9


def _cumsum_kernel(x_ref, tri_ref, place_ref, ones_ref, kx_ref, carry_ref, *, scale):
    @pl.when(pl.program_id(1) == 0)
    def _():
        carry_ref[...] = jnp.zeros(carry_ref.shape, F32)
    tri = tri_ref[...]
    tot = None
    for p in _pieces(x_ref[0], 3):
        t = _mm(tri, p)
        tot = t if tot is None else tot + t
    acc = carry_ref[...] + tot
    carry_ref[...] = acc[-1:, :]
    ext = None
    for c, p in enumerate(_pieces3(acc * (-scale))):
        t = _mm(p, place_ref[c])
        ext = t if ext is None else ext + t
    kx_ref[0] = (ext + ones_ref[...]).astype(kx_ref.dtype)


def _fox_key_lanes(x, tb, scale, nh):
    b, s, w = x.shape
    npair = nh // 2
    place = np.zeros((3, w, npair * LANES), np.float32)
    ones = np.zeros((1, npair * LANES), np.float32)
    for h in range(nh):
        for c in range(3):
            place[c, h, (h // 2) * LANES + (h % 2) * 6 + c] = 1.0
    for g in range(npair):
        ones[0, g * LANES + 3:g * LANES + 6] = 1.0
    tri = jnp.tril(jnp.ones((tb, tb), F32)).astype(CDT)
    place = jnp.asarray(place, CDT)
    spec = lambda wd: pl.BlockSpec((1, tb, wd), lambda bb, i: (bb, i, 0))
    return pl.pallas_call(
        functools.partial(_cumsum_kernel, scale=scale),
        grid=(b, s // tb),
        in_specs=[spec(w), _const_spec(tri.shape), _const_spec(place.shape), _const_spec(ones.shape)],
        out_specs=spec(npair * LANES),
        out_shape=jax.ShapeDtypeStruct((b, s, npair * LANES), CDT),
        scratch_shapes=[pltpu.VMEM((1, w), F32)],
        compiler_params=_params(("parallel", "arbitrary")),
        name="logf_cumsum",
    )(x, tri, place, jnp.asarray(ones))


def _fox_kernel(ii_ref, jj_ref, dead_ref, jeff_ref, q_ref, k_ref, kx_ref, vt_ref, aug_ref, o_ref, *scr,
                tq, tk, qoff, length):
    qst_ref, m_ref, l_ref, acc_ref = scr[:4]
    i = ii_ref[pl.program_id(1)]
    j = jj_ref[pl.program_id(1)]
    live = dead_ref[pl.program_id(1)] == 0
    last = _last_tile(i, tq, tk, qoff, length, False)
    q0 = qoff + i * tq

    @pl.when(j == 0)
    def _():
        _flash_init(q_ref, aug_ref, qst_ref, m_ref, l_ref, acc_ref, tq)

    def step(masked):
        if masked:
            kpos = j * tk + lax.broadcasted_iota(I32, (tk, 2 * tq), 0)
            lane = lax.broadcasted_iota(I32, (tk, 2 * tq), 1)
            causal = kpos <= q0 + jnp.where(lane >= tq, lane - tq, lane)

        def score(g):
            sl = slice(g * LANES, (g + 1) * LANES)
            st = _mm_nt(jnp.concatenate([k_ref[0, :, sl], kx_ref[0, :, sl]], axis=1), qst_ref[g])
            return jnp.where(causal, st, NEG) if masked else st
        _flash_pairs(score, vt_ref, scr, tq)

    diag = (j + 1) * tk - 1 > q0

    @pl.when(live & diag)
    def _():
        step(True)

    @pl.when(live & jnp.logical_not(diag))
    def _():
        step(False)

    @pl.when(j == last)
    def _():
        _flash_finish(o_ref, l_ref, acc_ref, tq)


FORGOTTEN_LOG2 = 150.0


def _fox_dead_tiles(kx, steps, tq, tk, qoff, qk_bound):
    ii, jj = steps
    nq, nk = int(ii.max()) + 1, int(jj.max()) + 1

    def neg_f(rows):
        a = kx[0, rows].reshape(len(rows), -1, LANES).astype(F32)
        return jnp.stack([a[..., 0:3].sum(-1), a[..., 6:9].sum(-1)], axis=-1).reshape(len(rows), -1)

    q0 = qoff + np.arange(nq) * tq
    ends = (np.arange(nk) + 1) * tk - 1
    drop = neg_f(q0)[:, None, :] - neg_f(ends)[None, :, :]
    dead = (jnp.min(drop, axis=-1) > 2.0 * qk_bound + FORGOTTEN_LOG2) & jnp.asarray(ends[None, :] < q0[:, None])
    lead = jnp.cumprod(dead.astype(I32), axis=1)
    first_live = jnp.sum(lead, axis=1)
    return lead[ii, jj], jnp.maximum(jnp.asarray(jj), first_live[ii])


def _fox(q, k, kx, vt, qaug, *, tq, tk, qoff, length, qk_bound=None):
    b, s, w = q.shape
    npair = w // LANES
    steps = _tri_steps(s // tq, tq, tk, qoff, length, False)
    if qk_bound is None or b != 1:
        tables = (jnp.zeros(steps[0].shape, I32), jnp.asarray(steps[1]))
    else:
        tables = _fox_dead_tiles(kx, steps, tq, tk, qoff, qk_bound)
    qmap = lambda bb, t, ii, jj, dead, jeff: (bb, ii[t], 0)
    kmap = lambda bb, t, ii, jj, dead, jeff: (bb, jeff[t], 0)
    return _tri_call(
        functools.partial(_fox_kernel, tq=tq, tk=tk, qoff=qoff, length=length),
        steps, b,
        in_specs=[pl.BlockSpec((1, tq, w), qmap),
                  pl.BlockSpec((1, tk, w), kmap),
                  pl.BlockSpec((1, tk, npair * LANES), kmap),
                  pl.BlockSpec((1, w, tk), lambda bb, t, ii, jj, dead, jeff: (bb, 0, jeff[t])),
                  pl.BlockSpec((1, 1, 2 * npair, LANES), lambda bb, t, ii, jj, dead, jeff: (bb, ii[t], 0, 0))],
        out_specs=pl.BlockSpec((1, tq, w), qmap),
        out_shape=jax.ShapeDtypeStruct((b, s, w), CDT),
        scratch=_flash_scratch(npair, tq, tk, 2 * LANES, w),
        name="fox_attn", extra=tables,
    )(q, k, kx, vt, qaug)


def _fox_qaug(kx0):
    b, nq, _ = kx0.shape
    a = kx0.reshape(b, nq, -1, LANES).astype(F32)
    one, zero = jnp.ones_like(a[..., 0:3]), jnp.zeros_like(a[..., 0:3])
    even = jnp.concatenate([one, -a[..., 0:3], zero], axis=-1)
    odd = jnp.concatenate([zero, -a[..., 6:9], one], axis=-1)
    heads = jnp.stack([even, odd], axis=3).reshape(b, nq, -1, FOX_EXT)
    return jnp.pad(heads, ((0, 0), (0, 0), (0, 0), (0, LANES - FOX_EXT)))


def _rope_tables(pos):
    half = HEAD_DIM // 2
    inv = ROPE_THETA ** (-jnp.arange(half, dtype=F32) / half)
    ang = pos.astype(F32)[:, None] * inv[None, :]
    cos, sin = jnp.cos(ang), jnp.sin(ang)
    return (jnp.tile(jnp.concatenate([cos, cos], axis=1), (1, 2)),
            jnp.tile(jnp.concatenate([-sin, sin], axis=1), (1, 2)))


def _pad_rows(x, n):
    return x if x.shape[1] == n else jnp.pad(x, ((0, 0), (0, n - x.shape[1]), (0, 0)))


def _tiles(s):
    tk = 512 if s % 512 == 0 else s
    tq = 512 if s % 512 == 0 else s
    tqi = 256 if s % 256 == 0 else s
    tqf = tq
    return tq, tqi, tk, tqf


def kernel(x_prompt, x_sample, cache_a_k, cache_a_v, cache_b_k, cache_b_v, cache_b_ik, cache_c_k, cache_c_v,
           cache_c_logf, p_prompt, p_sample, g_mix, w_in_even, qn_a, kn_a, rel_bias_a, qn_b, kn_b, w_out_even,
           w_in_odd, b_f, qn_c, kn_c, w_out_odd, g_mlp, w_up, w_down, g_ple, w_ple_gate, w_ple_proj):
    B, S, D = x_prompt.shape
    DB, n, _ = x_sample.shape
    P = cache_b_k.shape[2]
    W_A = cache_a_k.shape[2]
    W_AP = min(A_PAST, S)
    depth = g_mix.shape[0]
    L = P + n
    Lp = -(-L // LANES) * LANES
    nqp = -(-n // LANES) * LANES
    assert B == 1 and S % 128 == 0 and n % 16 == 0 and P % 16 == 0

    row = lambda v: v.astype(F32).reshape(1, -1)
    bd = jnp.asarray(np.kron(np.eye(MXU_TILE // HEAD_DIM), np.full((HEAD_DIM, HEAD_DIM), 1.0 / HEAD_DIM)), CDT)
    cos_p, sin_p = _rope_tables(jnp.arange(S))
    cos_s, sin_s = (jnp.tile(t, (DB, 1)) for t in _rope_tables(P + jnp.arange(n)))
    tq, tqi, tk, tqf = _tiles(S)
    tr = lambda t: jnp.swapaxes(t, 1, 2)
    catp = lambda c, new: _pad_rows(jnp.concatenate([c.astype(CDT), new], axis=1), Lp)

    outs = {name: [] for name in ("a_k_p", "a_v_p", "b_k_p", "b_v_p", "b_ik_p", "c_k_p", "c_v_p", "c_lf_p",
                                  "a_k_s", "a_v_s", "b_k_s", "b_v_s", "b_ik_s", "c_k_s", "c_v_s", "c_lf_s")}
    h_p = x_prompt.reshape(B * S, D)
    h_s = x_sample.reshape(DB * n, D)
    for i in range(depth):
        g = row(g_mix[i])
        if i % 2 == 0:
            e = i // 2
            wsz = 512
            w = w_in_even[e]
            w_all = jnp.concatenate([w[:, :7 * wsz], jnp.pad(w[:, 7 * wsz:], ((0, 0), (0, LANES - 72)))],
                                    axis=1).astype(CDT)
            gn = jnp.stack([jnp.tile(v[e].astype(F32), wsz // HEAD_DIM) for v in (qn_a, kn_a, qn_b, kn_b)])
            (ka, va, kb, vb, kw, qa_c, ka_c, va_c, qb_c, kb_c, vb_c, qi_c, ki2_c) = _proj_even(
                h_p, g, w_all, gn, bd, cos_p, sin_p)
            r3 = lambda t: t.reshape(B, S, t.shape[-1])
            tqa = 256 if S % 256 == 0 else 128
            npc = (A_PAST + tqa) // tqa
            k_specs = [pl.BlockSpec((1, tqa, wsz), functools.partial(
                lambda bb, ii, pp: (bb, jnp.maximum(ii + pp - A_PAST // tqa, 0), 0), pp=pp)) for pp in range(npc)]
            bias_p = _band_bias(rel_bias_a[e], np.arange(tqa), np.arange(-A_PAST, tqa))
            oa = _band(r3(qa_c), [r3(ka_c)] * npc, [r3(va_c)] * npc, k_specs, bias_p, tqa, True)
            topk_p = min(TOPK_MAX, S // 4)
            wt = tr(r3(kw)[..., 64:64 + IDX_HEADS])
            keys, thr, cut = _dsa_index(r3(qi_c), r3(ki2_c), wt, tq=tqi, tk=tk, qoff=0, length=S, topk=topk_p,
                                        nreal=S)
            ob = _dsa_attn(r3(qb_c), r3(kb_c), tr(r3(vb_c)), keys, thr, cut, tq=tq, tk=tk, qoff=0, length=S)
            os_p = [oa.reshape(B * S, wsz), ob.reshape(B * S, wsz)]
            hd = lambda t: t.reshape(B, S, -1, HEAD_DIM)
            outs["a_k_p"].append(hd(ka)[:, S - W_AP:]); outs["a_v_p"].append(hd(va)[:, S - W_AP:])
            outs["b_k_p"].append(hd(kb)); outs["b_v_p"].append(hd(vb))
            outs["b_ik_p"].append(kw[:, :64].reshape(B, S, 64))
            (ka, va, kb, vb, kw, qa_c, ka_c, va_c, qb_c, kb_c, vb_c, qi_c, ki2_c) = _proj_even(
                h_s, g, w_all, gn, bd, cos_s, sin_s)
            r3 = lambda t: t.reshape(DB, n, t.shape[-1])
            ca_k = cache_a_k[e].reshape(DB, W_A, wsz)
            ca_v = cache_a_v[e].reshape(DB, W_A, wsz)
            full = lambda rows, wd: pl.BlockSpec((1, rows, wd), lambda bb, ii: (bb, 0, 0))
            bias_s = _band_bias(rel_bias_a[e], P + np.arange(n), P - W_A + np.arange(W_A + n))
            bias_s = jnp.pad(bias_s, ((0, 0), (0, 0), (0, -(W_A + n) % LANES)), constant_values=NEG)
            oa = _band(r3(qa_c), [ca_k, r3(ka)], [ca_v, r3(va)], [full(W_A, wsz), full(n, wsz)], bias_s, n, False)
            ik = cache_b_ik[e].astype(CDT)
            ki2_all = catp(jnp.concatenate([ik, ik], axis=-1), r3(ki2_c))
            kb_all = catp(cache_b_k[e].reshape(DB, P, wsz), r3(kb_c))
            vb_all = catp(cache_b_v[e].reshape(DB, P, wsz), r3(vb_c))
            topk_s = min(TOPK_MAX, L // 4)
            wt = _pad_rows(r3(kw)[..., 64:64 + IDX_HEADS], nqp)
            keys, thr, cut = _dsa_index(_pad_rows(r3(qi_c), nqp), ki2_all, tr(wt),
                                        tq=nqp, tk=Lp, qoff=P, length=L, topk=topk_s, nreal=n)
            ob = _dsa_attn(_pad_rows(r3(qb_c), nqp), kb_all, tr(vb_all), keys, thr, cut,
                           tq=nqp, tk=Lp, qoff=P, length=L)[:, :n]
            os_s = [oa.reshape(DB * n, wsz), ob.reshape(DB * n, wsz)]
            hd = lambda t: t.reshape(DB, n, -1, HEAD_DIM)
            outs["a_k_s"].append(jnp.concatenate([cache_a_k[e], hd(ka)], axis=1)[:, n:])
            outs["a_v_s"].append(jnp.concatenate([cache_a_v[e], hd(va)], axis=1)[:, n:])
            outs["b_k_s"].append(hd(kb)); outs["b_v_s"].append(hd(vb))
            outs["b_ik_s"].append(kw[:, :64].reshape(DB, n, 64))
            w_out = w_out_even[e].astype(CDT)
            wos = [w_out[:wsz], w_out[wsz:]]
        else:
            o = i // 2
            nh = b_f.shape[1]
            wsz = nh * HEAD_DIM
            w = w_in_odd[o]
            w_all = jnp.concatenate([w[:, :3 * wsz], jnp.pad(w[:, 3 * wsz:], ((0, 0), (0, LANES - nh)))],
                                    axis=1).astype(CDT)
            gn = jnp.stack([jnp.tile(v[o].astype(F32), wsz // HEAD_DIM) for v in (qn_c, kn_c)])
            bf = jnp.pad(b_f[o].astype(F32), (0, LANES - nh)).reshape(1, LANES)
            k, v, lf, q_c, k_c, v_c = _proj_odd(h_p, g, w_all, gn, bf, bd)
            r3 = lambda t: t.reshape(B, S, t.shape[-1])
            kx = _fox_key_lanes(r3(lf), 512 if S % 512 == 0 else S, LOG2E, nh)
            qk_bound = (1.05 * HEAD_DIM * (HEAD_DIM ** -0.5 * LOG2E)
                        * jnp.max(jnp.abs(qn_c[o].astype(F32))) * jnp.max(jnp.abs(kn_c[o].astype(F32))))
            oc = _fox(r3(q_c), r3(k_c), kx, tr(r3(v_c)), _fox_qaug(kx[:, ::tqf]), tq=tqf, tk=tk, qoff=0, length=S,
                      qk_bound=qk_bound)
            os_p = [oc.reshape(B * S, wsz)]
            hd = lambda t: t.reshape(B, S, nh, HEAD_DIM)
            outs["c_k_p"].append(hd(k)); outs["c_v_p"].append(hd(v)); outs["c_lf_p"].append(lf[:, :nh].reshape(B, S, nh))
            k, v, lf, q_c, k_c, v_c = _proj_odd(h_s, g, w_all, gn, bf, bd)
            r3 = lambda t: t.reshape(DB, n, t.shape[-1])
            lf_all = jnp.concatenate([jnp.pad(cache_c_logf[o].astype(F32), ((0, 0), (0, 0), (0, LANES - nh))), r3(lf)],
                                     axis=1)
            kx = _fox_key_lanes(lf_all, L, LOG2E, nh)
            k_all = catp(cache_c_k[o].reshape(DB, P, wsz), r3(k_c))
            v_all = catp(cache_c_v[o].reshape(DB, P, wsz), r3(v_c))
            oc = _fox(_pad_rows(r3(q_c), nqp), k_all, _pad_rows(kx, Lp), tr(v_all),
                      _fox_qaug(kx[:, P:P + 1]), tq=nqp, tk=Lp, qoff=P, length=L)[:, :n]
            os_s = [oc.reshape(DB * n, wsz)]
            hd = lambda t: t.reshape(DB, n, nh, HEAD_DIM)
            outs["c_k_s"].append(hd(k)); outs["c_v_s"].append(hd(v)); outs["c_lf_s"].append(lf[:, :nh].reshape(DB, n, nh))
            w_out = w_out_odd[o].astype(CDT)
            wos = [w_out]
        post_w = (row(g_mlp[i]), w_up[i].astype(CDT), w_down[i].astype(CDT), row(g_ple[i]),
                  w_ple_gate[i].astype(CDT), w_ple_proj[i].astype(CDT))
        h_p = _post(h_p, os_p, p_prompt[i].reshape(B * S, -1), wos, *post_w)
        h_s = _post(h_s, os_s, p_sample[i].reshape(DB * n, -1), wos, *post_w)

    st = jnp.stack
    names = ("a_k_p", "a_v_p", "b_k_p", "b_v_p", "b_ik_p", "c_k_p", "c_v_p", "c_lf_p",
             "a_k_s", "a_v_s", "b_k_s", "b_v_s", "b_ik_s", "c_k_s", "c_v_s", "c_lf_s")
    return (h_p.reshape(B, S, D), h_s.reshape(DB, n, D)) + tuple(st(outs[nm]) for nm in names)
```

```python
import functools

import numpy as np
import jax
import jax.numpy as jnp
from jax import lax
from jax.experimental import pallas as pl
from jax.experimental.pallas import tpu as pltpu

F32 = jnp.float32
I32 = jnp.int32
CDT = jnp.bfloat16

CHUNK = 64
HEAD_DIM = 64
IDX_HEADS = 8
A_PAST = 8 * CHUNK
REL_CLIP = 128
TOPK_MAX = 256
ROPE_THETA = 10000.0
RMS_EPS = 1e-6
LANES = 128
MXU_TILE = 256
PACKED_ROWS = 16
CHUNK_SHIFT = CHUNK.bit_length() - 1
NEG = -1e30
INT_MIN = -(2 ** 31)
INT_MAX = 2 ** 31 - 1
LOG2E = 1.4426950408889634
VMEM_LIMIT = 56 * 1024 * 1024


def _mm(a, b):
    return jnp.dot(a, b, preferred_element_type=F32)


def _mm_nt(a, b):
    return lax.dot_general(a, b, (((1,), (1,)), ((), ())), preferred_element_type=F32)


def _pieces(x, n):
    if CDT == F32:
        return [x]
    out = []
    for _ in range(n - 1):
        p = x.astype(CDT)
        out.append(p)
        x = x - p.astype(F32)
    out.append(x.astype(CDT))
    return out


def _rms(x, g):
    return x * lax.rsqrt(jnp.mean(x * x, axis=-1, keepdims=True) + RMS_EPS) * g


def _head_rms(x, bd, g):
    x2 = x * x
    pcs = _pieces(x2, 2)
    cols = []
    for s in range(x.shape[1] // MXU_TILE):
        sl = slice(s * MXU_TILE, (s + 1) * MXU_TILE)
        ms = _mm(pcs[0][:, sl], bd)
        for p in pcs[1:]:
            ms = ms + _mm(p[:, sl], bd)
        cols.append(ms)
    ms = cols[0] if len(cols) == 1 else jnp.concatenate(cols, axis=1)
    return x * lax.rsqrt(ms + RMS_EPS) * g


def _rope(x, cos, sin):
    w = x.shape[1]
    lane = lax.broadcasted_iota(I32, x.shape, 1)
    first = (lane & 63) < 32
    swapped = jnp.where(first, pltpu.roll(x, w - 32, 1), pltpu.roll(x, 32, 1))
    return x * cos + swapped * sin


def _tile_lanes(t, w):
    return t if w == t.shape[1] else jnp.concatenate([t] * (w // t.shape[1]), axis=1)


def _proj_even_kernel(h_ref, g_ref, w_ref, gn_ref, bd_ref, cos_ref, sin_ref,
                      ka_ref, va_ref, kb_ref, vb_ref, kw_ref,
                      qa_c, ka_c, va_c, qb_c, kb_c, vb_c, qi_c, ki2_c):
    a = _rms(h_ref[...], g_ref[...]).astype(CDT)
    z = _mm(a, w_ref[...])
    w = 512
    bd = bd_ref[...]
    gn = gn_ref[...]
    cos1, sin1 = cos_ref[...], sin_ref[...]
    cos, sin = _tile_lanes(cos1, w), _tile_lanes(sin1, w)
    scale = HEAD_DIM ** -0.5

    qa = _head_rms(z[:, 0:w], bd, gn[0:1])
    qa_c[...] = (qa * scale).astype(CDT)
    ka = _head_rms(z[:, w:2 * w], bd, gn[1:2])
    ka_ref[...] = ka
    ka_c[...] = ka.astype(CDT)
    va = z[:, 2 * w:3 * w]
    va_ref[...] = va
    va_c[...] = va.astype(CDT)
    qb = _rope(_head_rms(z[:, 3 * w:4 * w], bd, gn[2:3]), cos, sin)
    qb_c[...] = (qb * (scale * LOG2E)).astype(CDT)
    kb = _rope(_head_rms(z[:, 4 * w:5 * w], bd, gn[3:4]), cos, sin)
    kb_ref[...] = kb
    kb_c[...] = kb.astype(CDT)
    vb = z[:, 5 * w:6 * w]
    vb_ref[...] = vb
    vb_c[...] = vb.astype(CDT)
    qi = _rope(z[:, 6 * w:7 * w], cos, sin)
    qi_c[...] = (qi * scale).astype(CDT)
    kw = z[:, 7 * w:7 * w + LANES]
    kwr = _rope(kw, cos1, sin1)
    lane = lax.broadcasted_iota(I32, kw.shape, 1)
    kw_ref[...] = jnp.where(lane < 64, kwr, kw)
    ki2_c[...] = jnp.where(lane < 64, kwr, pltpu.roll(kwr, 64, 1)).astype(CDT)


def _proj_odd_kernel(h_ref, g_ref, w_ref, gn_ref, bf_ref, bd_ref,
                     k_ref, v_ref, lf_ref, q_c, k_c, v_c):
    a = _rms(h_ref[...], g_ref[...]).astype(CDT)
    z = _mm(a, w_ref[...])
    w = 1024
    bd = bd_ref[...]
    gn = gn_ref[...]
    q = _head_rms(z[:, 0:w], bd, gn[0:1])
    q_c[...] = (q * (HEAD_DIM ** -0.5 * LOG2E)).astype(CDT)
    k = _head_rms(z[:, w:2 * w], bd, gn[1:2])
    k_ref[...] = k
    k_c[...] = k.astype(CDT)
    v = z[:, 2 * w:3 * w]
    v_ref[...] = v
    v_c[...] = v.astype(CDT)
    fl = z[:, 3 * w:3 * w + LANES] + bf_ref[...]
    lf_ref[...] = jnp.minimum(fl, 0.0) - jnp.log1p(jnp.exp(-jnp.abs(fl)))


def _const_spec(shape):
    nd = len(shape)
    return pl.BlockSpec(shape, lambda *_: (0,) * nd, pipeline_mode=pl.Buffered(1))


def _row_spec(tm, w):
    return pl.BlockSpec((tm, w), lambda i: (i, 0))


def _params(sem):
    return pltpu.CompilerParams(dimension_semantics=sem, vmem_limit_bytes=VMEM_LIMIT)


def _row_tile(rows, pref):
    return pref if rows % pref == 0 else rows


def _proj_even(h, g, w_all, gn, bd, cos, sin):
    rows, d = h.shape
    tm = _row_tile(rows, 256)
    f = lambda w, dt: jax.ShapeDtypeStruct((rows, w), dt)
    out_shape = [f(512, F32)] * 4 + [f(LANES, F32)] + [f(512, CDT)] * 7 + [f(LANES, CDT)]
    out_specs = [_row_spec(tm, 512)] * 4 + [_row_spec(tm, LANES)] + [_row_spec(tm, 512)] * 7 + [_row_spec(tm, LANES)]
    return pl.pallas_call(
        _proj_even_kernel,
        grid=(rows // tm,),
        in_specs=[_row_spec(tm, d), _const_spec(g.shape), _const_spec(w_all.shape), _const_spec(gn.shape),
                  _const_spec(bd.shape), _row_spec(tm, LANES), _row_spec(tm, LANES)],
        out_specs=out_specs,
        out_shape=out_shape,
        compiler_params=_params(("parallel",)),
        name="proj_even",
    )(h, g, w_all, gn, bd, cos, sin)


def _proj_odd(h, g, w_all, gn, bf, bd):
    rows, d = h.shape
    tm = _row_tile(rows, 256)
    f = lambda w, dt: jax.ShapeDtypeStruct((rows, w), dt)
    out_shape = [f(1024, F32)] * 2 + [f(LANES, F32)] + [f(1024, CDT)] * 3
    out_specs = [_row_spec(tm, 1024)] * 2 + [_row_spec(tm, LANES)] + [_row_spec(tm, 1024)] * 3
    return pl.pallas_call(
        _proj_odd_kernel,
        grid=(rows // tm,),
        in_specs=[_row_spec(tm, d), _const_spec(g.shape), _const_spec(w_all.shape), _const_spec(gn.shape),
                  _const_spec(bf.shape), _const_spec(bd.shape)],
        out_specs=out_specs,
        out_shape=out_shape,
        compiler_params=_params(("parallel",)),
        name="proj_odd",
    )(h, g, w_all, gn, bf, bd)


def _post_kernel(*refs, n_o, ff_chunk):
    h_ref = refs[0]
    o_refs = refs[1:1 + n_o]
    p_ref = refs[1 + n_o]
    wo_refs = refs[2 + n_o:2 + 2 * n_o]
    gm_ref, wu_ref, wd_ref, gp_ref, wg_ref, wp_ref, out_ref = refs[2 + 2 * n_o:]
    mix = None
    for o_ref, wo_ref in zip(o_refs, wo_refs):
        t = _mm(o_ref[...], wo_ref[...])
        mix = t if mix is None else mix + t
    h = h_ref[...] + mix
    m = _rms(h, gm_ref[...]).astype(CDT)
    mlp = None
    d_ff = wu_ref.shape[1]
    for c in range(d_ff // ff_chunk):
        sl = slice(c * ff_chunk, (c + 1) * ff_chunk)
        u = jnp.square(jnp.maximum(_mm(m, wu_ref[:, sl]), 0.0)).astype(CDT)
        t = _mm(u, wd_ref[sl, :])
        mlp = t if mlp is None else mlp + t
    acc = h + mlp
    gate_in = _rms(acc, gp_ref[...]).astype(CDT)
    gate = 1.0 / (1.0 + jnp.exp(-_mm(gate_in, wg_ref[...])))
    out_ref[...] = acc + gate * _mm(p_ref[...].astype(CDT), wp_ref[...])


def _post(h, os_, p, wos, gm, wu, wd, gp, wg, wp):
    rows, d = h.shape
    tm = _row_tile(rows, 512)
    n_o = len(os_)
    in_specs = ([_row_spec(tm, d)] + [_row_spec(tm, o.shape[1]) for o in os_] + [_row_spec(tm, p.shape[1])]
                + [_const_spec(w.shape) for w in wos]
                + [_const_spec(x.shape) for x in (gm, wu, wd, gp, wg, wp)])
    return pl.pallas_call(
        functools.partial(_post_kernel, n_o=n_o, ff_chunk=1024),
        grid=(rows // tm,),
        in_specs=in_specs,
        out_specs=_row_spec(tm, d),
        out_shape=jax.ShapeDtypeStruct((rows, d), F32),
        compiler_params=_params(("parallel",)),
        name="post",
    )(h, *os_, p, *wos, gm, wu, wd, gp, wg, wp)


def _pair_stack(qp, tq):
    lane = lax.broadcasted_iota(I32, qp.shape, 1)
    zero = jnp.zeros_like(qp)
    return jnp.concatenate([jnp.where(lane < 64, qp, zero), jnp.where(lane >= 64, qp, zero)], axis=0)


def _pair_merge(o2, tq):
    lane = lax.broadcasted_iota(I32, (tq, LANES), 1)
    return jnp.where(lane < 64, o2[:tq], o2[tq:])


def _last_tile(i, tq, tk, qoff, length, chunked):
    qend = qoff + i * tq + tq - 1
    kmax = (qend // CHUNK + 1) * CHUNK if chunked else qend + 1
    kmax = jnp.minimum(kmax, length)
    return (kmax - 1) // tk


def _tri_steps(nq, tq, tk, qoff, length, chunked):
    ii, jj = [], []
    for i in range(nq):
        qend = qoff + i * tq + tq - 1
        kmax = min((qend // CHUNK + 1) * CHUNK if chunked else qend + 1, length)
        for j in range((kmax - 1) // tk + 1):
            ii.append(i)
            jj.append(j)
    return np.asarray(ii, np.int32), np.asarray(jj, np.int32)


def _tri_call(kernel_fn, steps, batch, in_specs, out_specs, out_shape, scratch, name, extra=()):
    ii, jj = steps
    grid_spec = pltpu.PrefetchScalarGridSpec(
        num_scalar_prefetch=2 + len(extra), grid=(batch, ii.shape[0]),
        in_specs=in_specs, out_specs=out_specs, scratch_shapes=scratch)
    call = pl.pallas_call(kernel_fn, grid_spec=grid_spec, out_shape=out_shape,
                          compiler_params=_params(("parallel", "arbitrary")), name=name)
    return functools.partial(call, jnp.asarray(ii), jnp.asarray(jj), *extra)


def _flash_init(q_ref, aug_ref, qst_ref, m_ref, l_ref, acc_ref, tq):
    q = q_ref[0]
    for g in range(qst_ref.shape[0]):
        st = _pair_stack(q[:, g * LANES:(g + 1) * LANES], tq)
        if aug_ref is not None:
            a = aug_ref[0, 0]
            ext = jnp.concatenate([jnp.broadcast_to(a[2 * g:2 * g + 1], (tq, LANES)),
                                   jnp.broadcast_to(a[2 * g + 1:2 * g + 2], (tq, LANES))], axis=0)
            st = jnp.concatenate([st, ext.astype(CDT)], axis=1)
        qst_ref[g] = st
    m_ref[...] = jnp.full(m_ref.shape, NEG, F32)
    l_ref[...] = jnp.zeros(l_ref.shape, F32)
    acc_ref[...] = jnp.zeros(acc_ref.shape, F32)


def _flash_pairs(score_fn, vt_ref, scr, tq):
    _, m_ref, l_ref, acc_ref, s_ref = scr
    npair = m_ref.shape[0]
    s_ref[0] = score_fn(0)
    for g in range(npair):
        if g + 1 < npair:
            s_ref[(g + 1) % 2] = score_fn(g + 1)
        _flash_step(s_ref[g % 2], vt_ref, m_ref, l_ref, acc_ref, g, tq)


def _flash_step(st, vt_ref, m_ref, l_ref, acc_ref, g, tq):
    m_old = m_ref[g]
    m_new = jnp.maximum(m_old, jnp.max(st, axis=0, keepdims=True))
    alpha = jnp.exp2(m_old - m_new)
    m_ref[g] = m_new
    pc = jnp.exp2(st - m_new).astype(CDT)
    ones = jnp.ones((PACKED_ROWS, st.shape[0]), CDT)
    sums = []
    for e in range(2):
        rows = slice(g * LANES + e * HEAD_DIM, g * LANES + (e + 1) * HEAD_DIM)
        cols = slice(e * tq, (e + 1) * tq)
        pv = _mm(jnp.concatenate([vt_ref[0, rows, :], ones], axis=0), pc[:, cols])
        acc_ref[rows, :] = alpha[:, cols] * acc_ref[rows, :] + pv[:HEAD_DIM]
        sums.append(pv[HEAD_DIM:HEAD_DIM + 1])
    l_ref[g] = alpha * l_ref[g] + jnp.concatenate(sums, axis=1)


def _flash_heads(score_fn, vt_ref, scr, tq):
    _, m_ref, l_ref, acc_ref, s_ref = scr
    nhead = 2 * m_ref.shape[0]
    s_ref[0, :, :tq] = score_fn(0)
    for h in range(nhead):
        if h + 1 < nhead:
            s_ref[(h + 1) % 2, :, :tq] = score_fn(h + 1)
        st = s_ref[h % 2, :, :tq]
        g, cols = h // 2, slice((h % 2) * tq, (h % 2 + 1) * tq)
        rows = slice(h * HEAD_DIM, (h + 1) * HEAD_DIM)
        m_old = m_ref[g, :, cols]
        m_new = jnp.maximum(m_old, jnp.max(st, axis=0, keepdims=True))
        alpha = jnp.exp2(m_old - m_new)
        m_ref[g, :, cols] = m_new
        pc = jnp.exp2(st - m_new).astype(CDT)
        ones = jnp.ones((PACKED_ROWS, st.shape[0]), CDT)
        pv = _mm(jnp.concatenate([vt_ref[0, rows, :], ones], axis=0), pc)
        acc_ref[rows, :] = alpha * acc_ref[rows, :] + pv[:HEAD_DIM]
        l_ref[g, :, cols] = alpha * l_ref[g, :, cols] + pv[HEAD_DIM:HEAD_DIM + 1]


def _flash_finish(o_ref, l_ref, acc_ref, tq):
    for g in range(l_ref.shape[0]):
        l = l_ref[g]
        den = jnp.concatenate([jnp.broadcast_to(l[:, :tq], (HEAD_DIM, tq)),
                               jnp.broadcast_to(l[:, tq:], (HEAD_DIM, tq))], axis=0)
        o = acc_ref[g * LANES:(g + 1) * LANES, :] / den
        o_ref[0, :, g * LANES:(g + 1) * LANES] = o.T.astype(o_ref.dtype)


def _flash_scratch(npair, tq, tk, kd, w):
    return [pltpu.VMEM((npair, 2 * tq, kd), CDT),
            pltpu.VMEM((npair, 1, 2 * tq), F32),
            pltpu.VMEM((npair, 1, 2 * tq), F32),
            pltpu.VMEM((w, tq), F32),
            pltpu.VMEM((2, tk, 2 * tq), F32)]


def _band_kernel(*refs, n_kv, tq, mask_neg):
    q_ref = refs[0]
    k_refs = refs[1:1 + n_kv]
    v_refs = refs[1 + n_kv:1 + 2 * n_kv]
    bias_ref = refs[1 + 2 * n_kv]
    o_ref = refs[2 + 2 * n_kv]
    i = pl.program_id(1)
    q = q_ref[0]
    cat = lambda rs: (rs[0][0].astype(CDT) if len(rs) == 1
                      else jnp.concatenate([r[0].astype(CDT) for r in rs], axis=0))
    k = cat(k_refs)
    v = cat(v_refs)
    tk = bias_ref.shape[2]
    if k.shape[0] < tk:
        zpad = jnp.zeros((tk - k.shape[0], k.shape[1]), CDT)
        k = jnp.concatenate([k, zpad], axis=0)
        v = jnp.concatenate([v, zpad], axis=0)
    if mask_neg:
        kpos = lax.broadcasted_iota(I32, (2 * tq, tk), 1) + (i * tq - A_PAST)
        neg = kpos < 0
    for g in range(q.shape[1] // LANES):
        sl = slice(g * LANES, (g + 1) * LANES)
        s = _mm_nt(_pair_stack(q[:, sl], tq), k[:, sl])
        s = s + jnp.concatenate([bias_ref[2 * g], bias_ref[2 * g + 1]], axis=0)
        if mask_neg:
            s = jnp.where(neg, NEG, s)
        m = jnp.max(s, axis=-1, keepdims=True)
        p = jnp.exp(s - m)
        l = jnp.sum(p, axis=-1, keepdims=True)
        o2 = _mm(p.astype(CDT), v[:, sl]) / l
        o_ref[0, :, sl] = _pair_merge(o2, tq).astype(o_ref.dtype)


def _band(q, k_parts, v_parts, k_specs, bias, tq, mask_neg):
    b, s, w = q.shape
    n_kv = len(k_parts)
    q_spec = pl.BlockSpec((1, tq, w), lambda bb, i: (bb, i, 0))
    return pl.pallas_call(
        functools.partial(_band_kernel, n_kv=n_kv, tq=tq, mask_neg=mask_neg),
        grid=(b, s // tq),
        in_specs=[q_spec] + k_specs + k_specs + [_const_spec(bias.shape)],
        out_specs=q_spec,
        out_shape=jax.ShapeDtypeStruct((b, s, w), CDT),
        compiler_params=_params(("parallel", "parallel")),
        name="band_attn",
    )(q, *k_parts, *v_parts, bias)


def _band_bias(rel_bias, q_pos, k_pos):
    nq, nk = len(q_pos), len(k_pos)
    qc = q_pos[:, None] // CHUNK
    kc = k_pos[None, :] // CHUNK
    ok = (kc <= qc) & (kc >= qc - A_PAST // CHUNK)
    n = nq + nk - 1
    d0 = int(q_pos[0] - k_pos[0])
    m = np.arange(n + 1)
    rel = np.where(m < nk, d0 - m, d0 - m + n + 1)
    table = jnp.take(rel_bias.astype(F32), jnp.asarray(np.clip(rel, -REL_CLIP, REL_CLIP) + REL_CLIP), axis=1)
    bias = jnp.tile(table, (1, nq))[:, :nq * n].reshape(-1, nq, n)[:, :, :nk]
    return jnp.where(jnp.asarray(ok)[None], bias, NEG)


def _dsa_index_kernel(ii_ref, jj_ref, qi_ref, ki2_ref, wt_ref, keys_ref, t_ref, c_ref, qst_ref, kscr_ref, k16_ref,
                      *, tq, tk, nk, unroll, qoff, length, topk, idx_bits, nreal):
    i = ii_ref[pl.program_id(1)]
    j = jj_ref[pl.program_id(1)]
    last = _last_tile(i, tq, tk, qoff, length, True)
    q0 = qoff + i * tq

    @pl.when(j == 0)
    def _():
        q = qi_ref[0]
        for g in range(IDX_HEADS // 2):
            qst_ref[2 * g * tq:(2 * g + 2) * tq, :] = _pair_stack(q[:, g * LANES:(g + 1) * LANES], tq)

    @pl.when(j <= last)
    def _():
        lg = _mm_nt(ki2_ref[0], qst_ref[...])
        wgt = wt_ref[0] * (IDX_HEADS ** -0.5)
        sc = jnp.zeros((tk, tq), F32)
        for h in range(IDX_HEADS):
            sc = sc + jnp.maximum(lg[:, h * tq:(h + 1) * tq], 0.0) * wgt[h:h + 1, :]
        sc = jnp.where(sc == 0.0, 0.0, sc)
        bits = lax.bitcast_convert_type(sc, I32)
        key = bits ^ ((bits >> 31) & 0x7FFFFFFF)
        kpos = j * tk + lax.broadcasted_iota(I32, (tk, tq), 0)
        qpos = q0 + lax.broadcasted_iota(I32, (tk, tq), 1)
        ok = ((kpos >> CHUNK_SHIFT) <= (qpos >> CHUNK_SHIFT)) & (kpos < length)
        key = jnp.where(ok, key, INT_MIN)
        keys_ref[0] = key
        kscr_ref[j] = key
        top = lax.bitcast_convert_type(bits & -65536, F32)
        k16_ref[j] = jnp.where(ok, top, -jnp.inf).astype(jnp.bfloat16)
        if unroll > 1:
            @pl.when(j + 1 < nk)
            def _():
                kscr_ref[j + 1] = jnp.full((tk, tq), INT_MIN, I32)
                k16_ref[j + 1] = jnp.full((tk, tq), -jnp.inf, jnp.bfloat16)

    @pl.when(j == last)
    def _():
        def count(pred):
            def body(t, acc):
                for u in range(unroll):
                    tt = unroll * t + u
                    kpos = tt * tk + lax.broadcasted_iota(I32, (tk, tq), 0)
                    m = jnp.where(pred(kscr_ref[tt], kpos), 1.0, 0.0)
                    acc = acc + jnp.sum(m, axis=0, keepdims=True)
                return acc
            return lax.fori_loop(0, (last + unroll) // unroll, body, jnp.zeros((1, tq), F32))

        kf = float(topk)
        qpos = q0 + lax.broadcasted_iota(I32, (1, tq), 1)
        n_adm = jnp.minimum(((qpos >> CHUNK_SHIFT) + 1) << CHUNK_SHIFT, length).astype(F32)
        real = qpos - qoff < nreal
        take_all = (n_adm < kf) | jnp.logical_not(real)

        def count16(c16):
            one, zero = jnp.ones((), jnp.bfloat16), jnp.zeros((), jnp.bfloat16)
            nslab = tk // PACKED_ROWS
            assert nslab <= 256

            def body(t, acc):
                for u in range(unroll):
                    m = jnp.where(k16_ref[unroll * t + u] >= c16, one, zero).reshape(nslab, PACKED_ROWS, tq)
                    part = m[0]
                    for r in range(1, nslab):
                        part = part + m[r]
                    acc = acc + jnp.sum(part.astype(F32), axis=0, keepdims=True)
                return acc
            return lax.fori_loop(0, (last + unroll) // unroll, body, jnp.zeros((1, tq), F32))

        def bit_step(it, prefix, nge, top_half):
            cand = prefix + lax.shift_left(jnp.int32(1), 31 - it)
            if top_half:
                cbits = cand ^ ((cand >> 31) & 0x7FFFFFFF)
                cbits = jnp.where((cand > 0) & (cand < 0x00800000), 0x00800000, cbits)
                cnt = count16(lax.bitcast_convert_type(cbits & -65536, F32).astype(jnp.bfloat16))
            else:
                cnt = count(lambda kt, _: kt >= cand)
            take = cnt >= kf
            return jnp.where(take, cand, prefix), jnp.where(take, cnt, nge)

        def all_settled(nge):
            return (jnp.min(jnp.where(take_all | (nge == kf), 1.0, 0.0)) > 0.0).astype(I32)

        def low_body(carry):
            it, prefix, nge, _ = carry
            prefix, nge = bit_step(it, prefix, nge, False)
            return it + 1, prefix, nge, all_settled(nge)

        prefix, nge = lax.fori_loop(0, 16, lambda it, c: bit_step(it, c[0], c[1], True),
                                    (jnp.full((1, tq), INT_MIN, I32), jnp.zeros((1, tq), F32)))
        _, prefix, nge, _ = lax.while_loop(lambda c: (c[0] < 32) & (c[3] == 0), low_body,
                                           (jnp.int32(16), prefix, nge, all_settled(nge)))
        found = prefix > INT_MIN
        thr = jnp.maximum(prefix, INT_MIN + 1)
        t_ref[0] = thr
        c_ref[0] = jnp.full((1, tq), 2 ** 30, I32)
        straddle = jnp.max(jnp.where(found & (nge > kf) & real, 1.0, 0.0))

        @pl.when(straddle > 0.0)
        def _():
            need = kf - count(lambda kt, _: kt > thr)

            def tie_body(it, x):
                cx = x + lax.shift_left(jnp.int32(1), idx_bits - 1 - it)
                g = count(lambda kt, kpos: jnp.where(kt == thr, kpos, INT_MAX) < cx)
                return jnp.where(g < need, cx, x)
            c_ref[0] = lax.fori_loop(0, idx_bits, tie_body, jnp.zeros((1, tq), I32))


def _dsa_index(qi, ki2, wt, *, tq, tk, qoff, length, topk, nreal):
    b, s, _ = qi.shape
    lp = ki2.shape[1]
    nq, nk = s // tq, lp // tk
    idx_bits = max(1, int(np.ceil(np.log2(lp))))
    qvec = pl.BlockSpec((1, 1, tq), lambda bb, t, ii, jj: (bb, 0, ii[t]))
    return _tri_call(
        functools.partial(_dsa_index_kernel, tq=tq, tk=tk, nk=nk, unroll=2 if nk % 2 == 0 else 1,
                          qoff=qoff, length=length, topk=topk, idx_bits=idx_bits, nreal=nreal),
        _tri_steps(nq, tq, tk, qoff, length, True), b,
        in_specs=[pl.BlockSpec((1, tq, 512), lambda bb, t, ii, jj: (bb, ii[t], 0)),
                  pl.BlockSpec((1, tk, LANES), lambda bb, t, ii, jj: (bb, jj[t], 0)),
                  pl.BlockSpec((1, IDX_HEADS, tq), lambda bb, t, ii, jj: (bb, 0, ii[t]))],
        out_specs=[pl.BlockSpec((1, tk, tq), lambda bb, t, ii, jj: (bb, jj[t], ii[t])), qvec, qvec],
        out_shape=[jax.ShapeDtypeStruct((b, lp, s), I32),
                   jax.ShapeDtypeStruct((b, 1, s), I32),
                   jax.ShapeDtypeStruct((b, 1, s), I32)],
        scratch=[pltpu.VMEM((IDX_HEADS * tq, LANES), CDT),
                 pltpu.VMEM((nk, tk, tq), I32),
                 pltpu.VMEM((nk, tk, tq), jnp.bfloat16)],
        name="dsa_index",
    )(qi, ki2, wt)


def _dsa_attn_kernel(ii_ref, jj_ref, q_ref, k_ref, vt_ref, keys_ref, t_ref, c_ref, o_ref, *scr,
                     tq, tk, qoff, length):
    qst_ref, m_ref, l_ref, acc_ref = scr[:4]
    i = ii_ref[pl.program_id(1)]
    j = jj_ref[pl.program_id(1)]
    last = _last_tile(i, tq, tk, qoff, length, True)

    @pl.when(j == 0)
    def _():
        _flash_init(q_ref, None, qst_ref, m_ref, l_ref, acc_ref, tq)

    @pl.when(j <= last)
    def _():
        kt = keys_ref[0]
        thr = t_ref[0]
        kpos = j * tk + lax.broadcasted_iota(I32, (tk, tq), 0)
        sel = (kt > thr) | (jnp.where(kt == thr, kpos, INT_MAX) <= c_ref[0])
        mb = jnp.where(sel, 0.0, NEG)
        score = lambda h: _mm_nt(k_ref[0, :, (h // 2) * LANES:(h // 2 + 1) * LANES],
                                 qst_ref[h // 2, (h % 2) * tq:(h % 2 + 1) * tq, :]) + mb
        _flash_heads(score, vt_ref, scr, tq)

    @pl.when(j == last)
    def _():
        _flash_finish(o_ref, l_ref, acc_ref, tq)


def _dsa_attn(q, k, vt, keys, thr, cut, *, tq, tk, qoff, length):
    b, s, w = q.shape
    qmap = lambda bb, t, ii, jj: (bb, ii[t], 0)
    qvec = pl.BlockSpec((1, 1, tq), lambda bb, t, ii, jj: (bb, 0, ii[t]))
    npair = w // LANES
    return _tri_call(
        functools.partial(_dsa_attn_kernel, tq=tq, tk=tk, qoff=qoff, length=length),
        _tri_steps(s // tq, tq, tk, qoff, length, True), b,
        in_specs=[pl.BlockSpec((1, tq, w), qmap),
                  pl.BlockSpec((1, tk, w), lambda bb, t, ii, jj: (bb, jj[t], 0)),
                  pl.BlockSpec((1, w, tk), lambda bb, t, ii, jj: (bb, 0, jj[t])),
                  pl.BlockSpec((1, tk, tq), lambda bb, t, ii, jj: (bb, jj[t], ii[t])),
                  qvec, qvec],
        out_specs=pl.BlockSpec((1, tq, w), qmap),
        out_shape=jax.ShapeDtypeStruct((b, s, w), CDT),
        scratch=_flash_scratch(npair, tq, tk, LANES, w),
        name="dsa_attn",
    )(q, k, vt, keys, thr, cut)


def _pieces3(x):
    pcs = _pieces(x, 3)
    return pcs + [jnp.zeros_like(pcs[0])] * (3 - len(pcs))


FOX_EXT = 9


def _cumsum_kernel(x_ref, tri_ref, place_ref, ones_ref, kx_ref, carry_ref, *, scale):
    @pl.when(pl.program_id(1) == 0)
    def _():
        carry_ref[...] = jnp.zeros(carry_ref.shape, F32)
    tri = tri_ref[...]
    tot = None
    for p in _pieces(x_ref[0], 3):
        t = _mm(tri, p)
        tot = t if tot is None else tot + t
    acc = carry_ref[...] + tot
    carry_ref[...] = acc[-1:, :]
    ext = None
    for c, p in enumerate(_pieces3(acc * (-scale))):
        t = _mm(p, place_ref[c])
        ext = t if ext is None else ext + t
    kx_ref[0] = (ext + ones_ref[...]).astype(kx_ref.dtype)


def _fox_key_lanes(x, tb, scale, nh):
    b, s, w = x.shape
    npair = nh // 2
    place = np.zeros((3, w, npair * LANES), np.float32)
    ones = np.zeros((1, npair * LANES), np.float32)
    for h in range(nh):
        for c in range(3):
            place[c, h, (h // 2) * LANES + (h % 2) * 6 + c] = 1.0
    for g in range(npair):
        ones[0, g * LANES + 3:g * LANES + 6] = 1.0
    tri = jnp.tril(jnp.ones((tb, tb), F32)).astype(CDT)
    place = jnp.asarray(place, CDT)
    spec = lambda wd: pl.BlockSpec((1, tb, wd), lambda bb, i: (bb, i, 0))
    return pl.pallas_call(
        functools.partial(_cumsum_kernel, scale=scale),
        grid=(b, s // tb),
        in_specs=[spec(w), _const_spec(tri.shape), _const_spec(place.shape), _const_spec(ones.shape)],
        out_specs=spec(npair * LANES),
        out_shape=jax.ShapeDtypeStruct((b, s, npair * LANES), CDT),
        scratch_shapes=[pltpu.VMEM((1, w), F32)],
        compiler_params=_params(("parallel", "arbitrary")),
        name="logf_cumsum",
    )(x, tri, place, jnp.asarray(ones))


def _fox_kernel(ii_ref, jj_ref, dead_ref, jeff_ref, q_ref, k_ref, kx_ref, vt_ref, aug_ref, o_ref, *scr,
                tq, tk, qoff, length):
    qst_ref, m_ref, l_ref, acc_ref = scr[:4]
    i = ii_ref[pl.program_id(1)]
    j = jj_ref[pl.program_id(1)]
    live = dead_ref[pl.program_id(1)] == 0
    last = _last_tile(i, tq, tk, qoff, length, False)
    q0 = qoff + i * tq

    @pl.when(j == 0)
    def _():
        _flash_init(q_ref, aug_ref, qst_ref, m_ref, l_ref, acc_ref, tq)

    def step(masked):
        if masked:
            kpos = j * tk + lax.broadcasted_iota(I32, (tk, 2 * tq), 0)
            lane = lax.broadcasted_iota(I32, (tk, 2 * tq), 1)
            causal = kpos <= q0 + jnp.where(lane >= tq, lane - tq, lane)

        def score(g):
            sl = slice(g * LANES, (g + 1) * LANES)
            st = _mm_nt(jnp.concatenate([k_ref[0, :, sl], kx_ref[0, :, sl]], axis=1), qst_ref[g])
            return jnp.where(causal, st, NEG) if masked else st
        _flash_pairs(score, vt_ref, scr, tq)

    diag = (j + 1) * tk - 1 > q0

    @pl.when(live & diag)
    def _():
        step(True)

    @pl.when(live & jnp.logical_not(diag))
    def _():
        step(False)

    @pl.when(j == last)
    def _():
        _flash_finish(o_ref, l_ref, acc_ref, tq)


FORGOTTEN_LOG2 = 150.0


def _fox_dead_tiles(kx, steps, tq, tk, qoff, qk_bound):
    ii, jj = steps
    nq, nk = int(ii.max()) + 1, int(jj.max()) + 1

    def neg_f(rows):
        a = kx[0, rows].reshape(len(rows), -1, LANES).astype(F32)
        return jnp.stack([a[..., 0:3].sum(-1), a[..., 6:9].sum(-1)], axis=-1).reshape(len(rows), -1)

    q0 = qoff + np.arange(nq) * tq
    ends = (np.arange(nk) + 1) * tk - 1
    drop = neg_f(q0)[:, None, :] - neg_f(ends)[None, :, :]
    dead = (jnp.min(drop, axis=-1) > 2.0 * qk_bound + FORGOTTEN_LOG2) & jnp.asarray(ends[None, :] < q0[:, None])
    lead = jnp.cumprod(dead.astype(I32), axis=1)
    first_live = jnp.sum(lead, axis=1)
    return lead[ii, jj], jnp.maximum(jnp.asarray(jj), first_live[ii])


def _fox(q, k, kx, vt, qaug, *, tq, tk, qoff, length, qk_bound=None):
    b, s, w = q.shape
    npair = w // LANES
    steps = _tri_steps(s // tq, tq, tk, qoff, length, False)
    if qk_bound is None or b != 1:
        tables = (jnp.zeros(steps[0].shape, I32), jnp.asarray(steps[1]))
    else:
        tables = _fox_dead_tiles(kx, steps, tq, tk, qoff, qk_bound)
    qmap = lambda bb, t, ii, jj, dead, jeff: (bb, ii[t], 0)
    kmap = lambda bb, t, ii, jj, dead, jeff: (bb, jeff[t], 0)
    return _tri_call(
        functools.partial(_fox_kernel, tq=tq, tk=tk, qoff=qoff, length=length),
        steps, b,
        in_specs=[pl.BlockSpec((1, tq, w), qmap),
                  pl.BlockSpec((1, tk, w), kmap),
                  pl.BlockSpec((1, tk, npair * LANES), kmap),
                  pl.BlockSpec((1, w, tk), lambda bb, t, ii, jj, dead, jeff: (bb, 0, jeff[t])),
                  pl.BlockSpec((1, 1, 2 * npair, LANES), lambda bb, t, ii, jj, dead, jeff: (bb, ii[t], 0, 0))],
        out_specs=pl.BlockSpec((1, tq, w), qmap),
        out_shape=jax.ShapeDtypeStruct((b, s, w), CDT),
        scratch=_flash_scratch(npair, tq, tk, 2 * LANES, w),
        name="fox_attn", extra=tables,
    )(q, k, kx, vt, qaug)


def _fox_qaug(kx0):
    b, nq, _ = kx0.shape
    a = kx0.reshape(b, nq, -1, LANES).astype(F32)
    one, zero = jnp.ones_like(a[..., 0:3]), jnp.zeros_like(a[..., 0:3])
    even = jnp.concatenate([one, -a[..., 0:3], zero], axis=-1)
    odd = jnp.concatenate([zero, -a[..., 6:9], one], axis=-1)
    heads = jnp.stack([even, odd], axis=3).reshape(b, nq, -1, FOX_EXT)
    return jnp.pad(heads, ((0, 0), (0, 0), (0, 0), (0, LANES - FOX_EXT)))


def _rope_tables(pos):
    half = HEAD_DIM // 2
    inv = ROPE_THETA ** (-jnp.arange(half, dtype=F32) / half)
    ang = pos.astype(F32)[:, None] * inv[None, :]
    cos, sin = jnp.cos(ang), jnp.sin(ang)
    return (jnp.tile(jnp.concatenate([cos, cos], axis=1), (1, 2)),
            jnp.tile(jnp.concatenate([-sin, sin], axis=1), (1, 2)))


def _pad_rows(x, n):
    return x if x.shape[1] == n else jnp.pad(x, ((0, 0), (0, n - x.shape[1]), (0, 0)))


def _tiles(s):
    tk = 512 if s % 512 == 0 else s
    tq = 512 if s % 512 == 0 else s
    tqi = 256 if s % 256 == 0 else s
    tqf = tq
    return tq, tqi, tk, tqf


def kernel(x_prompt, x_sample, cache_a_k, cache_a_v, cache_b_k, cache_b_v, cache_b_ik, cache_c_k, cache_c_v,
           cache_c_logf, p_prompt, p_sample, g_mix, w_in_even, qn_a, kn_a, rel_bias_a, qn_b, kn_b, w_out_even,
           w_in_odd, b_f, qn_c, kn_c, w_out_odd, g_mlp, w_up, w_down, g_ple, w_ple_gate, w_ple_proj):
    B, S, D = x_prompt.shape
    DB, n, _ = x_sample.shape
    P = cache_b_k.shape[2]
    W_A = cache_a_k.shape[2]
    W_AP = min(A_PAST, S)
    depth = g_mix.shape[0]
    L = P + n
    Lp = -(-L // LANES) * LANES
    nqp = -(-n // LANES) * LANES
    assert B == 1 and S % 128 == 0 and n % 16 == 0 and P % 16 == 0

    row = lambda v: v.astype(F32).reshape(1, -1)
    bd = jnp.asarray(np.kron(np.eye(MXU_TILE // HEAD_DIM), np.full((HEAD_DIM, HEAD_DIM), 1.0 / HEAD_DIM)), CDT)
    cos_p, sin_p = _rope_tables(jnp.arange(S))
    cos_s, sin_s = (jnp.tile(t, (DB, 1)) for t in _rope_tables(P + jnp.arange(n)))
    tq, tqi, tk, tqf = _tiles(S)
    tr = lambda t: jnp.swapaxes(t, 1, 2)
    catp = lambda c, new: _pad_rows(jnp.concatenate([c.astype(CDT), new], axis=1), Lp)

    outs = {name: [] for name in ("a_k_p", "a_v_p", "b_k_p", "b_v_p", "b_ik_p", "c_k_p", "c_v_p", "c_lf_p",
                                  "a_k_s", "a_v_s", "b_k_s", "b_v_s", "b_ik_s", "c_k_s", "c_v_s", "c_lf_s")}
    h_p = x_prompt.reshape(B * S, D)
    h_s = x_sample.reshape(DB * n, D)
    for i in range(depth):
        g = row(g_mix[i])
        if i % 2 == 0:
            e = i // 2
            wsz = 512
            w = w_in_even[e]
            w_all = jnp.concatenate([w[:, :7 * wsz], jnp.pad(w[:, 7 * wsz:], ((0, 0), (0, LANES - 72)))],
                                    axis=1).astype(CDT)
            gn = jnp.stack([jnp.tile(v[e].astype(F32), wsz // HEAD_DIM) for v in (qn_a, kn_a, qn_b, kn_b)])
            (ka, va, kb, vb, kw, qa_c, ka_c, va_c, qb_c, kb_c, vb_c, qi_c, ki2_c) = _proj_even(
                h_p, g, w_all, gn, bd, cos_p, sin_p)
            r3 = lambda t: t.reshape(B, S, t.shape[-1])
            tqa = 256 if S % 256 == 0 else 128
            npc = (A_PAST + tqa) // tqa
            k_specs = [pl.BlockSpec((1, tqa, wsz), functools.partial(
                lambda bb, ii, pp: (bb, jnp.maximum(ii + pp - A_PAST // tqa, 0), 0), pp=pp)) for pp in range(npc)]
            bias_p = _band_bias(rel_bias_a[e], np.arange(tqa), np.arange(-A_PAST, tqa))
            oa = _band(r3(qa_c), [r3(ka_c)] * npc, [r3(va_c)] * npc, k_specs, bias_p, tqa, True)
            topk_p = min(TOPK_MAX, S // 4)
            wt = tr(r3(kw)[..., 64:64 + IDX_HEADS])
            keys, thr, cut = _dsa_index(r3(qi_c), r3(ki2_c), wt, tq=tqi, tk=tk, qoff=0, length=S, topk=topk_p,
                                        nreal=S)
            ob = _dsa_attn(r3(qb_c), r3(kb_c), tr(r3(vb_c)), keys, thr, cut, tq=tq, tk=tk, qoff=0, length=S)
            os_p = [oa.reshape(B * S, wsz), ob.reshape(B * S, wsz)]
            hd = lambda t: t.reshape(B, S, -1, HEAD_DIM)
            outs["a_k_p"].append(hd(ka)[:, S - W_AP:]); outs["a_v_p"].append(hd(va)[:, S - W_AP:])
            outs["b_k_p"].append(hd(kb)); outs["b_v_p"].append(hd(vb))
            outs["b_ik_p"].append(kw[:, :64].reshape(B, S, 64))
            (ka, va, kb, vb, kw, qa_c, ka_c, va_c, qb_c, kb_c, vb_c, qi_c, ki2_c) = _proj_even(
                h_s, g, w_all, gn, bd, cos_s, sin_s)
            r3 = lambda t: t.reshape(DB, n, t.shape[-1])
            ca_k = cache_a_k[e].reshape(DB, W_A, wsz)
            ca_v = cache_a_v[e].reshape(DB, W_A, wsz)
            full = lambda rows, wd: pl.BlockSpec((1, rows, wd), lambda bb, ii: (bb, 0, 0))
            bias_s = _band_bias(rel_bias_a[e], P + np.arange(n), P - W_A + np.arange(W_A + n))
            bias_s = jnp.pad(bias_s, ((0, 0), (0, 0), (0, -(W_A + n) % LANES)), constant_values=NEG)
            oa = _band(r3(qa_c), [ca_k, r3(ka)], [ca_v, r3(va)], [full(W_A, wsz), full(n, wsz)], bias_s, n, False)
            ik = cache_b_ik[e].astype(CDT)
            ki2_all = catp(jnp.concatenate([ik, ik], axis=-1), r3(ki2_c))
            kb_all = catp(cache_b_k[e].reshape(DB, P, wsz), r3(kb_c))
            vb_all = catp(cache_b_v[e].reshape(DB, P, wsz), r3(vb_c))
            topk_s = min(TOPK_MAX, L // 4)
            wt = _pad_rows(r3(kw)[..., 64:64 + IDX_HEADS], nqp)
            keys, thr, cut = _dsa_index(_pad_rows(r3(qi_c), nqp), ki2_all, tr(wt),
                                        tq=nqp, tk=Lp, qoff=P, length=L, topk=topk_s, nreal=n)
            ob = _dsa_attn(_pad_rows(r3(qb_c), nqp), kb_all, tr(vb_all), keys, thr, cut,
                           tq=nqp, tk=Lp, qoff=P, length=L)[:, :n]
            os_s = [oa.reshape(DB * n, wsz), ob.reshape(DB * n, wsz)]
            hd = lambda t: t.reshape(DB, n, -1, HEAD_DIM)
            outs["a_k_s"].append(jnp.concatenate([cache_a_k[e], hd(ka)], axis=1)[:, n:])
            outs["a_v_s"].append(jnp.concatenate([cache_a_v[e], hd(va)], axis=1)[:, n:])
            outs["b_k_s"].append(hd(kb)); outs["b_v_s"].append(hd(vb))
            outs["b_ik_s"].append(kw[:, :64].reshape(DB, n, 64))
            w_out = w_out_even[e].astype(CDT)
            wos = [w_out[:wsz], w_out[wsz:]]
        else:
            o = i // 2
            nh = b_f.shape[1]
            wsz = nh * HEAD_DIM
            w = w_in_odd[o]
            w_all = jnp.concatenate([w[:, :3 * wsz], jnp.pad(w[:, 3 * wsz:], ((0, 0), (0, LANES - nh)))],
                                    axis=1).astype(CDT)
            gn = jnp.stack([jnp.tile(v[o].astype(F32), wsz // HEAD_DIM) for v in (qn_c, kn_c)])
            bf = jnp.pad(b_f[o].astype(F32), (0, LANES - nh)).reshape(1, LANES)
            k, v, lf, q_c, k_c, v_c = _proj_odd(h_p, g, w_all, gn, bf, bd)
            r3 = lambda t: t.reshape(B, S, t.shape[-1])
            kx = _fox_key_lanes(r3(lf), 512 if S % 512 == 0 else S, LOG2E, nh)
            qk_bound = (1.05 * HEAD_DIM * (HEAD_DIM ** -0.5 * LOG2E)
                        * jnp.max(jnp.abs(qn_c[o].astype(F32))) * jnp.max(jnp.abs(kn_c[o].astype(F32))))
            oc = _fox(r3(q_c), r3(k_c), kx, tr(r3(v_c)), _fox_qaug(kx[:, ::tqf]), tq=tqf, tk=tk, qoff=0, length=S,
                      qk_bound=qk_bound)
            os_p = [oc.reshape(B * S, wsz)]
            hd = lambda t: t.reshape(B, S, nh, HEAD_DIM)
            outs["c_k_p"].append(hd(k)); outs["c_v_p"].append(hd(v)); outs["c_lf_p"].append(lf[:, :nh].reshape(B, S, nh))
            k, v, lf, q_c, k_c, v_c = _proj_odd(h_s, g, w_all, gn, bf, bd)
            r3 = lambda t: t.reshape(DB, n, t.shape[-1])
            lf_all = jnp.concatenate([jnp.pad(cache_c_logf[o].astype(F32), ((0, 0), (0, 0), (0, LANES - nh))), r3(lf)],
                                     axis=1)
            kx = _fox_key_lanes(lf_all, L, LOG2E, nh)
            k_all = catp(cache_c_k[o].reshape(DB, P, wsz), r3(k_c))
            v_all = catp(cache_c_v[o].reshape(DB, P, wsz), r3(v_c))
            oc = _fox(_pad_rows(r3(q_c), nqp), k_all, _pad_rows(kx, Lp), tr(v_all),
                      _fox_qaug(kx[:, P:P + 1]), tq=nqp, tk=Lp, qoff=P, length=L)[:, :n]
            os_s = [oc.reshape(DB * n, wsz)]
            hd = lambda t: t.reshape(DB, n, nh, HEAD_DIM)
            outs["c_k_s"].append(hd(k)); outs["c_v_s"].append(hd(v)); outs["c_lf_s"].append(lf[:, :nh].reshape(DB, n, nh))
            w_out = w_out_odd[o].astype(CDT)
            wos = [w_out]
        post_w = (row(g_mlp[i]), w_up[i].astype(CDT), w_down[i].astype(CDT), row(g_ple[i]),
                  w_ple_gate[i].astype(CDT), w_ple_proj[i].astype(CDT))
        h_p = _post(h_p, os_p, p_prompt[i].reshape(B * S, -1), wos, *post_w)
        h_s = _post(h_s, os_s, p_sample[i].reshape(DB * n, -1), wos, *post_w)

    st = jnp.stack
    names = ("a_k_p", "a_v_p", "b_k_p", "b_v_p", "b_ik_p", "c_k_p", "c_v_p", "c_lf_p",
             "a_k_s", "a_v_s", "b_k_s", "b_v_s", "b_ik_s", "c_k_s", "c_v_s", "c_lf_s")
    return (h_p.reshape(B, S, D), h_s.reshape(DB, n, D)) + tuple(st(outs[nm]) for nm in names)
```
